```python
import math
import jax, jax.numpy as jnp
from jax import lax
import numpy as np

D_MODEL = 1024
BATCH = 4
SEQ = 4096
DEPTH = 2
DEC_BATCH = 128
DEC_SEQ = 4
PAST_LEN = 2048
PAGE_SIZE = 128

N_EVEN = (DEPTH + 1) // 2
N_ODD = DEPTH // 2

H_A = 8
HD_A = 64
W_A = H_A * HD_A
MOBA_BLOCK = 256
MOBA_TOPK = 3
MOBA_Q_BLOCK = 32
NUM_BUCKETS = 32
MAX_DISTANCE = 128

H_B = 4
DK_B = 64
DV_B = 128
W_B = H_B * DV_B
RET_CHUNK = 128

E_SPLITS = (W_A, W_A, W_A, W_A, H_B * DK_B, H_B * DK_B, W_B, W_B)
E_IN = sum(E_SPLITS)
E_MIX = W_A + W_B

G_C = 4
W_C = 512
DG_C = W_C // G_C
SGU_CHUNK = 128

H_D = 8
P_D = 64
N_D = 128
G_D = 2
HPG_D = H_D // G_D
W_D = H_D * P_D
CONV_W = 4
CONV_DIM = W_D + 2 * G_D * N_D
SSD_CHUNK = 128

O_SPLITS = (W_C, W_C, W_C, W_D, CONV_DIM, H_D)
O_IN = sum(O_SPLITS)
O_MIX = W_C + W_D

NEG_INF = -1e30
EPS = 1e-6

kernel_name = "hybrid_moba_retention_sgu_ssd_step"


def rms_norm(x, w=None):
    xf = x.astype(jnp.float32)
    y = xf * lax.rsqrt(jnp.mean(xf * xf, axis=-1, keepdims=True) + EPS)
    if w is not None:
        y = y * w.astype(jnp.float32)
    return y.astype(x.dtype)


def layer_norm(x):
    xf = x.astype(jnp.float32)
    mu = jnp.mean(xf, axis=-1, keepdims=True)
    var = jnp.mean(jnp.square(xf - mu), axis=-1, keepdims=True)
    return ((xf - mu) * lax.rsqrt(var + EPS)).astype(x.dtype)


def split_cols(z, sizes):
    return jnp.split(z, [int(s) for s in np.cumsum(sizes)[:-1]], axis=-1)


def to_chunks(a, c):
    b, s = a.shape[:2]
    return a.reshape(b, s // c, c, *a.shape[2:]).swapaxes(0, 1)


def from_chunks(a):
    n, b, c = a.shape[:3]
    return a.swapaxes(0, 1).reshape(b, n * c, *a.shape[3:])


def adaln_in(x, c, norm_w, ada_w, ada_b):
    mod = jnp.einsum('bd,de->be', jax.nn.silu(c), ada_w) + ada_b
    shift, scale, gate = jnp.split(mod, 3, axis=-1)
    h = rms_norm(x, norm_w) * (1.0 + scale[:, None]) + shift[:, None]
    return h, gate[:, None]


def t5_bucket(rel):
    n = jnp.maximum(rel, 0)
    max_exact = NUM_BUCKETS // 2
    nf = jnp.maximum(n, 1).astype(jnp.float32)
    large = max_exact + (jnp.log(nf / max_exact) / math.log(MAX_DISTANCE / max_exact)
                         * (NUM_BUCKETS - max_exact)).astype(jnp.int32)
    large = jnp.minimum(large, NUM_BUCKETS - 1)
    return jnp.where(n < max_exact, n, large)


def rotary(x, pos):
    half = x.shape[-1] // 2
    inv = 1.0 / (10000.0 ** (jnp.arange(half, dtype=jnp.float32) / half))
    ang = pos.astype(jnp.float32)[:, None] * inv[None, :]
    cos = jnp.cos(ang)[None, :, None, :].astype(x.dtype)
    sin = jnp.sin(ang)[None, :, None, :].astype(x.dtype)
    x1, x2 = x[..., :half], x[..., half:]
    return jnp.concatenate([x1 * cos - x2 * sin, x1 * sin + x2 * cos], axis=-1)


def moba_attend(q, q_pos, own_k, own_v, own_pos, rel_bias, sel_kv):
    b, nq, h, d = q.shape
    scale = d ** -0.5
    tab = rel_bias.astype(jnp.float32)
    lo = jnp.einsum('bqhd,bjhd->bhqj', q, own_k).astype(jnp.float32) * scale
    lo = lo + tab[t5_bucket(q_pos[:, None] - own_pos[None, :])].transpose(2, 0, 1)
    lo = jnp.where(own_pos[None, :] <= q_pos[:, None], lo, NEG_INF)
    if sel_kv is None:
        p = jax.nn.softmax(lo, axis=-1).astype(own_v.dtype)
        return jnp.einsum('bhqj,bjhd->bqhd', p, own_v)
    sk, sv, spos, svalid = sel_kv
    nk = sk.shape[3]
    hidx = jnp.arange(h)[None, :, None, None, None]
    ls = jnp.einsum('bqhd,bhqkjd->bhqkj', q, sk).astype(jnp.float32) * scale
    ls = ls + tab[t5_bucket(q_pos[None, None, :, None, None] - spos), hidx]
    ls = jnp.where(svalid[..., None], ls, NEG_INF).reshape(b, h, nq, nk * MOBA_BLOCK)
    p = jax.nn.softmax(jnp.concatenate([ls, lo], axis=-1), axis=-1).astype(own_v.dtype)
    n_sel_rows = nk * MOBA_BLOCK
    out = jnp.einsum('bhqj,bjhd->bqhd', p[..., n_sel_rows:], own_v)
    out = out + jnp.einsum('bhqkj,bhqkjd->bqhd',
                           p[..., :n_sel_rows].reshape(b, h, nq, nk, MOBA_BLOCK), sv)
    return out


def moba_prompt(q, k, v, rel_bias):
    B, S, H, D = q.shape
    nb = -(-S // MOBA_BLOCK)
    pad = nb * MOBA_BLOCK - S
    kb = jnp.pad(k, ((0, 0), (0, pad), (0, 0), (0, 0))).reshape(B, nb, MOBA_BLOCK, H, D)
    vb = jnp.pad(v, ((0, 0), (0, pad), (0, 0), (0, 0))).reshape(B, nb, MOBA_BLOCK, H, D)
    n_sel = min(MOBA_TOPK, nb - 1)
    if n_sel > 0:
        kmean = kb.astype(jnp.float32).mean(axis=2)
        gate = jnp.einsum('bshd,bnhd->bhsn', q.astype(jnp.float32), kmean)
        qblk = jnp.arange(S) // MOBA_BLOCK
        gate = jnp.where(jnp.arange(nb)[None, :] < qblk[:, None], gate, NEG_INF)
        _, sel = lax.top_k(gate, n_sel)
        sel_valid = sel < qblk[None, None, :, None]
    bidx = jnp.arange(B)[:, None, None, None]
    hidx = jnp.arange(H)[None, :, None, None]
    blk_off = jnp.arange(MOBA_BLOCK)

    def one_query_block(i):
        s0 = i * MOBA_Q_BLOCK
        qc = lax.dynamic_slice_in_dim(q, s0, MOBA_Q_BLOCK, axis=1)
        q_pos = s0 + jnp.arange(MOBA_Q_BLOCK)
        ob = s0 // MOBA_BLOCK
        own_k = lax.dynamic_index_in_dim(kb, ob, axis=1, keepdims=False)
        own_v = lax.dynamic_index_in_dim(vb, ob, axis=1, keepdims=False)
        own_pos = ob * MOBA_BLOCK + blk_off
        sel_kv = None
        if n_sel > 0:
            sc = lax.dynamic_slice_in_dim(sel, s0, MOBA_Q_BLOCK, axis=2)
            vc = lax.dynamic_slice_in_dim(sel_valid, s0, MOBA_Q_BLOCK, axis=2)
            sel_kv = (kb[bidx, sc, :, hidx], vb[bidx, sc, :, hidx],
                      sc[..., None] * MOBA_BLOCK + blk_off, vc)
        return moba_attend(qc, q_pos, own_k, own_v, own_pos, rel_bias, sel_kv)

    out = lax.map(one_query_block, jnp.arange(S // MOBA_Q_BLOCK))
    return from_chunks(out)


def moba_sample(q, k_new, v_new, cache_k, cache_v, page_table, rel_bias):
    DB, T, H, D = q.shape
    ppb = MOBA_BLOCK // PAGE_SIZE
    npb = PAST_LEN // MOBA_BLOCK
    own_start = npb * MOBA_BLOCK
    q_pos = PAST_LEN + jnp.arange(T)
    pt_own = page_table[:, own_start // PAGE_SIZE:]
    n_own = pt_own.shape[1] * PAGE_SIZE
    own_k = jnp.concatenate([cache_k[pt_own].reshape(DB, n_own, H, D), k_new], axis=1)
    own_v = jnp.concatenate([cache_v[pt_own].reshape(DB, n_own, H, D), v_new], axis=1)
    own_pos = own_start + jnp.arange(n_own + T)
    n_sel = min(MOBA_TOPK, npb)
    sel_kv = None
    if n_sel > 0:
        page_mean = cache_k.astype(jnp.float32).mean(axis=1)
        kmean = page_mean[page_table[:, :npb * ppb]].reshape(DB, npb, ppb, H, D).mean(axis=2)
        gate = jnp.einsum('bthd,bnhd->bhtn', q.astype(jnp.float32), kmean)
        _, sel = lax.top_k(gate, n_sel)
        bidx = jnp.arange(DB)[:, None, None, None, None]
        hidx = jnp.arange(H)[None, :, None, None, None]
        phys = page_table[bidx, sel[..., None] * ppb + jnp.arange(ppb)]
        sk = cache_k[phys, :, hidx].reshape(DB, H, T, n_sel, MOBA_BLOCK, D)
        sv = cache_v[phys, :, hidx].reshape(DB, H, T, n_sel, MOBA_BLOCK, D)
        spos = sel[..., None] * MOBA_BLOCK + jnp.arange(MOBA_BLOCK)
        sel_kv = (sk, sv, spos, jnp.ones(sel.shape, bool))
    return moba_attend(q, q_pos, own_k, own_v, own_pos, rel_bias, sel_kv)


def retention_chunk(state, q, k, v):
    L = q.shape[1]
    log_g = jnp.log(1.0 - 2.0 ** (-5.0 - jnp.arange(H_B, dtype=jnp.float32)))
    idx = jnp.arange(L, dtype=jnp.float32)
    diff = idx[:, None] - idx[None, :]
    dmat = jnp.where((diff >= 0)[..., None], jnp.exp(jnp.maximum(diff, 0.0)[..., None] * log_g), 0.0)
    dmat = dmat.transpose(2, 0, 1).astype(q.dtype)
    scores = jnp.einsum('bihd,bjhd->bhij', q, k) * dmat
    out = jnp.einsum('bhij,bjhe->bihe', scores, v)
    q_dec = jnp.exp((idx + 1.0)[:, None] * log_g).astype(q.dtype)
    out = out + jnp.einsum('bihd,bhde->bihe', q * q_dec[None, :, :, None], state)
    k_dec = jnp.exp((L - 1.0 - idx)[:, None] * log_g).astype(k.dtype)
    new_state = (state * jnp.exp(L * log_g).astype(state.dtype)[None, :, None, None]
                 + jnp.einsum('bjhd,bjhe->bhde', k * k_dec[None, :, :, None], v))
    return new_state, out


def retention_prompt(q, k, v):
    state0 = jnp.zeros((q.shape[0], H_B, DK_B, DV_B), v.dtype)

    def step(st, inp):
        return retention_chunk(st, *inp)

    st, out = lax.scan(step, state0, (to_chunks(q, RET_CHUNK), to_chunks(k, RET_CHUNK),
                                      to_chunks(v, RET_CHUNK)))
    return st, from_chunks(out)


def ssd_chunk(state, x, dt, A, Bm, Cm):
    L = x.shape[1]
    a = (dt * A).astype(jnp.float32)
    cum = jnp.cumsum(a, axis=1)
    seg = cum[:, :, None] - cum[:, None, :]
    causal = jnp.tril(jnp.ones((L, L), bool))[None, :, :, None, None]
    decay = jnp.exp(jnp.where(causal, seg, -jnp.inf)).astype(x.dtype)
    cb = jnp.einsum('bign,bjgn->bijg', Cm, Bm)
    y = jnp.einsum('bijg,bijgh,bjgh,bjghp->bighp', cb, decay, dt, x)
    y = y + jnp.einsum('bign,bghpn,bigh->bighp', Cm, state, jnp.exp(cum).astype(x.dtype))
    last = cum[:, -1]
    w = jnp.exp(last[:, None] - cum).astype(x.dtype) * dt
    new_state = (state * jnp.exp(last).astype(x.dtype)[..., None, None]
                 + jnp.einsum('bjgn,bjgh,bjghp->bghpn', Bm, w, x))
    return new_state, y


def ssd_prompt(x, dt, A, Bm, Cm):
    state0 = jnp.zeros((x.shape[0], G_D, HPG_D, P_D, N_D), x.dtype)

    def step(st, inp):
        xc, dtc, bc, cc = inp
        return ssd_chunk(st, xc, dtc, A, bc, cc)

    st, y = lax.scan(step, state0, (to_chunks(x, SSD_CHUNK), to_chunks(dt, SSD_CHUNK),
                                    to_chunks(Bm, SSD_CHUNK), to_chunks(Cm, SSD_CHUNK)))
    return st, from_chunks(y)


def dwconv(xpad, w, b):
    c = xpad.shape[-1]
    out = lax.conv_general_dilated(xpad, w[:, None, :], (1,), 'VALID',
                                   dimension_numbers=('NWC', 'WIO', 'NWC'),
                                   feature_group_count=c)
    return out + b


def even_mix(h, pos, in_w, q_norm_w, k_norm_w):
    b, L, _ = h.shape
    z = jnp.einsum('bld,de->ble', h, in_w)
    qa, ka, va, ga, qb, kb, vb, gb = split_cols(z, E_SPLITS)
    qa = rms_norm(qa.reshape(b, L, H_A, HD_A), q_norm_w)
    ka = rms_norm(ka.reshape(b, L, H_A, HD_A), k_norm_w)
    va = va.reshape(b, L, H_A, HD_A)
    qb = rotary(qb.reshape(b, L, H_B, DK_B), pos)
    kb = rotary(kb.reshape(b, L, H_B, DK_B), pos) * (DK_B ** -0.5)
    vb = vb.reshape(b, L, H_B, DV_B)
    return qa, ka, va, ga, qb, kb, vb, gb


def even_out(oa, ga, ob, gb, out_w):
    b, L = oa.shape[:2]
    oa = oa.reshape(b, L, W_A) * jax.nn.silu(ga)
    ob = rms_norm(ob).reshape(b, L, W_B) * jax.nn.silu(gb)
    return jnp.einsum('ble,ed->bld', jnp.concatenate([oa, ob], axis=-1), out_w)


def odd_mix(h, in_w):
    z = jnp.einsum('bld,de->ble', h, in_w)
    u, v, gc, zg, xbc, dt = split_cols(z, O_SPLITS)
    u = jax.nn.gelu(u)
    v = layer_norm(jax.nn.gelu(v))
    return u, v, gc, zg, xbc, dt


def sgu_prompt(v, sgu_w, sgu_b):
    b, s, _ = v.shape
    w = jnp.tril(sgu_w)
    vc = v.reshape(b, s // SGU_CHUNK, SGU_CHUNK, G_C, DG_C)
    out = jnp.einsum('gij,bcjgd->bcigd', w, vc) + sgu_b.T[None, None, :, :, None]
    return out.reshape(b, s, W_C)


def sgu_sample(v, sgu_w, sgu_b):
    b, t, _ = v.shape
    w = jnp.tril(sgu_w[:, :t, :t])
    out = jnp.einsum('gij,bjgd->bigd', w, v.reshape(b, t, G_C, DG_C)) + sgu_b[:, :t].T[None, :, :, None]
    return out.reshape(b, t, W_C)


def ssd_inputs(conv_out, dt, dt_bias, a_log):
    b, L = conv_out.shape[:2]
    xh, bm, cm = split_cols(jax.nn.silu(conv_out), (W_D, G_D * N_D, G_D * N_D))
    xh = xh.reshape(b, L, G_D, HPG_D, P_D)
    bm = bm.reshape(b, L, G_D, N_D)
    cm = cm.reshape(b, L, G_D, N_D)
    dth = jax.nn.softplus(dt + dt_bias).reshape(b, L, G_D, HPG_D)
    A = -jnp.exp(a_log.astype(jnp.float32)).reshape(G_D, HPG_D)
    return xh, dth, A, bm, cm


def odd_out(u, s, gc, y, xh, d_skip, zg, ssm_norm_w, out_w):
    b, L = u.shape[:2]
    oc = u * s * jax.nn.silu(gc)
    y = y + xh * d_skip.reshape(G_D, HPG_D)[:, :, None]
    y = y.reshape(b, L, W_D) * jax.nn.silu(zg)
    y = rms_norm(y.reshape(b, L, G_D, W_D // G_D), ssm_norm_w.reshape(G_D, W_D // G_D)).reshape(b, L, W_D)
    return jnp.einsum('ble,ed->bld', jnp.concatenate([oc, y], axis=-1), out_w)


def setup_inputs(seed: int = 0) -> dict:
    key = jax.random.key(seed)
    keys = iter(jax.random.split(key, 40))

    def nrm(shape, scale):
        return jax.random.normal(next(keys), shape, jnp.float32) * scale

    n_pages = PAST_LEN // PAGE_SIZE
    n_used = DEC_BATCH * n_pages
    n_phys = n_used + max(1, n_used // 4)
    page_table = jax.random.permutation(next(keys), n_phys)[:n_used].reshape(DEC_BATCH, n_pages).astype(jnp.int32)
    dt0 = jnp.exp(jax.random.uniform(next(keys), (N_ODD, H_D), jnp.float32,
                                     minval=math.log(1e-3), maxval=math.log(1e-1)))
    dt_bias = dt0 + jnp.log(-jnp.expm1(-dt0))
    a_log = jnp.log(jax.random.uniform(next(keys), (N_ODD, H_D), jnp.float32, minval=1.0, maxval=16.0))
    d = D_MODEL
    return {
        "x_prompt": nrm((BATCH, SEQ, d), 1.0),
        "x_sample": nrm((DEC_BATCH, DEC_SEQ, d), 1.0),
        "cache_k": nrm((N_EVEN, n_phys, PAGE_SIZE, H_A, HD_A), 1.0),
        "cache_v": nrm((N_EVEN, n_phys, PAGE_SIZE, H_A, HD_A), 1.0),
        "state_ret": nrm((N_EVEN, DEC_BATCH, H_B, DK_B, DV_B), 1.0),
        "state_ssm": nrm((N_ODD, DEC_BATCH, H_D, P_D, N_D), 0.5),
        "state_conv": nrm((N_ODD, DEC_BATCH, CONV_W - 1, CONV_DIM), 1.0),
        "page_table": page_table,
        "c_prompt": nrm((BATCH, d), 1.0),
        "c_sample": nrm((DEC_BATCH, d), 1.0),
        "rel_bias": nrm((NUM_BUCKETS, H_A), 0.5),
        "e_norm_w": 1.0 + nrm((N_EVEN, d), 0.02),
        "e_ada_w": nrm((N_EVEN, d, 3 * d), 0.5 * d ** -0.5),
        "e_ada_b": nrm((N_EVEN, 3 * d), 0.01),
        "e_in_w": nrm((N_EVEN, d, E_IN), d ** -0.5),
        "e_q_norm_w": 1.0 + nrm((N_EVEN, HD_A), 0.02),
        "e_k_norm_w": 1.0 + nrm((N_EVEN, HD_A), 0.02),
        "e_out_w": nrm((N_EVEN, E_MIX, d), E_MIX ** -0.5),
        "o_norm_w": 1.0 + nrm((N_ODD, d), 0.02),
        "o_ada_w": nrm((N_ODD, d, 3 * d), 0.5 * d ** -0.5),
        "o_ada_b": nrm((N_ODD, 3 * d), 0.01),
        "o_in_w": nrm((N_ODD, d, O_IN), d ** -0.5),
        "o_sgu_w": nrm((N_ODD, G_C, SGU_CHUNK, SGU_CHUNK), SGU_CHUNK ** -0.5),
        "o_sgu_b": 1.0 + nrm((N_ODD, G_C, SGU_CHUNK), 0.01),
        "o_conv_w": nrm((N_ODD, CONV_W, CONV_DIM), CONV_W ** -0.5),
        "o_conv_b": nrm((N_ODD, CONV_DIM), 0.01),
        "o_dt_bias": dt_bias,
        "o_A_log": a_log,
        "o_D": 1.0 + nrm((N_ODD, H_D), 0.02),
        "o_ssm_norm_w": 1.0 + nrm((N_ODD, W_D), 0.02),
        "o_out_w": nrm((N_ODD, O_MIX, d), O_MIX ** -0.5),
    }


def reference(x_prompt, x_sample, cache_k, cache_v, state_ret, state_ssm, state_conv, page_table,
              c_prompt, c_sample, rel_bias, e_norm_w, e_ada_w, e_ada_b, e_in_w, e_q_norm_w, e_k_norm_w,
              e_out_w, o_norm_w, o_ada_w, o_ada_b, o_in_w, o_sgu_w, o_sgu_b, o_conv_w, o_conv_b,
              o_dt_bias, o_A_log, o_D, o_ssm_norm_w, o_out_w):
    xp, xs = x_prompt, x_sample
    bp, s_len = xp.shape[:2]
    bs, t_len = xs.shape[:2]
    pos_p = jnp.arange(s_len)
    pos_s = PAST_LEN + jnp.arange(t_len)
    kp_l, vp_l, ks_l, vs_l, retp_l, rets_l = [], [], [], [], [], []
    sguv_l, ssmp_l, ssms_l, convp_l, convs_l = [], [], [], [], []
    for layer in range(DEPTH):
        i = layer // 2
        if layer % 2 == 0:
            h, g = adaln_in(xp, c_prompt, e_norm_w[i], e_ada_w[i], e_ada_b[i])
            qa, ka, va, ga, qb, kb, vb, gb = even_mix(h, pos_p, e_in_w[i], e_q_norm_w[i], e_k_norm_w[i])
            oa = moba_prompt(qa, ka, va, rel_bias)
            st, ob = retention_prompt(qb, kb, vb)
            xp = xp + g * even_out(oa, ga, ob, gb, e_out_w[i])
            kp_l.append(ka)
            vp_l.append(va)
            retp_l.append(st)
            h, g = adaln_in(xs, c_sample, e_norm_w[i], e_ada_w[i], e_ada_b[i])
            qa, ka, va, ga, qb, kb, vb, gb = even_mix(h, pos_s, e_in_w[i], e_q_norm_w[i], e_k_norm_w[i])
            oa = moba_sample(qa, ka, va, cache_k[i], cache_v[i], page_table, rel_bias)
            st, ob = retention_chunk(state_ret[i], qb, kb, vb)
            xs = xs + g * even_out(oa, ga, ob, gb, e_out_w[i])
            ks_l.append(ka)
            vs_l.append(va)
            rets_l.append(st)
        else:
            h, g = adaln_in(xp, c_prompt, o_norm_w[i], o_ada_w[i], o_ada_b[i])
            u, v, gc, zg, xbc, dt = odd_mix(h, o_in_w[i])
            sg = sgu_prompt(v, o_sgu_w[i], o_sgu_b[i])
            conv_out = dwconv(jnp.pad(xbc, ((0, 0), (CONV_W - 1, 0), (0, 0))), o_conv_w[i], o_conv_b[i])
            xh, dth, A, bm, cm = ssd_inputs(conv_out, dt, o_dt_bias[i], o_A_log[i])
            st, y = ssd_prompt(xh, dth, A, bm, cm)
            xp = xp + g * odd_out(u, sg, gc, y, xh, o_D[i], zg, o_ssm_norm_w[i], o_out_w[i])
            ssmp_l.append(st.reshape(bp, H_D, P_D, N_D))
            convp_l.append(xbc[:, -(CONV_W - 1):])
            h, g = adaln_in(xs, c_sample, o_norm_w[i], o_ada_w[i], o_ada_b[i])
            u, v, gc, zg, xbc, dt = odd_mix(h, o_in_w[i])
            sg = sgu_sample(v, o_sgu_w[i], o_sgu_b[i])
            xin = jnp.concatenate([state_conv[i], xbc], axis=1)
            conv_out = dwconv(xin, o_conv_w[i], o_conv_b[i])
            xh, dth, A, bm, cm = ssd_inputs(conv_out, dt, o_dt_bias[i], o_A_log[i])
            st, y = ssd_chunk(state_ssm[i].reshape(bs, G_D, HPG_D, P_D, N_D), xh, dth, A, bm, cm)
            xs = xs + g * odd_out(u, sg, gc, y, xh, o_D[i], zg, o_ssm_norm_w[i], o_out_w[i])
            sguv_l.append(v)
            ssms_l.append(st.reshape(bs, H_D, P_D, N_D))
            convs_l.append(xin[:, -(CONV_W - 1):])
    k_prompt = jnp.stack(kp_l)
    v_prompt = jnp.stack(vp_l)
    k_sample = jnp.stack(ks_l)
    v_sample = jnp.stack(vs_l)
    ret_state_prompt = jnp.stack(retp_l)
    ret_state_sample = jnp.stack(rets_l)
    sgu_v_sample = jnp.stack(sguv_l)
    ssm_state_prompt = jnp.stack(ssmp_l)
    ssm_state_sample = jnp.stack(ssms_l)
    conv_state_prompt = jnp.stack(convp_l)
    conv_state_sample = jnp.stack(convs_l)
    return (xp, xs, k_prompt, v_prompt, k_sample, v_sample, ret_state_prompt, ret_state_sample,
            sgu_v_sample, ssm_state_prompt, ssm_state_sample, conv_state_prompt, conv_state_sample)
```

```python
import functools
import math

import numpy as np
import jax
import jax.numpy as jnp
from jax import lax
from jax.experimental import pallas as pl
from jax.experimental.pallas import tpu as pltpu

F32 = jnp.float32
BF16 = jnp.bfloat16

D_MODEL = 1024
PAST_LEN = 2048
PAGE_SIZE = 128
H_A, HD_A, W_A = 8, 64, 512
MOBA_BLOCK = 256
MOBA_TOPK = 3
NUM_BUCKETS = 32
MAX_DISTANCE = 128
H_B, DK_B, DV_B, W_B = 4, 64, 128, 512
G_C, W_C = 4, 512
H_D, P_D, N_D, G_D, W_D = 8, 64, 128, 2, 512
CONV_W = 4
CONV_DIM = 1024
CHUNK = 128
NEG_INF = -1e30
EPS = 1e-6
VMEM_LIMIT = 56 * 1024 * 1024


def _bf(x):
    return x.astype(BF16)


def _dg(a, b, ca, cb):
    return lax.dot_general(a, b, (((ca,), (cb,)), ((), ())), preferred_element_type=F32)


def _mm(a, b):
    return _dg(_bf(a), _bf(b), 1, 0)


def _mm_nt(a, b):
    return _dg(_bf(a), _bf(b), 1, 1)


def _mm_tn(a, b):
    return _dg(_bf(a), _bf(b), 0, 0)


def _split2(x):
    hi = _bf(x)
    return hi, _bf(x - hi.astype(F32))


def _split3(x):
    hi = _bf(x)
    r = x - hi.astype(F32)
    mid = _bf(r)
    return hi, mid, _bf(r - mid.astype(F32))


def _mm_hp(a, b, ca, cb):
    ah, al = _split2(a)
    bh, bl = _split2(b)
    return _dg(ah, bh, ca, cb) + (_dg(ah, bl, ca, cb) + _dg(al, bh, ca, cb))


def _mm_exact_lhs(e, x, ca, cb):
    h, m, l = _split3(x)
    return _dg(e, h, ca, cb) + (_dg(e, m, ca, cb) + _dg(e, l, ca, cb))


def _silu(x):
    return x * (1.0 / (1.0 + jnp.exp(-x)))


def _gelu_tanh(x):
    return 0.5 * x * (1.0 + jnp.tanh(math.sqrt(2.0 / math.pi) * (x + 0.044715 * (x * x * x))))


def _softplus(x):
    return jnp.maximum(x, 0.0) + jnp.log1p(jnp.exp(-jnp.abs(x)))


def _cparams(n_grid):
    return pltpu.CompilerParams(dimension_semantics=("arbitrary",) * n_grid,
                                vmem_limit_bytes=VMEM_LIMIT)


def _top3_rows(g, blk, nblk):
    sel = jnp.zeros(g.shape, jnp.bool_)
    for _ in range(MOBA_TOPK):
        m = jnp.max(g, axis=0, keepdims=True)
        idx = jnp.min(jnp.where(g == m, blk, nblk), axis=0, keepdims=True)
        pick = blk == idx
        sel = jnp.logical_or(sel, pick)
        g = jnp.where(pick, -jnp.inf, g)
    return sel


def _ada_kernel(c_ref, w_ref, b_ref, o_ref):
    s = _silu(c_ref[...])
    o_ref[...] = _mm_hp(s, w_ref[...], 1, 0) + b_ref[...]


def _ada_mod(c_all, w, b):
    m, d = c_all.shape
    n = w.shape[1]
    tn = 512
    return pl.pallas_call(
        _ada_kernel,
        grid=(n // tn,),
        in_specs=[pl.BlockSpec((m, d), lambda j: (0, 0)),
                  pl.BlockSpec((d, tn), lambda j: (0, j)),
                  pl.BlockSpec((1, tn), lambda j: (0, j))],
        out_specs=pl.BlockSpec((m, tn), lambda j: (0, j)),
        out_shape=jax.ShapeDtypeStruct((m, n), F32),
        compiler_params=_cparams(1),
        name="ada_mod",
    )(c_all, w, b.reshape(1, n))


def _t5_bucket_np(rel):
    n = np.maximum(rel, 0)
    max_exact = NUM_BUCKETS // 2
    nf = np.maximum(n, 1).astype(np.float64)
    large = max_exact + (np.log(nf / max_exact) / math.log(MAX_DISTANCE / max_exact)
                         * (NUM_BUCKETS - max_exact)).astype(np.int64)
    large = np.minimum(large, NUM_BUCKETS - 1)
    return np.where(n < max_exact, n, large).astype(np.int32)


def _bias_kernel(tab_ref, idx_ref, o_ref):
    h = pl.program_id(0)
    idx = idx_ref[...]
    acc = jnp.where(idx == -1, NEG_INF, 0.0).astype(F32)
    for b in range(NUM_BUCKETS):
        acc = jnp.where(idx == b, tab_ref[b, h], acc)
    o_ref[0] = acc


def _bias_tables(rel_bias, idx):
    r, c = idx.shape
    return pl.pallas_call(
        _bias_kernel,
        grid=(H_A,),
        in_specs=[pl.BlockSpec(memory_space=pltpu.SMEM),
                  pl.BlockSpec((r, c), lambda h: (0, 0))],
        out_specs=pl.BlockSpec((1, r, c), lambda h: (h, 0, 0)),
        out_shape=jax.ShapeDtypeStruct((H_A, r, c), F32),
        compiler_params=_cparams(1),
        name="t5_bias",
    )(rel_bias, jnp.asarray(idx))


def _even_in_kernel(x_ref, sc_ref, sh_ref, nw_ref, w_ref, qnw_ref, knw_ref, seg_ref, cos_ref, sin_ref,
                    qa_ref, ka_ref, va_ref, kbf_ref, vt_ref, ga_ref, qb_ref, kb_ref, vb_ref, gb_ref, km_ref):
    x = x_ref[0]
    ms = jnp.mean(x * x, axis=-1, keepdims=True)
    h = (x * lax.rsqrt(ms + EPS) * nw_ref[...]) * (1.0 + sc_ref[0]) + sh_ref[0]
    hb = _bf(h)

    def proj(lo, hi):
        return jnp.dot(hb, w_ref[:, lo:hi], preferred_element_type=F32)

    def head_rms(t, w_row):
        ss = jnp.dot(_bf(t * t), seg_ref[...], preferred_element_type=F32)
        return t * lax.rsqrt(ss * (1.0 / HD_A) + EPS) * w_row

    qa_ref[0] = head_rms(proj(0, 512), qnw_ref[...])
    ka = head_rms(proj(512, 1024), knw_ref[...])
    ka_ref[0] = ka
    kbf_ref[0] = _bf(ka)
    km_ref[0, 0] = jnp.mean(ka, axis=0, keepdims=True)
    va = proj(1024, 1536)
    va_ref[0] = va
    vt_ref[0] = _bf(va.T)
    ga_ref[0] = proj(1536, 2048)

    lane = lax.broadcasted_iota(jnp.int32, (1, 256), 1) % DK_B
    first_half = lane < (DK_B // 2)
    cos = cos_ref[...]
    sin = sin_ref[...]

    def rotary(t):
        up = pltpu.roll(t, 256 - DK_B // 2, 1)
        dn = pltpu.roll(t, DK_B // 2, 1)
        return t * cos + jnp.where(first_half, up, dn) * sin

    qb_ref[0] = rotary(proj(2048, 2304))
    kb_ref[0] = rotary(proj(2304, 2560)) * (DK_B ** -0.5)
    vb_ref[0] = proj(2560, 3072)
    gb_ref[0] = proj(3072, 3584)


def _even_in(x, scale, shift, norm_w, w_bf, qnw, knw, seg, cos, sin, per_row_mod):
    nb, s, d = x.shape
    tm = MOBA_BLOCK
    ns = s // tm
    if per_row_mod:
        mod_spec = pl.BlockSpec((1, tm, d), lambda b, i: (b, i, 0))
    else:
        mod_spec = pl.BlockSpec((1, 1, d), lambda b, i: (b, 0, 0))
    row = lambda c: pl.BlockSpec((1, tm, c), lambda b, i: (b, i, 0))
    const = lambda shp: pl.BlockSpec(shp, lambda b, i: (0,) * len(shp))
    out_shape = (
        jax.ShapeDtypeStruct((nb, s, 512), F32),
        jax.ShapeDtypeStruct((nb, s, 512), F32),
        jax.ShapeDtypeStruct((nb, s, 512), F32),
        jax.ShapeDtypeStruct((nb, s, 512), BF16),
        jax.ShapeDtypeStruct((nb, 512, s), BF16),
        jax.ShapeDtypeStruct((nb, s, 512), F32),
        jax.ShapeDtypeStruct((nb, s, 256), F32),
        jax.ShapeDtypeStruct((nb, s, 256), F32),
        jax.ShapeDtypeStruct((nb, s, 512), F32),
        jax.ShapeDtypeStruct((nb, s, 512), F32),
        jax.ShapeDtypeStruct((nb, ns, 1, 512), F32),
    )
    out_specs = (row(512), row(512), row(512), row(512),
                 pl.BlockSpec((1, 512, tm), lambda b, i: (b, 0, i)),
                 row(512), row(256), row(256), row(512), row(512),
                 pl.BlockSpec((1, 1, 1, 512), lambda b, i: (b, i, 0, 0)))
    return pl.pallas_call(
        _even_in_kernel,
        grid=(nb, ns),
        in_specs=[row(d), mod_spec, mod_spec, const((1, d)), const((d, 3584)),
                  const((1, 512)), const((1, 512)), const((512, 512)),
                  pl.BlockSpec((tm, 256), lambda b, i: (i, 0)),
                  pl.BlockSpec((tm, 256), lambda b, i: (i, 0))],
        out_specs=out_specs,
        out_shape=out_shape,
        compiler_params=_cparams(2),
        name="even_in",
    )(x, scale, shift, norm_w, w_bf, qnw, knw, seg, cos, sin)


def _moba_p_kernel(tab_ref, q_ref, k_ref, vt_ref, km_ref, bias_ref, ga_ref, o_ref, rb_ref):
    hp = pl.program_id(1)
    qi = pl.program_id(2)
    nblk = km_ref.shape[1]
    q = q_ref[0]
    km = km_ref[0]
    lane = lax.broadcasted_iota(jnp.int32, (1, 128), 1)
    blk = lax.broadcasted_iota(jnp.int32, (nblk, MOBA_BLOCK), 0)
    outs = []
    for hh in range(2):
        qm = jnp.where((lane // HD_A) == hh, q, 0.0)
        gate = _mm_hp(km, qm, 1, 1)
        gate = jnp.where(blk < qi, gate, NEG_INF)
        sel = jnp.logical_and(_top3_rows(gate, blk, nblk), blk < qi)
        far_c = tab_ref[NUM_BUCKETS - 1, 2 * hp + hh]
        rb = jnp.where(blk < qi - 1, jnp.where(sel, far_c, NEG_INF),
                       jnp.where(blk == qi - 1, jnp.where(sel, 0.0, NEG_INF), 0.0))
        rb_ref[hh] = rb
        qs = _bf(qm * (HD_A ** -0.5))

        def body(j, carry, hh=hh, qs=qs):
            m, l, acc = carry
            off = pl.multiple_of(j * MOBA_BLOCK, MOBA_BLOCK)
            kj = k_ref[0, pl.ds(off, MOBA_BLOCK), :]
            s = _dg(kj, qs, 1, 1)
            cls = jnp.minimum(qi - j, 2)
            s = s + bias_ref[hh, cls] + rb_ref[hh, pl.ds(j, 1), :]
            mn = jnp.maximum(m, jnp.max(s, axis=0, keepdims=True))
            alpha = jnp.exp(m - mn)
            p = jnp.exp(s - mn)
            l = alpha * l + jnp.sum(p, axis=0, keepdims=True)
            vj = vt_ref[0, hh * HD_A:(hh + 1) * HD_A, pl.ds(off, MOBA_BLOCK)]
            acc = alpha * acc + jnp.dot(vj, _bf(p), preferred_element_type=F32)
            return mn, l, acc

        init = (jnp.full((1, MOBA_BLOCK), -jnp.inf, F32), jnp.zeros((1, MOBA_BLOCK), F32),
                jnp.zeros((HD_A, MOBA_BLOCK), F32))
        _, l, acc = lax.fori_loop(0, qi + 1, body, init)
        outs.append(acc * (1.0 / l))
    o = jnp.concatenate(outs, axis=0).T
    o_ref[0] = o * _silu(ga_ref[0])


def _moba_prompt(rel_bias, qa, kbf, vt, kmean, bias_t, ga):
    nb, s, _ = qa.shape
    nq = s // MOBA_BLOCK
    tile = pl.BlockSpec((1, MOBA_BLOCK, 128), lambda b, hp, i: (b, i, hp))
    return pl.pallas_call(
        _moba_p_kernel,
        grid=(nb, H_A // 2, nq),
        in_specs=[pl.BlockSpec(memory_space=pltpu.SMEM),
                  tile,
                  pl.BlockSpec((1, s, 128), lambda b, hp, i: (b, 0, hp)),
                  pl.BlockSpec((1, 128, s), lambda b, hp, i: (b, hp, 0)),
                  pl.BlockSpec((1, nq, 128), lambda b, hp, i: (b, 0, hp)),
                  pl.BlockSpec((2, 3, MOBA_BLOCK, MOBA_BLOCK), lambda b, hp, i: (hp, 0, 0, 0)),
                  tile],
        out_specs=tile,
        out_shape=jax.ShapeDtypeStruct((nb, s, 512), F32),
        scratch_shapes=[pltpu.VMEM((2, nq, MOBA_BLOCK), F32)],
        compiler_params=_cparams(3),
        name="moba_prompt",
    )(rel_bias, qa, kbf, vt, kmean, bias_t, ga)


N_PAST_BLK = PAST_LEN // MOBA_BLOCK
N_PAGES = PAST_LEN // PAGE_SIZE
KV_ROWS = PAST_LEN + 128
Q_COLS = 128


def _moba_s_kernel(pt_ref, qrep_ref, knew_ref, vnew_ref, ga_ref, bnear_ref, bfar_ref, ck_hbm, cv_hbm,
                   o_ref, kbuf, vbuf, qpad, km_ref, s_ref, sems):
    b = pl.program_id(0)
    nseq = pl.num_programs(0)
    slot = b % 2
    t_new = knew_ref.shape[1]

    def page_copies(seq, sl):
        cps = []
        for p in range(N_PAGES):
            dst = pl.ds(p * PAGE_SIZE, PAGE_SIZE)
            cps.append(pltpu.make_async_copy(ck_hbm.at[pt_ref[seq, p]], kbuf.at[sl, dst], sems.at[0, sl]))
            cps.append(pltpu.make_async_copy(cv_hbm.at[pt_ref[seq, p]], vbuf.at[sl, dst], sems.at[1, sl]))
        return cps

    @pl.when(b == 0)
    def _():
        pad = jnp.zeros((KV_ROWS - PAST_LEN, W_A), F32)
        for sl in range(2):
            kbuf[sl, pl.ds(PAST_LEN, KV_ROWS - PAST_LEN), :] = pad
            vbuf[sl, pl.ds(PAST_LEN, KV_ROWS - PAST_LEN), :] = pad
        qpad[...] = jnp.zeros(qpad.shape, F32)
        for cp in page_copies(0, 0):
            cp.start()

    @pl.when(b + 1 < nseq)
    def _():
        for cp in page_copies(b + 1, 1 - slot):
            cp.start()

    n_rep = qrep_ref.shape[1]
    rowh = lax.broadcasted_iota(jnp.int32, (n_rep, W_A), 0) % H_A
    laneh = lax.broadcasted_iota(jnp.int32, (n_rep, W_A), 1) // HD_A
    qpad[0:n_rep, :] = jnp.where(rowh == laneh, qrep_ref[0], 0.0)
    kbuf[slot, pl.ds(PAST_LEN, t_new), :] = knew_ref[0]
    vbuf[slot, pl.ds(PAST_LEN, t_new), :] = vnew_ref[0]

    for cp in page_copies(b, slot):
        cp.wait()

    for n in range(N_PAST_BLK):
        km_ref[n:n + 1, :] = jnp.mean(kbuf[slot, pl.ds(n * MOBA_BLOCK, MOBA_BLOCK), :], axis=0, keepdims=True)
    qp = qpad[...]
    gate = _mm_hp(km_ref[...], qp, 1, 1)
    blk = lax.broadcasted_iota(jnp.int32, (N_PAST_BLK, Q_COLS), 0)
    sel = _top3_rows(gate, blk, N_PAST_BLK)
    rb = jnp.where(sel, jnp.where(blk == N_PAST_BLK - 1, 0.0, bfar_ref[...]), NEG_INF)

    qs = _bf(qp * (HD_A ** -0.5))
    m = jnp.full((1, Q_COLS), -jnp.inf, F32)
    for n in range(N_PAST_BLK + 1):
        rows = MOBA_BLOCK if n < N_PAST_BLK else KV_ROWS - PAST_LEN
        kn = _bf(kbuf[slot, pl.ds(n * MOBA_BLOCK, rows), :])
        s = _dg(kn, qs, 1, 1)
        if n < N_PAST_BLK - 1:
            s = s + rb[n:n + 1, :]
        elif n == N_PAST_BLK - 1:
            s = s + rb[n:n + 1, :] + bnear_ref[0:MOBA_BLOCK, :]
        else:
            s = s + bnear_ref[MOBA_BLOCK:MOBA_BLOCK + rows, :]
        s_ref[pl.ds(n * MOBA_BLOCK, rows), :] = s
        m = jnp.maximum(m, jnp.max(s, axis=0, keepdims=True))

    acc = jnp.zeros((Q_COLS, W_A), F32)
    l = jnp.zeros((Q_COLS, 1), F32)
    for n in range(N_PAST_BLK + 1):
        rows = MOBA_BLOCK if n < N_PAST_BLK else KV_ROWS - PAST_LEN
        p = jnp.exp(s_ref[pl.ds(n * MOBA_BLOCK, rows), :] - m)
        pt = p.T
        l = l + jnp.sum(pt, axis=1, keepdims=True)
        vn = _bf(vbuf[slot, pl.ds(n * MOBA_BLOCK, rows), :])
        acc = acc + jnp.dot(_bf(pt), vn, preferred_element_type=F32)
    o = (acc * (1.0 / l))[0:n_rep, :]
    o = jnp.where(rowh == laneh, o, 0.0)
    o = jnp.sum(o.reshape(t_new, H_A, W_A), axis=1)
    o_ref[0] = o * _silu(ga_ref[0])


def _moba_sample(page_table, qrep, knew, vnew, ga, bnear, bfar, cache_k, cache_v):
    nseq, t_new, _ = knew.shape
    tok = pl.BlockSpec((1, t_new, W_A), lambda b, pt: (b, 0, 0))
    grid_spec = pltpu.PrefetchScalarGridSpec(
        num_scalar_prefetch=1,
        grid=(nseq,),
        in_specs=[pl.BlockSpec((1, qrep.shape[1], W_A), lambda b, pt: (b, 0, 0)),
                  tok, tok, tok,
                  pl.BlockSpec(bnear.shape, lambda b, pt: (0, 0)),
                  pl.BlockSpec((1, Q_COLS), lambda b, pt: (0, 0)),
                  pl.BlockSpec(memory_space=pl.ANY),
                  pl.BlockSpec(memory_space=pl.ANY)],
        out_specs=tok,
        scratch_shapes=[pltpu.VMEM((2, KV_ROWS, W_A), F32),
                        pltpu.VMEM((2, KV_ROWS, W_A), F32),
                        pltpu.VMEM((Q_COLS, W_A), F32),
                        pltpu.VMEM((N_PAST_BLK, W_A), F32),
                        pltpu.VMEM((KV_ROWS, Q_COLS), F32),
                        pltpu.SemaphoreType.DMA((2, 2))],
    )
    return pl.pallas_call(
        _moba_s_kernel,
        grid_spec=grid_spec,
        out_shape=jax.ShapeDtypeStruct((nseq, t_new, W_A), F32),
        compiler_params=_cparams(1),
        name="moba_sample",
    )(page_table, qrep, knew, vnew, ga, bnear, bfar, cache_k, cache_v)


def _pad_rows(ref, scratch, t):
    if t == CHUNK:
        return ref[0]
    scratch[...] = jnp.zeros(scratch.shape, scratch.dtype)
    scratch[0:t, :] = ref[0]
    return scratch[...]


def _ret_kernel(q_ref, k_ref, v_ref, g_ref, st0_ref, dmat_ref, qdec_ref, kdec_ref, gl_ref,
                o_ref, st_ref, qpad, kpad, vpad, gpad, *, t, lq):
    c = pl.program_id(1)

    @pl.when(c == 0)
    def _():
        st_ref[0] = st0_ref[0]

    q = _pad_rows(q_ref, qpad, t)[0:lq]
    k = _pad_rows(k_ref, kpad, t)
    v = _pad_rows(v_ref, vpad, t)
    g = _pad_rows(g_ref, gpad, t)[0:lq]
    lane = lax.broadcasted_iota(jnp.int32, (1, 128), 1)
    rowsel = lax.broadcasted_iota(jnp.int32, (128, 1), 0) < DK_B
    outs = []
    for hp in range(H_B // 2):
        cols = slice(hp * 128, (hp + 1) * 128)
        kp = k[:, cols]
        qp = q[:, cols]
        st = st_ref[0, cols, :]
        kd = kp * kdec_ref[:, cols]
        upd = []
        for hh in range(2):
            h = 2 * hp + hh
            qm = jnp.where((lane // DK_B) == hh, qp, 0.0)
            sc = _mm_nt(qm, kp) * dmat_ref[h, 0:lq, :]
            vh = v[:, h * DV_B:(h + 1) * DV_B]
            o = _mm(sc, vh) + _mm(qm * qdec_ref[0:lq, cols], st)
            o = o * lax.rsqrt(jnp.mean(o * o, axis=-1, keepdims=True) + EPS)
            outs.append(o * _silu(g[:, h * DV_B:(h + 1) * DV_B]))
            upd.append(_mm_tn(kd, vh))
        st_ref[0, cols, :] = st * gl_ref[cols, :] + jnp.where(rowsel, upd[0], upd[1])
    o_ref[0] = jnp.concatenate(outs, axis=1)[0:t]


def _retention(q, k, v, g, st0, dmat, qdec, kdec, gl, lq):
    nb, nc, t, _ = q.shape
    row = lambda c_: pl.BlockSpec((None, 1, t, c_), lambda b, c: (b, c, 0, 0))
    const = lambda a: pl.BlockSpec(a.shape, lambda b, c: (0,) * a.ndim)
    st_spec = pl.BlockSpec((1, H_B * DK_B, DV_B), lambda b, c: (b, 0, 0))
    return pl.pallas_call(
        functools.partial(_ret_kernel, t=t, lq=lq),
        grid=(nb, nc),
        in_specs=[row(256), row(256), row(512), row(512), st_spec,
                  const(dmat), const(qdec), const(kdec), const(gl)],
        out_specs=(row(512), st_spec),
        out_shape=(jax.ShapeDtypeStruct((nb, nc, t, 512), F32),
                   jax.ShapeDtypeStruct((nb, H_B * DK_B, DV_B), F32)),
        scratch_shapes=[pltpu.VMEM((CHUNK, 256), F32), pltpu.VMEM((CHUNK, 256), F32),
                        pltpu.VMEM((CHUNK, 512), F32), pltpu.VMEM((CHUNK, 512), F32)],
        compiler_params=_cparams(2),
        name="retention",
    )(q, k, v, g, st0, dmat, qdec, kdec, gl)


def _out_kernel(a_ref, b_ref, x_ref, g_ref, w_ref, o_ref):
    half = w_ref.shape[0] // 2
    y = (jnp.dot(_bf(a_ref[0]), w_ref[0:half, :], preferred_element_type=F32)
         + jnp.dot(_bf(b_ref[0]), w_ref[half:, :], preferred_element_type=F32))
    o_ref[0] = x_ref[0] + g_ref[0] * y


def _out_proj(a, b, x, gate, w_bf, per_row_mod):
    nb, s, d = x.shape
    tm = 512
    if per_row_mod:
        g_spec = pl.BlockSpec((1, tm, d), lambda bb, i: (bb, i, 0))
    else:
        g_spec = pl.BlockSpec((1, 1, d), lambda bb, i: (bb, 0, 0))
    row = lambda c: pl.BlockSpec((1, tm, c), lambda bb, i: (bb, i, 0))
    return pl.pallas_call(
        _out_kernel,
        grid=(nb, s // tm),
        in_specs=[row(512), row(512), row(d), g_spec, pl.BlockSpec(w_bf.shape, lambda bb, i: (0, 0))],
        out_specs=row(d),
        out_shape=jax.ShapeDtypeStruct((nb, s, d), F32),
        compiler_params=_cparams(2),
        name="out_proj",
    )(a, b, x, gate, w_bf)


def _odd_in_kernel(x_ref, sc_ref, sh_ref, nw_ref, w_ref, sguw_ref, sgub_ref, dtb_ref,
                   oc_ref, zg_ref, xbc_ref, dt_ref, v_ref):
    x = x_ref[0]
    tm = x.shape[0]
    ms = jnp.mean(x * x, axis=-1, keepdims=True)
    h = (x * lax.rsqrt(ms + EPS) * nw_ref[...]) * (1.0 + sc_ref[0]) + sh_ref[0]
    hb = _bf(h)

    def proj(lo, hi):
        return jnp.dot(hb, w_ref[:, lo:hi], preferred_element_type=F32)

    u = _gelu_tanh(proj(0, 512))
    v = _gelu_tanh(proj(512, 1024))
    mu = jnp.mean(v, axis=-1, keepdims=True)
    vc = v - mu
    v = vc * lax.rsqrt(jnp.mean(vc * vc, axis=-1, keepdims=True) + EPS)
    v_ref[0] = v
    ii = lax.broadcasted_iota(jnp.int32, (CHUNK, CHUNK), 0)
    jj = lax.broadcasted_iota(jnp.int32, (CHUNK, CHUNK), 1)
    rows = []
    for ci in range(tm // CHUNK):
        cols = []
        for g in range(G_C):
            wg = jnp.where(ii >= jj, sguw_ref[g], 0.0)
            cols.append(_mm(wg, v[ci * CHUNK:(ci + 1) * CHUNK, g * 128:(g + 1) * 128]))
        rows.append(jnp.concatenate(cols, axis=1) + sgub_ref[...])
    sg = jnp.concatenate(rows, axis=0) if len(rows) > 1 else rows[0]
    oc_ref[0] = u * sg * _silu(proj(1024, 1536))
    zg_ref[0] = proj(1536, 2048)
    xbc_ref[0] = proj(2048, 3072)
    dt_ref[0] = _softplus(proj(3072, 3584) + dtb_ref[...])


def _odd_in(x, scale, shift, norm_w, w_bf, sgu_w, sgu_b_tab, dt_bias, per_row_mod):
    nb, s, d = x.shape
    tm = 256
    if per_row_mod:
        mod_spec = pl.BlockSpec((1, tm, d), lambda b, i: (b, i, 0))
    else:
        mod_spec = pl.BlockSpec((1, 1, d), lambda b, i: (b, 0, 0))
    row = lambda c: pl.BlockSpec((1, tm, c), lambda b, i: (b, i, 0))
    const = lambda shp: pl.BlockSpec(shp, lambda b, i: (0,) * len(shp))
    return pl.pallas_call(
        _odd_in_kernel,
        grid=(nb, s // tm),
        in_specs=[row(d), mod_spec, mod_spec, const((1, d)), const(w_bf.shape),
                  const(sgu_w.shape), const(sgu_b_tab.shape), const((1, 512))],
        out_specs=(row(512), row(512), row(1024), row(512), row(512)),
        out_shape=(jax.ShapeDtypeStruct((nb, s, 512), F32),
                   jax.ShapeDtypeStruct((nb, s, 512), F32),
                   jax.ShapeDtypeStruct((nb, s, 1024), F32),
                   jax.ShapeDtypeStruct((nb, s, 512), F32),
                   jax.ShapeDtypeStruct((nb, s, 512), F32)),
        compiler_params=_cparams(2),
        name="odd_in",
    )(x, scale, shift, norm_w, w_bf, sgu_w, sgu_b_tab, dt_bias)


def _ssd_kernel(xbc_ref, dt_ref, zg_ref, tail_ref, st0_ref, cw_ref, cb_ref, alog_ref, dsk_ref, nw_ref,
                tri_ref, sel_ref, y_ref, st_ref, ext, dtpad, zpad, *, t, lq, nc):
    c = pl.program_id(1)

    @pl.when(c == 0)
    def _():
        st_ref[0] = st0_ref[0]
        ext[...] = jnp.zeros(ext.shape, F32)
        ext[0:8, :] = tail_ref[0]

    ext[8:8 + t, :] = xbc_ref[0]
    conv = cb_ref[...]
    for w in range(CONV_W):
        conv = conv + ext[pl.ds(8 - (CONV_W - 1) + w, CHUNK), :] * cw_ref[w:w + 1, :]
    if nc > 1:
        ext[0:8, :] = ext[CHUNK:CHUNK + 8, :]
    xc = _silu(conv)
    xh = xc[:, 0:W_D]
    dt = _pad_rows(dt_ref, dtpad, t)
    zg = _pad_rows(zg_ref, zpad, t)[0:lq]
    a = dt * (-jnp.exp(alog_ref[...]))
    cum = _mm_exact_lhs(tri_ref[...], a, 1, 0)
    last = cum[CHUNK - 1:CHUNK, :]
    dtx = xh * dt
    xw = xh * (jnp.exp(last - cum) * dt)
    ecum = jnp.exp(cum)
    elast = jnp.exp(last)
    cum_rows = _mm_exact_lhs(sel_ref[...], cum, 1, 1)
    ii = lax.broadcasted_iota(jnp.int32, (lq, CHUNK), 0)
    jj = lax.broadcasted_iota(jnp.int32, (lq, CHUNK), 1)
    lane = lax.broadcasted_iota(jnp.int32, (1, 128), 1)
    hpg = H_D // G_D
    ys = []
    for g in range(G_D):
        bg = xc[:, W_D + g * N_D:W_D + (g + 1) * N_D]
        cg = xc[0:lq, W_D + G_D * N_D + g * N_D:W_D + G_D * N_D + (g + 1) * N_D]
        cb = _mm_nt(cg, bg)
        gr = slice(g * hpg * P_D, (g + 1) * hpg * P_D)
        stg = st_ref[0, gr, :]
        yoff = _mm_nt(cg, stg)
        for pr in range(hpg // 2):
            l0 = g * hpg * P_D + pr * 128
            dtxp = dtx[:, l0:l0 + 128]
            yh = []
            for hh in range(2):
                h = g * hpg + pr * 2 + hh
                col = jnp.broadcast_to(cum[0:lq, h * P_D:h * P_D + 1], (lq, CHUNK))
                seg = jnp.minimum(col - cum_rows[h:h + 1, :], 0.0)
                mh = jnp.where(ii >= jj, cb * jnp.exp(seg), 0.0)
                yh.append(_mm(mh, dtxp))
            ypair = jnp.where(lane < P_D, yh[0], yh[1])
            ys.append(ypair + yoff[:, pr * 128:(pr + 1) * 128] * ecum[0:lq, l0:l0 + 128])
        upd = _mm_tn(xw[:, gr], bg)
        for hl in range(hpg):
            h = g * hpg + hl
            r = slice(h * P_D, (h + 1) * P_D)
            dec = jnp.broadcast_to(elast[0:1, h * P_D:h * P_D + 1], (P_D, N_D))
            st_ref[0, r, :] = st_ref[0, r, :] * dec + upd[hl * P_D:(hl + 1) * P_D, :]
    y = jnp.concatenate(ys, axis=1)
    y = (y + xh[0:lq] * dsk_ref[...]) * _silu(zg)
    gw = W_D // G_D
    outs = []
    for g in range(G_D):
        yg = y[:, g * gw:(g + 1) * gw]
        outs.append(yg * lax.rsqrt(jnp.mean(yg * yg, axis=-1, keepdims=True) + EPS))
    y_ref[0] = (jnp.concatenate(outs, axis=1) * nw_ref[...])[0:t]


def _ssd(xbc, dt, zg, tail, st0, conv_w, conv_b, a_log, d_skip, norm_w, tri, sel, lq):
    nb, nc, t, _ = xbc.shape
    row = lambda c_: pl.BlockSpec((None, 1, t, c_), lambda b, c: (b, c, 0, 0))
    const = lambda a: pl.BlockSpec(a.shape, lambda b, c: (0,) * a.ndim)
    st_spec = pl.BlockSpec((1, H_D * P_D, N_D), lambda b, c: (b, 0, 0))
    return pl.pallas_call(
        functools.partial(_ssd_kernel, t=t, lq=lq, nc=nc),
        grid=(nb, nc),
        in_specs=[row(1024), row(512), row(512),
                  pl.BlockSpec((1, 8, CONV_DIM), lambda b, c: (b, 0, 0)), st_spec,
                  const(conv_w), const(conv_b), const(a_log), const(d_skip), const(norm_w),
                  const(tri), const(sel)],
        out_specs=(row(512), st_spec),
        out_shape=(jax.ShapeDtypeStruct((nb, nc, t, 512), F32),
                   jax.ShapeDtypeStruct((nb, H_D * P_D, N_D), F32)),
        scratch_shapes=[pltpu.VMEM((CHUNK + 8, CONV_DIM), F32),
                        pltpu.VMEM((CHUNK, 512), F32), pltpu.VMEM((CHUNK, 512), F32)],
        compiler_params=_cparams(2),
        name="ssd",
    )(xbc, dt, zg, tail, st0, conv_w, conv_b, a_log, d_skip, norm_w, tri, sel)


def _rotary_tables(pos):
    half = DK_B // 2
    inv = 1.0 / (10000.0 ** (jnp.arange(half, dtype=F32) / half))
    ang = pos.astype(F32)[:, None] * inv[None, :]
    cos, sin = jnp.cos(ang), jnp.sin(ang)
    cos_t = jnp.tile(jnp.concatenate([cos, cos], axis=1), (1, H_B))
    sin_t = jnp.tile(jnp.concatenate([-sin, sin], axis=1), (1, H_B))
    return cos_t, sin_t


def _retention_tables(chunk_len):
    log_g = np.log(1.0 - 2.0 ** (-5.0 - np.arange(H_B, dtype=np.float64)))
    idx = np.arange(CHUNK, dtype=np.float64)
    diff = idx[:, None] - idx[None, :]
    dmat = np.where(diff[None] >= 0, np.exp(np.maximum(diff, 0.0)[None] * log_g[:, None, None]), 0.0)
    qdec = np.exp((idx + 1.0)[:, None] * log_g[None, :])
    kdec = np.where(idx[:, None] < chunk_len, np.exp((chunk_len - 1.0 - idx)[:, None] * log_g[None, :]), 0.0)
    gl = np.exp(chunk_len * log_g)
    return (jnp.asarray(dmat, F32),
            jnp.asarray(np.repeat(qdec, DK_B, axis=1), F32),
            jnp.asarray(np.repeat(kdec, DK_B, axis=1), F32),
            jnp.asarray(np.repeat(np.repeat(gl, DK_B)[:, None], DV_B, axis=1), F32))


def _prompt_bias_idx():
    kk = np.arange(MOBA_BLOCK)[:, None]
    qq = np.arange(MOBA_BLOCK)[None, :]
    diag = np.where(qq >= kk, _t5_bucket_np(qq - kk), -1)
    sub = _t5_bucket_np(qq + MOBA_BLOCK - kk)
    zero = np.full((MOBA_BLOCK, MOBA_BLOCK), -2)
    return np.concatenate([diag, sub, zero], axis=0).astype(np.int32)


def _sample_bias_idx(t_new):
    rows = np.arange(MOBA_BLOCK + KV_ROWS - PAST_LEN)[:, None]
    col_t = (np.arange(Q_COLS) // H_A)[None, :]
    kpos = PAST_LEN - MOBA_BLOCK + rows
    qpos = PAST_LEN + col_t
    ok = (kpos <= qpos) & (kpos < PAST_LEN + t_new) & (col_t < t_new)
    return np.where(ok, _t5_bucket_np(qpos - kpos), -1).astype(np.int32)


def kernel(x_prompt, x_sample, cache_k, cache_v, state_ret, state_ssm, state_conv, page_table, c_prompt, c_sample, rel_bias, e_norm_w, e_ada_w, e_ada_b, e_in_w, e_q_norm_w, e_k_norm_w, e_out_w, o_norm_w, o_ada_w, o_ada_b, o_in_w, o_sgu_w, o_sgu_b, o_conv_w, o_conv_b, o_dt_bias, o_A_log, o_D, o_ssm_norm_w, o_out_w):
    bp, s_len, d = x_prompt.shape
    bs, t_len, _ = x_sample.shape
    n_s = bs * t_len

    c_all = jnp.concatenate([c_prompt, c_sample, jnp.zeros((8 - (bp + bs) % 8, d), F32)], axis=0)
    mods = []
    for ada_w, ada_b in ((e_ada_w[0], e_ada_b[0]), (o_ada_w[0], o_ada_b[0])):
        mod = _ada_mod(c_all, ada_w, ada_b)
        parts_p = [mod[:bp, i * d:(i + 1) * d].reshape(bp, 1, d) for i in range(3)]
        parts_s = [jnp.repeat(mod[bp:bp + bs, i * d:(i + 1) * d], t_len, axis=0).reshape(1, n_s, d) for i in range(3)]
        mods.append((parts_p, parts_s))
    (e_mod_p, e_mod_s), (o_mod_p, o_mod_s) = mods

    seg = jnp.asarray(np.kron(np.eye(H_A), np.ones((HD_A, HD_A))), BF16)
    qnw = jnp.tile(e_q_norm_w[0], H_A).reshape(1, W_A)
    knw = jnp.tile(e_k_norm_w[0], H_A).reshape(1, W_A)
    e_in_bf = _bf(e_in_w[0])
    e_out_bf = _bf(e_out_w[0])
    o_in_bf = _bf(jnp.concatenate([o_in_w[0][:, :3072], jnp.repeat(o_in_w[0][:, 3072:], P_D, axis=1)], axis=1))
    o_out_bf = _bf(o_out_w[0])
    x_s = x_sample.reshape(1, n_s, d)
    cos_p, sin_p = _rotary_tables(jnp.arange(s_len))
    cos_s, sin_s = _rotary_tables(PAST_LEN + (jnp.arange(n_s) % t_len))
    bias_p = _bias_tables(rel_bias, _prompt_bias_idx()).reshape(H_A, 3, MOBA_BLOCK, MOBA_BLOCK)
    bias_s_h = _bias_tables(rel_bias, _sample_bias_idx(t_len))
    col_h = jnp.arange(Q_COLS) % H_A
    bnear = jnp.sum(jnp.where((jnp.arange(H_A)[:, None] == col_h[None, :])[:, None, :], bias_s_h, 0.0), axis=0)
    bfar = rel_bias[NUM_BUCKETS - 1, col_h].reshape(1, Q_COLS)

    (qa, ka, va, kbf, vt, ga, qb, kb, vb, gb, kmean) = _even_in(
        x_prompt, e_mod_p[1], e_mod_p[0], e_norm_w[0].reshape(1, d), e_in_bf, qnw, knw, seg, cos_p, sin_p, False)
    oa = _moba_prompt(rel_bias, qa, kbf, vt, kmean.reshape(bp, s_len // MOBA_BLOCK, W_A), bias_p, ga)
    nc_p = s_len // CHUNK
    ch = lambda a: a.reshape(bp, nc_p, CHUNK, a.shape[-1])
    ob, ret_p = _retention(ch(qb), ch(kb), ch(vb), ch(gb), jnp.zeros((bp, H_B * DK_B, DV_B), F32),
                           *_retention_tables(CHUNK), lq=CHUNK)
    xp1 = _out_proj(oa, ob.reshape(bp, s_len, W_B), x_prompt, e_mod_p[2], e_out_bf, False)
    k_prompt = ka.reshape(1, bp, s_len, H_A, HD_A)
    v_prompt = va.reshape(1, bp, s_len, H_A, HD_A)
    ret_state_prompt = ret_p.reshape(1, bp, H_B, DK_B, DV_B)

    (qa_s, ka_s, va_s, _, _, ga_s, qb_s, kb_s, vb_s, gb_s, _) = _even_in(
        x_s, e_mod_s[1], e_mod_s[0], e_norm_w[0].reshape(1, d), e_in_bf, qnw, knw, seg, cos_s, sin_s, True)
    sq = lambda a: a.reshape(bs, t_len, a.shape[-1])
    qrep = jnp.repeat(sq(qa_s), H_A, axis=1)
    n_phys = cache_k.shape[1]
    oa_s = _moba_sample(page_table, qrep, sq(ka_s), sq(va_s), sq(ga_s), bnear, bfar,
                        cache_k[0].reshape(n_phys, PAGE_SIZE, W_A), cache_v[0].reshape(n_phys, PAGE_SIZE, W_A))
    sc = lambda a: a.reshape(bs, 1, t_len, a.shape[-1])
    ob_s, ret_s = _retention(sc(qb_s), sc(kb_s), sc(vb_s), sc(gb_s),
                             state_ret[0].reshape(bs, H_B * DK_B, DV_B), *_retention_tables(t_len), lq=8)
    xs1 = _out_proj(oa_s.reshape(1, n_s, W_A), ob_s.reshape(1, n_s, W_B), x_s, e_mod_s[2], e_out_bf, True)
    k_sample = ka_s.reshape(1, bs, t_len, H_A, HD_A)
    v_sample = va_s.reshape(1, bs, t_len, H_A, HD_A)
    ret_state_sample = ret_s.reshape(1, bs, H_B, DK_B, DV_B)

    tri = jnp.asarray(np.tril(np.ones((CHUNK, CHUNK))), BF16)
    sel = jnp.asarray(np.kron(np.eye(H_D), np.eye(1, P_D)), BF16)
    rep = lambda a: jnp.repeat(a, P_D).reshape(1, W_D)
    dt_bias, a_log, d_skip = rep(o_dt_bias[0]), rep(o_A_log[0]), rep(o_D[0])
    ssm_nw = o_ssm_norm_w[0].reshape(1, W_D)
    conv_b = o_conv_b[0].reshape(1, CONV_DIM)
    o_nw = o_norm_w[0].reshape(1, d)

    sgu_b_p = jnp.repeat(o_sgu_b[0].T, W_C // G_C, axis=1)
    oc, zg, xbc, dtp, _ = _odd_in(xp1, o_mod_p[1], o_mod_p[0], o_nw, o_in_bf, o_sgu_w[0], sgu_b_p, dt_bias, False)
    yn, ssm_p = _ssd(ch(xbc), ch(dtp), ch(zg), jnp.zeros((bp, 8, CONV_DIM), F32),
                     jnp.zeros((bp, H_D * P_D, N_D), F32), o_conv_w[0], conv_b, a_log, d_skip, ssm_nw,
                     tri, sel, lq=CHUNK)
    y_prompt = _out_proj(oc, yn.reshape(bp, s_len, W_D), xp1, o_mod_p[2], o_out_bf, False)
    ssm_state_prompt = ssm_p.reshape(1, bp, H_D, P_D, N_D)
    conv_state_prompt = xbc[:, -(CONV_W - 1):][None]

    per_chunk = CHUNK // t_len
    w_small = o_sgu_w[0][:, :t_len, :t_len]
    sgu_w_s = jax.vmap(lambda w: jnp.kron(jnp.eye(per_chunk, dtype=F32), w))(w_small)
    sgu_b_s = jnp.repeat(jnp.tile(o_sgu_b[0][:, :t_len].T, (per_chunk, 1)), W_C // G_C, axis=1)
    oc_s, zg_s, xbc_s, dt_s, v_s = _odd_in(xs1, o_mod_s[1], o_mod_s[0], o_nw, o_in_bf, sgu_w_s, sgu_b_s, dt_bias, True)
    tail_s = jnp.concatenate([jnp.zeros((bs, 8 - (CONV_W - 1), CONV_DIM), F32), state_conv[0]], axis=1)
    yn_s, ssm_s = _ssd(sc(xbc_s), sc(dt_s), sc(zg_s), tail_s, state_ssm[0].reshape(bs, H_D * P_D, N_D),
                       o_conv_w[0], conv_b, a_log, d_skip, ssm_nw, tri, sel, lq=8)
    xs2 = _out_proj(oc_s, yn_s.reshape(1, n_s, W_D), xs1, o_mod_s[2], o_out_bf, True)
    y_sample = xs2.reshape(bs, t_len, d)
    sgu_v_sample = v_s.reshape(1, bs, t_len, W_C)
    ssm_state_sample = ssm_s.reshape(1, bs, H_D, P_D, N_D)
    xin = jnp.concatenate([state_conv[0], xbc_s.reshape(bs, t_len, CONV_DIM)], axis=1)
    conv_state_sample = xin[:, -(CONV_W - 1):][None]

    return (y_prompt, y_sample, k_prompt, v_prompt, k_sample, v_sample, ret_state_prompt, ret_state_sample,
            sgu_v_sample, ssm_state_prompt, ssm_state_sample, conv_state_prompt, conv_state_sample)
```

```python
import functools
import math

import numpy as np
import jax
import jax.numpy as jnp
from jax import lax
from jax.experimental import pallas as pl
from jax.experimental.pallas import tpu as pltpu

F32 = jnp.float32
BF16 = jnp.bfloat16

D_MODEL = 1024
PAST_LEN = 2048
PAGE_SIZE = 128
H_A, HD_A, W_A = 8, 64, 512
MOBA_BLOCK = 256
MOBA_TOPK = 3
NUM_BUCKETS = 32
MAX_DISTANCE = 128
H_B, DK_B, DV_B, W_B = 4, 64, 128, 512
G_C, W_C = 4, 512
H_D, P_D, N_D, G_D, W_D = 8, 64, 128, 2, 512
CONV_W = 4
CONV_DIM = 1024
CHUNK = 128
NEG_INF = -1e30
EPS = 1e-6
VMEM_LIMIT = 56 * 1024 * 1024


def _bf(x):
    return x.astype(BF16)


def _dg(a, b, ca, cb):
    return lax.dot_general(a, b, (((ca,), (cb,)), ((), ())), preferred_element_type=F32)


def _mm(a, b):
    return _dg(_bf(a), _bf(b), 1, 0)


def _mm_nt(a, b):
    return _dg(_bf(a), _bf(b), 1, 1)


def _mm_tn(a, b):
    return _dg(_bf(a), _bf(b), 0, 0)


def _split2(x):
    hi = _bf(x)
    return hi, _bf(x - hi.astype(F32))


def _split3(x):
    hi = _bf(x)
    r = x - hi.astype(F32)
    mid = _bf(r)
    return hi, mid, _bf(r - mid.astype(F32))


def _mm_hp(a, b, ca, cb):
    ah, al = _split2(a)
    bh, bl = _split2(b)
    return _dg(ah, bh, ca, cb) + (_dg(ah, bl, ca, cb) + _dg(al, bh, ca, cb))


def _mm_exact_lhs(e, x, ca, cb):
    h, m, l = _split3(x)
    return _dg(e, h, ca, cb) + (_dg(e, m, ca, cb) + _dg(e, l, ca, cb))


def _silu(x):
    return x * (1.0 / (1.0 + jnp.exp(-x)))


def _gelu_tanh(x):
    return 0.5 * x * (1.0 + jnp.tanh(math.sqrt(2.0 / math.pi) * (x + 0.044715 * (x * x * x))))


def _softplus(x):
    return jnp.maximum(x, 0.0) + jnp.log1p(jnp.exp(-jnp.abs(x)))


def _cparams(n_grid):
    return pltpu.CompilerParams(dimension_semantics=("arbitrary",) * n_grid,
                                vmem_limit_bytes=VMEM_LIMIT)


def _top3_rows(g, blk, nblk):
    sel = jnp.zeros(g.shape, jnp.bool_)
    for _ in range(MOBA_TOPK):
        m = jnp.max(g, axis=0, keepdims=True)
        idx = jnp.min(jnp.where(g == m, blk, nblk), axis=0, keepdims=True)
        pick = blk == idx
        sel = jnp.logical_or(sel, pick)
        g = jnp.where(pick, -jnp.inf, g)
    return sel


def _ada_kernel(c_ref, w_ref, b_ref, o_ref):
    s = _silu(c_ref[...])
    o_ref[...] = _mm_hp(s, w_ref[...], 1, 0) + b_ref[...]


def _ada_mod(c_all, w, b):
    m, d = c_all.shape
    n = w.shape[1]
    tn = 512
    return pl.pallas_call(
        _ada_kernel,
        grid=(n // tn,),
        in_specs=[pl.BlockSpec((m, d), lambda j: (0, 0)),
                  pl.BlockSpec((d, tn), lambda j: (0, j)),
                  pl.BlockSpec((1, tn), lambda j: (0, j))],
        out_specs=pl.BlockSpec((m, tn), lambda j: (0, j)),
        out_shape=jax.ShapeDtypeStruct((m, n), F32),
        compiler_params=_cparams(1),
        name="ada_mod",
    )(c_all, w, b.reshape(1, n))


def _t5_bucket_np(rel):
    n = np.maximum(rel, 0)
    max_exact = NUM_BUCKETS // 2
    nf = np.maximum(n, 1).astype(np.float64)
    large = max_exact + (np.log(nf / max_exact) / math.log(MAX_DISTANCE / max_exact)
                         * (NUM_BUCKETS - max_exact)).astype(np.int64)
    large = np.minimum(large, NUM_BUCKETS - 1)
    return np.where(n < max_exact, n, large).astype(np.int32)


def _bias_kernel(tab_ref, idx_ref, o_ref):
    h = pl.program_id(0)
    idx = idx_ref[...]
    acc = jnp.where(idx == -1, NEG_INF, 0.0).astype(F32)
    for b in range(NUM_BUCKETS):
        acc = jnp.where(idx == b, tab_ref[b, h], acc)
    o_ref[0] = acc


def _bias_tables(rel_bias, idx):
    r, c = idx.shape
    return pl.pallas_call(
        _bias_kernel,
        grid=(H_A,),
        in_specs=[pl.BlockSpec(memory_space=pltpu.SMEM),
                  pl.BlockSpec((r, c), lambda h: (0, 0))],
        out_specs=pl.BlockSpec((1, r, c), lambda h: (h, 0, 0)),
        out_shape=jax.ShapeDtypeStruct((H_A, r, c), F32),
        compiler_params=_cparams(1),
        name="t5_bias",
    )(rel_bias, jnp.asarray(idx))


def _even_in_kernel(x_ref, sc_ref, sh_ref, nw_ref, w_ref, qnw_ref, knw_ref, seg_ref, cos_ref, sin_ref,
                    qa_ref, ka_ref, va_ref, kbf_ref, vt_ref, ga_ref, qb_ref, kb_ref, vb_ref, gb_ref, km_ref):
    x = x_ref[0]
    ms = jnp.mean(x * x, axis=-1, keepdims=True)
    h = (x * lax.rsqrt(ms + EPS) * nw_ref[...]) * (1.0 + sc_ref[0]) + sh_ref[0]
    hb = _bf(h)

    def proj(lo, hi):
        return jnp.dot(hb, w_ref[:, lo:hi], preferred_element_type=F32)

    def head_rms(t, w_row):
        ss = jnp.dot(_bf(t * t), seg_ref[...], preferred_element_type=F32)
        return t * lax.rsqrt(ss * (1.0 / HD_A) + EPS) * w_row

    qa_ref[0] = head_rms(proj(0, 512), qnw_ref[...])
    ka = head_rms(proj(512, 1024), knw_ref[...])
    ka_ref[0] = ka
    kbf_ref[0] = _bf(ka)
    km_ref[0, 0] = jnp.mean(ka, axis=0, keepdims=True)
    va = proj(1024, 1536)
    va_ref[0] = va
    vt_ref[0] = _bf(va.T)
    ga_ref[0] = proj(1536, 2048)

    lane = lax.broadcasted_iota(jnp.int32, (1, 256), 1) % DK_B
    first_half = lane < (DK_B // 2)
    cos = cos_ref[...]
    sin = sin_ref[...]

    def rotary(t):
        up = pltpu.roll(t, 256 - DK_B // 2, 1)
        dn = pltpu.roll(t, DK_B // 2, 1)
        return t * cos + jnp.where(first_half, up, dn) * sin

    qb_ref[0] = rotary(proj(2048, 2304))
    kb_ref[0] = rotary(proj(2304, 2560)) * (DK_B ** -0.5)
    vb_ref[0] = proj(2560, 3072)
    gb_ref[0] = proj(3072, 3584)


def _even_in(x, scale, shift, norm_w, w_bf, qnw, knw, seg, cos, sin, per_row_mod):
    nb, s, d = x.shape
    tm = MOBA_BLOCK
    ns = s // tm
    if per_row_mod:
        mod_spec = pl.BlockSpec((1, tm, d), lambda b, i: (b, i, 0))
    else:
        mod_spec = pl.BlockSpec((1, 1, d), lambda b, i: (b, 0, 0))
    row = lambda c: pl.BlockSpec((1, tm, c), lambda b, i: (b, i, 0))
    const = lambda shp: pl.BlockSpec(shp, lambda b, i: (0,) * len(shp))
    out_shape = (
        jax.ShapeDtypeStruct((nb, s, 512), F32),
        jax.ShapeDtypeStruct((nb, s, 512), F32),
        jax.ShapeDtypeStruct((nb, s, 512), F32),
        jax.ShapeDtypeStruct((nb, s, 512), BF16),
        jax.ShapeDtypeStruct((nb, 512, s), BF16),
        jax.ShapeDtypeStruct((nb, s, 512), F32),
        jax.ShapeDtypeStruct((nb, s, 256), F32),
        jax.ShapeDtypeStruct((nb, s, 256), F32),
        jax.ShapeDtypeStruct((nb, s, 512), F32),
        jax.ShapeDtypeStruct((nb, s, 512), F32),
        jax.ShapeDtypeStruct((nb, ns, 1, 512), F32),
    )
    out_specs = (row(512), row(512), row(512), row(512),
                 pl.BlockSpec((1, 512, tm), lambda b, i: (b, 0, i)),
                 row(512), row(256), row(256), row(512), row(512),
                 pl.BlockSpec((1, 1, 1, 512), lambda b, i: (b, i, 0, 0)))
    return pl.pallas_call(
        _even_in_kernel,
        grid=(nb, ns),
        in_specs=[row(d), mod_spec, mod_spec, const((1, d)), const((d, 3584)),
                  const((1, 512)), const((1, 512)), const((512, 512)),
                  pl.BlockSpec((tm, 256), lambda b, i: (i, 0)),
                  pl.BlockSpec((tm, 256), lambda b, i: (i, 0))],
        out_specs=out_specs,
        out_shape=out_shape,
        compiler_params=_cparams(2),
        name="even_in",
    )(x, scale, shift, norm_w, w_bf, qnw, knw, seg, cos, sin)


MOBA_HS = 8
FAR_KEYS = 2 * MOBA_BLOCK


def _moba_p_kernel(tab_ref, q_ref, k_ref, vt_ref, km_ref, bias_ref, ga_ref, o_ref, rbf_ref, rbs_ref, qs_ref):
    hg = pl.program_id(1)
    qi = pl.program_id(2)
    nblk = km_ref.shape[1]
    lane = lax.broadcasted_iota(jnp.int32, (1, 128), 1)
    blk = lax.broadcasted_iota(jnp.int32, (nblk, MOBA_BLOCK), 0)
    for hl in range(MOBA_HS):
        pr, hh = divmod(hl, 2)
        pc = slice(pr * 128, (pr + 1) * 128)
        qm = jnp.where((lane // HD_A) == hh, q_ref[0, :, pc], 0.0)
        gate = _mm_hp(km_ref[0, :, pc], qm, 1, 1)
        gate = jnp.where(blk < qi, gate, NEG_INF)
        sel = jnp.logical_and(_top3_rows(gate, blk, nblk), blk < qi)
        far_c = tab_ref[NUM_BUCKETS - 1, MOBA_HS * hg + hl]
        rbf_ref[hl] = jnp.where(jnp.logical_and(sel, blk < qi - 1), far_c, NEG_INF)
        rbs_ref[hl] = jnp.where(sel, 0.0, NEG_INF)
        qs_ref[hl] = _bf(qm * (HD_A ** -0.5))

    def visit(carry, off, nkeys, extra_fn):
        ss = []
        for hl in range(MOBA_HS):
            pr = hl // 2
            kj = k_ref[0, pl.ds(off, nkeys), pr * 128:(pr + 1) * 128]
            ss.append(_dg(kj, qs_ref[hl], 1, 1))
        stats, ps = [], []
        for hl in range(MOBA_HS):
            m, l, _ = carry[hl]
            s = extra_fn(hl, ss[hl])
            mn = jnp.maximum(m, jnp.max(s, axis=0, keepdims=True))
            alpha = jnp.exp(m - mn)
            p = jnp.exp(s - mn)
            stats.append((mn, alpha, alpha * l + jnp.sum(p, axis=0, keepdims=True)))
            ps.append(_bf(p))
        pvs = []
        for hl in range(MOBA_HS):
            vj = vt_ref[0, hl * HD_A:(hl + 1) * HD_A, pl.ds(off, nkeys)]
            pvs.append(jnp.dot(vj, ps[hl], preferred_element_type=F32))
        return tuple((stats[hl][0], stats[hl][2], stats[hl][1] * carry[hl][2] + pvs[hl])
                     for hl in range(MOBA_HS))

    def far_body(jp, carry):
        off = pl.multiple_of(jp * FAR_KEYS, FAR_KEYS)

        def extra(hl, s):
            r0 = rbf_ref[hl, pl.ds(2 * jp, 1), :]
            r1 = rbf_ref[hl, pl.ds(2 * jp + 1, 1), :]
            return jnp.concatenate([s[:MOBA_BLOCK] + r0, s[MOBA_BLOCK:] + r1], axis=0)

        return visit(carry, off, FAR_KEYS, extra)

    init = tuple((jnp.full((1, MOBA_BLOCK), -jnp.inf, F32), jnp.zeros((1, MOBA_BLOCK), F32),
                  jnp.zeros((HD_A, MOBA_BLOCK), F32)) for _ in range(MOBA_HS))
    carry = lax.fori_loop(0, qi // 2, far_body, init)
    js = jnp.maximum(qi - 1, 0)
    none_prev = jnp.where(qi >= 1, 0.0, NEG_INF)
    carry = visit(carry, pl.multiple_of(js * MOBA_BLOCK, MOBA_BLOCK), MOBA_BLOCK,
                  lambda hl, s: s + bias_ref[hl, 1] + (rbs_ref[hl, pl.ds(js, 1), :] + none_prev))
    carry = visit(carry, pl.multiple_of(qi * MOBA_BLOCK, MOBA_BLOCK), MOBA_BLOCK,
                  lambda hl, s: s + bias_ref[hl, 0])
    for pr in range(MOBA_HS // 2):
        outs = [carry[2 * pr + hh][2] * (1.0 / carry[2 * pr + hh][1]) for hh in range(2)]
        o = jnp.concatenate(outs, axis=0).T
        pc = slice(pr * 128, (pr + 1) * 128)
        o_ref[0, :, pc] = o * _silu(ga_ref[0, :, pc])


def _moba_prompt(rel_bias, qa, kbf, vt, kmean, bias_t, ga):
    nb, s, _ = qa.shape
    nq = s // MOBA_BLOCK
    w = MOBA_HS * HD_A
    tile = pl.BlockSpec((1, MOBA_BLOCK, w), lambda b, hg, i: (b, i, hg))
    return pl.pallas_call(
        _moba_p_kernel,
        grid=(nb, H_A // MOBA_HS, nq),
        in_specs=[pl.BlockSpec(memory_space=pltpu.SMEM),
                  tile,
                  pl.BlockSpec((1, s, w), lambda b, hg, i: (b, 0, hg)),
                  pl.BlockSpec((1, w, s), lambda b, hg, i: (b, hg, 0)),
                  pl.BlockSpec((1, nq, w), lambda b, hg, i: (b, 0, hg)),
                  pl.BlockSpec((MOBA_HS, 2, MOBA_BLOCK, MOBA_BLOCK), lambda b, hg, i: (hg, 0, 0, 0)),
                  tile],
        out_specs=tile,
        out_shape=jax.ShapeDtypeStruct((nb, s, 512), F32),
        scratch_shapes=[pltpu.VMEM((MOBA_HS, nq, MOBA_BLOCK), F32),
                        pltpu.VMEM((MOBA_HS, nq, MOBA_BLOCK), F32),
                        pltpu.VMEM((MOBA_HS, MOBA_BLOCK, 128), BF16)],
        compiler_params=_cparams(3),
        name="moba_prompt",
    )(rel_bias, qa, kbf, vt, kmean, bias_t, ga)


N_PAST_BLK = PAST_LEN // MOBA_BLOCK
N_PAGES = PAST_LEN // PAGE_SIZE
KV_ROWS = PAST_LEN + 128
Q_COLS = 128


def _moba_s_kernel(pt_ref, qrep_ref, knew_ref, vnew_ref, ga_ref, bnear_ref, bfar_ref, ck_hbm, cv_hbm,
                   o_ref, kbuf, vbuf, qpad, km_ref, s_ref, sems):
    b = pl.program_id(0)
    nseq = pl.num_programs(0)
    slot = b % 2
    t_new = knew_ref.shape[1]

    def page_copies(seq, sl):
        cps = []
        for p in range(N_PAGES):
            dst = pl.ds(p * PAGE_SIZE, PAGE_SIZE)
            cps.append(pltpu.make_async_copy(ck_hbm.at[pt_ref[seq, p]], kbuf.at[sl, dst], sems.at[0, sl]))
            cps.append(pltpu.make_async_copy(cv_hbm.at[pt_ref[seq, p]], vbuf.at[sl, dst], sems.at[1, sl]))
        return cps

    @pl.when(b == 0)
    def _():
        pad = jnp.zeros((KV_ROWS - PAST_LEN, W_A), F32)
        for sl in range(2):
            kbuf[sl, pl.ds(PAST_LEN, KV_ROWS - PAST_LEN), :] = pad
            vbuf[sl, pl.ds(PAST_LEN, KV_ROWS - PAST_LEN), :] = pad
        qpad[...] = jnp.zeros(qpad.shape, F32)
        for cp in page_copies(0, 0):
            cp.start()

    @pl.when(b + 1 < nseq)
    def _():
        for cp in page_copies(b + 1, 1 - slot):
            cp.start()

    n_rep = qrep_ref.shape[1]
    rowh = lax.broadcasted_iota(jnp.int32, (n_rep, W_A), 0) % H_A
    laneh = lax.broadcasted_iota(jnp.int32, (n_rep, W_A), 1) // HD_A
    qpad[0:n_rep, :] = jnp.where(rowh == laneh, qrep_ref[0], 0.0)
    kbuf[slot, pl.ds(PAST_LEN, t_new), :] = knew_ref[0]
    vbuf[slot, pl.ds(PAST_LEN, t_new), :] = vnew_ref[0]

    for cp in page_copies(b, slot):
        cp.wait()

    for n in range(N_PAST_BLK):
        km_ref[n:n + 1, :] = jnp.mean(kbuf[slot, pl.ds(n * MOBA_BLOCK, MOBA_BLOCK), :], axis=0, keepdims=True)
    qp = qpad[...]
    gate = _mm_hp(km_ref[...], qp, 1, 1)
    blk = lax.broadcasted_iota(jnp.int32, (N_PAST_BLK, Q_COLS), 0)
    sel = _top3_rows(gate, blk, N_PAST_BLK)
    rb = jnp.where(sel, jnp.where(blk == N_PAST_BLK - 1, 0.0, bfar_ref[...]), NEG_INF)

    qs = _bf(qp * (HD_A ** -0.5))
    m = jnp.full((1, Q_COLS), -jnp.inf, F32)
    for n in range(N_PAST_BLK + 1):
        rows = MOBA_BLOCK if n < N_PAST_BLK else KV_ROWS - PAST_LEN
        kn = _bf(kbuf[slot, pl.ds(n * MOBA_BLOCK, rows), :])
        s = _dg(kn, qs, 1, 1)
        if n < N_PAST_BLK - 1:
            s = s + rb[n:n + 1, :]
        elif n == N_PAST_BLK - 1:
            s = s + rb[n:n + 1, :] + bnear_ref[0:MOBA_BLOCK, :]
        else:
            s = s + bnear_ref[MOBA_BLOCK:MOBA_BLOCK + rows, :]
        s_ref[pl.ds(n * MOBA_BLOCK, rows), :] = s
        m = jnp.maximum(m, jnp.max(s, axis=0, keepdims=True))

    acc = jnp.zeros((Q_COLS, W_A), F32)
    l = jnp.zeros((Q_COLS, 1), F32)
    for n in range(N_PAST_BLK + 1):
        rows = MOBA_BLOCK if n < N_PAST_BLK else KV_ROWS - PAST_LEN
        p = jnp.exp(s_ref[pl.ds(n * MOBA_BLOCK, rows), :] - m)
        pt = p.T
        l = l + jnp.sum(pt, axis=1, keepdims=True)
        vn = _bf(vbuf[slot, pl.ds(n * MOBA_BLOCK, rows), :])
        acc = acc + jnp.dot(_bf(pt), vn, preferred_element_type=F32)
    o = (acc * (1.0 / l))[0:n_rep, :]
    o = jnp.where(rowh == laneh, o, 0.0)
    o = jnp.sum(o.reshape(t_new, H_A, W_A), axis=1)
    o_ref[0] = o * _silu(ga_ref[0])


def _moba_sample(page_table, qrep, knew, vnew, ga, bnear, bfar, cache_k, cache_v):
    nseq, t_new, _ = knew.shape
    tok = pl.BlockSpec((1, t_new, W_A), lambda b, pt: (b, 0, 0))
    grid_spec = pltpu.PrefetchScalarGridSpec(
        num_scalar_prefetch=1,
        grid=(nseq,),
        in_specs=[pl.BlockSpec((1, qrep.shape[1], W_A), lambda b, pt: (b, 0, 0)),
                  tok, tok, tok,
                  pl.BlockSpec(bnear.shape, lambda b, pt: (0, 0)),
                  pl.BlockSpec((1, Q_COLS), lambda b, pt: (0, 0)),
                  pl.BlockSpec(memory_space=pl.ANY),
                  pl.BlockSpec(memory_space=pl.ANY)],
        out_specs=tok,
        scratch_shapes=[pltpu.VMEM((2, KV_ROWS, W_A), F32),
                        pltpu.VMEM((2, KV_ROWS, W_A), F32),
                        pltpu.VMEM((Q_COLS, W_A), F32),
                        pltpu.VMEM((N_PAST_BLK, W_A), F32),
                        pltpu.VMEM((KV_ROWS, Q_COLS), F32),
                        pltpu.SemaphoreType.DMA((2, 2))],
    )
    return pl.pallas_call(
        _moba_s_kernel,
        grid_spec=grid_spec,
        out_shape=jax.ShapeDtypeStruct((nseq, t_new, W_A), F32),
        compiler_params=_cparams(1),
        name="moba_sample",
    )(page_table, qrep, knew, vnew, ga, bnear, bfar, cache_k, cache_v)


def _pad_rows(ref, scratch, t):
    if t == CHUNK:
        return ref[0]
    scratch[...] = jnp.zeros(scratch.shape, scratch.dtype)
    scratch[0:t, :] = ref[0]
    return scratch[...]


def _ret_kernel(q_ref, k_ref, v_ref, g_ref, st0_ref, dmat_ref, qdec_ref, kdec_ref, gl_ref,
                o_ref, st_ref, qpad, kpad, vpad, gpad, *, t, lq):
    c = pl.program_id(1)

    @pl.when(c == 0)
    def _():
        st_ref[0] = st0_ref[0]

    q = _pad_rows(q_ref, qpad, t)[0:lq]
    k = _pad_rows(k_ref, kpad, t)
    v = _pad_rows(v_ref, vpad, t)
    g = _pad_rows(g_ref, gpad, t)[0:lq]
    lane = lax.broadcasted_iota(jnp.int32, (1, 128), 1)
    rowsel = lax.broadcasted_iota(jnp.int32, (128, 1), 0) < DK_B
    outs = []
    for hp in range(H_B // 2):
        cols = slice(hp * 128, (hp + 1) * 128)
        kp = k[:, cols]
        qp = q[:, cols]
        st = st_ref[0, cols, :]
        kd = kp * kdec_ref[:, cols]
        upd = []
        for hh in range(2):
            h = 2 * hp + hh
            qm = jnp.where((lane // DK_B) == hh, qp, 0.0)
            sc = _mm_nt(qm, kp) * dmat_ref[h, 0:lq, :]
            vh = v[:, h * DV_B:(h + 1) * DV_B]
            o = _mm(sc, vh) + _mm(qm * qdec_ref[0:lq, cols], st)
            o = o * lax.rsqrt(jnp.mean(o * o, axis=-1, keepdims=True) + EPS)
            outs.append(o * _silu(g[:, h * DV_B:(h + 1) * DV_B]))
            upd.append(_mm_tn(kd, vh))
        st_ref[0, cols, :] = st * gl_ref[cols, :] + jnp.where(rowsel, upd[0], upd[1])
    o_ref[0] = jnp.concatenate(outs, axis=1)[0:t]


def _retention(q, k, v, g, st0, dmat, qdec, kdec, gl, lq):
    nb, nc, t, _ = q.shape
    row = lambda c_: pl.BlockSpec((None, 1, t, c_), lambda b, c: (b, c, 0, 0))
    const = lambda a: pl.BlockSpec(a.shape, lambda b, c: (0,) * a.ndim)
    st_spec = pl.BlockSpec((1, H_B * DK_B, DV_B), lambda b, c: (b, 0, 0))
    return pl.pallas_call(
        functools.partial(_ret_kernel, t=t, lq=lq),
        grid=(nb, nc),
        in_specs=[row(256), row(256), row(512), row(512), st_spec,
                  const(dmat), const(qdec), const(kdec), const(gl)],
        out_specs=(row(512), st_spec),
        out_shape=(jax.ShapeDtypeStruct((nb, nc, t, 512), F32),
                   jax.ShapeDtypeStruct((nb, H_B * DK_B, DV_B), F32)),
        scratch_shapes=[pltpu.VMEM((CHUNK, 256), F32), pltpu.VMEM((CHUNK, 256), F32),
                        pltpu.VMEM((CHUNK, 512), F32), pltpu.VMEM((CHUNK, 512), F32)],
        compiler_params=_cparams(2),
        name="retention",
    )(q, k, v, g, st0, dmat, qdec, kdec, gl)


def _out_kernel(a_ref, b_ref, x_ref, g_ref, w_ref, o_ref):
    half = w_ref.shape[0] // 2
    y = (jnp.dot(_bf(a_ref[0]), w_ref[0:half, :], preferred_element_type=F32)
         + jnp.dot(_bf(b_ref[0]), w_ref[half:, :], preferred_element_type=F32))
    o_ref[0] = x_ref[0] + g_ref[0] * y


def _out_proj(a, b, x, gate, w_bf, per_row_mod):
    nb, s, d = x.shape
    tm = 512
    if per_row_mod:
        g_spec = pl.BlockSpec((1, tm, d), lambda bb, i: (bb, i, 0))
    else:
        g_spec = pl.BlockSpec((1, 1, d), lambda bb, i: (bb, 0, 0))
    row = lambda c: pl.BlockSpec((1, tm, c), lambda bb, i: (bb, i, 0))
    return pl.pallas_call(
        _out_kernel,
        grid=(nb, s // tm),
        in_specs=[row(512), row(512), row(d), g_spec, pl.BlockSpec(w_bf.shape, lambda bb, i: (0, 0))],
        out_specs=row(d),
        out_shape=jax.ShapeDtypeStruct((nb, s, d), F32),
        compiler_params=_cparams(2),
        name="out_proj",
    )(a, b, x, gate, w_bf)


def _odd_in_kernel(x_ref, sc_ref, sh_ref, nw_ref, w_ref, sguw_ref, sgub_ref, dtb_ref,
                   oc_ref, zg_ref, xbc_ref, dt_ref, v_ref):
    x = x_ref[0]
    tm = x.shape[0]
    ms = jnp.mean(x * x, axis=-1, keepdims=True)
    h = (x * lax.rsqrt(ms + EPS) * nw_ref[...]) * (1.0 + sc_ref[0]) + sh_ref[0]
    hb = _bf(h)

    def proj(lo, hi):
        return jnp.dot(hb, w_ref[:, lo:hi], preferred_element_type=F32)

    u = _gelu_tanh(proj(0, 512))
    v = _gelu_tanh(proj(512, 1024))
    mu = jnp.mean(v, axis=-1, keepdims=True)
    vc = v - mu
    v = vc * lax.rsqrt(jnp.mean(vc * vc, axis=-1, keepdims=True) + EPS)
    v_ref[0] = v
    ii = lax.broadcasted_iota(jnp.int32, (CHUNK, CHUNK), 0)
    jj = lax.broadcasted_iota(jnp.int32, (CHUNK, CHUNK), 1)
    rows = []
    for ci in range(tm // CHUNK):
        cols = []
        for g in range(G_C):
            wg = jnp.where(ii >= jj, sguw_ref[g], 0.0)
            cols.append(_mm(wg, v[ci * CHUNK:(ci + 1) * CHUNK, g * 128:(g + 1) * 128]))
        rows.append(jnp.concatenate(cols, axis=1) + sgub_ref[...])
    sg = jnp.concatenate(rows, axis=0) if len(rows) > 1 else rows[0]
    oc_ref[0] = u * sg * _silu(proj(1024, 1536))
    zg_ref[0] = proj(1536, 2048)
    xbc_ref[0] = proj(2048, 3072)
    dt_ref[0] = _softplus(proj(3072, 3584) + dtb_ref[...])


def _odd_in(x, scale, shift, norm_w, w_bf, sgu_w, sgu_b_tab, dt_bias, per_row_mod):
    nb, s, d = x.shape
    tm = 256
    if per_row_mod:
        mod_spec = pl.BlockSpec((1, tm, d), lambda b, i: (b, i, 0))
    else:
        mod_spec = pl.BlockSpec((1, 1, d), lambda b, i: (b, 0, 0))
    row = lambda c: pl.BlockSpec((1, tm, c), lambda b, i: (b, i, 0))
    const = lambda shp: pl.BlockSpec(shp, lambda b, i: (0,) * len(shp))
    return pl.pallas_call(
        _odd_in_kernel,
        grid=(nb, s // tm),
        in_specs=[row(d), mod_spec, mod_spec, const((1, d)), const(w_bf.shape),
                  const(sgu_w.shape), const(sgu_b_tab.shape), const((1, 512))],
        out_specs=(row(512), row(512), row(1024), row(512), row(512)),
        out_shape=(jax.ShapeDtypeStruct((nb, s, 512), F32),
                   jax.ShapeDtypeStruct((nb, s, 512), F32),
                   jax.ShapeDtypeStruct((nb, s, 1024), F32),
                   jax.ShapeDtypeStruct((nb, s, 512), F32),
                   jax.ShapeDtypeStruct((nb, s, 512), F32)),
        compiler_params=_cparams(2),
        name="odd_in",
    )(x, scale, shift, norm_w, w_bf, sgu_w, sgu_b_tab, dt_bias)


def _ssd_kernel(xbc_ref, dt_ref, zg_ref, tail_ref, st0_ref, cw_ref, cb_ref, alog_ref, dsk_ref, nw_ref,
                tri_ref, sel_ref, y_ref, st_ref, ext, dtpad, zpad, *, t, lq, nc):
    c = pl.program_id(1)

    @pl.when(c == 0)
    def _():
        st_ref[0] = st0_ref[0]
        ext[...] = jnp.zeros(ext.shape, F32)
        ext[0:8, :] = tail_ref[0]

    ext[8:8 + t, :] = xbc_ref[0]
    conv = cb_ref[...]
    for w in range(CONV_W):
        conv = conv + ext[pl.ds(8 - (CONV_W - 1) + w, CHUNK), :] * cw_ref[w:w + 1, :]
    if nc > 1:
        ext[0:8, :] = ext[CHUNK:CHUNK + 8, :]
    xc = _silu(conv)
    xh = xc[:, 0:W_D]
    dt = _pad_rows(dt_ref, dtpad, t)
    zg = _pad_rows(zg_ref, zpad, t)[0:lq]
    a = dt * (-jnp.exp(alog_ref[...]))
    cum = _mm_exact_lhs(tri_ref[...], a, 1, 0)
    last = cum[CHUNK - 1:CHUNK, :]
    dtx = xh * dt
    xw = xh * (jnp.exp(last - cum) * dt)
    ecum = jnp.exp(cum)
    elast = jnp.exp(last)
    cum_rows = _mm_exact_lhs(sel_ref[...], cum, 1, 1)
    ii = lax.broadcasted_iota(jnp.int32, (lq, CHUNK), 0)
    jj = lax.broadcasted_iota(jnp.int32, (lq, CHUNK), 1)
    lane = lax.broadcasted_iota(jnp.int32, (1, 128), 1)
    hpg = H_D // G_D
    ys = []
    for g in range(G_D):
        bg = xc[:, W_D + g * N_D:W_D + (g + 1) * N_D]
        cg = xc[0:lq, W_D + G_D * N_D + g * N_D:W_D + G_D * N_D + (g + 1) * N_D]
        cb = _mm_nt(cg, bg)
        gr = slice(g * hpg * P_D, (g + 1) * hpg * P_D)
        stg = st_ref[0, gr, :]
        yoff = _mm_nt(cg, stg)
        for pr in range(hpg // 2):
            l0 = g * hpg * P_D + pr * 128
            dtxp = dtx[:, l0:l0 + 128]
            yh = []
            for hh in range(2):
                h = g * hpg + pr * 2 + hh
                col = jnp.broadcast_to(cum[0:lq, h * P_D:h * P_D + 1], (lq, CHUNK))
                seg = jnp.minimum(col - cum_rows[h:h + 1, :], 0.0)
                mh = jnp.where(ii >= jj, cb * jnp.exp(seg), 0.0)
                yh.append(_mm(mh, dtxp))
            ypair = jnp.where(lane < P_D, yh[0], yh[1])
            ys.append(ypair + yoff[:, pr * 128:(pr + 1) * 128] * ecum[0:lq, l0:l0 + 128])
        upd = _mm_tn(xw[:, gr], bg)
        for hl in range(hpg):
            h = g * hpg + hl
            r = slice(h * P_D, (h + 1) * P_D)
            dec = jnp.broadcast_to(elast[0:1, h * P_D:h * P_D + 1], (P_D, N_D))
            st_ref[0, r, :] = st_ref[0, r, :] * dec + upd[hl * P_D:(hl + 1) * P_D, :]
    y = jnp.concatenate(ys, axis=1)
    y = (y + xh[0:lq] * dsk_ref[...]) * _silu(zg)
    gw = W_D // G_D
    outs = []
    for g in range(G_D):
        yg = y[:, g * gw:(g + 1) * gw]
        outs.append(yg * lax.rsqrt(jnp.mean(yg * yg, axis=-1, keepdims=True) + EPS))
    y_ref[0] = (jnp.concatenate(outs, axis=1) * nw_ref[...])[0:t]


def _ssd(xbc, dt, zg, tail, st0, conv_w, conv_b, a_log, d_skip, norm_w, tri, sel, lq):
    nb, nc, t, _ = xbc.shape
    row = lambda c_: pl.BlockSpec((None, 1, t, c_), lambda b, c: (b, c, 0, 0))
    const = lambda a: pl.BlockSpec(a.shape, lambda b, c: (0,) * a.ndim)
    st_spec = pl.BlockSpec((1, H_D * P_D, N_D), lambda b, c: (b, 0, 0))
    return pl.pallas_call(
        functools.partial(_ssd_kernel, t=t, lq=lq, nc=nc),
        grid=(nb, nc),
        in_specs=[row(1024), row(512), row(512),
                  pl.BlockSpec((1, 8, CONV_DIM), lambda b, c: (b, 0, 0)), st_spec,
                  const(conv_w), const(conv_b), const(a_log), const(d_skip), const(norm_w),
                  const(tri), const(sel)],
        out_specs=(row(512), st_spec),
        out_shape=(jax.ShapeDtypeStruct((nb, nc, t, 512), F32),
                   jax.ShapeDtypeStruct((nb, H_D * P_D, N_D), F32)),
        scratch_shapes=[pltpu.VMEM((CHUNK + 8, CONV_DIM), F32),
                        pltpu.VMEM((CHUNK, 512), F32), pltpu.VMEM((CHUNK, 512), F32)],
        compiler_params=_cparams(2),
        name="ssd",
    )(xbc, dt, zg, tail, st0, conv_w, conv_b, a_log, d_skip, norm_w, tri, sel)


def _rotary_tables(pos):
    half = DK_B // 2
    inv = 1.0 / (10000.0 ** (jnp.arange(half, dtype=F32) / half))
    ang = pos.astype(F32)[:, None] * inv[None, :]
    cos, sin = jnp.cos(ang), jnp.sin(ang)
    cos_t = jnp.tile(jnp.concatenate([cos, cos], axis=1), (1, H_B))
    sin_t = jnp.tile(jnp.concatenate([-sin, sin], axis=1), (1, H_B))
    return cos_t, sin_t


def _retention_tables(chunk_len):
    log_g = np.log(1.0 - 2.0 ** (-5.0 - np.arange(H_B, dtype=np.float64)))
    idx = np.arange(CHUNK, dtype=np.float64)
    diff = idx[:, None] - idx[None, :]
    dmat = np.where(diff[None] >= 0, np.exp(np.maximum(diff, 0.0)[None] * log_g[:, None, None]), 0.0)
    qdec = np.exp((idx + 1.0)[:, None] * log_g[None, :])
    kdec = np.where(idx[:, None] < chunk_len, np.exp((chunk_len - 1.0 - idx)[:, None] * log_g[None, :]), 0.0)
    gl = np.exp(chunk_len * log_g)
    return (jnp.asarray(dmat, F32),
            jnp.asarray(np.repeat(qdec, DK_B, axis=1), F32),
            jnp.asarray(np.repeat(kdec, DK_B, axis=1), F32),
            jnp.asarray(np.repeat(np.repeat(gl, DK_B)[:, None], DV_B, axis=1), F32))


def _prompt_bias_idx():
    kk = np.arange(MOBA_BLOCK)[:, None]
    qq = np.arange(MOBA_BLOCK)[None, :]
    diag = np.where(qq >= kk, _t5_bucket_np(qq - kk), -1)
    sub = _t5_bucket_np(qq + MOBA_BLOCK - kk)
    return np.concatenate([diag, sub], axis=0).astype(np.int32)


def _sample_bias_idx(t_new):
    rows = np.arange(MOBA_BLOCK + KV_ROWS - PAST_LEN)[:, None]
    col_t = (np.arange(Q_COLS) // H_A)[None, :]
    kpos = PAST_LEN - MOBA_BLOCK + rows
    qpos = PAST_LEN + col_t
    ok = (kpos <= qpos) & (kpos < PAST_LEN + t_new) & (col_t < t_new)
    return np.where(ok, _t5_bucket_np(qpos - kpos), -1).astype(np.int32)


def kernel(x_prompt, x_sample, cache_k, cache_v, state_ret, state_ssm, state_conv, page_table, c_prompt, c_sample, rel_bias, e_norm_w, e_ada_w, e_ada_b, e_in_w, e_q_norm_w, e_k_norm_w, e_out_w, o_norm_w, o_ada_w, o_ada_b, o_in_w, o_sgu_w, o_sgu_b, o_conv_w, o_conv_b, o_dt_bias, o_A_log, o_D, o_ssm_norm_w, o_out_w):
    bp, s_len, d = x_prompt.shape
    bs, t_len, _ = x_sample.shape
    n_s = bs * t_len

    c_all = jnp.concatenate([c_prompt, c_sample, jnp.zeros((8 - (bp + bs) % 8, d), F32)], axis=0)
    mods = []
    for ada_w, ada_b in ((e_ada_w[0], e_ada_b[0]), (o_ada_w[0], o_ada_b[0])):
        mod = _ada_mod(c_all, ada_w, ada_b)
        parts_p = [mod[:bp, i * d:(i + 1) * d].reshape(bp, 1, d) for i in range(3)]
        parts_s = [jnp.repeat(mod[bp:bp + bs, i * d:(i + 1) * d], t_len, axis=0).reshape(1, n_s, d) for i in range(3)]
        mods.append((parts_p, parts_s))
    (e_mod_p, e_mod_s), (o_mod_p, o_mod_s) = mods

    seg = jnp.asarray(np.kron(np.eye(H_A), np.ones((HD_A, HD_A))), BF16)
    qnw = jnp.tile(e_q_norm_w[0], H_A).reshape(1, W_A)
    knw = jnp.tile(e_k_norm_w[0], H_A).reshape(1, W_A)
    e_in_bf = _bf(e_in_w[0])
    e_out_bf = _bf(e_out_w[0])
    o_in_bf = _bf(jnp.concatenate([o_in_w[0][:, :3072], jnp.repeat(o_in_w[0][:, 3072:], P_D, axis=1)], axis=1))
    o_out_bf = _bf(o_out_w[0])
    x_s = x_sample.reshape(1, n_s, d)
    cos_p, sin_p = _rotary_tables(jnp.arange(s_len))
    cos_s, sin_s = _rotary_tables(PAST_LEN + (jnp.arange(n_s) % t_len))
    bias_p = _bias_tables(rel_bias, _prompt_bias_idx()).reshape(H_A, 2, MOBA_BLOCK, MOBA_BLOCK)
    bias_s_h = _bias_tables(rel_bias, _sample_bias_idx(t_len))
    col_h = jnp.arange(Q_COLS) % H_A
    bnear = jnp.sum(jnp.where((jnp.arange(H_A)[:, None] == col_h[None, :])[:, None, :], bias_s_h, 0.0), axis=0)
    bfar = rel_bias[NUM_BUCKETS - 1, col_h].reshape(1, Q_COLS)

    (qa, ka, va, kbf, vt, ga, qb, kb, vb, gb, kmean) = _even_in(
        x_prompt, e_mod_p[1], e_mod_p[0], e_norm_w[0].reshape(1, d), e_in_bf, qnw, knw, seg, cos_p, sin_p, False)
    oa = _moba_prompt(rel_bias, qa, kbf, vt, kmean.reshape(bp, s_len // MOBA_BLOCK, W_A), bias_p, ga)
    nc_p = s_len // CHUNK
    ch = lambda a: a.reshape(bp, nc_p, CHUNK, a.shape[-1])
    ob, ret_p = _retention(ch(qb), ch(kb), ch(vb), ch(gb), jnp.zeros((bp, H_B * DK_B, DV_B), F32),
                           *_retention_tables(CHUNK), lq=CHUNK)
    xp1 = _out_proj(oa, ob.reshape(bp, s_len, W_B), x_prompt, e_mod_p[2], e_out_bf, False)
    k_prompt = ka.reshape(1, bp, s_len, H_A, HD_A)
    v_prompt = va.reshape(1, bp, s_len, H_A, HD_A)
    ret_state_prompt = ret_p.reshape(1, bp, H_B, DK_B, DV_B)

    (qa_s, ka_s, va_s, _, _, ga_s, qb_s, kb_s, vb_s, gb_s, _) = _even_in(
        x_s, e_mod_s[1], e_mod_s[0], e_norm_w[0].reshape(1, d), e_in_bf, qnw, knw, seg, cos_s, sin_s, True)
    sq = lambda a: a.reshape(bs, t_len, a.shape[-1])
    qrep = jnp.repeat(sq(qa_s), H_A, axis=1)
    n_phys = cache_k.shape[1]
    oa_s = _moba_sample(page_table, qrep, sq(ka_s), sq(va_s), sq(ga_s), bnear, bfar,
                        cache_k[0].reshape(n_phys, PAGE_SIZE, W_A), cache_v[0].reshape(n_phys, PAGE_SIZE, W_A))
    sc = lambda a: a.reshape(bs, 1, t_len, a.shape[-1])
    ob_s, ret_s = _retention(sc(qb_s), sc(kb_s), sc(vb_s), sc(gb_s),
                             state_ret[0].reshape(bs, H_B * DK_B, DV_B), *_retention_tables(t_len), lq=8)
    xs1 = _out_proj(oa_s.reshape(1, n_s, W_A), ob_s.reshape(1, n_s, W_B), x_s, e_mod_s[2], e_out_bf, True)
    k_sample = ka_s.reshape(1, bs, t_len, H_A, HD_A)
    v_sample = va_s.reshape(1, bs, t_len, H_A, HD_A)
    ret_state_sample = ret_s.reshape(1, bs, H_B, DK_B, DV_B)

    tri = jnp.asarray(np.tril(np.ones((CHUNK, CHUNK))), BF16)
    sel = jnp.asarray(np.kron(np.eye(H_D), np.eye(1, P_D)), BF16)
    rep = lambda a: jnp.repeat(a, P_D).reshape(1, W_D)
    dt_bias, a_log, d_skip = rep(o_dt_bias[0]), rep(o_A_log[0]), rep(o_D[0])
    ssm_nw = o_ssm_norm_w[0].reshape(1, W_D)
    conv_b = o_conv_b[0].reshape(1, CONV_DIM)
    o_nw = o_norm_w[0].reshape(1, d)

    sgu_b_p = jnp.repeat(o_sgu_b[0].T, W_C // G_C, axis=1)
    oc, zg, xbc, dtp, _ = _odd_in(xp1, o_mod_p[1], o_mod_p[0], o_nw, o_in_bf, o_sgu_w[0], sgu_b_p, dt_bias, False)
    yn, ssm_p = _ssd(ch(xbc), ch(dtp), ch(zg), jnp.zeros((bp, 8, CONV_DIM), F32),
                     jnp.zeros((bp, H_D * P_D, N_D), F32), o_conv_w[0], conv_b, a_log, d_skip, ssm_nw,
                     tri, sel, lq=CHUNK)
    y_prompt = _out_proj(oc, yn.reshape(bp, s_len, W_D), xp1, o_mod_p[2], o_out_bf, False)
    ssm_state_prompt = ssm_p.reshape(1, bp, H_D, P_D, N_D)
    conv_state_prompt = xbc[:, -(CONV_W - 1):][None]

    per_chunk = CHUNK // t_len
    w_small = o_sgu_w[0][:, :t_len, :t_len]
    sgu_w_s = jax.vmap(lambda w: jnp.kron(jnp.eye(per_chunk, dtype=F32), w))(w_small)
    sgu_b_s = jnp.repeat(jnp.tile(o_sgu_b[0][:, :t_len].T, (per_chunk, 1)), W_C // G_C, axis=1)
    oc_s, zg_s, xbc_s, dt_s, v_s = _odd_in(xs1, o_mod_s[1], o_mod_s[0], o_nw, o_in_bf, sgu_w_s, sgu_b_s, dt_bias, True)
    tail_s = jnp.concatenate([jnp.zeros((bs, 8 - (CONV_W - 1), CONV_DIM), F32), state_conv[0]], axis=1)
    yn_s, ssm_s = _ssd(sc(xbc_s), sc(dt_s), sc(zg_s), tail_s, state_ssm[0].reshape(bs, H_D * P_D, N_D),
                       o_conv_w[0], conv_b, a_log, d_skip, ssm_nw, tri, sel, lq=8)
    xs2 = _out_proj(oc_s, yn_s.reshape(1, n_s, W_D), xs1, o_mod_s[2], o_out_bf, True)
    y_sample = xs2.reshape(bs, t_len, d)
    sgu_v_sample = v_s.reshape(1, bs, t_len, W_C)
    ssm_state_sample = ssm_s.reshape(1, bs, H_D, P_D, N_D)
    xin = jnp.concatenate([state_conv[0], xbc_s.reshape(bs, t_len, CONV_DIM)], axis=1)
    conv_state_sample = xin[:, -(CONV_W - 1):][None]

    return (y_prompt, y_sample, k_prompt, v_prompt, k_sample, v_sample, ret_state_prompt, ret_state_sample,
            sgu_v_sample, ssm_state_prompt, ssm_state_sample, conv_state_prompt, conv_state_sample)
```

```python
import functools
import math

import numpy as np
import jax
import jax.numpy as jnp
from jax import lax
from jax.experimental import pallas as pl
from jax.experimental.pallas import tpu as pltpu

F32 = jnp.float32
BF16 = jnp.bfloat16

D_MODEL = 1024
PAST_LEN = 2048
PAGE_SIZE = 128
H_A, HD_A, W_A = 8, 64, 512
MOBA_BLOCK = 256
MOBA_TOPK = 3
NUM_BUCKETS = 32
MAX_DISTANCE = 128
H_B, DK_B, DV_B, W_B = 4, 64, 128, 512
G_C, W_C = 4, 512
H_D, P_D, N_D, G_D, W_D = 8, 64, 128, 2, 512
CONV_W = 4
CONV_DIM = 1024
CHUNK = 128
NEG_INF = -1e30
EPS = 1e-6
VMEM_LIMIT = 56 * 1024 * 1024


def _bf(x):
    return x.astype(BF16)


def _dg(a, b, ca, cb):
    return lax.dot_general(a, b, (((ca,), (cb,)), ((), ())), preferred_element_type=F32)


def _mm(a, b):
    return _dg(_bf(a), _bf(b), 1, 0)


def _mm_nt(a, b):
    return _dg(_bf(a), _bf(b), 1, 1)


def _mm_tn(a, b):
    return _dg(_bf(a), _bf(b), 0, 0)


def _split2(x):
    hi = _bf(x)
    return hi, _bf(x - hi.astype(F32))


def _split3(x):
    hi = _bf(x)
    r = x - hi.astype(F32)
    mid = _bf(r)
    return hi, mid, _bf(r - mid.astype(F32))


def _mm_hp(a, b, ca, cb):
    ah, al = _split2(a)
    bh, bl = _split2(b)
    return _dg(ah, bh, ca, cb) + (_dg(ah, bl, ca, cb) + _dg(al, bh, ca, cb))


def _mm_exact_lhs(e, x, ca, cb):
    h, m, l = _split3(x)
    return _dg(e, h, ca, cb) + (_dg(e, m, ca, cb) + _dg(e, l, ca, cb))


def _silu(x):
    return x * (1.0 / (1.0 + jnp.exp(-x)))


def _gelu_tanh(x):
    return 0.5 * x * (1.0 + jnp.tanh(math.sqrt(2.0 / math.pi) * (x + 0.044715 * (x * x * x))))


def _softplus(x):
    return jnp.maximum(x, 0.0) + jnp.log1p(jnp.exp(-jnp.abs(x)))


def _cparams(n_grid):
    return pltpu.CompilerParams(dimension_semantics=("arbitrary",) * n_grid,
                                vmem_limit_bytes=VMEM_LIMIT)


def _top3_rows(g, blk, nblk):
    sel = jnp.zeros(g.shape, jnp.bool_)
    for _ in range(MOBA_TOPK):
        m = jnp.max(g, axis=0, keepdims=True)
        idx = jnp.min(jnp.where(g == m, blk, nblk), axis=0, keepdims=True)
        pick = blk == idx
        sel = jnp.logical_or(sel, pick)
        g = jnp.where(pick, -jnp.inf, g)
    return sel


def _ada_kernel(c_ref, w_ref, b_ref, o_ref):
    s = _silu(c_ref[...])
    o_ref[...] = _mm_hp(s, w_ref[...], 1, 0) + b_ref[...]


def _ada_mod(c_all, w, b):
    m, d = c_all.shape
    n = w.shape[1]
    tn = 512
    return pl.pallas_call(
        _ada_kernel,
        grid=(n // tn,),
        in_specs=[pl.BlockSpec((m, d), lambda j: (0, 0)),
                  pl.BlockSpec((d, tn), lambda j: (0, j)),
                  pl.BlockSpec((1, tn), lambda j: (0, j))],
        out_specs=pl.BlockSpec((m, tn), lambda j: (0, j)),
        out_shape=jax.ShapeDtypeStruct((m, n), F32),
        compiler_params=_cparams(1),
        name="ada_mod",
    )(c_all, w, b.reshape(1, n))


def _t5_bucket_np(rel):
    n = np.maximum(rel, 0)
    max_exact = NUM_BUCKETS // 2
    nf = np.maximum(n, 1).astype(np.float64)
    large = max_exact + (np.log(nf / max_exact) / math.log(MAX_DISTANCE / max_exact)
                         * (NUM_BUCKETS - max_exact)).astype(np.int64)
    large = np.minimum(large, NUM_BUCKETS - 1)
    return np.where(n < max_exact, n, large).astype(np.int32)


def _bias_kernel(tab_ref, idx_ref, o_ref):
    h = pl.program_id(0)
    idx = idx_ref[...]
    acc = jnp.where(idx == -1, NEG_INF, 0.0).astype(F32)
    for b in range(NUM_BUCKETS):
        acc = jnp.where(idx == b, tab_ref[b, h], acc)
    o_ref[0] = acc


def _bias_tables(rel_bias, idx):
    r, c = idx.shape
    return pl.pallas_call(
        _bias_kernel,
        grid=(H_A,),
        in_specs=[pl.BlockSpec(memory_space=pltpu.SMEM),
                  pl.BlockSpec((r, c), lambda h: (0, 0))],
        out_specs=pl.BlockSpec((1, r, c), lambda h: (h, 0, 0)),
        out_shape=jax.ShapeDtypeStruct((H_A, r, c), F32),
        compiler_params=_cparams(1),
        name="t5_bias",
    )(rel_bias, jnp.asarray(idx))


def _even_in_kernel(x_ref, sc_ref, sh_ref, nw_ref, w_ref, qnw_ref, knw_ref, seg_ref, cos_ref, sin_ref,
                    qa_ref, ka_ref, va_ref, kbf_ref, vt_ref, ga_ref, qb_ref, kb_ref, vb_ref, gb_ref, km_ref):
    x = x_ref[0]
    ms = jnp.mean(x * x, axis=-1, keepdims=True)
    h = (x * lax.rsqrt(ms + EPS) * nw_ref[...]) * (1.0 + sc_ref[0]) + sh_ref[0]
    hb = _bf(h)

    def proj(lo, hi):
        return jnp.dot(hb, w_ref[:, lo:hi], preferred_element_type=F32)

    def head_rms(t, w_row):
        ss = jnp.dot(_bf(t * t), seg_ref[...], preferred_element_type=F32)
        return t * lax.rsqrt(ss * (1.0 / HD_A) + EPS) * w_row

    qa_ref[0] = head_rms(proj(0, 512), qnw_ref[...])
    ka = head_rms(proj(512, 1024), knw_ref[...])
    ka_ref[0] = ka
    kbf_ref[0] = _bf(ka)
    km_ref[0, 0] = jnp.mean(ka, axis=0, keepdims=True)
    va = proj(1024, 1536)
    va_ref[0] = va
    vt_ref[0] = _bf(va.T)
    ga_ref[0] = proj(1536, 2048)

    lane = lax.broadcasted_iota(jnp.int32, (1, 256), 1) % DK_B
    first_half = lane < (DK_B // 2)
    cos = cos_ref[...]
    sin = sin_ref[...]

    def rotary(t):
        up = pltpu.roll(t, 256 - DK_B // 2, 1)
        dn = pltpu.roll(t, DK_B // 2, 1)
        return t * cos + jnp.where(first_half, up, dn) * sin

    qb_ref[0] = rotary(proj(2048, 2304))
    kb_ref[0] = rotary(proj(2304, 2560)) * (DK_B ** -0.5)
    vb_ref[0] = proj(2560, 3072)
    gb_ref[0] = proj(3072, 3584)


def _even_in(x, scale, shift, norm_w, w_bf, qnw, knw, seg, cos, sin, per_row_mod):
    nb, s, d = x.shape
    tm = MOBA_BLOCK
    ns = s // tm
    if per_row_mod:
        mod_spec = pl.BlockSpec((1, tm, d), lambda b, i: (b, i, 0))
    else:
        mod_spec = pl.BlockSpec((1, 1, d), lambda b, i: (b, 0, 0))
    row = lambda c: pl.BlockSpec((1, tm, c), lambda b, i: (b, i, 0))
    const = lambda shp: pl.BlockSpec(shp, lambda b, i: (0,) * len(shp))
    out_shape = (
        jax.ShapeDtypeStruct((nb, s, 512), F32),
        jax.ShapeDtypeStruct((nb, s, 512), F32),
        jax.ShapeDtypeStruct((nb, s, 512), F32),
        jax.ShapeDtypeStruct((nb, s, 512), BF16),
        jax.ShapeDtypeStruct((nb, 512, s), BF16),
        jax.ShapeDtypeStruct((nb, s, 512), F32),
        jax.ShapeDtypeStruct((nb, s, 256), F32),
        jax.ShapeDtypeStruct((nb, s, 256), F32),
        jax.ShapeDtypeStruct((nb, s, 512), F32),
        jax.ShapeDtypeStruct((nb, s, 512), F32),
        jax.ShapeDtypeStruct((nb, ns, 1, 512), F32),
    )
    out_specs = (row(512), row(512), row(512), row(512),
                 pl.BlockSpec((1, 512, tm), lambda b, i: (b, 0, i)),
                 row(512), row(256), row(256), row(512), row(512),
                 pl.BlockSpec((1, 1, 1, 512), lambda b, i: (b, i, 0, 0)))
    return pl.pallas_call(
        _even_in_kernel,
        grid=(nb, ns),
        in_specs=[row(d), mod_spec, mod_spec, const((1, d)), const((d, 3584)),
                  const((1, 512)), const((1, 512)), const((512, 512)),
                  pl.BlockSpec((tm, 256), lambda b, i: (i, 0)),
                  pl.BlockSpec((tm, 256), lambda b, i: (i, 0))],
        out_specs=out_specs,
        out_shape=out_shape,
        compiler_params=_cparams(2),
        name="even_in",
    )(x, scale, shift, norm_w, w_bf, qnw, knw, seg, cos, sin)


MOBA_HS = 8
FAR_KEYS = 2 * MOBA_BLOCK


def _moba_p_kernel(tab_ref, q_ref, k_ref, vt_ref, km_ref, bias_ref, ga_ref, o_ref, rbf_ref, rbs_ref, qs_ref):
    hg = pl.program_id(1)
    qi = pl.program_id(2)
    nblk = km_ref.shape[1]
    lane = lax.broadcasted_iota(jnp.int32, (1, 128), 1)
    blk = lax.broadcasted_iota(jnp.int32, (nblk, MOBA_BLOCK), 0)
    for hl in range(MOBA_HS):
        pr, hh = divmod(hl, 2)
        pc = slice(pr * 128, (pr + 1) * 128)
        qm = jnp.where((lane // HD_A) == hh, q_ref[0, :, pc], 0.0)
        gate = _mm_hp(km_ref[0, :, pc], qm, 1, 1)
        gate = jnp.where(blk < qi, gate, NEG_INF)
        sel = jnp.logical_and(_top3_rows(gate, blk, nblk), blk < qi)
        far_c = tab_ref[NUM_BUCKETS - 1, MOBA_HS * hg + hl]
        rbf_ref[hl] = jnp.where(jnp.logical_and(sel, blk < qi - 1), far_c, NEG_INF)
        rbs_ref[hl] = jnp.where(sel, 0.0, NEG_INF)
        qs_ref[hl] = _bf(qm * (HD_A ** -0.5))

    def visit(carry, off, nkeys, extra_fn):
        ss = []
        for hl in range(MOBA_HS):
            pr = hl // 2
            kj = k_ref[0, pl.ds(off, nkeys), pr * 128:(pr + 1) * 128]
            ss.append(_dg(kj, qs_ref[hl], 1, 1))
        stats, ps = [], []
        for hl in range(MOBA_HS):
            m, l, _ = carry[hl]
            s = extra_fn(hl, ss[hl])
            mn = jnp.maximum(m, jnp.max(s, axis=0, keepdims=True))
            alpha = jnp.exp(m - mn)
            p = jnp.exp(s - mn)
            stats.append((mn, alpha, alpha * l + jnp.sum(p, axis=0, keepdims=True)))
            ps.append(_bf(p))
        pvs = []
        for hl in range(MOBA_HS):
            vj = vt_ref[0, hl * HD_A:(hl + 1) * HD_A, pl.ds(off, nkeys)]
            pvs.append(jnp.dot(vj, ps[hl], preferred_element_type=F32))
        return tuple((stats[hl][0], stats[hl][2], stats[hl][1] * carry[hl][2] + pvs[hl])
                     for hl in range(MOBA_HS))

    def far_body(jp, carry):
        off = pl.multiple_of(jp * FAR_KEYS, FAR_KEYS)

        def extra(hl, s):
            r0 = rbf_ref[hl, pl.ds(2 * jp, 1), :]
            r1 = rbf_ref[hl, pl.ds(2 * jp + 1, 1), :]
            return jnp.concatenate([s[:MOBA_BLOCK] + r0, s[MOBA_BLOCK:] + r1], axis=0)

        return visit(carry, off, FAR_KEYS, extra)

    init = tuple((jnp.full((1, MOBA_BLOCK), -jnp.inf, F32), jnp.zeros((1, MOBA_BLOCK), F32),
                  jnp.zeros((HD_A, MOBA_BLOCK), F32)) for _ in range(MOBA_HS))
    carry = lax.fori_loop(0, qi // 2, far_body, init)
    js = jnp.maximum(qi - 1, 0)
    none_prev = jnp.where(qi >= 1, 0.0, NEG_INF)
    carry = visit(carry, pl.multiple_of(js * MOBA_BLOCK, MOBA_BLOCK), MOBA_BLOCK,
                  lambda hl, s: s + bias_ref[hl, 1] + (rbs_ref[hl, pl.ds(js, 1), :] + none_prev))
    carry = visit(carry, pl.multiple_of(qi * MOBA_BLOCK, MOBA_BLOCK), MOBA_BLOCK,
                  lambda hl, s: s + bias_ref[hl, 0])
    for pr in range(MOBA_HS // 2):
        outs = [carry[2 * pr + hh][2] * (1.0 / carry[2 * pr + hh][1]) for hh in range(2)]
        o = jnp.concatenate(outs, axis=0).T
        pc = slice(pr * 128, (pr + 1) * 128)
        o_ref[0, :, pc] = o * _silu(ga_ref[0, :, pc])


def _moba_prompt(rel_bias, qa, kbf, vt, kmean, bias_t, ga):
    nb, s, _ = qa.shape
    nq = s // MOBA_BLOCK
    w = MOBA_HS * HD_A
    tile = pl.BlockSpec((1, MOBA_BLOCK, w), lambda b, hg, i: (b, i, hg))
    return pl.pallas_call(
        _moba_p_kernel,
        grid=(nb, H_A // MOBA_HS, nq),
        in_specs=[pl.BlockSpec(memory_space=pltpu.SMEM),
                  tile,
                  pl.BlockSpec((1, s, w), lambda b, hg, i: (b, 0, hg)),
                  pl.BlockSpec((1, w, s), lambda b, hg, i: (b, hg, 0)),
                  pl.BlockSpec((1, nq, w), lambda b, hg, i: (b, 0, hg)),
                  pl.BlockSpec((MOBA_HS, 2, MOBA_BLOCK, MOBA_BLOCK), lambda b, hg, i: (hg, 0, 0, 0)),
                  tile],
        out_specs=tile,
        out_shape=jax.ShapeDtypeStruct((nb, s, 512), F32),
        scratch_shapes=[pltpu.VMEM((MOBA_HS, nq, MOBA_BLOCK), F32),
                        pltpu.VMEM((MOBA_HS, nq, MOBA_BLOCK), F32),
                        pltpu.VMEM((MOBA_HS, MOBA_BLOCK, 128), BF16)],
        compiler_params=_cparams(3),
        name="moba_prompt",
    )(rel_bias, qa, kbf, vt, kmean, bias_t, ga)


N_PAST_BLK = PAST_LEN // MOBA_BLOCK
N_PAGES = PAST_LEN // PAGE_SIZE
N_ROWS_S = 32


def _moba_s_kernel(pt_ref, qrep_ref, knew_ref, vnew_ref, ga_ref, bias_ref, bfar_ref, ck_hbm, cv_hbm,
                   o_ref, kt_buf, vt_buf, kpad, vpad, sems):
    b = pl.program_id(0)
    nseq = pl.num_programs(0)
    slot = b % 2
    t_new = knew_ref.shape[1]

    def page_copies(seq, sl):
        cps = []
        for p in range(N_PAGES):
            dst = pl.ds(p * PAGE_SIZE, PAGE_SIZE)
            cps.append(pltpu.make_async_copy(ck_hbm.at[pt_ref[seq, p]], kt_buf.at[sl, :, dst], sems.at[0, sl]))
            cps.append(pltpu.make_async_copy(cv_hbm.at[pt_ref[seq, p]], vt_buf.at[sl, :, dst], sems.at[1, sl]))
        return cps

    @pl.when(b == 0)
    def _():
        kpad[...] = jnp.zeros(kpad.shape, F32)
        vpad[...] = jnp.zeros(vpad.shape, F32)
        for cp in page_copies(0, 0):
            cp.start()

    @pl.when(b + 1 < nseq)
    def _():
        for cp in page_copies(b + 1, 1 - slot):
            cp.start()

    rowh = lax.broadcasted_iota(jnp.int32, (N_ROWS_S, W_A), 0) % H_A
    laneh = lax.broadcasted_iota(jnp.int32, (N_ROWS_S, W_A), 1) // HD_A
    own_head = rowh == laneh
    qf = jnp.where(own_head, qrep_ref[0], 0.0) * (HD_A ** -0.5)
    qbd = _bf(qf)
    q2 = jnp.concatenate([qbd, _bf(qf - qbd.astype(F32))], axis=0)
    kpad[0:t_new, :] = knew_ref[0]
    vpad[0:t_new, :] = vnew_ref[0]

    for cp in page_copies(b, slot):
        cp.wait()

    blocks = [slice(n * MOBA_BLOCK, (n + 1) * MOBA_BLOCK) for n in range(N_PAST_BLK)]
    s_past = []
    for n in range(N_PAST_BLK):
        s2 = jnp.dot(q2, _bf(kt_buf[slot, :, blocks[n]]), preferred_element_type=F32)
        s_past.append(s2[0:N_ROWS_S] + s2[N_ROWS_S:])
    s_own = _dg(qbd, _bf(kpad[...]), 1, 1) + bias_ref[:, MOBA_BLOCK:]
    g = [jnp.sum(s, axis=1, keepdims=True) for s in s_past]
    sel = [jnp.zeros((N_ROWS_S, 1), jnp.bool_) for _ in range(N_PAST_BLK)]
    for _ in range(MOBA_TOPK):
        m = functools.reduce(jnp.maximum, g)
        idx = functools.reduce(jnp.minimum, [jnp.where(g[n] == m, n, N_PAST_BLK) for n in range(N_PAST_BLK)])
        for n in range(N_PAST_BLK):
            pick = idx == n
            sel[n] = jnp.logical_or(sel[n], pick)
            g[n] = jnp.where(pick, -jnp.inf, g[n])
    bfar = bfar_ref[...]
    for n in range(N_PAST_BLK):
        if n < N_PAST_BLK - 1:
            s_past[n] = s_past[n] + jnp.where(sel[n], bfar, NEG_INF)
        else:
            s_past[n] = s_past[n] + bias_ref[:, 0:MOBA_BLOCK] + jnp.where(sel[n], 0.0, NEG_INF)
    m = jnp.max(s_own, axis=1, keepdims=True)
    for s in s_past:
        m = jnp.maximum(m, jnp.max(s, axis=1, keepdims=True))
    p_own = jnp.exp(s_own - m)
    l = jnp.sum(p_own, axis=1, keepdims=True)
    acc = jnp.dot(_bf(p_own), _bf(vpad[...]), preferred_element_type=F32)
    for n in range(N_PAST_BLK):
        p = jnp.exp(s_past[n] - m)
        l = l + jnp.sum(p, axis=1, keepdims=True)
        acc = acc + _dg(_bf(p), _bf(vt_buf[slot, :, blocks[n]]), 1, 1)
    o = jnp.where(own_head, acc * (1.0 / l), 0.0)
    o = jnp.sum(o.reshape(t_new, H_A, W_A), axis=1)
    o_ref[0] = o * _silu(ga_ref[0])


def _moba_sample(page_table, qrep, knew, vnew, ga, bias_s, bfar, cache_kt, cache_vt):
    nseq, t_new, _ = knew.shape
    tok = pl.BlockSpec((1, t_new, W_A), lambda b, pt: (b, 0, 0))
    grid_spec = pltpu.PrefetchScalarGridSpec(
        num_scalar_prefetch=1,
        grid=(nseq,),
        in_specs=[pl.BlockSpec((1, N_ROWS_S, W_A), lambda b, pt: (b, 0, 0)),
                  tok, tok, tok,
                  pl.BlockSpec(bias_s.shape, lambda b, pt: (0, 0)),
                  pl.BlockSpec((N_ROWS_S, 1), lambda b, pt: (0, 0)),
                  pl.BlockSpec(memory_space=pl.ANY),
                  pl.BlockSpec(memory_space=pl.ANY)],
        out_specs=tok,
        scratch_shapes=[pltpu.VMEM((2, W_A, PAST_LEN), F32),
                        pltpu.VMEM((2, W_A, PAST_LEN), F32),
                        pltpu.VMEM((PAGE_SIZE, W_A), F32),
                        pltpu.VMEM((PAGE_SIZE, W_A), F32),
                        pltpu.SemaphoreType.DMA((2, 2))],
    )
    return pl.pallas_call(
        _moba_s_kernel,
        grid_spec=grid_spec,
        out_shape=jax.ShapeDtypeStruct((nseq, t_new, W_A), F32),
        compiler_params=_cparams(1),
        name="moba_sample",
    )(page_table, qrep, knew, vnew, ga, bias_s, bfar, cache_kt, cache_vt)


def _pad_rows(ref, scratch, t):
    if t == CHUNK:
        return ref[0]
    scratch[...] = jnp.zeros(scratch.shape, scratch.dtype)
    scratch[0:t, :] = ref[0]
    return scratch[...]


def _ret_kernel(q_ref, k_ref, v_ref, g_ref, st0_ref, dmat_ref, qdec_ref, kdec_ref, gl_ref,
                o_ref, st_ref, qpad, kpad, vpad, gpad, *, t, lq):
    c = pl.program_id(1)

    @pl.when(c == 0)
    def _():
        st_ref[0] = st0_ref[0]

    q = _pad_rows(q_ref, qpad, t)[0:lq]
    k = _pad_rows(k_ref, kpad, t)
    v = _pad_rows(v_ref, vpad, t)
    g = _pad_rows(g_ref, gpad, t)[0:lq]
    lane = lax.broadcasted_iota(jnp.int32, (1, 128), 1)
    rowsel = lax.broadcasted_iota(jnp.int32, (128, 1), 0) < DK_B
    outs = []
    for hp in range(H_B // 2):
        cols = slice(hp * 128, (hp + 1) * 128)
        kp = k[:, cols]
        qp = q[:, cols]
        st = st_ref[0, cols, :]
        kd = kp * kdec_ref[:, cols]
        upd = []
        for hh in range(2):
            h = 2 * hp + hh
            qm = jnp.where((lane // DK_B) == hh, qp, 0.0)
            sc = _mm_nt(qm, kp) * dmat_ref[h, 0:lq, :]
            vh = v[:, h * DV_B:(h + 1) * DV_B]
            o = _mm(sc, vh) + _mm(qm * qdec_ref[0:lq, cols], st)
            o = o * lax.rsqrt(jnp.mean(o * o, axis=-1, keepdims=True) + EPS)
            outs.append(o * _silu(g[:, h * DV_B:(h + 1) * DV_B]))
            upd.append(_mm_tn(kd, vh))
        st_ref[0, cols, :] = st * gl_ref[cols, :] + jnp.where(rowsel, upd[0], upd[1])
    o_ref[0] = jnp.concatenate(outs, axis=1)[0:t]


def _retention(q, k, v, g, st0, dmat, qdec, kdec, gl, lq):
    nb, nc, t, _ = q.shape
    row = lambda c_: pl.BlockSpec((None, 1, t, c_), lambda b, c: (b, c, 0, 0))
    const = lambda a: pl.BlockSpec(a.shape, lambda b, c: (0,) * a.ndim)
    st_spec = pl.BlockSpec((1, H_B * DK_B, DV_B), lambda b, c: (b, 0, 0))
    return pl.pallas_call(
        functools.partial(_ret_kernel, t=t, lq=lq),
        grid=(nb, nc),
        in_specs=[row(256), row(256), row(512), row(512), st_spec,
                  const(dmat), const(qdec), const(kdec), const(gl)],
        out_specs=(row(512), st_spec),
        out_shape=(jax.ShapeDtypeStruct((nb, nc, t, 512), F32),
                   jax.ShapeDtypeStruct((nb, H_B * DK_B, DV_B), F32)),
        scratch_shapes=[pltpu.VMEM((CHUNK, 256), F32), pltpu.VMEM((CHUNK, 256), F32),
                        pltpu.VMEM((CHUNK, 512), F32), pltpu.VMEM((CHUNK, 512), F32)],
        compiler_params=_cparams(2),
        name="retention",
    )(q, k, v, g, st0, dmat, qdec, kdec, gl)


def _out_kernel(a_ref, b_ref, x_ref, g_ref, w_ref, o_ref):
    half = w_ref.shape[0] // 2
    y = (jnp.dot(_bf(a_ref[0]), w_ref[0:half, :], preferred_element_type=F32)
         + jnp.dot(_bf(b_ref[0]), w_ref[half:, :], preferred_element_type=F32))
    o_ref[0] = x_ref[0] + g_ref[0] * y


def _out_proj(a, b, x, gate, w_bf, per_row_mod):
    nb, s, d = x.shape
    tm = 512
    if per_row_mod:
        g_spec = pl.BlockSpec((1, tm, d), lambda bb, i: (bb, i, 0))
    else:
        g_spec = pl.BlockSpec((1, 1, d), lambda bb, i: (bb, 0, 0))
    row = lambda c: pl.BlockSpec((1, tm, c), lambda bb, i: (bb, i, 0))
    return pl.pallas_call(
        _out_kernel,
        grid=(nb, s // tm),
        in_specs=[row(512), row(512), row(d), g_spec, pl.BlockSpec(w_bf.shape, lambda bb, i: (0, 0))],
        out_specs=row(d),
        out_shape=jax.ShapeDtypeStruct((nb, s, d), F32),
        compiler_params=_cparams(2),
        name="out_proj",
    )(a, b, x, gate, w_bf)


def _odd_in_kernel(x_ref, sc_ref, sh_ref, nw_ref, w_ref, sguw_ref, sgub_ref, dtb_ref,
                   oc_ref, zg_ref, xbc_ref, dt_ref, v_ref):
    x = x_ref[0]
    tm = x.shape[0]
    ms = jnp.mean(x * x, axis=-1, keepdims=True)
    h = (x * lax.rsqrt(ms + EPS) * nw_ref[...]) * (1.0 + sc_ref[0]) + sh_ref[0]
    hb = _bf(h)

    def proj(lo, hi):
        return jnp.dot(hb, w_ref[:, lo:hi], preferred_element_type=F32)

    u = _gelu_tanh(proj(0, 512))
    v = _gelu_tanh(proj(512, 1024))
    mu = jnp.mean(v, axis=-1, keepdims=True)
    vc = v - mu
    v = vc * lax.rsqrt(jnp.mean(vc * vc, axis=-1, keepdims=True) + EPS)
    v_ref[0] = v
    ii = lax.broadcasted_iota(jnp.int32, (CHUNK, CHUNK), 0)
    jj = lax.broadcasted_iota(jnp.int32, (CHUNK, CHUNK), 1)
    rows = []
    for ci in range(tm // CHUNK):
        cols = []
        for g in range(G_C):
            wg = jnp.where(ii >= jj, sguw_ref[g], 0.0)
            cols.append(_mm(wg, v[ci * CHUNK:(ci + 1) * CHUNK, g * 128:(g + 1) * 128]))
        rows.append(jnp.concatenate(cols, axis=1) + sgub_ref[...])
    sg = jnp.concatenate(rows, axis=0) if len(rows) > 1 else rows[0]
    oc_ref[0] = u * sg * _silu(proj(1024, 1536))
    zg_ref[0] = proj(1536, 2048)
    xbc_ref[0] = proj(2048, 3072)
    dt_ref[0] = _softplus(proj(3072, 3584) + dtb_ref[...])


def _odd_in(x, scale, shift, norm_w, w_bf, sgu_w, sgu_b_tab, dt_bias, per_row_mod):
    nb, s, d = x.shape
    tm = 256
    if per_row_mod:
        mod_spec = pl.BlockSpec((1, tm, d), lambda b, i: (b, i, 0))
    else:
        mod_spec = pl.BlockSpec((1, 1, d), lambda b, i: (b, 0, 0))
    row = lambda c: pl.BlockSpec((1, tm, c), lambda b, i: (b, i, 0))
    const = lambda shp: pl.BlockSpec(shp, lambda b, i: (0,) * len(shp))
    return pl.pallas_call(
        _odd_in_kernel,
        grid=(nb, s // tm),
        in_specs=[row(d), mod_spec, mod_spec, const((1, d)), const(w_bf.shape),
                  const(sgu_w.shape), const(sgu_b_tab.shape), const((1, 512))],
        out_specs=(row(512), row(512), row(1024), row(512), row(512)),
        out_shape=(jax.ShapeDtypeStruct((nb, s, 512), F32),
                   jax.ShapeDtypeStruct((nb, s, 512), F32),
                   jax.ShapeDtypeStruct((nb, s, 1024), F32),
                   jax.ShapeDtypeStruct((nb, s, 512), F32),
                   jax.ShapeDtypeStruct((nb, s, 512), F32)),
        compiler_params=_cparams(2),
        name="odd_in",
    )(x, scale, shift, norm_w, w_bf, sgu_w, sgu_b_tab, dt_bias)


def _ssd_kernel(xbc_ref, dt_ref, zg_ref, tail_ref, st0_ref, cw_ref, cb_ref, alog_ref, dsk_ref, nw_ref,
                tri_ref, sel_ref, y_ref, st_ref, ext, dtpad, zpad, *, t, lq, nc):
    c = pl.program_id(1)

    @pl.when(c == 0)
    def _():
        st_ref[0] = st0_ref[0]
        ext[...] = jnp.zeros(ext.shape, F32)
        ext[0:8, :] = tail_ref[0]

    ext[8:8 + t, :] = xbc_ref[0]
    conv = cb_ref[...]
    for w in range(CONV_W):
        conv = conv + ext[pl.ds(8 - (CONV_W - 1) + w, CHUNK), :] * cw_ref[w:w + 1, :]
    if nc > 1:
        ext[0:8, :] = ext[CHUNK:CHUNK + 8, :]
    xc = _silu(conv)
    xh = xc[:, 0:W_D]
    dt = _pad_rows(dt_ref, dtpad, t)
    zg = _pad_rows(zg_ref, zpad, t)[0:lq]
    a = dt * (-jnp.exp(alog_ref[...]))
    cum = _mm_exact_lhs(tri_ref[...], a, 1, 0)
    last = cum[CHUNK - 1:CHUNK, :]
    dtx = xh * dt
    xw = xh * (jnp.exp(last - cum) * dt)
    ecum = jnp.exp(cum)
    elast = jnp.exp(last)
    cum_rows = _mm_exact_lhs(sel_ref[...], cum, 1, 1)
    ii = lax.broadcasted_iota(jnp.int32, (lq, CHUNK), 0)
    jj = lax.broadcasted_iota(jnp.int32, (lq, CHUNK), 1)
    lane = lax.broadcasted_iota(jnp.int32, (1, 128), 1)
    hpg = H_D // G_D
    ys = []
    for g in range(G_D):
        bg = xc[:, W_D + g * N_D:W_D + (g + 1) * N_D]
        cg = xc[0:lq, W_D + G_D * N_D + g * N_D:W_D + G_D * N_D + (g + 1) * N_D]
        cb = _mm_nt(cg, bg)
        gr = slice(g * hpg * P_D, (g + 1) * hpg * P_D)
        stg = st_ref[0, gr, :]
        yoff = _mm_nt(cg, stg)
        for pr in range(hpg // 2):
            l0 = g * hpg * P_D + pr * 128
            dtxp = dtx[:, l0:l0 + 128]
            yh = []
            for hh in range(2):
                h = g * hpg + pr * 2 + hh
                col = jnp.broadcast_to(cum[0:lq, h * P_D:h * P_D + 1], (lq, CHUNK))
                seg = jnp.minimum(col - cum_rows[h:h + 1, :], 0.0)
                mh = jnp.where(ii >= jj, cb * jnp.exp(seg), 0.0)
                yh.append(_mm(mh, dtxp))
            ypair = jnp.where(lane < P_D, yh[0], yh[1])
            ys.append(ypair + yoff[:, pr * 128:(pr + 1) * 128] * ecum[0:lq, l0:l0 + 128])
        upd = _mm_tn(xw[:, gr], bg)
        for hl in range(hpg):
            h = g * hpg + hl
            r = slice(h * P_D, (h + 1) * P_D)
            dec = jnp.broadcast_to(elast[0:1, h * P_D:h * P_D + 1], (P_D, N_D))
            st_ref[0, r, :] = st_ref[0, r, :] * dec + upd[hl * P_D:(hl + 1) * P_D, :]
    y = jnp.concatenate(ys, axis=1)
    y = (y + xh[0:lq] * dsk_ref[...]) * _silu(zg)
    gw = W_D // G_D
    outs = []
    for g in range(G_D):
        yg = y[:, g * gw:(g + 1) * gw]
        outs.append(yg * lax.rsqrt(jnp.mean(yg * yg, axis=-1, keepdims=True) + EPS))
    y_ref[0] = (jnp.concatenate(outs, axis=1) * nw_ref[...])[0:t]


def _ssd(xbc, dt, zg, tail, st0, conv_w, conv_b, a_log, d_skip, norm_w, tri, sel, lq):
    nb, nc, t, _ = xbc.shape
    row = lambda c_: pl.BlockSpec((None, 1, t, c_), lambda b, c: (b, c, 0, 0))
    const = lambda a: pl.BlockSpec(a.shape, lambda b, c: (0,) * a.ndim)
    st_spec = pl.BlockSpec((1, H_D * P_D, N_D), lambda b, c: (b, 0, 0))
    return pl.pallas_call(
        functools.partial(_ssd_kernel, t=t, lq=lq, nc=nc),
        grid=(nb, nc),
        in_specs=[row(1024), row(512), row(512),
                  pl.BlockSpec((1, 8, CONV_DIM), lambda b, c: (b, 0, 0)), st_spec,
                  const(conv_w), const(conv_b), const(a_log), const(d_skip), const(norm_w),
                  const(tri), const(sel)],
        out_specs=(row(512), st_spec),
        out_shape=(jax.ShapeDtypeStruct((nb, nc, t, 512), F32),
                   jax.ShapeDtypeStruct((nb, H_D * P_D, N_D), F32)),
        scratch_shapes=[pltpu.VMEM((CHUNK + 8, CONV_DIM), F32),
                        pltpu.VMEM((CHUNK, 512), F32), pltpu.VMEM((CHUNK, 512), F32)],
        compiler_params=_cparams(2),
        name="ssd",
    )(xbc, dt, zg, tail, st0, conv_w, conv_b, a_log, d_skip, norm_w, tri, sel)


def _rotary_tables(pos):
    half = DK_B // 2
    inv = 1.0 / (10000.0 ** (jnp.arange(half, dtype=F32) / half))
    ang = pos.astype(F32)[:, None] * inv[None, :]
    cos, sin = jnp.cos(ang), jnp.sin(ang)
    cos_t = jnp.tile(jnp.concatenate([cos, cos], axis=1), (1, H_B))
    sin_t = jnp.tile(jnp.concatenate([-sin, sin], axis=1), (1, H_B))
    return cos_t, sin_t


def _retention_tables(chunk_len):
    log_g = np.log(1.0 - 2.0 ** (-5.0 - np.arange(H_B, dtype=np.float64)))
    idx = np.arange(CHUNK, dtype=np.float64)
    diff = idx[:, None] - idx[None, :]
    dmat = np.where(diff[None] >= 0, np.exp(np.maximum(diff, 0.0)[None] * log_g[:, None, None]), 0.0)
    qdec = np.exp((idx + 1.0)[:, None] * log_g[None, :])
    kdec = np.where(idx[:, None] < chunk_len, np.exp((chunk_len - 1.0 - idx)[:, None] * log_g[None, :]), 0.0)
    gl = np.exp(chunk_len * log_g)
    return (jnp.asarray(dmat, F32),
            jnp.asarray(np.repeat(qdec, DK_B, axis=1), F32),
            jnp.asarray(np.repeat(kdec, DK_B, axis=1), F32),
            jnp.asarray(np.repeat(np.repeat(gl, DK_B)[:, None], DV_B, axis=1), F32))


def _prompt_bias_idx():
    kk = np.arange(MOBA_BLOCK)[:, None]
    qq = np.arange(MOBA_BLOCK)[None, :]
    diag = np.where(qq >= kk, _t5_bucket_np(qq - kk), -1)
    sub = _t5_bucket_np(qq + MOBA_BLOCK - kk)
    return np.concatenate([diag, sub], axis=0).astype(np.int32)


def _sample_bias_idx(t_new):
    row_t = (np.arange(N_ROWS_S) // H_A)[:, None]
    qpos = PAST_LEN + row_t
    near = _t5_bucket_np(qpos - (PAST_LEN - MOBA_BLOCK + np.arange(MOBA_BLOCK))[None, :])
    own_k = np.arange(PAGE_SIZE)[None, :]
    own = np.where((own_k <= row_t) & (own_k < t_new), _t5_bucket_np(row_t - own_k), -1)
    return np.concatenate([near, own], axis=1).astype(np.int32)


def kernel(x_prompt, x_sample, cache_k, cache_v, state_ret, state_ssm, state_conv, page_table, c_prompt, c_sample, rel_bias, e_norm_w, e_ada_w, e_ada_b, e_in_w, e_q_norm_w, e_k_norm_w, e_out_w, o_norm_w, o_ada_w, o_ada_b, o_in_w, o_sgu_w, o_sgu_b, o_conv_w, o_conv_b, o_dt_bias, o_A_log, o_D, o_ssm_norm_w, o_out_w):
    bp, s_len, d = x_prompt.shape
    bs, t_len, _ = x_sample.shape
    n_s = bs * t_len

    c_all = jnp.concatenate([c_prompt, c_sample, jnp.zeros((8 - (bp + bs) % 8, d), F32)], axis=0)
    mods = []
    for ada_w, ada_b in ((e_ada_w[0], e_ada_b[0]), (o_ada_w[0], o_ada_b[0])):
        mod = _ada_mod(c_all, ada_w, ada_b)
        parts_p = [mod[:bp, i * d:(i + 1) * d].reshape(bp, 1, d) for i in range(3)]
        parts_s = [jnp.repeat(mod[bp:bp + bs, i * d:(i + 1) * d], t_len, axis=0).reshape(1, n_s, d) for i in range(3)]
        mods.append((parts_p, parts_s))
    (e_mod_p, e_mod_s), (o_mod_p, o_mod_s) = mods

    seg = jnp.asarray(np.kron(np.eye(H_A), np.ones((HD_A, HD_A))), BF16)
    qnw = jnp.tile(e_q_norm_w[0], H_A).reshape(1, W_A)
    knw = jnp.tile(e_k_norm_w[0], H_A).reshape(1, W_A)
    e_in_bf = _bf(e_in_w[0])
    e_out_bf = _bf(e_out_w[0])
    o_in_bf = _bf(jnp.concatenate([o_in_w[0][:, :3072], jnp.repeat(o_in_w[0][:, 3072:], P_D, axis=1)], axis=1))
    o_out_bf = _bf(o_out_w[0])
    x_s = x_sample.reshape(1, n_s, d)
    cos_p, sin_p = _rotary_tables(jnp.arange(s_len))
    cos_s, sin_s = _rotary_tables(PAST_LEN + (jnp.arange(n_s) % t_len))
    bias_p = _bias_tables(rel_bias, _prompt_bias_idx()).reshape(H_A, 2, MOBA_BLOCK, MOBA_BLOCK)
    bias_s_h = _bias_tables(rel_bias, _sample_bias_idx(t_len))
    row_h = jnp.arange(N_ROWS_S) % H_A
    bias_s = jnp.sum(jnp.where((jnp.arange(H_A)[:, None] == row_h[None, :])[:, :, None], bias_s_h, 0.0), axis=0)
    bfar = rel_bias[NUM_BUCKETS - 1, row_h].reshape(N_ROWS_S, 1)

    (qa, ka, va, kbf, vt, ga, qb, kb, vb, gb, kmean) = _even_in(
        x_prompt, e_mod_p[1], e_mod_p[0], e_norm_w[0].reshape(1, d), e_in_bf, qnw, knw, seg, cos_p, sin_p, False)
    oa = _moba_prompt(rel_bias, qa, kbf, vt, kmean.reshape(bp, s_len // MOBA_BLOCK, W_A), bias_p, ga)
    nc_p = s_len // CHUNK
    ch = lambda a: a.reshape(bp, nc_p, CHUNK, a.shape[-1])
    ob, ret_p = _retention(ch(qb), ch(kb), ch(vb), ch(gb), jnp.zeros((bp, H_B * DK_B, DV_B), F32),
                           *_retention_tables(CHUNK), lq=CHUNK)
    xp1 = _out_proj(oa, ob.reshape(bp, s_len, W_B), x_prompt, e_mod_p[2], e_out_bf, False)
    k_prompt = ka.reshape(1, bp, s_len, H_A, HD_A)
    v_prompt = va.reshape(1, bp, s_len, H_A, HD_A)
    ret_state_prompt = ret_p.reshape(1, bp, H_B, DK_B, DV_B)

    (qa_s, ka_s, va_s, _, _, ga_s, qb_s, kb_s, vb_s, gb_s, _) = _even_in(
        x_s, e_mod_s[1], e_mod_s[0], e_norm_w[0].reshape(1, d), e_in_bf, qnw, knw, seg, cos_s, sin_s, True)
    sq = lambda a: a.reshape(bs, t_len, a.shape[-1])
    qrep = jnp.repeat(sq(qa_s), H_A, axis=1)
    n_phys = cache_k.shape[1]
    page_t = lambda c: jnp.transpose(c[0], (0, 2, 3, 1)).reshape(n_phys, W_A, PAGE_SIZE)
    oa_s = _moba_sample(page_table, qrep, sq(ka_s), sq(va_s), sq(ga_s), bias_s, bfar,
                        page_t(cache_k), page_t(cache_v))
    sc = lambda a: a.reshape(bs, 1, t_len, a.shape[-1])
    ob_s, ret_s = _retention(sc(qb_s), sc(kb_s), sc(vb_s), sc(gb_s),
                             state_ret[0].reshape(bs, H_B * DK_B, DV_B), *_retention_tables(t_len), lq=8)
    xs1 = _out_proj(oa_s.reshape(1, n_s, W_A), ob_s.reshape(1, n_s, W_B), x_s, e_mod_s[2], e_out_bf, True)
    k_sample = ka_s.reshape(1, bs, t_len, H_A, HD_A)
    v_sample = va_s.reshape(1, bs, t_len, H_A, HD_A)
    ret_state_sample = ret_s.reshape(1, bs, H_B, DK_B, DV_B)

    tri = jnp.asarray(np.tril(np.ones((CHUNK, CHUNK))), BF16)
    sel = jnp.asarray(np.kron(np.eye(H_D), np.eye(1, P_D)), BF16)
    rep = lambda a: jnp.repeat(a, P_D).reshape(1, W_D)
    dt_bias, a_log, d_skip = rep(o_dt_bias[0]), rep(o_A_log[0]), rep(o_D[0])
    ssm_nw = o_ssm_norm_w[0].reshape(1, W_D)
    conv_b = o_conv_b[0].reshape(1, CONV_DIM)
    o_nw = o_norm_w[0].reshape(1, d)

    sgu_b_p = jnp.repeat(o_sgu_b[0].T, W_C // G_C, axis=1)
    oc, zg, xbc, dtp, _ = _odd_in(xp1, o_mod_p[1], o_mod_p[0], o_nw, o_in_bf, o_sgu_w[0], sgu_b_p, dt_bias, False)
    yn, ssm_p = _ssd(ch(xbc), ch(dtp), ch(zg), jnp.zeros((bp, 8, CONV_DIM), F32),
                     jnp.zeros((bp, H_D * P_D, N_D), F32), o_conv_w[0], conv_b, a_log, d_skip, ssm_nw,
                     tri, sel, lq=CHUNK)
    y_prompt = _out_proj(oc, yn.reshape(bp, s_len, W_D), xp1, o_mod_p[2], o_out_bf, False)
    ssm_state_prompt = ssm_p.reshape(1, bp, H_D, P_D, N_D)
    conv_state_prompt = xbc[:, -(CONV_W - 1):][None]

    per_chunk = CHUNK // t_len
    w_small = o_sgu_w[0][:, :t_len, :t_len]
    sgu_w_s = jax.vmap(lambda w: jnp.kron(jnp.eye(per_chunk, dtype=F32), w))(w_small)
    sgu_b_s = jnp.repeat(jnp.tile(o_sgu_b[0][:, :t_len].T, (per_chunk, 1)), W_C // G_C, axis=1)
    oc_s, zg_s, xbc_s, dt_s, v_s = _odd_in(xs1, o_mod_s[1], o_mod_s[0], o_nw, o_in_bf, sgu_w_s, sgu_b_s, dt_bias, True)
    tail_s = jnp.concatenate([jnp.zeros((bs, 8 - (CONV_W - 1), CONV_DIM), F32), state_conv[0]], axis=1)
    yn_s, ssm_s = _ssd(sc(xbc_s), sc(dt_s), sc(zg_s), tail_s, state_ssm[0].reshape(bs, H_D * P_D, N_D),
                       o_conv_w[0], conv_b, a_log, d_skip, ssm_nw, tri, sel, lq=8)
    xs2 = _out_proj(oc_s, yn_s.reshape(1, n_s, W_D), xs1, o_mod_s[2], o_out_bf, True)
    y_sample = xs2.reshape(bs, t_len, d)
    sgu_v_sample = v_s.reshape(1, bs, t_len, W_C)
    ssm_state_sample = ssm_s.reshape(1, bs, H_D, P_D, N_D)
    xin = jnp.concatenate([state_conv[0], xbc_s.reshape(bs, t_len, CONV_DIM)], axis=1)
    conv_state_sample = xin[:, -(CONV_W - 1):][None]

    return (y_prompt, y_sample, k_prompt, v_prompt, k_sample, v_sample, ret_state_prompt, ret_state_sample,
            sgu_v_sample, ssm_state_prompt, ssm_state_sample, conv_state_prompt, conv_state_sample)
```

```python
import functools
import math

import numpy as np
import jax
import jax.numpy as jnp
from jax import lax
from jax.experimental import pallas as pl
from jax.experimental.pallas import tpu as pltpu

F32 = jnp.float32
BF16 = jnp.bfloat16

D_MODEL = 1024
PAST_LEN = 2048
PAGE_SIZE = 128
H_A, HD_A, W_A = 8, 64, 512
MOBA_BLOCK = 256
MOBA_TOPK = 3
NUM_BUCKETS = 32
MAX_DISTANCE = 128
H_B, DK_B, DV_B, W_B = 4, 64, 128, 512
G_C, W_C = 4, 512
H_D, P_D, N_D, G_D, W_D = 8, 64, 128, 2, 512
CONV_W = 4
CONV_DIM = 1024
CHUNK = 128
SEQ_PER_STEP = 8
NEG_INF = -1e30
EPS = 1e-6
VMEM_LIMIT = 56 * 1024 * 1024


def _bf(x):
    return x.astype(BF16)


def _dg(a, b, ca, cb):
    return lax.dot_general(a, b, (((ca,), (cb,)), ((), ())), preferred_element_type=F32)


def _mm(a, b):
    return _dg(_bf(a), _bf(b), 1, 0)


def _mm_nt(a, b):
    return _dg(_bf(a), _bf(b), 1, 1)


def _mm_tn(a, b):
    return _dg(_bf(a), _bf(b), 0, 0)


def _split2(x):
    hi = _bf(x)
    return hi, _bf(x - hi.astype(F32))


def _split3(x):
    hi = _bf(x)
    r = x - hi.astype(F32)
    mid = _bf(r)
    return hi, mid, _bf(r - mid.astype(F32))


def _mm_hp(a, b, ca, cb):
    ah, al = _split2(a)
    bh, bl = _split2(b)
    return _dg(ah, bh, ca, cb) + (_dg(ah, bl, ca, cb) + _dg(al, bh, ca, cb))


def _mm_exact_lhs(e, x, ca, cb):
    h, m, l = _split3(x)
    return _dg(e, h, ca, cb) + (_dg(e, m, ca, cb) + _dg(e, l, ca, cb))


def _silu(x):
    return x * (1.0 / (1.0 + jnp.exp(-x)))


def _gelu_tanh(x):
    return 0.5 * x * (1.0 + jnp.tanh(math.sqrt(2.0 / math.pi) * (x + 0.044715 * (x * x * x))))


def _softplus(x):
    return jnp.maximum(x, 0.0) + jnp.log1p(jnp.exp(-jnp.abs(x)))


def _cparams(n_grid):
    return pltpu.CompilerParams(dimension_semantics=("arbitrary",) * n_grid,
                                vmem_limit_bytes=VMEM_LIMIT)


def _top3_rows(g, blk, nblk):
    sel = jnp.zeros(g.shape, jnp.bool_)
    for _ in range(MOBA_TOPK):
        m = jnp.max(g, axis=0, keepdims=True)
        idx = jnp.min(jnp.where(g == m, blk, nblk), axis=0, keepdims=True)
        pick = blk == idx
        sel = jnp.logical_or(sel, pick)
        g = jnp.where(pick, -jnp.inf, g)
    return sel


def _ada_kernel(c_ref, w_ref, b_ref, o_ref):
    s = _silu(c_ref[...])
    o_ref[...] = _mm_hp(s, w_ref[...], 1, 0) + b_ref[...]


def _ada_mod(c_all, w, b):
    m, d = c_all.shape
    n = w.shape[1]
    tn = 512
    return pl.pallas_call(
        _ada_kernel,
        grid=(n // tn,),
        in_specs=[pl.BlockSpec((m, d), lambda j: (0, 0)),
                  pl.BlockSpec((d, tn), lambda j: (0, j)),
                  pl.BlockSpec((1, tn), lambda j: (0, j))],
        out_specs=pl.BlockSpec((m, tn), lambda j: (0, j)),
        out_shape=jax.ShapeDtypeStruct((m, n), F32),
        compiler_params=_cparams(1),
        name="ada_mod",
    )(c_all, w, b.reshape(1, n))


def _t5_bucket_np(rel):
    n = np.maximum(rel, 0)
    max_exact = NUM_BUCKETS // 2
    nf = np.maximum(n, 1).astype(np.float64)
    large = max_exact + (np.log(nf / max_exact) / math.log(MAX_DISTANCE / max_exact)
                         * (NUM_BUCKETS - max_exact)).astype(np.int64)
    large = np.minimum(large, NUM_BUCKETS - 1)
    return np.where(n < max_exact, n, large).astype(np.int32)


def _bias_kernel(tab_ref, idx_ref, o_ref):
    h = pl.program_id(0)
    idx = idx_ref[...]
    acc = jnp.where(idx == -1, NEG_INF, 0.0).astype(F32)
    for b in range(NUM_BUCKETS):
        acc = jnp.where(idx == b, tab_ref[b, h], acc)
    o_ref[0] = acc


def _bias_tables(rel_bias, idx):
    r, c = idx.shape
    return pl.pallas_call(
        _bias_kernel,
        grid=(H_A,),
        in_specs=[pl.BlockSpec(memory_space=pltpu.SMEM),
                  pl.BlockSpec((r, c), lambda h: (0, 0))],
        out_specs=pl.BlockSpec((1, r, c), lambda h: (h, 0, 0)),
        out_shape=jax.ShapeDtypeStruct((H_A, r, c), F32),
        compiler_params=_cparams(1),
        name="t5_bias",
    )(rel_bias, jnp.asarray(idx))


def _even_in_kernel(x_ref, sc_ref, sh_ref, nw_ref, w_ref, qnw_ref, knw_ref, seg_ref, cos_ref, sin_ref,
                    qa_ref, ka_ref, va_ref, kbf_ref, vt_ref, ga_ref, qb_ref, kb_ref, vb_ref, gb_ref, km_ref):
    x = x_ref[0]
    ms = jnp.mean(x * x, axis=-1, keepdims=True)
    h = (x * lax.rsqrt(ms + EPS) * nw_ref[...]) * (1.0 + sc_ref[0]) + sh_ref[0]
    hb = _bf(h)

    def proj(lo, hi):
        return jnp.dot(hb, w_ref[:, lo:hi], preferred_element_type=F32)

    def head_rms(t, w_row):
        ss = jnp.dot(_bf(t * t), seg_ref[...], preferred_element_type=F32)
        return t * lax.rsqrt(ss * (1.0 / HD_A) + EPS) * w_row

    qa_ref[0] = head_rms(proj(0, 512), qnw_ref[...])
    ka = head_rms(proj(512, 1024), knw_ref[...])
    ka_ref[0] = ka
    kbf_ref[0] = _bf(ka)
    km_ref[0, 0] = jnp.mean(ka, axis=0, keepdims=True)
    va = proj(1024, 1536)
    va_ref[0] = va
    vt_ref[0] = _bf(va.T)
    ga_ref[0] = proj(1536, 2048)

    lane = lax.broadcasted_iota(jnp.int32, (1, 256), 1) % DK_B
    first_half = lane < (DK_B // 2)
    cos = cos_ref[...]
    sin = sin_ref[...]

    def rotary(t):
        up = pltpu.roll(t, 256 - DK_B // 2, 1)
        dn = pltpu.roll(t, DK_B // 2, 1)
        return t * cos + jnp.where(first_half, up, dn) * sin

    qb_ref[0] = rotary(proj(2048, 2304))
    kb_ref[0] = rotary(proj(2304, 2560)) * (DK_B ** -0.5)
    vb_ref[0] = proj(2560, 3072)
    gb_ref[0] = proj(3072, 3584)


def _even_in(x, scale, shift, norm_w, w_bf, qnw, knw, seg, cos, sin, per_row_mod):
    nb, s, d = x.shape
    tm = MOBA_BLOCK
    ns = s // tm
    if per_row_mod:
        mod_spec = pl.BlockSpec((1, tm, d), lambda b, i: (b, i, 0))
    else:
        mod_spec = pl.BlockSpec((1, 1, d), lambda b, i: (b, 0, 0))
    row = lambda c: pl.BlockSpec((1, tm, c), lambda b, i: (b, i, 0))
    const = lambda shp: pl.BlockSpec(shp, lambda b, i: (0,) * len(shp))
    out_shape = (
        jax.ShapeDtypeStruct((nb, s, 512), F32),
        jax.ShapeDtypeStruct((nb, s, 512), F32),
        jax.ShapeDtypeStruct((nb, s, 512), F32),
        jax.ShapeDtypeStruct((nb, s, 512), BF16),
        jax.ShapeDtypeStruct((nb, 512, s), BF16),
        jax.ShapeDtypeStruct((nb, s, 512), F32),
        jax.ShapeDtypeStruct((nb, s, 256), F32),
        jax.ShapeDtypeStruct((nb, s, 256), F32),
        jax.ShapeDtypeStruct((nb, s, 512), F32),
        jax.ShapeDtypeStruct((nb, s, 512), F32),
        jax.ShapeDtypeStruct((nb, ns, 1, 512), F32),
    )
    out_specs = (row(512), row(512), row(512), row(512),
                 pl.BlockSpec((1, 512, tm), lambda b, i: (b, 0, i)),
                 row(512), row(256), row(256), row(512), row(512),
                 pl.BlockSpec((1, 1, 1, 512), lambda b, i: (b, i, 0, 0)))
    return pl.pallas_call(
        _even_in_kernel,
        grid=(nb, ns),
        in_specs=[row(d), mod_spec, mod_spec, const((1, d)), const((d, 3584)),
                  const((1, 512)), const((1, 512)), const((512, 512)),
                  pl.BlockSpec((tm, 256), lambda b, i: (i, 0)),
                  pl.BlockSpec((tm, 256), lambda b, i: (i, 0))],
        out_specs=out_specs,
        out_shape=out_shape,
        compiler_params=_cparams(2),
        name="even_in",
    )(x, scale, shift, norm_w, w_bf, qnw, knw, seg, cos, sin)


MOBA_HS = 8
FAR_KEYS = 2 * MOBA_BLOCK


def _moba_p_kernel(tab_ref, q_ref, k_ref, vt_ref, km_ref, bias_ref, ga_ref, o_ref, rbf_ref, rbs_ref, qs_ref):
    hg = pl.program_id(1)
    qi = pl.program_id(2)
    nblk = km_ref.shape[1]
    lane = lax.broadcasted_iota(jnp.int32, (1, 128), 1)
    blk = lax.broadcasted_iota(jnp.int32, (nblk, MOBA_BLOCK), 0)
    for hl in range(MOBA_HS):
        pr, hh = divmod(hl, 2)
        pc = slice(pr * 128, (pr + 1) * 128)
        qm = jnp.where((lane // HD_A) == hh, q_ref[0, :, pc], 0.0)
        gate = _mm_hp(km_ref[0, :, pc], qm, 1, 1)
        gate = jnp.where(blk < qi, gate, NEG_INF)
        sel = jnp.logical_and(_top3_rows(gate, blk, nblk), blk < qi)
        far_c = tab_ref[NUM_BUCKETS - 1, MOBA_HS * hg + hl]
        rbf_ref[hl] = jnp.where(jnp.logical_and(sel, blk < qi - 1), far_c, NEG_INF)
        rbs_ref[hl] = jnp.where(sel, 0.0, NEG_INF)
        qs_ref[hl] = _bf(qm * (HD_A ** -0.5))

    def visit(carry, off, nkeys, extra_fn):
        ss = []
        for hl in range(MOBA_HS):
            pr = hl // 2
            kj = k_ref[0, pl.ds(off, nkeys), pr * 128:(pr + 1) * 128]
            ss.append(_dg(kj, qs_ref[hl], 1, 1))
        stats, ps = [], []
        for hl in range(MOBA_HS):
            m, l, _ = carry[hl]
            s = extra_fn(hl, ss[hl])
            mn = jnp.maximum(m, jnp.max(s, axis=0, keepdims=True))
            alpha = jnp.exp(m - mn)
            p = jnp.exp(s - mn)
            stats.append((mn, alpha, alpha * l + jnp.sum(p, axis=0, keepdims=True)))
            ps.append(_bf(p))
        pvs = []
        for hl in range(MOBA_HS):
            vj = vt_ref[0, hl * HD_A:(hl + 1) * HD_A, pl.ds(off, nkeys)]
            pvs.append(jnp.dot(vj, ps[hl], preferred_element_type=F32))
        return tuple((stats[hl][0], stats[hl][2], stats[hl][1] * carry[hl][2] + pvs[hl])
                     for hl in range(MOBA_HS))

    def far_body(jp, carry):
        off = pl.multiple_of(jp * FAR_KEYS, FAR_KEYS)

        def extra(hl, s):
            r0 = rbf_ref[hl, pl.ds(2 * jp, 1), :]
            r1 = rbf_ref[hl, pl.ds(2 * jp + 1, 1), :]
            return jnp.concatenate([s[:MOBA_BLOCK] + r0, s[MOBA_BLOCK:] + r1], axis=0)

        return visit(carry, off, FAR_KEYS, extra)

    init = tuple((jnp.full((1, MOBA_BLOCK), -jnp.inf, F32), jnp.zeros((1, MOBA_BLOCK), F32),
                  jnp.zeros((HD_A, MOBA_BLOCK), F32)) for _ in range(MOBA_HS))
    carry = lax.fori_loop(0, qi // 2, far_body, init)
    js = jnp.maximum(qi - 1, 0)
    none_prev = jnp.where(qi >= 1, 0.0, NEG_INF)
    carry = visit(carry, pl.multiple_of(js * MOBA_BLOCK, MOBA_BLOCK), MOBA_BLOCK,
                  lambda hl, s: s + bias_ref[hl, 1] + (rbs_ref[hl, pl.ds(js, 1), :] + none_prev))
    carry = visit(carry, pl.multiple_of(qi * MOBA_BLOCK, MOBA_BLOCK), MOBA_BLOCK,
                  lambda hl, s: s + bias_ref[hl, 0])
    for pr in range(MOBA_HS // 2):
        outs = [carry[2 * pr + hh][2] * (1.0 / carry[2 * pr + hh][1]) for hh in range(2)]
        o = jnp.concatenate(outs, axis=0).T
        pc = slice(pr * 128, (pr + 1) * 128)
        o_ref[0, :, pc] = o * _silu(ga_ref[0, :, pc])


def _moba_prompt(rel_bias, qa, kbf, vt, kmean, bias_t, ga):
    nb, s, _ = qa.shape
    nq = s // MOBA_BLOCK
    w = MOBA_HS * HD_A
    tile = pl.BlockSpec((1, MOBA_BLOCK, w), lambda b, hg, i: (b, i, hg))
    return pl.pallas_call(
        _moba_p_kernel,
        grid=(nb, H_A // MOBA_HS, nq),
        in_specs=[pl.BlockSpec(memory_space=pltpu.SMEM),
                  tile,
                  pl.BlockSpec((1, s, w), lambda b, hg, i: (b, 0, hg)),
                  pl.BlockSpec((1, w, s), lambda b, hg, i: (b, hg, 0)),
                  pl.BlockSpec((1, nq, w), lambda b, hg, i: (b, 0, hg)),
                  pl.BlockSpec((MOBA_HS, 2, MOBA_BLOCK, MOBA_BLOCK), lambda b, hg, i: (hg, 0, 0, 0)),
                  tile],
        out_specs=tile,
        out_shape=jax.ShapeDtypeStruct((nb, s, 512), F32),
        scratch_shapes=[pltpu.VMEM((MOBA_HS, nq, MOBA_BLOCK), F32),
                        pltpu.VMEM((MOBA_HS, nq, MOBA_BLOCK), F32),
                        pltpu.VMEM((MOBA_HS, MOBA_BLOCK, 128), BF16)],
        compiler_params=_cparams(3),
        name="moba_prompt",
    )(rel_bias, qa, kbf, vt, kmean, bias_t, ga)


N_PAST_BLK = PAST_LEN // MOBA_BLOCK
N_PAGES = PAST_LEN // PAGE_SIZE
N_ROWS_S = 32


def _moba_s_kernel(pt_ref, qrep_ref, knew_ref, vnew_ref, ga_ref, bias_ref, bfar_ref, ck_hbm, cv_hbm,
                   o_ref, kt_buf, vt_buf, kpad, vpad, sems):
    b = pl.program_id(0)
    nseq = pl.num_programs(0)
    slot = b % 2
    t_new = knew_ref.shape[1]

    def page_copies(seq, sl):
        cps = []
        for p in range(N_PAGES):
            dst = pl.ds(p * PAGE_SIZE, PAGE_SIZE)
            cps.append(pltpu.make_async_copy(ck_hbm.at[pt_ref[seq, p]], kt_buf.at[sl, :, dst], sems.at[0, sl]))
            cps.append(pltpu.make_async_copy(cv_hbm.at[pt_ref[seq, p]], vt_buf.at[sl, :, dst], sems.at[1, sl]))
        return cps

    @pl.when(b == 0)
    def _():
        kpad[...] = jnp.zeros(kpad.shape, F32)
        vpad[...] = jnp.zeros(vpad.shape, F32)
        for cp in page_copies(0, 0):
            cp.start()

    @pl.when(b + 1 < nseq)
    def _():
        for cp in page_copies(b + 1, 1 - slot):
            cp.start()

    rowh = lax.broadcasted_iota(jnp.int32, (N_ROWS_S, W_A), 0) % H_A
    laneh = lax.broadcasted_iota(jnp.int32, (N_ROWS_S, W_A), 1) // HD_A
    own_head = rowh == laneh
    qf = jnp.where(own_head, qrep_ref[0], 0.0) * (HD_A ** -0.5)
    qbd = _bf(qf)
    q2 = jnp.concatenate([qbd, _bf(qf - qbd.astype(F32))], axis=0)
    kpad[0:t_new, :] = knew_ref[0]
    vpad[0:t_new, :] = vnew_ref[0]

    for cp in page_copies(b, slot):
        cp.wait()

    blocks = [slice(n * MOBA_BLOCK, (n + 1) * MOBA_BLOCK) for n in range(N_PAST_BLK)]
    s_past = []
    for n in range(N_PAST_BLK):
        s2 = jnp.dot(q2, _bf(kt_buf[slot, :, blocks[n]]), preferred_element_type=F32)
        s_past.append(s2[0:N_ROWS_S] + s2[N_ROWS_S:])
    s_own = _dg(qbd, _bf(kpad[...]), 1, 1) + bias_ref[:, MOBA_BLOCK:]
    g = [jnp.sum(s, axis=1, keepdims=True) for s in s_past]
    sel = [jnp.zeros((N_ROWS_S, 1), jnp.bool_) for _ in range(N_PAST_BLK)]
    for _ in range(MOBA_TOPK):
        m = functools.reduce(jnp.maximum, g)
        idx = functools.reduce(jnp.minimum, [jnp.where(g[n] == m, n, N_PAST_BLK) for n in range(N_PAST_BLK)])
        for n in range(N_PAST_BLK):
            pick = idx == n
            sel[n] = jnp.logical_or(sel[n], pick)
            g[n] = jnp.where(pick, -jnp.inf, g[n])
    bfar = bfar_ref[...]
    for n in range(N_PAST_BLK):
        if n < N_PAST_BLK - 1:
            s_past[n] = s_past[n] + jnp.where(sel[n], bfar, NEG_INF)
        else:
            s_past[n] = s_past[n] + bias_ref[:, 0:MOBA_BLOCK] + jnp.where(sel[n], 0.0, NEG_INF)
    m = jnp.max(s_own, axis=1, keepdims=True)
    for s in s_past:
        m = jnp.maximum(m, jnp.max(s, axis=1, keepdims=True))
    p_own = jnp.exp(s_own - m)
    l = jnp.sum(p_own, axis=1, keepdims=True)
    acc = jnp.dot(_bf(p_own), _bf(vpad[...]), preferred_element_type=F32)
    for n in range(N_PAST_BLK):
        p = jnp.exp(s_past[n] - m)
        l = l + jnp.sum(p, axis=1, keepdims=True)
        acc = acc + _dg(_bf(p), _bf(vt_buf[slot, :, blocks[n]]), 1, 1)
    o = jnp.where(own_head, acc * (1.0 / l), 0.0)
    o = jnp.sum(o.reshape(t_new, H_A, W_A), axis=1)
    o_ref[0] = o * _silu(ga_ref[0])


def _moba_sample(page_table, qrep, knew, vnew, ga, bias_s, bfar, cache_kt, cache_vt):
    nseq, t_new, _ = knew.shape
    tok = pl.BlockSpec((1, t_new, W_A), lambda b, pt: (b, 0, 0))
    grid_spec = pltpu.PrefetchScalarGridSpec(
        num_scalar_prefetch=1,
        grid=(nseq,),
        in_specs=[pl.BlockSpec((1, N_ROWS_S, W_A), lambda b, pt: (b, 0, 0)),
                  tok, tok, tok,
                  pl.BlockSpec(bias_s.shape, lambda b, pt: (0, 0)),
                  pl.BlockSpec((N_ROWS_S, 1), lambda b, pt: (0, 0)),
                  pl.BlockSpec(memory_space=pl.ANY),
                  pl.BlockSpec(memory_space=pl.ANY)],
        out_specs=tok,
        scratch_shapes=[pltpu.VMEM((2, W_A, PAST_LEN), F32),
                        pltpu.VMEM((2, W_A, PAST_LEN), F32),
                        pltpu.VMEM((PAGE_SIZE, W_A), F32),
                        pltpu.VMEM((PAGE_SIZE, W_A), F32),
                        pltpu.SemaphoreType.DMA((2, 2))],
    )
    return pl.pallas_call(
        _moba_s_kernel,
        grid_spec=grid_spec,
        out_shape=jax.ShapeDtypeStruct((nseq, t_new, W_A), F32),
        compiler_params=_cparams(1),
        name="moba_sample",
    )(page_table, qrep, knew, vnew, ga, bias_s, bfar, cache_kt, cache_vt)


def _pad_rows(ref, scratch, s, t):
    if t == CHUNK:
        return ref[s]
    scratch[s] = jnp.zeros(scratch.shape[1:], scratch.dtype)
    scratch[s, 0:t, :] = ref[s]
    return scratch[s]


def _ret_kernel(q_ref, k_ref, v_ref, g_ref, st0_ref, dmat_ref, qdec_ref, kdec_ref, gl_ref,
                o_ref, st_ref, qpad, kpad, vpad, gpad, *, t, lq, nbs):
    c = pl.program_id(1)

    @pl.when(c == 0)
    def _():
        st_ref[...] = st0_ref[...]

    lane = lax.broadcasted_iota(jnp.int32, (1, 128), 1)
    rowsel = lax.broadcasted_iota(jnp.int32, (128, 1), 0) < DK_B
    work = []
    for s in range(nbs):
        q = _pad_rows(q_ref, qpad, s, t)[0:lq]
        k = _pad_rows(k_ref, kpad, s, t)
        v = _pad_rows(v_ref, vpad, s, t)
        for hp in range(H_B // 2):
            cols = slice(hp * 128, (hp + 1) * 128)
            kp = _bf(k[:, cols])
            st = st_ref[s, cols, :]
            kd = _bf(k[:, cols] * kdec_ref[:, cols])
            for hh in range(2):
                h = 2 * hp + hh
                qm = jnp.where((lane // DK_B) == hh, q[:, cols], 0.0)
                vh = _bf(v[:, h * DV_B:(h + 1) * DV_B])
                sc = _dg(_bf(qm), kp, 1, 1)
                so = _mm(qm * qdec_ref[0:lq, cols], st)
                upd = _dg(kd, vh, 0, 0)
                work.append((s, hp, hh, sc, so, upd, vh, st))
    outs = {}
    for (s, hp, hh, sc, so, upd, vh, st) in work:
        h = 2 * hp + hh
        o = jnp.dot(_bf(sc * dmat_ref[h, 0:lq, :]), vh, preferred_element_type=F32) + so
        outs[(s, h)] = o * lax.rsqrt(jnp.mean(o * o, axis=-1, keepdims=True) + EPS)
    for i in range(0, len(work), 2):
        s, hp, _, _, _, upd0, _, st = work[i]
        cols = slice(hp * 128, (hp + 1) * 128)
        st_ref[s, cols, :] = st * gl_ref[cols, :] + jnp.where(rowsel, upd0, work[i + 1][5])
    for s in range(nbs):
        g = _pad_rows(g_ref, gpad, s, t)[0:lq]
        o = jnp.concatenate([outs[(s, h)] for h in range(H_B)], axis=1)
        o_ref[s] = (o * _silu(g))[0:t]


def _retention(q, k, v, g, st0, dmat, qdec, kdec, gl, lq, nbs):
    nb, nc, t, _ = q.shape
    row = lambda c_: pl.BlockSpec((nbs, None, t, c_), lambda b, c: (b, c, 0, 0))
    const = lambda a: pl.BlockSpec(a.shape, lambda b, c: (0,) * a.ndim)
    st_spec = pl.BlockSpec((nbs, H_B * DK_B, DV_B), lambda b, c: (b, 0, 0))
    pad = lambda c_: pltpu.VMEM((nbs, CHUNK, c_), F32)
    return pl.pallas_call(
        functools.partial(_ret_kernel, t=t, lq=lq, nbs=nbs),
        grid=(nb // nbs, nc),
        in_specs=[row(256), row(256), row(512), row(512), st_spec,
                  const(dmat), const(qdec), const(kdec), const(gl)],
        out_specs=(row(512), st_spec),
        out_shape=(jax.ShapeDtypeStruct((nb, nc, t, 512), F32),
                   jax.ShapeDtypeStruct((nb, H_B * DK_B, DV_B), F32)),
        scratch_shapes=[pad(256), pad(256), pad(512), pad(512)],
        compiler_params=_cparams(2),
        name="retention",
    )(q, k, v, g, st0, dmat, qdec, kdec, gl)


def _out_kernel(a_ref, b_ref, x_ref, g_ref, w_ref, o_ref):
    half = w_ref.shape[0] // 2
    y = (jnp.dot(_bf(a_ref[0]), w_ref[0:half, :], preferred_element_type=F32)
         + jnp.dot(_bf(b_ref[0]), w_ref[half:, :], preferred_element_type=F32))
    o_ref[0] = x_ref[0] + g_ref[0] * y


def _out_proj(a, b, x, gate, w_bf, per_row_mod):
    nb, s, d = x.shape
    tm = 512
    if per_row_mod:
        g_spec = pl.BlockSpec((1, tm, d), lambda bb, i: (bb, i, 0))
    else:
        g_spec = pl.BlockSpec((1, 1, d), lambda bb, i: (bb, 0, 0))
    row = lambda c: pl.BlockSpec((1, tm, c), lambda bb, i: (bb, i, 0))
    return pl.pallas_call(
        _out_kernel,
        grid=(nb, s // tm),
        in_specs=[row(512), row(512), row(d), g_spec, pl.BlockSpec(w_bf.shape, lambda bb, i: (0, 0))],
        out_specs=row(d),
        out_shape=jax.ShapeDtypeStruct((nb, s, d), F32),
        compiler_params=_cparams(2),
        name="out_proj",
    )(a, b, x, gate, w_bf)


def _odd_in_kernel(x_ref, sc_ref, sh_ref, nw_ref, w_ref, sguw_ref, sgub_ref, dtb_ref,
                   oc_ref, zg_ref, xbc_ref, dt_ref, v_ref):
    x = x_ref[0]
    tm = x.shape[0]
    ms = jnp.mean(x * x, axis=-1, keepdims=True)
    h = (x * lax.rsqrt(ms + EPS) * nw_ref[...]) * (1.0 + sc_ref[0]) + sh_ref[0]
    hb = _bf(h)

    def proj(lo, hi):
        return jnp.dot(hb, w_ref[:, lo:hi], preferred_element_type=F32)

    u = _gelu_tanh(proj(0, 512))
    v = _gelu_tanh(proj(512, 1024))
    mu = jnp.mean(v, axis=-1, keepdims=True)
    vc = v - mu
    v = vc * lax.rsqrt(jnp.mean(vc * vc, axis=-1, keepdims=True) + EPS)
    v_ref[0] = v
    ii = lax.broadcasted_iota(jnp.int32, (CHUNK, CHUNK), 0)
    jj = lax.broadcasted_iota(jnp.int32, (CHUNK, CHUNK), 1)
    rows = []
    for ci in range(tm // CHUNK):
        cols = []
        for g in range(G_C):
            wg = jnp.where(ii >= jj, sguw_ref[g], 0.0)
            cols.append(_mm(wg, v[ci * CHUNK:(ci + 1) * CHUNK, g * 128:(g + 1) * 128]))
        rows.append(jnp.concatenate(cols, axis=1) + sgub_ref[...])
    sg = jnp.concatenate(rows, axis=0) if len(rows) > 1 else rows[0]
    oc_ref[0] = u * sg * _silu(proj(1024, 1536))
    zg_ref[0] = proj(1536, 2048)
    xbc_ref[0] = proj(2048, 3072)
    dt_ref[0] = _softplus(proj(3072, 3584) + dtb_ref[...])


def _odd_in(x, scale, shift, norm_w, w_bf, sgu_w, sgu_b_tab, dt_bias, per_row_mod):
    nb, s, d = x.shape
    tm = 256
    if per_row_mod:
        mod_spec = pl.BlockSpec((1, tm, d), lambda b, i: (b, i, 0))
    else:
        mod_spec = pl.BlockSpec((1, 1, d), lambda b, i: (b, 0, 0))
    row = lambda c: pl.BlockSpec((1, tm, c), lambda b, i: (b, i, 0))
    const = lambda shp: pl.BlockSpec(shp, lambda b, i: (0,) * len(shp))
    return pl.pallas_call(
        _odd_in_kernel,
        grid=(nb, s // tm),
        in_specs=[row(d), mod_spec, mod_spec, const((1, d)), const(w_bf.shape),
                  const(sgu_w.shape), const(sgu_b_tab.shape), const((1, 512))],
        out_specs=(row(512), row(512), row(1024), row(512), row(512)),
        out_shape=(jax.ShapeDtypeStruct((nb, s, 512), F32),
                   jax.ShapeDtypeStruct((nb, s, 512), F32),
                   jax.ShapeDtypeStruct((nb, s, 1024), F32),
                   jax.ShapeDtypeStruct((nb, s, 512), F32),
                   jax.ShapeDtypeStruct((nb, s, 512), F32)),
        compiler_params=_cparams(2),
        name="odd_in",
    )(x, scale, shift, norm_w, w_bf, sgu_w, sgu_b_tab, dt_bias)


def _ssd_kernel(xbc_ref, dt_ref, zg_ref, tail_ref, st0_ref, cw_ref, cb_ref, alog_ref, dsk_ref, nw_ref,
                tri_ref, sel_ref, y_ref, st_ref, ext, dtpad, zpad, *, t, lq, nc, nbs):
    c = pl.program_id(1)

    @pl.when(c == 0)
    def _():
        st_ref[...] = st0_ref[...]
        ext[...] = jnp.zeros(ext.shape, F32)
        ext[:, 0:8, :] = tail_ref[...]

    ii = lax.broadcasted_iota(jnp.int32, (lq, CHUNK), 0)
    jj = lax.broadcasted_iota(jnp.int32, (lq, CHUNK), 1)
    lane = lax.broadcasted_iota(jnp.int32, (1, 128), 1)
    hpg = H_D // G_D
    neg_a = -jnp.exp(alog_ref[...])
    seqs = []
    for s in range(nbs):
        ext[s, 8:8 + t, :] = xbc_ref[s]
        conv = cb_ref[...]
        for w in range(CONV_W):
            conv = conv + ext[s, pl.ds(8 - (CONV_W - 1) + w, CHUNK), :] * cw_ref[w:w + 1, :]
        if nc > 1:
            ext[s, 0:8, :] = ext[s, CHUNK:CHUNK + 8, :]
        xc = _silu(conv)
        dt = _pad_rows(dt_ref, dtpad, s, t)
        cum = _mm_exact_lhs(tri_ref[...], dt * neg_a, 1, 0)
        seqs.append((xc, dt, cum))
    st1 = []
    for s in range(nbs):
        xc, dt, cum = seqs[s]
        xh = xc[:, 0:W_D]
        last = cum[CHUNK - 1:CHUNK, :]
        dtx = _bf(xh * dt)
        xw = _bf(xh * (jnp.exp(last - cum) * dt))
        cum_rows = _mm_exact_lhs(sel_ref[...], cum, 1, 1)
        per_g = []
        for g in range(G_D):
            bg = _bf(xc[:, W_D + g * N_D:W_D + (g + 1) * N_D])
            cg = _bf(xc[0:lq, W_D + G_D * N_D + g * N_D:W_D + G_D * N_D + (g + 1) * N_D])
            gr = slice(g * hpg * P_D, (g + 1) * hpg * P_D)
            cb = _dg(cg, bg, 1, 1)
            yoff = _dg(cg, _bf(st_ref[s, gr, :]), 1, 1)
            upd = _dg(xw[:, gr], bg, 0, 0)
            per_g.append((cb, yoff, upd))
        st1.append((dtx, cum_rows, per_g))
    for s in range(nbs):
        xc, dt, cum = seqs[s]
        dtx, cum_rows, per_g = st1[s]
        ecum = jnp.exp(cum[0:lq])
        elast = jnp.exp(cum[CHUNK - 1:CHUNK, :])
        ys = []
        for g in range(G_D):
            cb, yoff, upd = per_g[g]
            for pr in range(hpg // 2):
                l0 = g * hpg * P_D + pr * 128
                yh = []
                for hh in range(2):
                    h = g * hpg + pr * 2 + hh
                    col = jnp.broadcast_to(cum[0:lq, h * P_D:h * P_D + 1], (lq, CHUNK))
                    seg = jnp.minimum(col - cum_rows[h:h + 1, :], 0.0)
                    mh = jnp.where(ii >= jj, cb * jnp.exp(seg), 0.0)
                    yh.append(jnp.dot(_bf(mh), dtx[:, l0:l0 + 128], preferred_element_type=F32))
                ypair = jnp.where(lane < P_D, yh[0], yh[1])
                ys.append(ypair + yoff[:, pr * 128:(pr + 1) * 128] * ecum[:, l0:l0 + 128])
            for hl in range(hpg):
                h = g * hpg + hl
                r = slice(h * P_D, (h + 1) * P_D)
                dec = jnp.broadcast_to(elast[0:1, h * P_D:h * P_D + 1], (P_D, N_D))
                st_ref[s, r, :] = st_ref[s, r, :] * dec + upd[hl * P_D:(hl + 1) * P_D, :]
        y = jnp.concatenate(ys, axis=1)
        zg = _pad_rows(zg_ref, zpad, s, t)[0:lq]
        y = (y + xc[0:lq, 0:W_D] * dsk_ref[...]) * _silu(zg)
        gw = W_D // G_D
        outs = []
        for g in range(G_D):
            yg = y[:, g * gw:(g + 1) * gw]
            outs.append(yg * lax.rsqrt(jnp.mean(yg * yg, axis=-1, keepdims=True) + EPS))
        y_ref[s] = (jnp.concatenate(outs, axis=1) * nw_ref[...])[0:t]


def _ssd(xbc, dt, zg, tail, st0, conv_w, conv_b, a_log, d_skip, norm_w, tri, sel, lq, nbs):
    nb, nc, t, _ = xbc.shape
    row = lambda c_: pl.BlockSpec((nbs, None, t, c_), lambda b, c: (b, c, 0, 0))
    const = lambda a: pl.BlockSpec(a.shape, lambda b, c: (0,) * a.ndim)
    st_spec = pl.BlockSpec((nbs, H_D * P_D, N_D), lambda b, c: (b, 0, 0))
    return pl.pallas_call(
        functools.partial(_ssd_kernel, t=t, lq=lq, nc=nc, nbs=nbs),
        grid=(nb // nbs, nc),
        in_specs=[row(1024), row(512), row(512),
                  pl.BlockSpec((nbs, 8, CONV_DIM), lambda b, c: (b, 0, 0)), st_spec,
                  const(conv_w), const(conv_b), const(a_log), const(d_skip), const(norm_w),
                  const(tri), const(sel)],
        out_specs=(row(512), st_spec),
        out_shape=(jax.ShapeDtypeStruct((nb, nc, t, 512), F32),
                   jax.ShapeDtypeStruct((nb, H_D * P_D, N_D), F32)),
        scratch_shapes=[pltpu.VMEM((nbs, CHUNK + 8, CONV_DIM), F32),
                        pltpu.VMEM((nbs, CHUNK, 512), F32), pltpu.VMEM((nbs, CHUNK, 512), F32)],
        compiler_params=_cparams(2),
        name="ssd",
    )(xbc, dt, zg, tail, st0, conv_w, conv_b, a_log, d_skip, norm_w, tri, sel)


def _rotary_tables(pos):
    half = DK_B // 2
    inv = 1.0 / (10000.0 ** (jnp.arange(half, dtype=F32) / half))
    ang = pos.astype(F32)[:, None] * inv[None, :]
    cos, sin = jnp.cos(ang), jnp.sin(ang)
    cos_t = jnp.tile(jnp.concatenate([cos, cos], axis=1), (1, H_B))
    sin_t = jnp.tile(jnp.concatenate([-sin, sin], axis=1), (1, H_B))
    return cos_t, sin_t


def _retention_tables(chunk_len):
    log_g = np.log(1.0 - 2.0 ** (-5.0 - np.arange(H_B, dtype=np.float64)))
    idx = np.arange(CHUNK, dtype=np.float64)
    diff = idx[:, None] - idx[None, :]
    dmat = np.where(diff[None] >= 0, np.exp(np.maximum(diff, 0.0)[None] * log_g[:, None, None]), 0.0)
    qdec = np.exp((idx + 1.0)[:, None] * log_g[None, :])
    kdec = np.where(idx[:, None] < chunk_len, np.exp((chunk_len - 1.0 - idx)[:, None] * log_g[None, :]), 0.0)
    gl = np.exp(chunk_len * log_g)
    return (jnp.asarray(dmat, F32),
            jnp.asarray(np.repeat(qdec, DK_B, axis=1), F32),
            jnp.asarray(np.repeat(kdec, DK_B, axis=1), F32),
            jnp.asarray(np.repeat(np.repeat(gl, DK_B)[:, None], DV_B, axis=1), F32))


def _prompt_bias_idx():
    kk = np.arange(MOBA_BLOCK)[:, None]
    qq = np.arange(MOBA_BLOCK)[None, :]
    diag = np.where(qq >= kk, _t5_bucket_np(qq - kk), -1)
    sub = _t5_bucket_np(qq + MOBA_BLOCK - kk)
    return np.concatenate([diag, sub], axis=0).astype(np.int32)


def _sample_bias_idx(t_new):
    row_t = (np.arange(N_ROWS_S) // H_A)[:, None]
    qpos = PAST_LEN + row_t
    near = _t5_bucket_np(qpos - (PAST_LEN - MOBA_BLOCK + np.arange(MOBA_BLOCK))[None, :])
    own_k = np.arange(PAGE_SIZE)[None, :]
    own = np.where((own_k <= row_t) & (own_k < t_new), _t5_bucket_np(row_t - own_k), -1)
    return np.concatenate([near, own], axis=1).astype(np.int32)


def kernel(x_prompt, x_sample, cache_k, cache_v, state_ret, state_ssm, state_conv, page_table, c_prompt, c_sample, rel_bias, e_norm_w, e_ada_w, e_ada_b, e_in_w, e_q_norm_w, e_k_norm_w, e_out_w, o_norm_w, o_ada_w, o_ada_b, o_in_w, o_sgu_w, o_sgu_b, o_conv_w, o_conv_b, o_dt_bias, o_A_log, o_D, o_ssm_norm_w, o_out_w):
    bp, s_len, d = x_prompt.shape
    bs, t_len, _ = x_sample.shape
    n_s = bs * t_len

    c_all = jnp.concatenate([c_prompt, c_sample, jnp.zeros((8 - (bp + bs) % 8, d), F32)], axis=0)
    mods = []
    for ada_w, ada_b in ((e_ada_w[0], e_ada_b[0]), (o_ada_w[0], o_ada_b[0])):
        mod = _ada_mod(c_all, ada_w, ada_b)
        parts_p = [mod[:bp, i * d:(i + 1) * d].reshape(bp, 1, d) for i in range(3)]
        parts_s = [jnp.repeat(mod[bp:bp + bs, i * d:(i + 1) * d], t_len, axis=0).reshape(1, n_s, d) for i in range(3)]
        mods.append((parts_p, parts_s))
    (e_mod_p, e_mod_s), (o_mod_p, o_mod_s) = mods

    seg = jnp.asarray(np.kron(np.eye(H_A), np.ones((HD_A, HD_A))), BF16)
    qnw = jnp.tile(e_q_norm_w[0], H_A).reshape(1, W_A)
    knw = jnp.tile(e_k_norm_w[0], H_A).reshape(1, W_A)
    e_in_bf = _bf(e_in_w[0])
    e_out_bf = _bf(e_out_w[0])
    o_in_bf = _bf(jnp.concatenate([o_in_w[0][:, :3072], jnp.repeat(o_in_w[0][:, 3072:], P_D, axis=1)], axis=1))
    o_out_bf = _bf(o_out_w[0])
    x_s = x_sample.reshape(1, n_s, d)
    cos_p, sin_p = _rotary_tables(jnp.arange(s_len))
    cos_s, sin_s = _rotary_tables(PAST_LEN + (jnp.arange(n_s) % t_len))
    bias_p = _bias_tables(rel_bias, _prompt_bias_idx()).reshape(H_A, 2, MOBA_BLOCK, MOBA_BLOCK)
    bias_s_h = _bias_tables(rel_bias, _sample_bias_idx(t_len))
    row_h = jnp.arange(N_ROWS_S) % H_A
    bias_s = jnp.sum(jnp.where((jnp.arange(H_A)[:, None] == row_h[None, :])[:, :, None], bias_s_h, 0.0), axis=0)
    bfar = rel_bias[NUM_BUCKETS - 1, row_h].reshape(N_ROWS_S, 1)

    (qa, ka, va, kbf, vt, ga, qb, kb, vb, gb, kmean) = _even_in(
        x_prompt, e_mod_p[1], e_mod_p[0], e_norm_w[0].reshape(1, d), e_in_bf, qnw, knw, seg, cos_p, sin_p, False)
    oa = _moba_prompt(rel_bias, qa, kbf, vt, kmean.reshape(bp, s_len // MOBA_BLOCK, W_A), bias_p, ga)
    nc_p = s_len // CHUNK
    ch = lambda a: a.reshape(bp, nc_p, CHUNK, a.shape[-1])
    ob, ret_p = _retention(ch(qb), ch(kb), ch(vb), ch(gb), jnp.zeros((bp, H_B * DK_B, DV_B), F32),
                           *_retention_tables(CHUNK), lq=CHUNK, nbs=bp)
    xp1 = _out_proj(oa, ob.reshape(bp, s_len, W_B), x_prompt, e_mod_p[2], e_out_bf, False)
    k_prompt = ka.reshape(1, bp, s_len, H_A, HD_A)
    v_prompt = va.reshape(1, bp, s_len, H_A, HD_A)
    ret_state_prompt = ret_p.reshape(1, bp, H_B, DK_B, DV_B)

    (qa_s, ka_s, va_s, _, _, ga_s, qb_s, kb_s, vb_s, gb_s, _) = _even_in(
        x_s, e_mod_s[1], e_mod_s[0], e_norm_w[0].reshape(1, d), e_in_bf, qnw, knw, seg, cos_s, sin_s, True)
    sq = lambda a: a.reshape(bs, t_len, a.shape[-1])
    qrep = jnp.repeat(sq(qa_s), H_A, axis=1)
    n_phys = cache_k.shape[1]
    page_t = lambda c: jnp.transpose(c[0], (0, 2, 3, 1)).reshape(n_phys, W_A, PAGE_SIZE)
    oa_s = _moba_sample(page_table, qrep, sq(ka_s), sq(va_s), sq(ga_s), bias_s, bfar,
                        page_t(cache_k), page_t(cache_v))
    sc = lambda a: a.reshape(bs, 1, t_len, a.shape[-1])
    ob_s, ret_s = _retention(sc(qb_s), sc(kb_s), sc(vb_s), sc(gb_s),
                             state_ret[0].reshape(bs, H_B * DK_B, DV_B), *_retention_tables(t_len), lq=8, nbs=SEQ_PER_STEP)
    xs1 = _out_proj(oa_s.reshape(1, n_s, W_A), ob_s.reshape(1, n_s, W_B), x_s, e_mod_s[2], e_out_bf, True)
    k_sample = ka_s.reshape(1, bs, t_len, H_A, HD_A)
    v_sample = va_s.reshape(1, bs, t_len, H_A, HD_A)
    ret_state_sample = ret_s.reshape(1, bs, H_B, DK_B, DV_B)

    tri = jnp.asarray(np.tril(np.ones((CHUNK, CHUNK))), BF16)
    sel = jnp.asarray(np.kron(np.eye(H_D), np.eye(1, P_D)), BF16)
    rep = lambda a: jnp.repeat(a, P_D).reshape(1, W_D)
    dt_bias, a_log, d_skip = rep(o_dt_bias[0]), rep(o_A_log[0]), rep(o_D[0])
    ssm_nw = o_ssm_norm_w[0].reshape(1, W_D)
    conv_b = o_conv_b[0].reshape(1, CONV_DIM)
    o_nw = o_norm_w[0].reshape(1, d)

    sgu_b_p = jnp.repeat(o_sgu_b[0].T, W_C // G_C, axis=1)
    oc, zg, xbc, dtp, _ = _odd_in(xp1, o_mod_p[1], o_mod_p[0], o_nw, o_in_bf, o_sgu_w[0], sgu_b_p, dt_bias, False)
    yn, ssm_p = _ssd(ch(xbc), ch(dtp), ch(zg), jnp.zeros((bp, 8, CONV_DIM), F32),
                     jnp.zeros((bp, H_D * P_D, N_D), F32), o_conv_w[0], conv_b, a_log, d_skip, ssm_nw,
                     tri, sel, lq=CHUNK, nbs=bp)
    y_prompt = _out_proj(oc, yn.reshape(bp, s_len, W_D), xp1, o_mod_p[2], o_out_bf, False)
    ssm_state_prompt = ssm_p.reshape(1, bp, H_D, P_D, N_D)
    conv_state_prompt = xbc[:, -(CONV_W - 1):][None]

    per_chunk = CHUNK // t_len
    w_small = o_sgu_w[0][:, :t_len, :t_len]
    sgu_w_s = jax.vmap(lambda w: jnp.kron(jnp.eye(per_chunk, dtype=F32), w))(w_small)
    sgu_b_s = jnp.repeat(jnp.tile(o_sgu_b[0][:, :t_len].T, (per_chunk, 1)), W_C // G_C, axis=1)
    oc_s, zg_s, xbc_s, dt_s, v_s = _odd_in(xs1, o_mod_s[1], o_mod_s[0], o_nw, o_in_bf, sgu_w_s, sgu_b_s, dt_bias, True)
    tail_s = jnp.concatenate([jnp.zeros((bs, 8 - (CONV_W - 1), CONV_DIM), F32), state_conv[0]], axis=1)
    yn_s, ssm_s = _ssd(sc(xbc_s), sc(dt_s), sc(zg_s), tail_s, state_ssm[0].reshape(bs, H_D * P_D, N_D),
                       o_conv_w[0], conv_b, a_log, d_skip, ssm_nw, tri, sel, lq=8, nbs=SEQ_PER_STEP)
    xs2 = _out_proj(oc_s, yn_s.reshape(1, n_s, W_D), xs1, o_mod_s[2], o_out_bf, True)
    y_sample = xs2.reshape(bs, t_len, d)
    sgu_v_sample = v_s.reshape(1, bs, t_len, W_C)
    ssm_state_sample = ssm_s.reshape(1, bs, H_D, P_D, N_D)
    xin = jnp.concatenate([state_conv[0], xbc_s.reshape(bs, t_len, CONV_DIM)], axis=1)
    conv_state_sample = xin[:, -(CONV_W - 1):][None]

    return (y_prompt, y_sample, k_prompt, v_prompt, k_sample, v_sample, ret_state_prompt, ret_state_sample,
            sgu_v_sample, ssm_state_prompt, ssm_state_sample, conv_state_prompt, conv_state_sample)
```

```python
import functools
import math

import numpy as np
import jax
import jax.numpy as jnp
from jax import lax
from jax.experimental import pallas as pl
from jax.experimental.pallas import tpu as pltpu

F32 = jnp.float32
BF16 = jnp.bfloat16

D_MODEL = 1024
PAST_LEN = 2048
PAGE_SIZE = 128
H_A, HD_A, W_A = 8, 64, 512
MOBA_BLOCK = 256
MOBA_TOPK = 3
NUM_BUCKETS = 32
MAX_DISTANCE = 128
H_B, DK_B, DV_B, W_B = 4, 64, 128, 512
G_C, W_C = 4, 512
H_D, P_D, N_D, G_D, W_D = 8, 64, 128, 2, 512
CONV_W = 4
CONV_DIM = 1024
CHUNK = 128
SEQ_PER_STEP = 8
PROJ_ROWS = 512
OUT_ROWS = 1024
NEG_INF = -1e30
EPS = 1e-6
LOG2E = math.log2(math.e)
VMEM_LIMIT = 56 * 1024 * 1024


def _bf(x):
    return x.astype(BF16)


def _dg(a, b, ca, cb):
    return lax.dot_general(a, b, (((ca,), (cb,)), ((), ())), preferred_element_type=F32)


def _mm(a, b):
    return _dg(_bf(a), _bf(b), 1, 0)


def _mm_nt(a, b):
    return _dg(_bf(a), _bf(b), 1, 1)


def _mm_tn(a, b):
    return _dg(_bf(a), _bf(b), 0, 0)


def _split2(x):
    hi = _bf(x)
    return hi, _bf(x - hi.astype(F32))


def _split3(x):
    hi = _bf(x)
    r = x - hi.astype(F32)
    mid = _bf(r)
    return hi, mid, _bf(r - mid.astype(F32))


def _mm_hp(a, b, ca, cb):
    ah, al = _split2(a)
    bh, bl = _split2(b)
    return _dg(ah, bh, ca, cb) + (_dg(ah, bl, ca, cb) + _dg(al, bh, ca, cb))


def _mm_exact_lhs(e, x, ca, cb):
    h, m, l = _split3(x)
    return _dg(e, h, ca, cb) + (_dg(e, m, ca, cb) + _dg(e, l, ca, cb))


def _silu(x):
    return x * (1.0 / (1.0 + jnp.exp(-x)))


def _gelu_tanh(x):
    return 0.5 * x * (1.0 + jnp.tanh(math.sqrt(2.0 / math.pi) * (x + 0.044715 * (x * x * x))))


def _softplus(x):
    return jnp.maximum(x, 0.0) + jnp.log1p(jnp.exp(-jnp.abs(x)))


def _cparams(n_grid):
    return pltpu.CompilerParams(dimension_semantics=("arbitrary",) * n_grid,
                                vmem_limit_bytes=VMEM_LIMIT)


def _top3_rows(g, blk, nblk):
    sel = jnp.zeros(g.shape, jnp.bool_)
    for _ in range(MOBA_TOPK):
        m = jnp.max(g, axis=0, keepdims=True)
        idx = jnp.min(jnp.where(g == m, blk, nblk), axis=0, keepdims=True)
        pick = blk == idx
        sel = jnp.logical_or(sel, pick)
        g = jnp.where(pick, -jnp.inf, g)
    return sel


def _ada_kernel(c_ref, w_ref, b_ref, o_ref):
    s = _silu(c_ref[...])
    o_ref[...] = _mm_hp(s, w_ref[...], 1, 0) + b_ref[...]


def _ada_mod(c_all, w, b):
    m, d = c_all.shape
    n = w.shape[1]
    tn = 512
    return pl.pallas_call(
        _ada_kernel,
        grid=(n // tn,),
        in_specs=[pl.BlockSpec((m, d), lambda j: (0, 0)),
                  pl.BlockSpec((d, tn), lambda j: (0, j)),
                  pl.BlockSpec((1, tn), lambda j: (0, j))],
        out_specs=pl.BlockSpec((m, tn), lambda j: (0, j)),
        out_shape=jax.ShapeDtypeStruct((m, n), F32),
        compiler_params=_cparams(1),
        name="ada_mod",
    )(c_all, w, b.reshape(1, n))


def _t5_bucket_np(rel):
    n = np.maximum(rel, 0)
    max_exact = NUM_BUCKETS // 2
    nf = np.maximum(n, 1).astype(np.float64)
    large = max_exact + (np.log(nf / max_exact) / math.log(MAX_DISTANCE / max_exact)
                         * (NUM_BUCKETS - max_exact)).astype(np.int64)
    large = np.minimum(large, NUM_BUCKETS - 1)
    return np.where(n < max_exact, n, large).astype(np.int32)


def _bias_kernel(tab_ref, idx_ref, o_ref, *, scale):
    h = pl.program_id(0)
    idx = idx_ref[...]
    acc = jnp.zeros(idx.shape, F32)
    for b in range(NUM_BUCKETS):
        acc = jnp.where(idx == b, tab_ref[b, h], acc)
    o_ref[0] = jnp.where(idx == -1, NEG_INF, acc * scale)


def _bias_tables(rel_bias, idx, scale=1.0):
    r, c = idx.shape
    return pl.pallas_call(
        functools.partial(_bias_kernel, scale=scale),
        grid=(H_A,),
        in_specs=[pl.BlockSpec(memory_space=pltpu.SMEM),
                  pl.BlockSpec((r, c), lambda h: (0, 0))],
        out_specs=pl.BlockSpec((1, r, c), lambda h: (h, 0, 0)),
        out_shape=jax.ShapeDtypeStruct((H_A, r, c), F32),
        compiler_params=_cparams(1),
        name="t5_bias",
    )(rel_bias, jnp.asarray(idx))


def _even_in_kernel(x_ref, sc_ref, sh_ref, nw_ref, w_ref, qnw_ref, knw_ref, seg_ref, cos_ref, sin_ref,
                    qa_ref, ka_ref, va_ref, kbf_ref, vt_ref, ga_ref, qb_ref, kb_ref, vb_ref, gb_ref, km_ref):
    x = x_ref[0]
    ms = jnp.mean(x * x, axis=-1, keepdims=True)
    h = (x * lax.rsqrt(ms + EPS) * nw_ref[...]) * (1.0 + sc_ref[0]) + sh_ref[0]
    hb = _bf(h)

    def proj(lo, hi):
        return jnp.dot(hb, w_ref[:, lo:hi], preferred_element_type=F32)

    def head_rms(t, w_row):
        ss = jnp.dot(_bf(t * t), seg_ref[...], preferred_element_type=F32)
        return t * lax.rsqrt(ss * (1.0 / HD_A) + EPS) * w_row

    qa_ref[0] = head_rms(proj(0, 512), qnw_ref[...])
    ka = head_rms(proj(512, 1024), knw_ref[...])
    ka_ref[0] = ka
    kbf_ref[0] = _bf(ka)
    for j in range(ka.shape[0] // MOBA_BLOCK):
        km_ref[0, j] = jnp.mean(ka[j * MOBA_BLOCK:(j + 1) * MOBA_BLOCK], axis=0, keepdims=True)
    va = proj(1024, 1536)
    va_ref[0] = va
    vt_ref[0] = _bf(va.T)
    ga_ref[0] = proj(1536, 2048)

    lane = lax.broadcasted_iota(jnp.int32, (1, 256), 1) % DK_B
    first_half = lane < (DK_B // 2)
    cos = cos_ref[...]
    sin = sin_ref[...]

    def rotary(t):
        up = pltpu.roll(t, 256 - DK_B // 2, 1)
        dn = pltpu.roll(t, DK_B // 2, 1)
        return t * cos + jnp.where(first_half, up, dn) * sin

    qb_ref[0] = rotary(proj(2048, 2304))
    kb_ref[0] = rotary(proj(2304, 2560)) * (DK_B ** -0.5)
    vb_ref[0] = proj(2560, 3072)
    gb_ref[0] = proj(3072, 3584)


def _even_in(x, scale, shift, norm_w, w_bf, qnw, knw, seg, cos, sin, per_row_mod):
    nb, s, d = x.shape
    tm = PROJ_ROWS
    ns = s // tm
    nkb = tm // MOBA_BLOCK
    if per_row_mod:
        mod_spec = pl.BlockSpec((1, tm, d), lambda b, i: (b, i, 0))
    else:
        mod_spec = pl.BlockSpec((1, 1, d), lambda b, i: (b, 0, 0))
    row = lambda c: pl.BlockSpec((1, tm, c), lambda b, i: (b, i, 0))
    const = lambda shp: pl.BlockSpec(shp, lambda b, i: (0,) * len(shp))
    out_shape = (
        jax.ShapeDtypeStruct((nb, s, 512), F32),
        jax.ShapeDtypeStruct((nb, s, 512), F32),
        jax.ShapeDtypeStruct((nb, s, 512), F32),
        jax.ShapeDtypeStruct((nb, s, 512), BF16),
        jax.ShapeDtypeStruct((nb, 512, s), BF16),
        jax.ShapeDtypeStruct((nb, s, 512), F32),
        jax.ShapeDtypeStruct((nb, s, 256), F32),
        jax.ShapeDtypeStruct((nb, s, 256), F32),
        jax.ShapeDtypeStruct((nb, s, 512), F32),
        jax.ShapeDtypeStruct((nb, s, 512), F32),
        jax.ShapeDtypeStruct((nb, ns * nkb, 1, 512), F32),
    )
    out_specs = (row(512), row(512), row(512), row(512),
                 pl.BlockSpec((1, 512, tm), lambda b, i: (b, 0, i)),
                 row(512), row(256), row(256), row(512), row(512),
                 pl.BlockSpec((1, nkb, 1, 512), lambda b, i: (b, i, 0, 0)))
    return pl.pallas_call(
        _even_in_kernel,
        grid=(nb, ns),
        in_specs=[row(d), mod_spec, mod_spec, const((1, d)), const((d, 3584)),
                  const((1, 512)), const((1, 512)), const((512, 512)),
                  pl.BlockSpec((tm, 256), lambda b, i: (i, 0)),
                  pl.BlockSpec((tm, 256), lambda b, i: (i, 0))],
        out_specs=out_specs,
        out_shape=out_shape,
        compiler_params=_cparams(2),
        name="even_in",
    )(x, scale, shift, norm_w, w_bf, qnw, knw, seg, cos, sin)


MOBA_HS = 8
FAR_KEYS = 2 * MOBA_BLOCK


def _moba_p_kernel(tab_ref, q_ref, k_ref, vt_ref, km_ref, bias_ref, ga_ref, o_ref, rbf_ref, rbs_ref, qs_ref):
    hg = pl.program_id(1)
    qi = pl.program_id(2)
    nblk = km_ref.shape[1]
    lane = lax.broadcasted_iota(jnp.int32, (1, 128), 1)
    blk = lax.broadcasted_iota(jnp.int32, (nblk, MOBA_BLOCK), 0)
    for hl in range(MOBA_HS):
        pr, hh = divmod(hl, 2)
        pc = slice(pr * 128, (pr + 1) * 128)
        qm = jnp.where((lane // HD_A) == hh, q_ref[0, :, pc], 0.0)
        gate = _mm_hp(km_ref[0, :, pc], qm, 1, 1)
        gate = jnp.where(blk < qi, gate, NEG_INF)
        sel = jnp.logical_and(_top3_rows(gate, blk, nblk), blk < qi)
        far_c = tab_ref[NUM_BUCKETS - 1, MOBA_HS * hg + hl] * LOG2E
        rbf_ref[hl] = jnp.where(jnp.logical_and(sel, blk < qi - 1), far_c, NEG_INF)
        rbs_ref[hl] = jnp.where(jnp.logical_or(sel, blk == qi), 0.0, NEG_INF)
        qs_ref[hl] = _bf(qm * (HD_A ** -0.5 * LOG2E))

    def visit(carry, off, nkeys, extra_fn):
        ss = []
        for hl in range(MOBA_HS):
            pr = hl // 2
            kj = k_ref[0, pl.ds(off, nkeys), pr * 128:(pr + 1) * 128]
            ss.append(_dg(kj, qs_ref[hl], 1, 1))
        stats, ps = [], []
        for hl in range(MOBA_HS):
            m, l, _ = carry[hl]
            s = extra_fn(hl, ss[hl])
            mn = jnp.maximum(m, jnp.max(s, axis=0, keepdims=True))
            alpha = jnp.exp2(m - mn)
            p = jnp.exp2(s - mn)
            stats.append((mn, alpha, alpha * l + jnp.sum(p, axis=0, keepdims=True)))
            ps.append(_bf(p))
        pvs = []
        for hl in range(MOBA_HS):
            vj = vt_ref[0, hl * HD_A:(hl + 1) * HD_A, pl.ds(off, nkeys)]
            pvs.append(jnp.dot(vj, ps[hl], preferred_element_type=F32))
        return tuple((stats[hl][0], stats[hl][2], stats[hl][1] * carry[hl][2] + pvs[hl])
                     for hl in range(MOBA_HS))

    def far_body(jp, carry):
        off = pl.multiple_of(jp * FAR_KEYS, FAR_KEYS)

        def extra(hl, s):
            r0 = rbf_ref[hl, pl.ds(2 * jp, 1), :]
            r1 = rbf_ref[hl, pl.ds(2 * jp + 1, 1), :]
            return jnp.concatenate([s[:MOBA_BLOCK] + r0, s[MOBA_BLOCK:] + r1], axis=0)

        return visit(carry, off, FAR_KEYS, extra)

    init = tuple((jnp.full((1, MOBA_BLOCK), -jnp.inf, F32), jnp.zeros((1, MOBA_BLOCK), F32),
                  jnp.zeros((HD_A, MOBA_BLOCK), F32)) for _ in range(MOBA_HS))
    carry = lax.fori_loop(0, qi // 2, far_body, init)
    js = jnp.maximum(qi - 1, 0)
    first = qi == 0
    top_tab = jnp.where(first, 0, 1)
    bot_mask = jnp.where(first, NEG_INF, 0.0)

    def near_extra(hl, s):
        top = s[:MOBA_BLOCK] + bias_ref[hl, top_tab] + rbs_ref[hl, pl.ds(js, 1), :]
        bot = s[MOBA_BLOCK:] + (bias_ref[hl, 0] + bot_mask)
        return jnp.concatenate([top, bot], axis=0)

    carry = visit(carry, pl.multiple_of(js * MOBA_BLOCK, MOBA_BLOCK), FAR_KEYS, near_extra)
    for pr in range(MOBA_HS // 2):
        outs = [carry[2 * pr + hh][2] * (1.0 / carry[2 * pr + hh][1]) for hh in range(2)]
        o = jnp.concatenate(outs, axis=0).T
        pc = slice(pr * 128, (pr + 1) * 128)
        o_ref[0, :, pc] = o * _silu(ga_ref[0, :, pc])


def _moba_prompt(rel_bias, qa, kbf, vt, kmean, bias_t, ga):
    nb, s, _ = qa.shape
    nq = s // MOBA_BLOCK
    w = MOBA_HS * HD_A
    tile = pl.BlockSpec((1, MOBA_BLOCK, w), lambda b, hg, i: (b, i, hg))
    return pl.pallas_call(
        _moba_p_kernel,
        grid=(nb, H_A // MOBA_HS, nq),
        in_specs=[pl.BlockSpec(memory_space=pltpu.SMEM),
                  tile,
                  pl.BlockSpec((1, s, w), lambda b, hg, i: (b, 0, hg)),
                  pl.BlockSpec((1, w, s), lambda b, hg, i: (b, hg, 0)),
                  pl.BlockSpec((1, nq, w), lambda b, hg, i: (b, 0, hg)),
                  pl.BlockSpec((MOBA_HS, 2, MOBA_BLOCK, MOBA_BLOCK), lambda b, hg, i: (hg, 0, 0, 0)),
                  tile],
        out_specs=tile,
        out_shape=jax.ShapeDtypeStruct((nb, s, 512), F32),
        scratch_shapes=[pltpu.VMEM((MOBA_HS, nq, MOBA_BLOCK), F32),
                        pltpu.VMEM((MOBA_HS, nq, MOBA_BLOCK), F32),
                        pltpu.VMEM((MOBA_HS, MOBA_BLOCK, 128), BF16)],
        compiler_params=_cparams(3),
        name="moba_prompt",
    )(rel_bias, qa, kbf, vt, kmean, bias_t, ga)


N_PAST_BLK = PAST_LEN // MOBA_BLOCK
N_PAGES = PAST_LEN // PAGE_SIZE
N_ROWS_S = 32


def _moba_s_kernel(pt_ref, qrep_ref, knew_ref, vnew_ref, ga_ref, bias_ref, bfar_ref, ck_hbm, cv_hbm,
                   o_ref, kt_buf, vt_buf, kpad, vpad, sems):
    b = pl.program_id(0)
    nseq = pl.num_programs(0)
    slot = b % 2
    t_new = knew_ref.shape[1]

    def page_copies(seq, sl):
        cps = []
        for p in range(N_PAGES):
            dst = pl.ds(p * PAGE_SIZE, PAGE_SIZE)
            cps.append(pltpu.make_async_copy(ck_hbm.at[pt_ref[seq, p]], kt_buf.at[sl, :, dst], sems.at[0, sl]))
            cps.append(pltpu.make_async_copy(cv_hbm.at[pt_ref[seq, p]], vt_buf.at[sl, :, dst], sems.at[1, sl]))
        return cps

    @pl.when(b == 0)
    def _():
        kpad[...] = jnp.zeros(kpad.shape, F32)
        vpad[...] = jnp.zeros(vpad.shape, F32)
        for cp in page_copies(0, 0):
            cp.start()

    @pl.when(b + 1 < nseq)
    def _():
        for cp in page_copies(b + 1, 1 - slot):
            cp.start()

    rowh = lax.broadcasted_iota(jnp.int32, (N_ROWS_S, W_A), 0) % H_A
    laneh = lax.broadcasted_iota(jnp.int32, (N_ROWS_S, W_A), 1) // HD_A
    own_head = rowh == laneh
    qf = jnp.where(own_head, qrep_ref[0], 0.0) * (HD_A ** -0.5)
    qbd = _bf(qf)
    q2 = jnp.concatenate([qbd, _bf(qf - qbd.astype(F32))], axis=0)
    kpad[0:t_new, :] = knew_ref[0]
    vpad[0:t_new, :] = vnew_ref[0]

    for cp in page_copies(b, slot):
        cp.wait()

    blocks = [slice(n * MOBA_BLOCK, (n + 1) * MOBA_BLOCK) for n in range(N_PAST_BLK)]
    s_past = []
    for n in range(N_PAST_BLK):
        s2 = jnp.dot(q2, _bf(kt_buf[slot, :, blocks[n]]), preferred_element_type=F32)
        s_past.append(s2[0:N_ROWS_S] + s2[N_ROWS_S:])
    s_own = _dg(qbd, _bf(kpad[...]), 1, 1) + bias_ref[:, MOBA_BLOCK:]
    g = [jnp.sum(s, axis=1, keepdims=True) for s in s_past]
    sel = [jnp.zeros((N_ROWS_S, 1), jnp.bool_) for _ in range(N_PAST_BLK)]
    for _ in range(MOBA_TOPK):
        m = functools.reduce(jnp.maximum, g)
        idx = functools.reduce(jnp.minimum, [jnp.where(g[n] == m, n, N_PAST_BLK) for n in range(N_PAST_BLK)])
        for n in range(N_PAST_BLK):
            pick = idx == n
            sel[n] = jnp.logical_or(sel[n], pick)
            g[n] = jnp.where(pick, -jnp.inf, g[n])
    bfar = bfar_ref[...]
    for n in range(N_PAST_BLK):
        if n < N_PAST_BLK - 1:
            s_past[n] = s_past[n] + jnp.where(sel[n], bfar, NEG_INF)
        else:
            s_past[n] = s_past[n] + bias_ref[:, 0:MOBA_BLOCK] + jnp.where(sel[n], 0.0, NEG_INF)
    m = jnp.max(s_own, axis=1, keepdims=True)
    for s in s_past:
        m = jnp.maximum(m, jnp.max(s, axis=1, keepdims=True))
    p_own = jnp.exp(s_own - m)
    l = jnp.sum(p_own, axis=1, keepdims=True)
    acc = jnp.dot(_bf(p_own), _bf(vpad[...]), preferred_element_type=F32)
    for n in range(N_PAST_BLK):
        p = jnp.exp(s_past[n] - m)
        l = l + jnp.sum(p, axis=1, keepdims=True)
        acc = acc + _dg(_bf(p), _bf(vt_buf[slot, :, blocks[n]]), 1, 1)
    o = jnp.where(own_head, acc * (1.0 / l), 0.0)
    o = jnp.sum(o.reshape(t_new, H_A, W_A), axis=1)
    o_ref[0] = o * _silu(ga_ref[0])


def _moba_sample(page_table, qrep, knew, vnew, ga, bias_s, bfar, cache_kt, cache_vt):
    nseq, t_new, _ = knew.shape
    tok = pl.BlockSpec((1, t_new, W_A), lambda b, pt: (b, 0, 0))
    grid_spec = pltpu.PrefetchScalarGridSpec(
        num_scalar_prefetch=1,
        grid=(nseq,),
        in_specs=[pl.BlockSpec((1, N_ROWS_S, W_A), lambda b, pt: (b, 0, 0)),
                  tok, tok, tok,
                  pl.BlockSpec(bias_s.shape, lambda b, pt: (0, 0)),
                  pl.BlockSpec((N_ROWS_S, 1), lambda b, pt: (0, 0)),
                  pl.BlockSpec(memory_space=pl.ANY),
                  pl.BlockSpec(memory_space=pl.ANY)],
        out_specs=tok,
        scratch_shapes=[pltpu.VMEM((2, W_A, PAST_LEN), F32),
                        pltpu.VMEM((2, W_A, PAST_LEN), F32),
                        pltpu.VMEM((PAGE_SIZE, W_A), F32),
                        pltpu.VMEM((PAGE_SIZE, W_A), F32),
                        pltpu.SemaphoreType.DMA((2, 2))],
    )
    return pl.pallas_call(
        _moba_s_kernel,
        grid_spec=grid_spec,
        out_shape=jax.ShapeDtypeStruct((nseq, t_new, W_A), F32),
        compiler_params=_cparams(1),
        name="moba_sample",
    )(page_table, qrep, knew, vnew, ga, bias_s, bfar, cache_kt, cache_vt)


def _pad_rows(ref, scratch, s, t):
    if t == CHUNK:
        return ref[s]
    scratch[s] = jnp.zeros(scratch.shape[1:], scratch.dtype)
    scratch[s, 0:t, :] = ref[s]
    return scratch[s]


def _ret_kernel(q_ref, k_ref, v_ref, g_ref, st0_ref, dmat_ref, qdec_ref, kdec_ref, gl_ref,
                o_ref, st_ref, qpad, kpad, vpad, gpad, *, t, lq, nbs):
    c = pl.program_id(1)

    @pl.when(c == 0)
    def _():
        st_ref[...] = st0_ref[...]

    lane = lax.broadcasted_iota(jnp.int32, (1, 128), 1)
    rowsel = lax.broadcasted_iota(jnp.int32, (128, 1), 0) < DK_B
    work = []
    for s in range(nbs):
        q = _pad_rows(q_ref, qpad, s, t)[0:lq]
        k = _pad_rows(k_ref, kpad, s, t)
        v = _pad_rows(v_ref, vpad, s, t)
        for hp in range(H_B // 2):
            cols = slice(hp * 128, (hp + 1) * 128)
            kp = _bf(k[:, cols])
            st = st_ref[s, cols, :]
            kd = _bf(k[:, cols] * kdec_ref[:, cols])
            for hh in range(2):
                h = 2 * hp + hh
                qm = jnp.where((lane // DK_B) == hh, q[:, cols], 0.0)
                vh = _bf(v[:, h * DV_B:(h + 1) * DV_B])
                sc = _dg(_bf(qm), kp, 1, 1)
                so = _mm(qm * qdec_ref[0:lq, cols], st)
                upd = _dg(kd, vh, 0, 0)
                work.append((s, hp, hh, sc, so, upd, vh, st))
    outs = {}
    for (s, hp, hh, sc, so, upd, vh, st) in work:
        h = 2 * hp + hh
        o = jnp.dot(_bf(sc * dmat_ref[h, 0:lq, :]), vh, preferred_element_type=F32) + so
        outs[(s, h)] = o * lax.rsqrt(jnp.mean(o * o, axis=-1, keepdims=True) + EPS)
    for i in range(0, len(work), 2):
        s, hp, _, _, _, upd0, _, st = work[i]
        cols = slice(hp * 128, (hp + 1) * 128)
        st_ref[s, cols, :] = st * gl_ref[cols, :] + jnp.where(rowsel, upd0, work[i + 1][5])
    for s in range(nbs):
        g = _pad_rows(g_ref, gpad, s, t)[0:lq]
        o = jnp.concatenate([outs[(s, h)] for h in range(H_B)], axis=1)
        o_ref[s] = (o * _silu(g))[0:t]


def _retention(q, k, v, g, st0, dmat, qdec, kdec, gl, lq, nbs):
    nb, nc, t, _ = q.shape
    row = lambda c_: pl.BlockSpec((nbs, None, t, c_), lambda b, c: (b, c, 0, 0))
    const = lambda a: pl.BlockSpec(a.shape, lambda b, c: (0,) * a.ndim)
    st_spec = pl.BlockSpec((nbs, H_B * DK_B, DV_B), lambda b, c: (b, 0, 0))
    pad = lambda c_: pltpu.VMEM((nbs, CHUNK, c_), F32)
    return pl.pallas_call(
        functools.partial(_ret_kernel, t=t, lq=lq, nbs=nbs),
        grid=(nb // nbs, nc),
        in_specs=[row(256), row(256), row(512), row(512), st_spec,
                  const(dmat), const(qdec), const(kdec), const(gl)],
        out_specs=(row(512), st_spec),
        out_shape=(jax.ShapeDtypeStruct((nb, nc, t, 512), F32),
                   jax.ShapeDtypeStruct((nb, H_B * DK_B, DV_B), F32)),
        scratch_shapes=[pad(256), pad(256), pad(512), pad(512)],
        compiler_params=_cparams(2),
        name="retention",
    )(q, k, v, g, st0, dmat, qdec, kdec, gl)


def _out_kernel(a_ref, b_ref, x_ref, g_ref, w_ref, o_ref):
    half = w_ref.shape[0] // 2
    y = (jnp.dot(_bf(a_ref[0]), w_ref[0:half, :], preferred_element_type=F32)
         + jnp.dot(_bf(b_ref[0]), w_ref[half:, :], preferred_element_type=F32))
    o_ref[0] = x_ref[0] + g_ref[0] * y


def _out_proj(a, b, x, gate, w_bf, per_row_mod):
    nb, s, d = x.shape
    tm = min(OUT_ROWS, s)
    if per_row_mod:
        g_spec = pl.BlockSpec((1, tm, d), lambda bb, i: (bb, i, 0))
    else:
        g_spec = pl.BlockSpec((1, 1, d), lambda bb, i: (bb, 0, 0))
    row = lambda c: pl.BlockSpec((1, tm, c), lambda bb, i: (bb, i, 0))
    return pl.pallas_call(
        _out_kernel,
        grid=(nb, s // tm),
        in_specs=[row(512), row(512), row(d), g_spec, pl.BlockSpec(w_bf.shape, lambda bb, i: (0, 0))],
        out_specs=row(d),
        out_shape=jax.ShapeDtypeStruct((nb, s, d), F32),
        compiler_params=_cparams(2),
        name="out_proj",
    )(a, b, x, gate, w_bf)


def _odd_in_kernel(x_ref, sc_ref, sh_ref, nw_ref, w_ref, sguw_ref, sgub_ref, dtb_ref,
                   oc_ref, zg_ref, xbc_ref, dt_ref, v_ref):
    x = x_ref[0]
    tm = x.shape[0]
    ms = jnp.mean(x * x, axis=-1, keepdims=True)
    h = (x * lax.rsqrt(ms + EPS) * nw_ref[...]) * (1.0 + sc_ref[0]) + sh_ref[0]
    hb = _bf(h)

    def proj(lo, hi):
        return jnp.dot(hb, w_ref[:, lo:hi], preferred_element_type=F32)

    u = _gelu_tanh(proj(0, 512))
    v = _gelu_tanh(proj(512, 1024))
    mu = jnp.mean(v, axis=-1, keepdims=True)
    vc = v - mu
    v = vc * lax.rsqrt(jnp.mean(vc * vc, axis=-1, keepdims=True) + EPS)
    v_ref[0] = v
    ii = lax.broadcasted_iota(jnp.int32, (CHUNK, CHUNK), 0)
    jj = lax.broadcasted_iota(jnp.int32, (CHUNK, CHUNK), 1)
    rows = []
    for ci in range(tm // CHUNK):
        cols = []
        for g in range(G_C):
            wg = jnp.where(ii >= jj, sguw_ref[g], 0.0)
            cols.append(_mm(wg, v[ci * CHUNK:(ci + 1) * CHUNK, g * 128:(g + 1) * 128]))
        rows.append(jnp.concatenate(cols, axis=1) + sgub_ref[...])
    sg = jnp.concatenate(rows, axis=0) if len(rows) > 1 else rows[0]
    oc_ref[0] = u * sg * _silu(proj(1024, 1536))
    zg_ref[0] = proj(1536, 2048)
    xbc_ref[0] = proj(2048, 3072)
    dt_ref[0] = _softplus(proj(3072, 3584) + dtb_ref[...])


def _odd_in(x, scale, shift, norm_w, w_bf, sgu_w, sgu_b_tab, dt_bias, per_row_mod):
    nb, s, d = x.shape
    tm = PROJ_ROWS
    if per_row_mod:
        mod_spec = pl.BlockSpec((1, tm, d), lambda b, i: (b, i, 0))
    else:
        mod_spec = pl.BlockSpec((1, 1, d), lambda b, i: (b, 0, 0))
    row = lambda c: pl.BlockSpec((1, tm, c), lambda b, i: (b, i, 0))
    const = lambda shp: pl.BlockSpec(shp, lambda b, i: (0,) * len(shp))
    return pl.pallas_call(
        _odd_in_kernel,
        grid=(nb, s // tm),
        in_specs=[row(d), mod_spec, mod_spec, const((1, d)), const(w_bf.shape),
                  const(sgu_w.shape), const(sgu_b_tab.shape), const((1, 512))],
        out_specs=(row(512), row(512), row(1024), row(512), row(512)),
        out_shape=(jax.ShapeDtypeStruct((nb, s, 512), F32),
                   jax.ShapeDtypeStruct((nb, s, 512), F32),
                   jax.ShapeDtypeStruct((nb, s, 1024), F32),
                   jax.ShapeDtypeStruct((nb, s, 512), F32),
                   jax.ShapeDtypeStruct((nb, s, 512), F32)),
        compiler_params=_cparams(2),
        name="odd_in",
    )(x, scale, shift, norm_w, w_bf, sgu_w, sgu_b_tab, dt_bias)


def _ssd_kernel(xbc_ref, dt_ref, zg_ref, tail_ref, st0_ref, cw_ref, cb_ref, alog_ref, dsk_ref, nw_ref,
                tri_ref, sel_ref, y_ref, st_ref, ext, dtpad, zpad, *, t, lq, nc, nbs):
    c = pl.program_id(1)

    @pl.when(c == 0)
    def _():
        st_ref[...] = st0_ref[...]
        ext[...] = jnp.zeros(ext.shape, F32)
        ext[:, 0:8, :] = tail_ref[...]

    ii = lax.broadcasted_iota(jnp.int32, (lq, CHUNK), 0)
    jj = lax.broadcasted_iota(jnp.int32, (lq, CHUNK), 1)
    lane = lax.broadcasted_iota(jnp.int32, (1, 128), 1)
    hpg = H_D // G_D
    neg_a = -jnp.exp(alog_ref[...])
    seqs = []
    for s in range(nbs):
        ext[s, 8:8 + t, :] = xbc_ref[s]
        conv = cb_ref[...]
        for w in range(CONV_W):
            conv = conv + ext[s, pl.ds(8 - (CONV_W - 1) + w, CHUNK), :] * cw_ref[w:w + 1, :]
        if nc > 1:
            ext[s, 0:8, :] = ext[s, CHUNK:CHUNK + 8, :]
        xc = _silu(conv)
        dt = _pad_rows(dt_ref, dtpad, s, t)
        cum = _mm_exact_lhs(tri_ref[...], dt * neg_a, 1, 0)
        seqs.append((xc, dt, cum))
    st1 = []
    for s in range(nbs):
        xc, dt, cum = seqs[s]
        xh = xc[:, 0:W_D]
        last = cum[CHUNK - 1:CHUNK, :]
        dtx = _bf(xh * dt)
        xw = _bf(xh * (jnp.exp(last - cum) * dt))
        cum_rows = _mm_exact_lhs(sel_ref[...], cum, 1, 1)
        per_g = []
        for g in range(G_D):
            bg = _bf(xc[:, W_D + g * N_D:W_D + (g + 1) * N_D])
            cg = _bf(xc[0:lq, W_D + G_D * N_D + g * N_D:W_D + G_D * N_D + (g + 1) * N_D])
            gr = slice(g * hpg * P_D, (g + 1) * hpg * P_D)
            cb = _dg(cg, bg, 1, 1)
            yoff = _dg(cg, _bf(st_ref[s, gr, :]), 1, 1)
            upd = _dg(xw[:, gr], bg, 0, 0)
            per_g.append((cb, yoff, upd))
        st1.append((dtx, cum_rows, per_g))
    for s in range(nbs):
        xc, dt, cum = seqs[s]
        dtx, cum_rows, per_g = st1[s]
        ecum = jnp.exp(cum[0:lq])
        elast = jnp.exp(cum[CHUNK - 1:CHUNK, :])
        ys = []
        for g in range(G_D):
            cb, yoff, upd = per_g[g]
            for pr in range(hpg // 2):
                l0 = g * hpg * P_D + pr * 128
                yh = []
                for hh in range(2):
                    h = g * hpg + pr * 2 + hh
                    col = jnp.broadcast_to(cum[0:lq, h * P_D:h * P_D + 1], (lq, CHUNK))
                    seg = jnp.minimum(col - cum_rows[h:h + 1, :], 0.0)
                    mh = jnp.where(ii >= jj, cb * jnp.exp(seg), 0.0)
                    yh.append(jnp.dot(_bf(mh), dtx[:, l0:l0 + 128], preferred_element_type=F32))
                ypair = jnp.where(lane < P_D, yh[0], yh[1])
                ys.append(ypair + yoff[:, pr * 128:(pr + 1) * 128] * ecum[:, l0:l0 + 128])
            for hl in range(hpg):
                h = g * hpg + hl
                r = slice(h * P_D, (h + 1) * P_D)
                dec = jnp.broadcast_to(elast[0:1, h * P_D:h * P_D + 1], (P_D, N_D))
                st_ref[s, r, :] = st_ref[s, r, :] * dec + upd[hl * P_D:(hl + 1) * P_D, :]
        y = jnp.concatenate(ys, axis=1)
        zg = _pad_rows(zg_ref, zpad, s, t)[0:lq]
        y = (y + xc[0:lq, 0:W_D] * dsk_ref[...]) * _silu(zg)
        gw = W_D // G_D
        outs = []
        for g in range(G_D):
            yg = y[:, g * gw:(g + 1) * gw]
            outs.append(yg * lax.rsqrt(jnp.mean(yg * yg, axis=-1, keepdims=True) + EPS))
        y_ref[s] = (jnp.concatenate(outs, axis=1) * nw_ref[...])[0:t]


def _ssd(xbc, dt, zg, tail, st0, conv_w, conv_b, a_log, d_skip, norm_w, tri, sel, lq, nbs):
    nb, nc, t, _ = xbc.shape
    row = lambda c_: pl.BlockSpec((nbs, None, t, c_), lambda b, c: (b, c, 0, 0))
    const = lambda a: pl.BlockSpec(a.shape, lambda b, c: (0,) * a.ndim)
    st_spec = pl.BlockSpec((nbs, H_D * P_D, N_D), lambda b, c: (b, 0, 0))
    return pl.pallas_call(
        functools.partial(_ssd_kernel, t=t, lq=lq, nc=nc, nbs=nbs),
        grid=(nb // nbs, nc),
        in_specs=[row(1024), row(512), row(512),
                  pl.BlockSpec((nbs, 8, CONV_DIM), lambda b, c: (b, 0, 0)), st_spec,
                  const(conv_w), const(conv_b), const(a_log), const(d_skip), const(norm_w),
                  const(tri), const(sel)],
        out_specs=(row(512), st_spec),
        out_shape=(jax.ShapeDtypeStruct((nb, nc, t, 512), F32),
                   jax.ShapeDtypeStruct((nb, H_D * P_D, N_D), F32)),
        scratch_shapes=[pltpu.VMEM((nbs, CHUNK + 8, CONV_DIM), F32),
                        pltpu.VMEM((nbs, CHUNK, 512), F32), pltpu.VMEM((nbs, CHUNK, 512), F32)],
        compiler_params=_cparams(2),
        name="ssd",
    )(xbc, dt, zg, tail, st0, conv_w, conv_b, a_log, d_skip, norm_w, tri, sel)


def _rotary_tables(pos):
    half = DK_B // 2
    inv = 1.0 / (10000.0 ** (jnp.arange(half, dtype=F32) / half))
    ang = pos.astype(F32)[:, None] * inv[None, :]
    cos, sin = jnp.cos(ang), jnp.sin(ang)
    cos_t = jnp.tile(jnp.concatenate([cos, cos], axis=1), (1, H_B))
    sin_t = jnp.tile(jnp.concatenate([-sin, sin], axis=1), (1, H_B))
    return cos_t, sin_t


def _retention_tables(chunk_len):
    log_g = np.log(1.0 - 2.0 ** (-5.0 - np.arange(H_B, dtype=np.float64)))
    idx = np.arange(CHUNK, dtype=np.float64)
    diff = idx[:, None] - idx[None, :]
    dmat = np.where(diff[None] >= 0, np.exp(np.maximum(diff, 0.0)[None] * log_g[:, None, None]), 0.0)
    qdec = np.exp((idx + 1.0)[:, None] * log_g[None, :])
    kdec = np.where(idx[:, None] < chunk_len, np.exp((chunk_len - 1.0 - idx)[:, None] * log_g[None, :]), 0.0)
    gl = np.exp(chunk_len * log_g)
    return (jnp.asarray(dmat, F32),
            jnp.asarray(np.repeat(qdec, DK_B, axis=1), F32),
            jnp.asarray(np.repeat(kdec, DK_B, axis=1), F32),
            jnp.asarray(np.repeat(np.repeat(gl, DK_B)[:, None], DV_B, axis=1), F32))


def _prompt_bias_idx():
    kk = np.arange(MOBA_BLOCK)[:, None]
    qq = np.arange(MOBA_BLOCK)[None, :]
    diag = np.where(qq >= kk, _t5_bucket_np(qq - kk), -1)
    sub = _t5_bucket_np(qq + MOBA_BLOCK - kk)
    return np.concatenate([diag, sub], axis=0).astype(np.int32)


def _sample_bias_idx(t_new):
    row_t = (np.arange(N_ROWS_S) // H_A)[:, None]
    qpos = PAST_LEN + row_t
    near = _t5_bucket_np(qpos - (PAST_LEN - MOBA_BLOCK + np.arange(MOBA_BLOCK))[None, :])
    own_k = np.arange(PAGE_SIZE)[None, :]
    own = np.where((own_k <= row_t) & (own_k < t_new), _t5_bucket_np(row_t - own_k), -1)
    return np.concatenate([near, own], axis=1).astype(np.int32)


def kernel(x_prompt, x_sample, cache_k, cache_v, state_ret, state_ssm, state_conv, page_table, c_prompt, c_sample, rel_bias, e_norm_w, e_ada_w, e_ada_b, e_in_w, e_q_norm_w, e_k_norm_w, e_out_w, o_norm_w, o_ada_w, o_ada_b, o_in_w, o_sgu_w, o_sgu_b, o_conv_w, o_conv_b, o_dt_bias, o_A_log, o_D, o_ssm_norm_w, o_out_w):
    bp, s_len, d = x_prompt.shape
    bs, t_len, _ = x_sample.shape
    n_s = bs * t_len

    c_all = jnp.concatenate([c_prompt, c_sample, jnp.zeros((8 - (bp + bs) % 8, d), F32)], axis=0)
    mods = []
    for ada_w, ada_b in ((e_ada_w[0], e_ada_b[0]), (o_ada_w[0], o_ada_b[0])):
        mod = _ada_mod(c_all, ada_w, ada_b)
        parts_p = [mod[:bp, i * d:(i + 1) * d].reshape(bp, 1, d) for i in range(3)]
        parts_s = [jnp.repeat(mod[bp:bp + bs, i * d:(i + 1) * d], t_len, axis=0).reshape(1, n_s, d) for i in range(3)]
        mods.append((parts_p, parts_s))
    (e_mod_p, e_mod_s), (o_mod_p, o_mod_s) = mods

    seg = jnp.asarray(np.kron(np.eye(H_A), np.ones((HD_A, HD_A))), BF16)
    qnw = jnp.tile(e_q_norm_w[0], H_A).reshape(1, W_A)
    knw = jnp.tile(e_k_norm_w[0], H_A).reshape(1, W_A)
    e_in_bf = _bf(e_in_w[0])
    e_out_bf = _bf(e_out_w[0])
    o_in_bf = _bf(jnp.concatenate([o_in_w[0][:, :3072], jnp.repeat(o_in_w[0][:, 3072:], P_D, axis=1)], axis=1))
    o_out_bf = _bf(o_out_w[0])
    x_s = x_sample.reshape(1, n_s, d)
    cos_p, sin_p = _rotary_tables(jnp.arange(s_len))
    cos_s, sin_s = _rotary_tables(PAST_LEN + (jnp.arange(n_s) % t_len))
    bias_p = _bias_tables(rel_bias, _prompt_bias_idx(), LOG2E).reshape(H_A, 2, MOBA_BLOCK, MOBA_BLOCK)
    bias_s_h = _bias_tables(rel_bias, _sample_bias_idx(t_len))
    row_h = jnp.arange(N_ROWS_S) % H_A
    bias_s = jnp.sum(jnp.where((jnp.arange(H_A)[:, None] == row_h[None, :])[:, :, None], bias_s_h, 0.0), axis=0)
    bfar = rel_bias[NUM_BUCKETS - 1, row_h].reshape(N_ROWS_S, 1)

    (qa, ka, va, kbf, vt, ga, qb, kb, vb, gb, kmean) = _even_in(
        x_prompt, e_mod_p[1], e_mod_p[0], e_norm_w[0].reshape(1, d), e_in_bf, qnw, knw, seg, cos_p, sin_p, False)
    oa = _moba_prompt(rel_bias, qa, kbf, vt, kmean.reshape(bp, s_len // MOBA_BLOCK, W_A), bias_p, ga)
    nc_p = s_len // CHUNK
    ch = lambda a: a.reshape(bp, nc_p, CHUNK, a.shape[-1])
    ob, ret_p = _retention(ch(qb), ch(kb), ch(vb), ch(gb), jnp.zeros((bp, H_B * DK_B, DV_B), F32),
                           *_retention_tables(CHUNK), lq=CHUNK, nbs=bp)
    xp1 = _out_proj(oa, ob.reshape(bp, s_len, W_B), x_prompt, e_mod_p[2], e_out_bf, False)
    k_prompt = ka.reshape(1, bp, s_len, H_A, HD_A)
    v_prompt = va.reshape(1, bp, s_len, H_A, HD_A)
    ret_state_prompt = ret_p.reshape(1, bp, H_B, DK_B, DV_B)

    (qa_s, ka_s, va_s, _, _, ga_s, qb_s, kb_s, vb_s, gb_s, _) = _even_in(
        x_s, e_mod_s[1], e_mod_s[0], e_norm_w[0].reshape(1, d), e_in_bf, qnw, knw, seg, cos_s, sin_s, True)
    sq = lambda a: a.reshape(bs, t_len, a.shape[-1])
    qrep = jnp.repeat(sq(qa_s), H_A, axis=1)
    n_phys = cache_k.shape[1]
    page_t = lambda c: jnp.transpose(c[0], (0, 2, 3, 1)).reshape(n_phys, W_A, PAGE_SIZE)
    oa_s = _moba_sample(page_table, qrep, sq(ka_s), sq(va_s), sq(ga_s), bias_s, bfar,
                        page_t(cache_k), page_t(cache_v))
    sc = lambda a: a.reshape(bs, 1, t_len, a.shape[-1])
    ob_s, ret_s = _retention(sc(qb_s), sc(kb_s), sc(vb_s), sc(gb_s),
                             state_ret[0].reshape(bs, H_B * DK_B, DV_B), *_retention_tables(t_len), lq=8, nbs=SEQ_PER_STEP)
    xs1 = _out_proj(oa_s.reshape(1, n_s, W_A), ob_s.reshape(1, n_s, W_B), x_s, e_mod_s[2], e_out_bf, True)
    k_sample = ka_s.reshape(1, bs, t_len, H_A, HD_A)
    v_sample = va_s.reshape(1, bs, t_len, H_A, HD_A)
    ret_state_sample = ret_s.reshape(1, bs, H_B, DK_B, DV_B)

    tri = jnp.asarray(np.tril(np.ones((CHUNK, CHUNK))), BF16)
    sel = jnp.asarray(np.kron(np.eye(H_D), np.eye(1, P_D)), BF16)
    rep = lambda a: jnp.repeat(a, P_D).reshape(1, W_D)
    dt_bias, a_log, d_skip = rep(o_dt_bias[0]), rep(o_A_log[0]), rep(o_D[0])
    ssm_nw = o_ssm_norm_w[0].reshape(1, W_D)
    conv_b = o_conv_b[0].reshape(1, CONV_DIM)
    o_nw = o_norm_w[0].reshape(1, d)

    sgu_b_p = jnp.repeat(o_sgu_b[0].T, W_C // G_C, axis=1)
    oc, zg, xbc, dtp, _ = _odd_in(xp1, o_mod_p[1], o_mod_p[0], o_nw, o_in_bf, o_sgu_w[0], sgu_b_p, dt_bias, False)
    yn, ssm_p = _ssd(ch(xbc), ch(dtp), ch(zg), jnp.zeros((bp, 8, CONV_DIM), F32),
                     jnp.zeros((bp, H_D * P_D, N_D), F32), o_conv_w[0], conv_b, a_log, d_skip, ssm_nw,
                     tri, sel, lq=CHUNK, nbs=bp)
    y_prompt = _out_proj(oc, yn.reshape(bp, s_len, W_D), xp1, o_mod_p[2], o_out_bf, False)
    ssm_state_prompt = ssm_p.reshape(1, bp, H_D, P_D, N_D)
    conv_state_prompt = xbc[:, -(CONV_W - 1):][None]

    per_chunk = CHUNK // t_len
    w_small = o_sgu_w[0][:, :t_len, :t_len]
    sgu_w_s = jax.vmap(lambda w: jnp.kron(jnp.eye(per_chunk, dtype=F32), w))(w_small)
    sgu_b_s = jnp.repeat(jnp.tile(o_sgu_b[0][:, :t_len].T, (per_chunk, 1)), W_C // G_C, axis=1)
    oc_s, zg_s, xbc_s, dt_s, v_s = _odd_in(xs1, o_mod_s[1], o_mod_s[0], o_nw, o_in_bf, sgu_w_s, sgu_b_s, dt_bias, True)
    tail_s = jnp.concatenate([jnp.zeros((bs, 8 - (CONV_W - 1), CONV_DIM), F32), state_conv[0]], axis=1)
    yn_s, ssm_s = _ssd(sc(xbc_s), sc(dt_s), sc(zg_s), tail_s, state_ssm[0].reshape(bs, H_D * P_D, N_D),
                       o_conv_w[0], conv_b, a_log, d_skip, ssm_nw, tri, sel, lq=8, nbs=SEQ_PER_STEP)
    xs2 = _out_proj(oc_s, yn_s.reshape(1, n_s, W_D), xs1, o_mod_s[2], o_out_bf, True)
    y_sample = xs2.reshape(bs, t_len, d)
    sgu_v_sample = v_s.reshape(1, bs, t_len, W_C)
    ssm_state_sample = ssm_s.reshape(1, bs, H_D, P_D, N_D)
    xin = jnp.concatenate([state_conv[0], xbc_s.reshape(bs, t_len, CONV_DIM)], axis=1)
    conv_state_sample = xin[:, -(CONV_W - 1):][None]

    return (y_prompt, y_sample, k_prompt, v_prompt, k_sample, v_sample, ret_state_prompt, ret_state_sample,
            sgu_v_sample, ssm_state_prompt, ssm_state_sample, conv_state_prompt, conv_state_sample)
```

```python
import functools
import math

import numpy as np
import jax
import jax.numpy as jnp
from jax import lax
from jax.experimental import pallas as pl
from jax.experimental.pallas import tpu as pltpu

F32 = jnp.float32
BF16 = jnp.bfloat16

D_MODEL = 1024
PAST_LEN = 2048
PAGE_SIZE = 128
H_A, HD_A, W_A = 8, 64, 512
MOBA_BLOCK = 256
MOBA_TOPK = 3
NUM_BUCKETS = 32
MAX_DISTANCE = 128
H_B, DK_B, DV_B, W_B = 4, 64, 128, 512
G_C, W_C = 4, 512
H_D, P_D, N_D, G_D, W_D = 8, 64, 128, 2, 512
CONV_W = 4
CONV_DIM = 1024
CHUNK = 128
SEQ_PER_STEP = 8
PROJ_ROWS = 512
OUT_ROWS = 1024
NEG_INF = -1e30
EPS = 1e-6
LOG2E = math.log2(math.e)
VMEM_LIMIT = 56 * 1024 * 1024


def _bf(x):
    return x.astype(BF16)


def _dg(a, b, ca, cb):
    return lax.dot_general(a, b, (((ca,), (cb,)), ((), ())), preferred_element_type=F32)


def _mm(a, b):
    return _dg(_bf(a), _bf(b), 1, 0)


def _mm_nt(a, b):
    return _dg(_bf(a), _bf(b), 1, 1)


def _mm_tn(a, b):
    return _dg(_bf(a), _bf(b), 0, 0)


def _split2(x):
    hi = _bf(x)
    return hi, _bf(x - hi.astype(F32))


def _split3(x):
    hi = _bf(x)
    r = x - hi.astype(F32)
    mid = _bf(r)
    return hi, mid, _bf(r - mid.astype(F32))


def _mm_hp(a, b, ca, cb):
    ah, al = _split2(a)
    bh, bl = _split2(b)
    return _dg(ah, bh, ca, cb) + (_dg(ah, bl, ca, cb) + _dg(al, bh, ca, cb))


def _mm_exact_lhs(e, x, ca, cb):
    h, m, l = _split3(x)
    return _dg(e, h, ca, cb) + (_dg(e, m, ca, cb) + _dg(e, l, ca, cb))


def _silu(x):
    return x * (1.0 / (1.0 + jnp.exp(-x)))


def _gelu_tanh(x):
    return 0.5 * x * (1.0 + jnp.tanh(math.sqrt(2.0 / math.pi) * (x + 0.044715 * (x * x * x))))


def _softplus(x):
    return jnp.maximum(x, 0.0) + jnp.log1p(jnp.exp(-jnp.abs(x)))


def _cparams(n_grid):
    return pltpu.CompilerParams(dimension_semantics=("arbitrary",) * n_grid,
                                vmem_limit_bytes=VMEM_LIMIT)


def _top3_rows(g, blk, nblk):
    sel = jnp.zeros(g.shape, jnp.bool_)
    for _ in range(MOBA_TOPK):
        m = jnp.max(g, axis=0, keepdims=True)
        idx = jnp.min(jnp.where(g == m, blk, nblk), axis=0, keepdims=True)
        pick = blk == idx
        sel = jnp.logical_or(sel, pick)
        g = jnp.where(pick, -jnp.inf, g)
    return sel


def _ada_kernel(c_ref, w_ref, b_ref, o_ref):
    s = _silu(c_ref[...])
    o_ref[...] = _mm_hp(s, w_ref[...], 1, 0) + b_ref[...]


def _ada_mod(c_all, w, b):
    m, d = c_all.shape
    n = w.shape[1]
    tn = 512
    return pl.pallas_call(
        _ada_kernel,
        grid=(n // tn,),
        in_specs=[pl.BlockSpec((m, d), lambda j: (0, 0)),
                  pl.BlockSpec((d, tn), lambda j: (0, j)),
                  pl.BlockSpec((1, tn), lambda j: (0, j))],
        out_specs=pl.BlockSpec((m, tn), lambda j: (0, j)),
        out_shape=jax.ShapeDtypeStruct((m, n), F32),
        compiler_params=_cparams(1),
        name="ada_mod",
    )(c_all, w, b.reshape(1, n))


def _t5_bucket_np(rel):
    n = np.maximum(rel, 0)
    max_exact = NUM_BUCKETS // 2
    nf = np.maximum(n, 1).astype(np.float64)
    large = max_exact + (np.log(nf / max_exact) / math.log(MAX_DISTANCE / max_exact)
                         * (NUM_BUCKETS - max_exact)).astype(np.int64)
    large = np.minimum(large, NUM_BUCKETS - 1)
    return np.where(n < max_exact, n, large).astype(np.int32)


def _bias_kernel(tab_ref, idx_ref, o_ref, *, scale):
    h = pl.program_id(0)
    idx = idx_ref[...]
    acc = jnp.zeros(idx.shape, F32)
    for b in range(NUM_BUCKETS):
        acc = jnp.where(idx == b, tab_ref[b, h], acc)
    o_ref[0] = jnp.where(idx == -1, NEG_INF, acc * scale)


def _bias_tables(rel_bias, idx, scale=1.0):
    r, c = idx.shape
    return pl.pallas_call(
        functools.partial(_bias_kernel, scale=scale),
        grid=(H_A,),
        in_specs=[pl.BlockSpec(memory_space=pltpu.SMEM),
                  pl.BlockSpec((r, c), lambda h: (0, 0))],
        out_specs=pl.BlockSpec((1, r, c), lambda h: (h, 0, 0)),
        out_shape=jax.ShapeDtypeStruct((H_A, r, c), F32),
        compiler_params=_cparams(1),
        name="t5_bias",
    )(rel_bias, jnp.asarray(idx))


def _even_in_kernel(x_ref, sc_ref, sh_ref, nw_ref, w_ref, qnw_ref, knw_ref, seg_ref, cos_ref, sin_ref,
                    qa_ref, ka_ref, va_ref, kbf_ref, vt_ref, ga_ref, qb_ref, kb_ref, vb_ref, gb_ref, km_ref):
    x = x_ref[0]
    ms = jnp.mean(x * x, axis=-1, keepdims=True)
    h = (x * lax.rsqrt(ms + EPS) * nw_ref[...]) * (1.0 + sc_ref[0]) + sh_ref[0]
    hb = _bf(h)

    def proj(lo, hi):
        return jnp.dot(hb, w_ref[:, lo:hi], preferred_element_type=F32)

    def head_rms(t, w_row):
        ss = jnp.dot(_bf(t * t), seg_ref[...], preferred_element_type=F32)
        return t * lax.rsqrt(ss * (1.0 / HD_A) + EPS) * w_row

    qa_ref[0] = head_rms(proj(0, 512), qnw_ref[...])
    ka = head_rms(proj(512, 1024), knw_ref[...])
    ka_ref[0] = ka
    kbf_ref[0] = _bf(ka)
    for j in range(ka.shape[0] // MOBA_BLOCK):
        km_ref[0, j] = jnp.mean(ka[j * MOBA_BLOCK:(j + 1) * MOBA_BLOCK], axis=0, keepdims=True)
    va = proj(1024, 1536)
    va_ref[0] = va
    vt_ref[0] = _bf(va.T)
    ga_ref[0] = proj(1536, 2048).astype(ga_ref.dtype)

    lane = lax.broadcasted_iota(jnp.int32, (1, 256), 1) % DK_B
    first_half = lane < (DK_B // 2)
    cos = cos_ref[...]
    sin = sin_ref[...]

    def rotary(t):
        up = pltpu.roll(t, 256 - DK_B // 2, 1)
        dn = pltpu.roll(t, DK_B // 2, 1)
        return t * cos + jnp.where(first_half, up, dn) * sin

    qb_ref[0] = rotary(proj(2048, 2304)).astype(qb_ref.dtype)
    kb_ref[0] = (rotary(proj(2304, 2560)) * (DK_B ** -0.5)).astype(kb_ref.dtype)
    vb_ref[0] = proj(2560, 3072).astype(vb_ref.dtype)
    gb_ref[0] = proj(3072, 3584).astype(gb_ref.dtype)


def _even_in(x, scale, shift, norm_w, w_bf, qnw, knw, seg, cos, sin, per_row_mod, act):
    nb, s, d = x.shape
    tm = PROJ_ROWS
    ns = s // tm
    nkb = tm // MOBA_BLOCK
    if per_row_mod:
        mod_spec = pl.BlockSpec((1, tm, d), lambda b, i: (b, i, 0))
    else:
        mod_spec = pl.BlockSpec((1, 1, d), lambda b, i: (b, 0, 0))
    row = lambda c: pl.BlockSpec((1, tm, c), lambda b, i: (b, i, 0))
    const = lambda shp: pl.BlockSpec(shp, lambda b, i: (0,) * len(shp))
    out_shape = (
        jax.ShapeDtypeStruct((nb, s, 512), F32),
        jax.ShapeDtypeStruct((nb, s, 512), F32),
        jax.ShapeDtypeStruct((nb, s, 512), F32),
        jax.ShapeDtypeStruct((nb, s, 512), BF16),
        jax.ShapeDtypeStruct((nb, 512, s), BF16),
        jax.ShapeDtypeStruct((nb, s, 512), act),
        jax.ShapeDtypeStruct((nb, s, 256), act),
        jax.ShapeDtypeStruct((nb, s, 256), act),
        jax.ShapeDtypeStruct((nb, s, 512), act),
        jax.ShapeDtypeStruct((nb, s, 512), act),
        jax.ShapeDtypeStruct((nb, ns * nkb, 1, 512), F32),
    )
    out_specs = (row(512), row(512), row(512), row(512),
                 pl.BlockSpec((1, 512, tm), lambda b, i: (b, 0, i)),
                 row(512), row(256), row(256), row(512), row(512),
                 pl.BlockSpec((1, nkb, 1, 512), lambda b, i: (b, i, 0, 0)))
    return pl.pallas_call(
        _even_in_kernel,
        grid=(nb, ns),
        in_specs=[row(d), mod_spec, mod_spec, const((1, d)), const((d, 3584)),
                  const((1, 512)), const((1, 512)), const((512, 512)),
                  pl.BlockSpec((tm, 256), lambda b, i: (i, 0)),
                  pl.BlockSpec((tm, 256), lambda b, i: (i, 0))],
        out_specs=out_specs,
        out_shape=out_shape,
        compiler_params=_cparams(2),
        name="even_in",
    )(x, scale, shift, norm_w, w_bf, qnw, knw, seg, cos, sin)


MOBA_HS = 8
FAR_KEYS = 2 * MOBA_BLOCK


def _moba_p_kernel(tab_ref, q_ref, k_ref, vt_ref, km_ref, bias_ref, ga_ref, o_ref, rbf_ref, rbs_ref, qs_ref):
    hg = pl.program_id(1)
    qi = pl.program_id(2)
    nblk = km_ref.shape[1]
    lane = lax.broadcasted_iota(jnp.int32, (1, 128), 1)
    blk = lax.broadcasted_iota(jnp.int32, (nblk, MOBA_BLOCK), 0)
    for hl in range(MOBA_HS):
        pr, hh = divmod(hl, 2)
        pc = slice(pr * 128, (pr + 1) * 128)
        qm = jnp.where((lane // HD_A) == hh, q_ref[0, :, pc], 0.0)
        gate = _mm_hp(km_ref[0, :, pc], qm, 1, 1)
        gate = jnp.where(blk < qi, gate, NEG_INF)
        sel = jnp.logical_and(_top3_rows(gate, blk, nblk), blk < qi)
        far_c = tab_ref[NUM_BUCKETS - 1, MOBA_HS * hg + hl] * LOG2E
        rbf_ref[hl] = jnp.where(jnp.logical_and(sel, blk < qi - 1), far_c, NEG_INF)
        rbs_ref[hl] = jnp.where(jnp.logical_or(sel, blk == qi), 0.0, NEG_INF)
        qs_ref[hl] = _bf(qm * (HD_A ** -0.5 * LOG2E))

    def visit(carry, off, nkeys, extra_fn):
        ss = []
        for hl in range(MOBA_HS):
            pr = hl // 2
            kj = k_ref[0, pl.ds(off, nkeys), pr * 128:(pr + 1) * 128]
            ss.append(_dg(kj, qs_ref[hl], 1, 1))
        stats, ps = [], []
        for hl in range(MOBA_HS):
            m, l, _ = carry[hl]
            s = extra_fn(hl, ss[hl])
            mn = jnp.maximum(m, jnp.max(s, axis=0, keepdims=True))
            alpha = jnp.exp2(m - mn)
            p = jnp.exp2(s - mn)
            stats.append((mn, alpha, alpha * l + jnp.sum(p, axis=0, keepdims=True)))
            ps.append(_bf(p))
        pvs = []
        for hl in range(MOBA_HS):
            vj = vt_ref[0, hl * HD_A:(hl + 1) * HD_A, pl.ds(off, nkeys)]
            pvs.append(jnp.dot(vj, ps[hl], preferred_element_type=F32))
        return tuple((stats[hl][0], stats[hl][2], stats[hl][1] * carry[hl][2] + pvs[hl])
                     for hl in range(MOBA_HS))

    def far_body(jp, carry):
        off = pl.multiple_of(jp * FAR_KEYS, FAR_KEYS)

        def extra(hl, s):
            r0 = rbf_ref[hl, pl.ds(2 * jp, 1), :]
            r1 = rbf_ref[hl, pl.ds(2 * jp + 1, 1), :]
            return jnp.concatenate([s[:MOBA_BLOCK] + r0, s[MOBA_BLOCK:] + r1], axis=0)

        return visit(carry, off, FAR_KEYS, extra)

    init = tuple((jnp.full((1, MOBA_BLOCK), -jnp.inf, F32), jnp.zeros((1, MOBA_BLOCK), F32),
                  jnp.zeros((HD_A, MOBA_BLOCK), F32)) for _ in range(MOBA_HS))
    carry = lax.fori_loop(0, qi // 2, far_body, init)
    js = jnp.maximum(qi - 1, 0)
    first = qi == 0
    top_tab = jnp.where(first, 0, 1)
    bot_mask = jnp.where(first, NEG_INF, 0.0)

    def near_extra(hl, s):
        top = s[:MOBA_BLOCK] + bias_ref[hl, top_tab] + rbs_ref[hl, pl.ds(js, 1), :]
        bot = s[MOBA_BLOCK:] + (bias_ref[hl, 0] + bot_mask)
        return jnp.concatenate([top, bot], axis=0)

    carry = visit(carry, pl.multiple_of(js * MOBA_BLOCK, MOBA_BLOCK), FAR_KEYS, near_extra)
    for pr in range(MOBA_HS // 2):
        outs = [carry[2 * pr + hh][2] * (1.0 / carry[2 * pr + hh][1]) for hh in range(2)]
        o = jnp.concatenate(outs, axis=0).T
        pc = slice(pr * 128, (pr + 1) * 128)
        o_ref[0, :, pc] = (o * _silu(ga_ref[0, :, pc].astype(F32))).astype(o_ref.dtype)


def _moba_prompt(rel_bias, qa, kbf, vt, kmean, bias_t, ga):
    nb, s, _ = qa.shape
    nq = s // MOBA_BLOCK
    w = MOBA_HS * HD_A
    tile = pl.BlockSpec((1, MOBA_BLOCK, w), lambda b, hg, i: (b, i, hg))
    return pl.pallas_call(
        _moba_p_kernel,
        grid=(nb, H_A // MOBA_HS, nq),
        in_specs=[pl.BlockSpec(memory_space=pltpu.SMEM),
                  tile,
                  pl.BlockSpec((1, s, w), lambda b, hg, i: (b, 0, hg)),
                  pl.BlockSpec((1, w, s), lambda b, hg, i: (b, hg, 0)),
                  pl.BlockSpec((1, nq, w), lambda b, hg, i: (b, 0, hg)),
                  pl.BlockSpec((MOBA_HS, 2, MOBA_BLOCK, MOBA_BLOCK), lambda b, hg, i: (hg, 0, 0, 0)),
                  tile],
        out_specs=tile,
        out_shape=jax.ShapeDtypeStruct((nb, s, 512), BF16),
        scratch_shapes=[pltpu.VMEM((MOBA_HS, nq, MOBA_BLOCK), F32),
                        pltpu.VMEM((MOBA_HS, nq, MOBA_BLOCK), F32),
                        pltpu.VMEM((MOBA_HS, MOBA_BLOCK, 128), BF16)],
        compiler_params=_cparams(3),
        name="moba_prompt",
    )(rel_bias, qa, kbf, vt, kmean, bias_t, ga)


N_PAST_BLK = PAST_LEN // MOBA_BLOCK
N_PAGES = PAST_LEN // PAGE_SIZE
N_ROWS_S = 32


def _moba_s_kernel(pt_ref, qrep_ref, knew_ref, vnew_ref, ga_ref, bias_ref, bfar_ref, ck_hbm, cv_hbm,
                   o_ref, kt_buf, vt_buf, kpad, vpad, sems):
    b = pl.program_id(0)
    nseq = pl.num_programs(0)
    slot = b % 2
    t_new = knew_ref.shape[1]

    def page_copies(seq, sl):
        cps = []
        for p in range(N_PAGES):
            dst = pl.ds(p * PAGE_SIZE, PAGE_SIZE)
            cps.append(pltpu.make_async_copy(ck_hbm.at[pt_ref[seq, p]], kt_buf.at[sl, :, dst], sems.at[0, sl]))
            cps.append(pltpu.make_async_copy(cv_hbm.at[pt_ref[seq, p]], vt_buf.at[sl, :, dst], sems.at[1, sl]))
        return cps

    @pl.when(b == 0)
    def _():
        kpad[...] = jnp.zeros(kpad.shape, F32)
        vpad[...] = jnp.zeros(vpad.shape, F32)
        for cp in page_copies(0, 0):
            cp.start()

    @pl.when(b + 1 < nseq)
    def _():
        for cp in page_copies(b + 1, 1 - slot):
            cp.start()

    rowh = lax.broadcasted_iota(jnp.int32, (N_ROWS_S, W_A), 0) % H_A
    laneh = lax.broadcasted_iota(jnp.int32, (N_ROWS_S, W_A), 1) // HD_A
    own_head = rowh == laneh
    qf = jnp.where(own_head, qrep_ref[0], 0.0) * (HD_A ** -0.5)
    qbd = _bf(qf)
    q2 = jnp.concatenate([qbd, _bf(qf - qbd.astype(F32))], axis=0)
    kpad[0:t_new, :] = knew_ref[0]
    vpad[0:t_new, :] = vnew_ref[0]

    for cp in page_copies(b, slot):
        cp.wait()

    blocks = [slice(n * MOBA_BLOCK, (n + 1) * MOBA_BLOCK) for n in range(N_PAST_BLK)]
    s_past = []
    for n in range(N_PAST_BLK):
        s2 = jnp.dot(q2, _bf(kt_buf[slot, :, blocks[n]]), preferred_element_type=F32)
        s_past.append(s2[0:N_ROWS_S] + s2[N_ROWS_S:])
    s_own = _dg(qbd, _bf(kpad[...]), 1, 1) + bias_ref[:, MOBA_BLOCK:]
    g = [jnp.sum(s, axis=1, keepdims=True) for s in s_past]
    sel = [jnp.zeros((N_ROWS_S, 1), jnp.bool_) for _ in range(N_PAST_BLK)]
    for _ in range(MOBA_TOPK):
        m = functools.reduce(jnp.maximum, g)
        idx = functools.reduce(jnp.minimum, [jnp.where(g[n] == m, n, N_PAST_BLK) for n in range(N_PAST_BLK)])
        for n in range(N_PAST_BLK):
            pick = idx == n
            sel[n] = jnp.logical_or(sel[n], pick)
            g[n] = jnp.where(pick, -jnp.inf, g[n])
    bfar = bfar_ref[...]
    for n in range(N_PAST_BLK):
        if n < N_PAST_BLK - 1:
            s_past[n] = s_past[n] + jnp.where(sel[n], bfar, NEG_INF)
        else:
            s_past[n] = s_past[n] + bias_ref[:, 0:MOBA_BLOCK] + jnp.where(sel[n], 0.0, NEG_INF)
    m = jnp.max(s_own, axis=1, keepdims=True)
    for s in s_past:
        m = jnp.maximum(m, jnp.max(s, axis=1, keepdims=True))
    p_own = jnp.exp(s_own - m)
    l = jnp.sum(p_own, axis=1, keepdims=True)
    acc = jnp.dot(_bf(p_own), _bf(vpad[...]), preferred_element_type=F32)
    for n in range(N_PAST_BLK):
        p = jnp.exp(s_past[n] - m)
        l = l + jnp.sum(p, axis=1, keepdims=True)
        acc = acc + _dg(_bf(p), _bf(vt_buf[slot, :, blocks[n]]), 1, 1)
    o = jnp.where(own_head, acc * (1.0 / l), 0.0)
    o = jnp.sum(o.reshape(t_new, H_A, W_A), axis=1)
    o_ref[0] = o * _silu(ga_ref[0])


def _moba_sample(page_table, qrep, knew, vnew, ga, bias_s, bfar, cache_kt, cache_vt):
    nseq, t_new, _ = knew.shape
    tok = pl.BlockSpec((1, t_new, W_A), lambda b, pt: (b, 0, 0))
    grid_spec = pltpu.PrefetchScalarGridSpec(
        num_scalar_prefetch=1,
        grid=(nseq,),
        in_specs=[pl.BlockSpec((1, N_ROWS_S, W_A), lambda b, pt: (b, 0, 0)),
                  tok, tok, tok,
                  pl.BlockSpec(bias_s.shape, lambda b, pt: (0, 0)),
                  pl.BlockSpec((N_ROWS_S, 1), lambda b, pt: (0, 0)),
                  pl.BlockSpec(memory_space=pl.ANY),
                  pl.BlockSpec(memory_space=pl.ANY)],
        out_specs=tok,
        scratch_shapes=[pltpu.VMEM((2, W_A, PAST_LEN), F32),
                        pltpu.VMEM((2, W_A, PAST_LEN), F32),
                        pltpu.VMEM((PAGE_SIZE, W_A), F32),
                        pltpu.VMEM((PAGE_SIZE, W_A), F32),
                        pltpu.SemaphoreType.DMA((2, 2))],
    )
    return pl.pallas_call(
        _moba_s_kernel,
        grid_spec=grid_spec,
        out_shape=jax.ShapeDtypeStruct((nseq, t_new, W_A), F32),
        compiler_params=_cparams(1),
        name="moba_sample",
    )(page_table, qrep, knew, vnew, ga, bias_s, bfar, cache_kt, cache_vt)


def _pad_rows(ref, scratch, s, t):
    if t == CHUNK:
        return ref[s]
    scratch[s] = jnp.zeros(scratch.shape[1:], scratch.dtype)
    scratch[s, 0:t, :] = ref[s].astype(scratch.dtype)
    return scratch[s]


def _ret_kernel(q_ref, k_ref, v_ref, g_ref, st0_ref, dmat_ref, qdec_ref, kdec_ref, gl_ref,
                o_ref, st_ref, qpad, kpad, vpad, gpad, *, t, lq, nbs):
    c = pl.program_id(1)

    @pl.when(c == 0)
    def _():
        st_ref[...] = st0_ref[...]

    lane = lax.broadcasted_iota(jnp.int32, (1, 128), 1)
    rowsel = lax.broadcasted_iota(jnp.int32, (128, 1), 0) < DK_B
    work = []
    for s in range(nbs):
        q = _pad_rows(q_ref, qpad, s, t)[0:lq]
        k = _pad_rows(k_ref, kpad, s, t)
        v = _pad_rows(v_ref, vpad, s, t)
        for hp in range(H_B // 2):
            cols = slice(hp * 128, (hp + 1) * 128)
            kp = _bf(k[:, cols])
            st = st_ref[s, cols, :]
            kd = _bf(k[:, cols] * kdec_ref[:, cols])
            for hh in range(2):
                h = 2 * hp + hh
                qm = jnp.where((lane // DK_B) == hh, q[:, cols], 0.0)
                vh = _bf(v[:, h * DV_B:(h + 1) * DV_B])
                sc = _dg(_bf(qm), kp, 1, 1)
                so = _mm(qm * qdec_ref[0:lq, cols], st)
                upd = _dg(kd, vh, 0, 0)
                work.append((s, hp, hh, sc, so, upd, vh, st))
    outs = {}
    for (s, hp, hh, sc, so, upd, vh, st) in work:
        h = 2 * hp + hh
        o = jnp.dot(_bf(sc * dmat_ref[h, 0:lq, :]), vh, preferred_element_type=F32) + so
        outs[(s, h)] = o * lax.rsqrt(jnp.mean(o * o, axis=-1, keepdims=True) + EPS)
    for i in range(0, len(work), 2):
        s, hp, _, _, _, upd0, _, st = work[i]
        cols = slice(hp * 128, (hp + 1) * 128)
        st_ref[s, cols, :] = st * gl_ref[cols, :] + jnp.where(rowsel, upd0, work[i + 1][5])
    for s in range(nbs):
        g = _pad_rows(g_ref, gpad, s, t)[0:lq].astype(F32)
        o = jnp.concatenate([outs[(s, h)] for h in range(H_B)], axis=1)
        o_ref[s] = (o * _silu(g))[0:t].astype(o_ref.dtype)


def _retention(q, k, v, g, st0, dmat, qdec, kdec, gl, lq, nbs):
    nb, nc, t, _ = q.shape
    row = lambda c_: pl.BlockSpec((nbs, None, t, c_), lambda b, c: (b, c, 0, 0))
    const = lambda a: pl.BlockSpec(a.shape, lambda b, c: (0,) * a.ndim)
    st_spec = pl.BlockSpec((nbs, H_B * DK_B, DV_B), lambda b, c: (b, 0, 0))
    pad = lambda c_: pltpu.VMEM((nbs, CHUNK, c_), F32)
    return pl.pallas_call(
        functools.partial(_ret_kernel, t=t, lq=lq, nbs=nbs),
        grid=(nb // nbs, nc),
        in_specs=[row(256), row(256), row(512), row(512), st_spec,
                  const(dmat), const(qdec), const(kdec), const(gl)],
        out_specs=(row(512), st_spec),
        out_shape=(jax.ShapeDtypeStruct((nb, nc, t, 512), q.dtype),
                   jax.ShapeDtypeStruct((nb, H_B * DK_B, DV_B), F32)),
        scratch_shapes=[pad(256), pad(256), pad(512), pad(512)],
        compiler_params=_cparams(2),
        name="retention",
    )(q, k, v, g, st0, dmat, qdec, kdec, gl)


def _out_kernel(a_ref, b_ref, x_ref, g_ref, w_ref, o_ref):
    half = w_ref.shape[0] // 2
    y = (jnp.dot(_bf(a_ref[0]), w_ref[0:half, :], preferred_element_type=F32)
         + jnp.dot(_bf(b_ref[0]), w_ref[half:, :], preferred_element_type=F32))
    o_ref[0] = x_ref[0] + g_ref[0] * y


def _out_proj(a, b, x, gate, w_bf, per_row_mod):
    nb, s, d = x.shape
    tm = min(OUT_ROWS, s)
    if per_row_mod:
        g_spec = pl.BlockSpec((1, tm, d), lambda bb, i: (bb, i, 0))
    else:
        g_spec = pl.BlockSpec((1, 1, d), lambda bb, i: (bb, 0, 0))
    row = lambda c: pl.BlockSpec((1, tm, c), lambda bb, i: (bb, i, 0))
    return pl.pallas_call(
        _out_kernel,
        grid=(nb, s // tm),
        in_specs=[row(512), row(512), row(d), g_spec, pl.BlockSpec(w_bf.shape, lambda bb, i: (0, 0))],
        out_specs=row(d),
        out_shape=jax.ShapeDtypeStruct((nb, s, d), F32),
        compiler_params=_cparams(2),
        name="out_proj",
    )(a, b, x, gate, w_bf)


def _odd_in_kernel(x_ref, sc_ref, sh_ref, nw_ref, w_ref, sguw_ref, sgub_ref, dtb_ref,
                   oc_ref, zg_ref, xbc_ref, dt_ref, *maybe_v_ref):
    x = x_ref[0]
    tm = x.shape[0]
    ms = jnp.mean(x * x, axis=-1, keepdims=True)
    h = (x * lax.rsqrt(ms + EPS) * nw_ref[...]) * (1.0 + sc_ref[0]) + sh_ref[0]
    hb = _bf(h)

    def proj(lo, hi):
        return jnp.dot(hb, w_ref[:, lo:hi], preferred_element_type=F32)

    u = _gelu_tanh(proj(0, 512))
    v = _gelu_tanh(proj(512, 1024))
    mu = jnp.mean(v, axis=-1, keepdims=True)
    vc = v - mu
    v = vc * lax.rsqrt(jnp.mean(vc * vc, axis=-1, keepdims=True) + EPS)
    for v_ref in maybe_v_ref:
        v_ref[0] = v
    ii = lax.broadcasted_iota(jnp.int32, (CHUNK, CHUNK), 0)
    jj = lax.broadcasted_iota(jnp.int32, (CHUNK, CHUNK), 1)
    rows = []
    for ci in range(tm // CHUNK):
        cols = []
        for g in range(G_C):
            wg = jnp.where(ii >= jj, sguw_ref[g], 0.0)
            cols.append(_mm(wg, v[ci * CHUNK:(ci + 1) * CHUNK, g * 128:(g + 1) * 128]))
        rows.append(jnp.concatenate(cols, axis=1) + sgub_ref[...])
    sg = jnp.concatenate(rows, axis=0) if len(rows) > 1 else rows[0]
    oc_ref[0] = (u * sg * _silu(proj(1024, 1536))).astype(oc_ref.dtype)
    zg_ref[0] = proj(1536, 2048).astype(zg_ref.dtype)
    xbc_ref[0] = proj(2048, 3072)
    dt_ref[0] = _softplus(proj(3072, 3584) + dtb_ref[...])


def _odd_in(x, scale, shift, norm_w, w_bf, sgu_w, sgu_b_tab, dt_bias, per_row_mod, act, emit_v):
    nb, s, d = x.shape
    tm = PROJ_ROWS
    if per_row_mod:
        mod_spec = pl.BlockSpec((1, tm, d), lambda b, i: (b, i, 0))
    else:
        mod_spec = pl.BlockSpec((1, 1, d), lambda b, i: (b, 0, 0))
    row = lambda c: pl.BlockSpec((1, tm, c), lambda b, i: (b, i, 0))
    const = lambda shp: pl.BlockSpec(shp, lambda b, i: (0,) * len(shp))
    return pl.pallas_call(
        _odd_in_kernel,
        grid=(nb, s // tm),
        in_specs=[row(d), mod_spec, mod_spec, const((1, d)), const(w_bf.shape),
                  const(sgu_w.shape), const(sgu_b_tab.shape), const((1, 512))],
        out_specs=(row(512), row(512), row(1024), row(512)) + ((row(512),) if emit_v else ()),
        out_shape=(jax.ShapeDtypeStruct((nb, s, 512), act),
                   jax.ShapeDtypeStruct((nb, s, 512), act),
                   jax.ShapeDtypeStruct((nb, s, 1024), F32),
                   jax.ShapeDtypeStruct((nb, s, 512), F32),
                   ) + ((jax.ShapeDtypeStruct((nb, s, 512), F32),) if emit_v else ()),
        compiler_params=_cparams(2),
        name="odd_in",
    )(x, scale, shift, norm_w, w_bf, sgu_w, sgu_b_tab, dt_bias)


def _ssd_kernel(xbc_ref, dt_ref, zg_ref, tail_ref, st0_ref, cw_ref, cb_ref, alog_ref, dsk_ref, nw_ref,
                tri_ref, sel_ref, y_ref, st_ref, ext, dtpad, zpad, *, t, lq, nc, nbs):
    c = pl.program_id(1)

    @pl.when(c == 0)
    def _():
        st_ref[...] = st0_ref[...]
        ext[...] = jnp.zeros(ext.shape, F32)
        ext[:, 0:8, :] = tail_ref[...]

    ii = lax.broadcasted_iota(jnp.int32, (lq, CHUNK), 0)
    jj = lax.broadcasted_iota(jnp.int32, (lq, CHUNK), 1)
    lane = lax.broadcasted_iota(jnp.int32, (1, 128), 1)
    hpg = H_D // G_D
    neg_a = -jnp.exp(alog_ref[...])
    seqs = []
    for s in range(nbs):
        ext[s, 8:8 + t, :] = xbc_ref[s]
        conv = cb_ref[...]
        for w in range(CONV_W):
            conv = conv + ext[s, pl.ds(8 - (CONV_W - 1) + w, CHUNK), :] * cw_ref[w:w + 1, :]
        if nc > 1:
            ext[s, 0:8, :] = ext[s, CHUNK:CHUNK + 8, :]
        xc = _silu(conv)
        dt = _pad_rows(dt_ref, dtpad, s, t)
        cum = _mm_exact_lhs(tri_ref[...], dt * neg_a, 1, 0)
        seqs.append((xc, dt, cum))
    st1 = []
    for s in range(nbs):
        xc, dt, cum = seqs[s]
        xh = xc[:, 0:W_D]
        last = cum[CHUNK - 1:CHUNK, :]
        dtx = _bf(xh * dt)
        xw = _bf(xh * (jnp.exp(last - cum) * dt))
        cum_rows = _mm_exact_lhs(sel_ref[...], cum, 1, 1)
        per_g = []
        for g in range(G_D):
            bg = _bf(xc[:, W_D + g * N_D:W_D + (g + 1) * N_D])
            cg = _bf(xc[0:lq, W_D + G_D * N_D + g * N_D:W_D + G_D * N_D + (g + 1) * N_D])
            gr = slice(g * hpg * P_D, (g + 1) * hpg * P_D)
            cb = _dg(cg, bg, 1, 1)
            yoff = _dg(cg, _bf(st_ref[s, gr, :]), 1, 1)
            upd = _dg(xw[:, gr], bg, 0, 0)
            per_g.append((cb, yoff, upd))
        st1.append((dtx, cum_rows, per_g))
    for s in range(nbs):
        xc, dt, cum = seqs[s]
        dtx, cum_rows, per_g = st1[s]
        ecum = jnp.exp(cum[0:lq])
        elast = jnp.exp(cum[CHUNK - 1:CHUNK, :])
        ys = []
        for g in range(G_D):
            cb, yoff, upd = per_g[g]
            for pr in range(hpg // 2):
                l0 = g * hpg * P_D + pr * 128
                yh = []
                for hh in range(2):
                    h = g * hpg + pr * 2 + hh
                    col = jnp.broadcast_to(cum[0:lq, h * P_D:h * P_D + 1], (lq, CHUNK))
                    seg = jnp.minimum(col - cum_rows[h:h + 1, :], 0.0)
                    mh = jnp.where(ii >= jj, cb * jnp.exp(seg), 0.0)
                    yh.append(jnp.dot(_bf(mh), dtx[:, l0:l0 + 128], preferred_element_type=F32))
                ypair = jnp.where(lane < P_D, yh[0], yh[1])
                ys.append(ypair + yoff[:, pr * 128:(pr + 1) * 128] * ecum[:, l0:l0 + 128])
            for hl in range(hpg):
                h = g * hpg + hl
                r = slice(h * P_D, (h + 1) * P_D)
                dec = jnp.broadcast_to(elast[0:1, h * P_D:h * P_D + 1], (P_D, N_D))
                st_ref[s, r, :] = st_ref[s, r, :] * dec + upd[hl * P_D:(hl + 1) * P_D, :]
        y = jnp.concatenate(ys, axis=1)
        zg = _pad_rows(zg_ref, zpad, s, t)[0:lq].astype(F32)
        y = (y + xc[0:lq, 0:W_D] * dsk_ref[...]) * _silu(zg)
        gw = W_D // G_D
        outs = []
        for g in range(G_D):
            yg = y[:, g * gw:(g + 1) * gw]
            outs.append(yg * lax.rsqrt(jnp.mean(yg * yg, axis=-1, keepdims=True) + EPS))
        y_ref[s] = (jnp.concatenate(outs, axis=1) * nw_ref[...])[0:t].astype(y_ref.dtype)


def _ssd(xbc, dt, zg, tail, st0, conv_w, conv_b, a_log, d_skip, norm_w, tri, sel, lq, nbs):
    nb, nc, t, _ = xbc.shape
    row = lambda c_: pl.BlockSpec((nbs, None, t, c_), lambda b, c: (b, c, 0, 0))
    const = lambda a: pl.BlockSpec(a.shape, lambda b, c: (0,) * a.ndim)
    st_spec = pl.BlockSpec((nbs, H_D * P_D, N_D), lambda b, c: (b, 0, 0))
    return pl.pallas_call(
        functools.partial(_ssd_kernel, t=t, lq=lq, nc=nc, nbs=nbs),
        grid=(nb // nbs, nc),
        in_specs=[row(1024), row(512), row(512),
                  pl.BlockSpec((nbs, 8, CONV_DIM), lambda b, c: (b, 0, 0)), st_spec,
                  const(conv_w), const(conv_b), const(a_log), const(d_skip), const(norm_w),
                  const(tri), const(sel)],
        out_specs=(row(512), st_spec),
        out_shape=(jax.ShapeDtypeStruct((nb, nc, t, 512), zg.dtype),
                   jax.ShapeDtypeStruct((nb, H_D * P_D, N_D), F32)),
        scratch_shapes=[pltpu.VMEM((nbs, CHUNK + 8, CONV_DIM), F32),
                        pltpu.VMEM((nbs, CHUNK, 512), F32), pltpu.VMEM((nbs, CHUNK, 512), F32)],
        compiler_params=_cparams(2),
        name="ssd",
    )(xbc, dt, zg, tail, st0, conv_w, conv_b, a_log, d_skip, norm_w, tri, sel)


def _rotary_tables(pos):
    half = DK_B // 2
    inv = 1.0 / (10000.0 ** (jnp.arange(half, dtype=F32) / half))
    ang = pos.astype(F32)[:, None] * inv[None, :]
    cos, sin = jnp.cos(ang), jnp.sin(ang)
    cos_t = jnp.tile(jnp.concatenate([cos, cos], axis=1), (1, H_B))
    sin_t = jnp.tile(jnp.concatenate([-sin, sin], axis=1), (1, H_B))
    return cos_t, sin_t


def _retention_tables(chunk_len):
    log_g = np.log(1.0 - 2.0 ** (-5.0 - np.arange(H_B, dtype=np.float64)))
    idx = np.arange(CHUNK, dtype=np.float64)
    diff = idx[:, None] - idx[None, :]
    dmat = np.where(diff[None] >= 0, np.exp(np.maximum(diff, 0.0)[None] * log_g[:, None, None]), 0.0)
    qdec = np.exp((idx + 1.0)[:, None] * log_g[None, :])
    kdec = np.where(idx[:, None] < chunk_len, np.exp((chunk_len - 1.0 - idx)[:, None] * log_g[None, :]), 0.0)
    gl = np.exp(chunk_len * log_g)
    return (jnp.asarray(dmat, F32),
            jnp.asarray(np.repeat(qdec, DK_B, axis=1), F32),
            jnp.asarray(np.repeat(kdec, DK_B, axis=1), F32),
            jnp.asarray(np.repeat(np.repeat(gl, DK_B)[:, None], DV_B, axis=1), F32))


def _prompt_bias_idx():
    kk = np.arange(MOBA_BLOCK)[:, None]
    qq = np.arange(MOBA_BLOCK)[None, :]
    diag = np.where(qq >= kk, _t5_bucket_np(qq - kk), -1)
    sub = _t5_bucket_np(qq + MOBA_BLOCK - kk)
    return np.concatenate([diag, sub], axis=0).astype(np.int32)


def _sample_bias_idx(t_new):
    row_t = (np.arange(N_ROWS_S) // H_A)[:, None]
    qpos = PAST_LEN + row_t
    near = _t5_bucket_np(qpos - (PAST_LEN - MOBA_BLOCK + np.arange(MOBA_BLOCK))[None, :])
    own_k = np.arange(PAGE_SIZE)[None, :]
    own = np.where((own_k <= row_t) & (own_k < t_new), _t5_bucket_np(row_t - own_k), -1)
    return np.concatenate([near, own], axis=1).astype(np.int32)


def kernel(x_prompt, x_sample, cache_k, cache_v, state_ret, state_ssm, state_conv, page_table, c_prompt, c_sample, rel_bias, e_norm_w, e_ada_w, e_ada_b, e_in_w, e_q_norm_w, e_k_norm_w, e_out_w, o_norm_w, o_ada_w, o_ada_b, o_in_w, o_sgu_w, o_sgu_b, o_conv_w, o_conv_b, o_dt_bias, o_A_log, o_D, o_ssm_norm_w, o_out_w):
    bp, s_len, d = x_prompt.shape
    bs, t_len, _ = x_sample.shape
    n_s = bs * t_len

    c_all = jnp.concatenate([c_prompt, c_sample, jnp.zeros((8 - (bp + bs) % 8, d), F32)], axis=0)
    mods = []
    for ada_w, ada_b in ((e_ada_w[0], e_ada_b[0]), (o_ada_w[0], o_ada_b[0])):
        mod = _ada_mod(c_all, ada_w, ada_b)
        parts_p = [mod[:bp, i * d:(i + 1) * d].reshape(bp, 1, d) for i in range(3)]
        parts_s = [jnp.repeat(mod[bp:bp + bs, i * d:(i + 1) * d], t_len, axis=0).reshape(1, n_s, d) for i in range(3)]
        mods.append((parts_p, parts_s))
    (e_mod_p, e_mod_s), (o_mod_p, o_mod_s) = mods

    seg = jnp.asarray(np.kron(np.eye(H_A), np.ones((HD_A, HD_A))), BF16)
    qnw = jnp.tile(e_q_norm_w[0], H_A).reshape(1, W_A)
    knw = jnp.tile(e_k_norm_w[0], H_A).reshape(1, W_A)
    e_in_bf = _bf(e_in_w[0])
    e_out_bf = _bf(e_out_w[0])
    o_in_bf = _bf(jnp.concatenate([o_in_w[0][:, :3072], jnp.repeat(o_in_w[0][:, 3072:], P_D, axis=1)], axis=1))
    o_out_bf = _bf(o_out_w[0])
    x_s = x_sample.reshape(1, n_s, d)
    cos_p, sin_p = _rotary_tables(jnp.arange(s_len))
    cos_s, sin_s = _rotary_tables(PAST_LEN + (jnp.arange(n_s) % t_len))
    bias_p = _bias_tables(rel_bias, _prompt_bias_idx(), LOG2E).reshape(H_A, 2, MOBA_BLOCK, MOBA_BLOCK)
    bias_s_h = _bias_tables(rel_bias, _sample_bias_idx(t_len))
    row_h = jnp.arange(N_ROWS_S) % H_A
    bias_s = jnp.sum(jnp.where((jnp.arange(H_A)[:, None] == row_h[None, :])[:, :, None], bias_s_h, 0.0), axis=0)
    bfar = rel_bias[NUM_BUCKETS - 1, row_h].reshape(N_ROWS_S, 1)

    (qa, ka, va, kbf, vt, ga, qb, kb, vb, gb, kmean) = _even_in(
        x_prompt, e_mod_p[1], e_mod_p[0], e_norm_w[0].reshape(1, d), e_in_bf, qnw, knw, seg, cos_p, sin_p, False, BF16)
    oa = _moba_prompt(rel_bias, qa, kbf, vt, kmean.reshape(bp, s_len // MOBA_BLOCK, W_A), bias_p, ga)
    nc_p = s_len // CHUNK
    ch = lambda a: a.reshape(bp, nc_p, CHUNK, a.shape[-1])
    ob, ret_p = _retention(ch(qb), ch(kb), ch(vb), ch(gb), jnp.zeros((bp, H_B * DK_B, DV_B), F32),
                           *_retention_tables(CHUNK), lq=CHUNK, nbs=bp)
    xp1 = _out_proj(oa, ob.reshape(bp, s_len, W_B), x_prompt, e_mod_p[2], e_out_bf, False)
    k_prompt = ka.reshape(1, bp, s_len, H_A, HD_A)
    v_prompt = va.reshape(1, bp, s_len, H_A, HD_A)
    ret_state_prompt = ret_p.reshape(1, bp, H_B, DK_B, DV_B)

    (qa_s, ka_s, va_s, _, _, ga_s, qb_s, kb_s, vb_s, gb_s, _) = _even_in(
        x_s, e_mod_s[1], e_mod_s[0], e_norm_w[0].reshape(1, d), e_in_bf, qnw, knw, seg, cos_s, sin_s, True, F32)
    sq = lambda a: a.reshape(bs, t_len, a.shape[-1])
    qrep = jnp.repeat(sq(qa_s), H_A, axis=1)
    n_phys = cache_k.shape[1]
    page_t = lambda c: jnp.transpose(c[0], (0, 2, 3, 1)).reshape(n_phys, W_A, PAGE_SIZE)
    oa_s = _moba_sample(page_table, qrep, sq(ka_s), sq(va_s), sq(ga_s), bias_s, bfar,
                        page_t(cache_k), page_t(cache_v))
    sc = lambda a: a.reshape(bs, 1, t_len, a.shape[-1])
    ob_s, ret_s = _retention(sc(qb_s), sc(kb_s), sc(vb_s), sc(gb_s),
                             state_ret[0].reshape(bs, H_B * DK_B, DV_B), *_retention_tables(t_len), lq=8, nbs=SEQ_PER_STEP)
    xs1 = _out_proj(oa_s.reshape(1, n_s, W_A), ob_s.reshape(1, n_s, W_B), x_s, e_mod_s[2], e_out_bf, True)
    k_sample = ka_s.reshape(1, bs, t_len, H_A, HD_A)
    v_sample = va_s.reshape(1, bs, t_len, H_A, HD_A)
    ret_state_sample = ret_s.reshape(1, bs, H_B, DK_B, DV_B)

    tri = jnp.asarray(np.tril(np.ones((CHUNK, CHUNK))), BF16)
    sel = jnp.asarray(np.kron(np.eye(H_D), np.eye(1, P_D)), BF16)
    rep = lambda a: jnp.repeat(a, P_D).reshape(1, W_D)
    dt_bias, a_log, d_skip = rep(o_dt_bias[0]), rep(o_A_log[0]), rep(o_D[0])
    ssm_nw = o_ssm_norm_w[0].reshape(1, W_D)
    conv_b = o_conv_b[0].reshape(1, CONV_DIM)
    o_nw = o_norm_w[0].reshape(1, d)

    sgu_b_p = jnp.repeat(o_sgu_b[0].T, W_C // G_C, axis=1)
    oc, zg, xbc, dtp = _odd_in(xp1, o_mod_p[1], o_mod_p[0], o_nw, o_in_bf, o_sgu_w[0], sgu_b_p, dt_bias,
                               False, BF16, False)
    yn, ssm_p = _ssd(ch(xbc), ch(dtp), ch(zg), jnp.zeros((bp, 8, CONV_DIM), F32),
                     jnp.zeros((bp, H_D * P_D, N_D), F32), o_conv_w[0], conv_b, a_log, d_skip, ssm_nw,
                     tri, sel, lq=CHUNK, nbs=bp)
    y_prompt = _out_proj(oc, yn.reshape(bp, s_len, W_D), xp1, o_mod_p[2], o_out_bf, False)
    ssm_state_prompt = ssm_p.reshape(1, bp, H_D, P_D, N_D)
    conv_state_prompt = xbc[:, -(CONV_W - 1):][None]

    per_chunk = CHUNK // t_len
    w_small = o_sgu_w[0][:, :t_len, :t_len]
    sgu_w_s = jax.vmap(lambda w: jnp.kron(jnp.eye(per_chunk, dtype=F32), w))(w_small)
    sgu_b_s = jnp.repeat(jnp.tile(o_sgu_b[0][:, :t_len].T, (per_chunk, 1)), W_C // G_C, axis=1)
    oc_s, zg_s, xbc_s, dt_s, v_s = _odd_in(xs1, o_mod_s[1], o_mod_s[0], o_nw, o_in_bf, sgu_w_s, sgu_b_s, dt_bias,
                                           True, F32, True)
    tail_s = jnp.concatenate([jnp.zeros((bs, 8 - (CONV_W - 1), CONV_DIM), F32), state_conv[0]], axis=1)
    yn_s, ssm_s = _ssd(sc(xbc_s), sc(dt_s), sc(zg_s), tail_s, state_ssm[0].reshape(bs, H_D * P_D, N_D),
                       o_conv_w[0], conv_b, a_log, d_skip, ssm_nw, tri, sel, lq=8, nbs=SEQ_PER_STEP)
    xs2 = _out_proj(oc_s, yn_s.reshape(1, n_s, W_D), xs1, o_mod_s[2], o_out_bf, True)
    y_sample = xs2.reshape(bs, t_len, d)
    sgu_v_sample = v_s.reshape(1, bs, t_len, W_C)
    ssm_state_sample = ssm_s.reshape(1, bs, H_D, P_D, N_D)
    xin = jnp.concatenate([state_conv[0], xbc_s.reshape(bs, t_len, CONV_DIM)], axis=1)
    conv_state_sample = xin[:, -(CONV_W - 1):][None]

    return (y_prompt, y_sample, k_prompt, v_prompt, k_sample, v_sample, ret_state_prompt, ret_state_sample,
            sgu_v_sample, ssm_state_prompt, ssm_state_sample, conv_state_prompt, conv_state_sample)
```

```python
import functools
import math

import numpy as np
import jax
import jax.numpy as jnp
from jax import lax
from jax.experimental import pallas as pl
from jax.experimental.pallas import tpu as pltpu

F32 = jnp.float32
BF16 = jnp.bfloat16

D_MODEL = 1024
PAST_LEN = 2048
PAGE_SIZE = 128
H_A, HD_A, W_A = 8, 64, 512
MOBA_BLOCK = 256
MOBA_TOPK = 3
NUM_BUCKETS = 32
MAX_DISTANCE = 128
H_B, DK_B, DV_B, W_B = 4, 64, 128, 512
G_C, W_C = 4, 512
H_D, P_D, N_D, G_D, W_D = 8, 64, 128, 2, 512
CONV_W = 4
CONV_DIM = 1024
CHUNK = 128
SEQ_PER_STEP = 8
PROJ_ROWS = 512
OUT_ROWS = 1024
NEG_INF = -1e30
EPS = 1e-6
LOG2E = math.log2(math.e)
VMEM_LIMIT = 56 * 1024 * 1024


def _bf(x):
    return x.astype(BF16)


def _dg(a, b, ca, cb):
    return lax.dot_general(a, b, (((ca,), (cb,)), ((), ())), preferred_element_type=F32)


def _mm(a, b):
    return _dg(_bf(a), _bf(b), 1, 0)


def _mm_nt(a, b):
    return _dg(_bf(a), _bf(b), 1, 1)


def _mm_tn(a, b):
    return _dg(_bf(a), _bf(b), 0, 0)


def _split2(x):
    hi = _bf(x)
    return hi, _bf(x - hi.astype(F32))


def _split3(x):
    hi = _bf(x)
    r = x - hi.astype(F32)
    mid = _bf(r)
    return hi, mid, _bf(r - mid.astype(F32))


def _mm_hp(a, b, ca, cb):
    ah, al = _split2(a)
    bh, bl = _split2(b)
    return _dg(ah, bh, ca, cb) + (_dg(ah, bl, ca, cb) + _dg(al, bh, ca, cb))


def _mm_exact_lhs(e, x, ca, cb):
    h, m, l = _split3(x)
    return _dg(e, h, ca, cb) + (_dg(e, m, ca, cb) + _dg(e, l, ca, cb))


def _silu(x):
    return x * (1.0 / (1.0 + jnp.exp(-x)))


def _gelu_tanh(x):
    return 0.5 * x * (1.0 + jnp.tanh(math.sqrt(2.0 / math.pi) * (x + 0.044715 * (x * x * x))))


def _softplus(x):
    return jnp.maximum(x, 0.0) + jnp.log1p(jnp.exp(-jnp.abs(x)))


def _cparams(n_grid):
    return pltpu.CompilerParams(dimension_semantics=("arbitrary",) * n_grid,
                                vmem_limit_bytes=VMEM_LIMIT)


def _top3_rows(g, blk, nblk):
    sel = jnp.zeros(g.shape, jnp.bool_)
    for _ in range(MOBA_TOPK):
        m = jnp.max(g, axis=0, keepdims=True)
        idx = jnp.min(jnp.where(g == m, blk, nblk), axis=0, keepdims=True)
        pick = blk == idx
        sel = jnp.logical_or(sel, pick)
        g = jnp.where(pick, -jnp.inf, g)
    return sel


def _ada_kernel(c_ref, w_ref, b_ref, o_ref):
    s = _silu(c_ref[...])
    o_ref[...] = _mm_hp(s, w_ref[...], 1, 0) + b_ref[...]


def _ada_mod(c_all, w, b):
    m, d = c_all.shape
    n = w.shape[1]
    tn = 512
    return pl.pallas_call(
        _ada_kernel,
        grid=(n // tn,),
        in_specs=[pl.BlockSpec((m, d), lambda j: (0, 0)),
                  pl.BlockSpec((d, tn), lambda j: (0, j)),
                  pl.BlockSpec((1, tn), lambda j: (0, j))],
        out_specs=pl.BlockSpec((m, tn), lambda j: (0, j)),
        out_shape=jax.ShapeDtypeStruct((m, n), F32),
        compiler_params=_cparams(1),
        name="ada_mod",
    )(c_all, w, b.reshape(1, n))


def _t5_bucket_np(rel):
    n = np.maximum(rel, 0)
    max_exact = NUM_BUCKETS // 2
    nf = np.maximum(n, 1).astype(np.float64)
    large = max_exact + (np.log(nf / max_exact) / math.log(MAX_DISTANCE / max_exact)
                         * (NUM_BUCKETS - max_exact)).astype(np.int64)
    large = np.minimum(large, NUM_BUCKETS - 1)
    return np.where(n < max_exact, n, large).astype(np.int32)


def _bias_kernel(tab_ref, idx_ref, o_ref, *, scale):
    h = pl.program_id(0)
    idx = idx_ref[...]
    acc = jnp.zeros(idx.shape, F32)
    for b in range(NUM_BUCKETS):
        acc = jnp.where(idx == b, tab_ref[b, h], acc)
    o_ref[0] = jnp.where(idx == -1, NEG_INF, acc * scale)


def _bias_tables(rel_bias, idx, scale=1.0):
    r, c = idx.shape
    return pl.pallas_call(
        functools.partial(_bias_kernel, scale=scale),
        grid=(H_A,),
        in_specs=[pl.BlockSpec(memory_space=pltpu.SMEM),
                  pl.BlockSpec((r, c), lambda h: (0, 0))],
        out_specs=pl.BlockSpec((1, r, c), lambda h: (h, 0, 0)),
        out_shape=jax.ShapeDtypeStruct((H_A, r, c), F32),
        compiler_params=_cparams(1),
        name="t5_bias",
    )(rel_bias, jnp.asarray(idx))


def _even_in_kernel(x_ref, sc_ref, sh_ref, nw_ref, w_ref, qnw_ref, knw_ref, seg_ref, cos_ref, sin_ref,
                    qa_ref, ka_ref, va_ref, kbf_ref, vt_ref, ga_ref, qb_ref, kb_ref, vb_ref, gb_ref, km_ref):
    x = x_ref[0]
    ms = jnp.mean(x * x, axis=-1, keepdims=True)
    h = (x * lax.rsqrt(ms + EPS) * nw_ref[...]) * (1.0 + sc_ref[0]) + sh_ref[0]
    hb = _bf(h)

    def proj(lo, hi):
        return jnp.dot(hb, w_ref[:, lo:hi], preferred_element_type=F32)

    def head_rms(t, w_row):
        ss = jnp.dot(_bf(t * t), seg_ref[...], preferred_element_type=F32)
        return t * lax.rsqrt(ss * (1.0 / HD_A) + EPS) * w_row

    qa_ref[0] = head_rms(proj(0, 512), qnw_ref[...])
    ka = head_rms(proj(512, 1024), knw_ref[...])
    ka_ref[0] = ka
    kbf_ref[0] = _bf(ka)
    for j in range(ka.shape[0] // MOBA_BLOCK):
        km_ref[0, j] = jnp.mean(ka[j * MOBA_BLOCK:(j + 1) * MOBA_BLOCK], axis=0, keepdims=True)
    va = proj(1024, 1536)
    va_ref[0] = va
    vat = va.T
    ones_pad = (lax.broadcasted_iota(jnp.int32, (V_ROWS - HD_A, vat.shape[1]), 0) == 0).astype(F32)
    vt_ref[0] = _bf(jnp.concatenate(
        [piece for h in range(H_A) for piece in (vat[h * HD_A:(h + 1) * HD_A], ones_pad)], axis=0))
    ga_ref[0] = proj(1536, 2048).astype(ga_ref.dtype)

    lane = lax.broadcasted_iota(jnp.int32, (1, 256), 1) % DK_B
    first_half = lane < (DK_B // 2)
    cos = cos_ref[...]
    sin = sin_ref[...]

    def rotary(t):
        up = pltpu.roll(t, 256 - DK_B // 2, 1)
        dn = pltpu.roll(t, DK_B // 2, 1)
        return t * cos + jnp.where(first_half, up, dn) * sin

    qb_ref[0] = rotary(proj(2048, 2304)).astype(qb_ref.dtype)
    kb_ref[0] = (rotary(proj(2304, 2560)) * (DK_B ** -0.5)).astype(kb_ref.dtype)
    vb_ref[0] = proj(2560, 3072).astype(vb_ref.dtype)
    gb_ref[0] = proj(3072, 3584).astype(gb_ref.dtype)


def _even_in(x, scale, shift, norm_w, w_bf, qnw, knw, seg, cos, sin, per_row_mod, act):
    nb, s, d = x.shape
    tm = PROJ_ROWS
    ns = s // tm
    nkb = tm // MOBA_BLOCK
    if per_row_mod:
        mod_spec = pl.BlockSpec((1, tm, d), lambda b, i: (b, i, 0))
    else:
        mod_spec = pl.BlockSpec((1, 1, d), lambda b, i: (b, 0, 0))
    row = lambda c: pl.BlockSpec((1, tm, c), lambda b, i: (b, i, 0))
    const = lambda shp: pl.BlockSpec(shp, lambda b, i: (0,) * len(shp))
    out_shape = (
        jax.ShapeDtypeStruct((nb, s, 512), F32),
        jax.ShapeDtypeStruct((nb, s, 512), F32),
        jax.ShapeDtypeStruct((nb, s, 512), F32),
        jax.ShapeDtypeStruct((nb, s, 512), BF16),
        jax.ShapeDtypeStruct((nb, H_A * V_ROWS, s), BF16),
        jax.ShapeDtypeStruct((nb, s, 512), act),
        jax.ShapeDtypeStruct((nb, s, 256), act),
        jax.ShapeDtypeStruct((nb, s, 256), act),
        jax.ShapeDtypeStruct((nb, s, 512), act),
        jax.ShapeDtypeStruct((nb, s, 512), act),
        jax.ShapeDtypeStruct((nb, ns * nkb, 1, 512), F32),
    )
    out_specs = (row(512), row(512), row(512), row(512),
                 pl.BlockSpec((1, H_A * V_ROWS, tm), lambda b, i: (b, 0, i)),
                 row(512), row(256), row(256), row(512), row(512),
                 pl.BlockSpec((1, nkb, 1, 512), lambda b, i: (b, i, 0, 0)))
    return pl.pallas_call(
        _even_in_kernel,
        grid=(nb, ns),
        in_specs=[row(d), mod_spec, mod_spec, const((1, d)), const((d, 3584)),
                  const((1, 512)), const((1, 512)), const((512, 512)),
                  pl.BlockSpec((tm, 256), lambda b, i: (i, 0)),
                  pl.BlockSpec((tm, 256), lambda b, i: (i, 0))],
        out_specs=out_specs,
        out_shape=out_shape,
        compiler_params=_cparams(2),
        name="even_in",
    )(x, scale, shift, norm_w, w_bf, qnw, knw, seg, cos, sin)


MOBA_HS = 8
V_ROWS = HD_A + 8
FAR_KEYS = 2 * MOBA_BLOCK


def _moba_p_kernel(tab_ref, q_ref, k_ref, vt_ref, km_ref, bias_ref, ga_ref, o_ref, rbf_ref, rbs_ref, qs_ref):
    hg = pl.program_id(1)
    qi = pl.program_id(2)
    nblk = km_ref.shape[1]
    lane = lax.broadcasted_iota(jnp.int32, (1, 128), 1)
    blk = lax.broadcasted_iota(jnp.int32, (nblk, MOBA_BLOCK), 0)
    for hl in range(MOBA_HS):
        pr, hh = divmod(hl, 2)
        pc = slice(pr * 128, (pr + 1) * 128)
        qm = jnp.where((lane // HD_A) == hh, q_ref[0, :, pc], 0.0)
        gate = _mm_hp(km_ref[0, :, pc], qm, 1, 1)
        gate = jnp.where(blk < qi, gate, NEG_INF)
        sel = jnp.logical_and(_top3_rows(gate, blk, nblk), blk < qi)
        far_c = tab_ref[NUM_BUCKETS - 1, MOBA_HS * hg + hl] * LOG2E
        rbf_ref[hl] = jnp.where(jnp.logical_and(sel, blk < qi - 1), far_c, NEG_INF)
        rbs_ref[hl] = jnp.where(jnp.logical_or(sel, blk == qi), 0.0, NEG_INF)
        qs_ref[hl] = _bf(qm * (HD_A ** -0.5 * LOG2E))

    def visit(carry, off, nkeys, extra_fn=None, block_rows=None):
        ss = []
        for hl in range(MOBA_HS):
            pr = hl // 2
            kj = k_ref[0, pl.ds(off, nkeys), pr * 128:(pr + 1) * 128]
            ss.append(_dg(kj, qs_ref[hl], 1, 1))
        stats, ps = [], []
        for hl in range(MOBA_HS):
            m = carry[hl][0]
            if block_rows is None:
                s = extra_fn(hl, ss[hl])
                mn = jnp.maximum(m, jnp.max(s, axis=0, keepdims=True))
                p = jnp.exp2(s - mn)
            else:
                halves = [ss[hl][i * MOBA_BLOCK:(i + 1) * MOBA_BLOCK] for i in range(nkeys // MOBA_BLOCK)]
                rows = block_rows(hl)
                mn = m
                for sh, r in zip(halves, rows):
                    mn = jnp.maximum(mn, jnp.max(sh, axis=0, keepdims=True) + r)
                p = jnp.concatenate([jnp.exp2(sh - (mn - r)) for sh, r in zip(halves, rows)], axis=0)
            stats.append((mn, jnp.exp2(m - mn)))
            ps.append(_bf(p))
        pvs = []
        for hl in range(MOBA_HS):
            vj = vt_ref[0, hl * V_ROWS:(hl + 1) * V_ROWS, pl.ds(off, nkeys)]
            pvs.append(jnp.dot(vj, ps[hl], preferred_element_type=F32))
        return tuple((stats[hl][0], stats[hl][1] * carry[hl][1] + pvs[hl]) for hl in range(MOBA_HS))

    def far_body(jp, carry):
        off = pl.multiple_of(jp * FAR_KEYS, FAR_KEYS)

        def rows(hl):
            return rbf_ref[hl, pl.ds(2 * jp, 1), :], rbf_ref[hl, pl.ds(2 * jp + 1, 1), :]

        return visit(carry, off, FAR_KEYS, block_rows=rows)

    init = tuple((jnp.full((1, MOBA_BLOCK), -jnp.inf, F32), jnp.zeros((V_ROWS, MOBA_BLOCK), F32))
                 for _ in range(MOBA_HS))
    carry = lax.fori_loop(0, qi // 2, far_body, init)
    js = jnp.maximum(qi - 1, 0)
    first = qi == 0
    top_tab = jnp.where(first, 0, 1)
    bot_mask = jnp.where(first, NEG_INF, 0.0)

    def near_extra(hl, s):
        top = s[:MOBA_BLOCK] + bias_ref[hl, top_tab] + rbs_ref[hl, pl.ds(js, 1), :]
        bot = s[MOBA_BLOCK:] + (bias_ref[hl, 0] + bot_mask)
        return jnp.concatenate([top, bot], axis=0)

    carry = visit(carry, pl.multiple_of(js * MOBA_BLOCK, MOBA_BLOCK), FAR_KEYS, near_extra)
    for pr in range(MOBA_HS // 2):
        accs = [carry[2 * pr + hh][1] for hh in range(2)]
        outs = [a[0:HD_A] * (1.0 / a[HD_A:HD_A + 1]) for a in accs]
        o = jnp.concatenate(outs, axis=0).T
        pc = slice(pr * 128, (pr + 1) * 128)
        o_ref[0, :, pc] = (o * _silu(ga_ref[0, :, pc].astype(F32))).astype(o_ref.dtype)


def _moba_prompt(rel_bias, qa, kbf, vt, kmean, bias_t, ga):
    nb, s, _ = qa.shape
    nq = s // MOBA_BLOCK
    w = MOBA_HS * HD_A
    tile = pl.BlockSpec((1, MOBA_BLOCK, w), lambda b, hg, i: (b, i, hg))
    return pl.pallas_call(
        _moba_p_kernel,
        grid=(nb, H_A // MOBA_HS, nq),
        in_specs=[pl.BlockSpec(memory_space=pltpu.SMEM),
                  tile,
                  pl.BlockSpec((1, s, w), lambda b, hg, i: (b, 0, hg)),
                  pl.BlockSpec((1, MOBA_HS * V_ROWS, s), lambda b, hg, i: (b, hg, 0)),
                  pl.BlockSpec((1, nq, w), lambda b, hg, i: (b, 0, hg)),
                  pl.BlockSpec((MOBA_HS, 2, MOBA_BLOCK, MOBA_BLOCK), lambda b, hg, i: (hg, 0, 0, 0)),
                  tile],
        out_specs=tile,
        out_shape=jax.ShapeDtypeStruct((nb, s, 512), BF16),
        scratch_shapes=[pltpu.VMEM((MOBA_HS, nq, MOBA_BLOCK), F32),
                        pltpu.VMEM((MOBA_HS, nq, MOBA_BLOCK), F32),
                        pltpu.VMEM((MOBA_HS, MOBA_BLOCK, 128), BF16)],
        compiler_params=_cparams(3),
        name="moba_prompt",
    )(rel_bias, qa, kbf, vt, kmean, bias_t, ga)


N_PAST_BLK = PAST_LEN // MOBA_BLOCK
N_PAGES = PAST_LEN // PAGE_SIZE
N_ROWS_S = 32


def _moba_s_kernel(pt_ref, qrep_ref, knew_ref, vnew_ref, ga_ref, bias_ref, bfar_ref, ck_hbm, cv_hbm,
                   o_ref, kt_buf, vt_buf, kpad, vpad, sems):
    b = pl.program_id(0)
    nseq = pl.num_programs(0)
    slot = b % 2
    t_new = knew_ref.shape[1]

    def page_copies(seq, sl):
        cps = []
        for p in range(N_PAGES):
            dst = pl.ds(p * PAGE_SIZE, PAGE_SIZE)
            cps.append(pltpu.make_async_copy(ck_hbm.at[pt_ref[seq, p]], kt_buf.at[sl, :, dst], sems.at[0, sl]))
            cps.append(pltpu.make_async_copy(cv_hbm.at[pt_ref[seq, p]], vt_buf.at[sl, :, dst], sems.at[1, sl]))
        return cps

    @pl.when(b == 0)
    def _():
        kpad[...] = jnp.zeros(kpad.shape, F32)
        vpad[...] = jnp.zeros(vpad.shape, F32)
        for cp in page_copies(0, 0):
            cp.start()

    @pl.when(b + 1 < nseq)
    def _():
        for cp in page_copies(b + 1, 1 - slot):
            cp.start()

    rowh = lax.broadcasted_iota(jnp.int32, (N_ROWS_S, W_A), 0) % H_A
    laneh = lax.broadcasted_iota(jnp.int32, (N_ROWS_S, W_A), 1) // HD_A
    own_head = rowh == laneh
    qf = jnp.where(own_head, qrep_ref[0], 0.0) * (HD_A ** -0.5)
    qbd = _bf(qf)
    q2 = jnp.concatenate([qbd, _bf(qf - qbd.astype(F32))], axis=0)
    kpad[0:t_new, :] = knew_ref[0]
    vpad[0:t_new, :] = vnew_ref[0]

    for cp in page_copies(b, slot):
        cp.wait()

    blocks = [slice(n * MOBA_BLOCK, (n + 1) * MOBA_BLOCK) for n in range(N_PAST_BLK)]
    s_past = []
    for n in range(N_PAST_BLK):
        s2 = jnp.dot(q2, _bf(kt_buf[slot, :, blocks[n]]), preferred_element_type=F32)
        s_past.append(s2[0:N_ROWS_S] + s2[N_ROWS_S:])
    s_own = _dg(qbd, _bf(kpad[...]), 1, 1) + bias_ref[:, MOBA_BLOCK:]
    g = [jnp.sum(s, axis=1, keepdims=True) for s in s_past]
    sel = [jnp.zeros((N_ROWS_S, 1), jnp.bool_) for _ in range(N_PAST_BLK)]
    for _ in range(MOBA_TOPK):
        m = functools.reduce(jnp.maximum, g)
        idx = functools.reduce(jnp.minimum, [jnp.where(g[n] == m, n, N_PAST_BLK) for n in range(N_PAST_BLK)])
        for n in range(N_PAST_BLK):
            pick = idx == n
            sel[n] = jnp.logical_or(sel[n], pick)
            g[n] = jnp.where(pick, -jnp.inf, g[n])
    bfar = bfar_ref[...]
    for n in range(N_PAST_BLK):
        if n < N_PAST_BLK - 1:
            s_past[n] = s_past[n] + jnp.where(sel[n], bfar, NEG_INF)
        else:
            s_past[n] = s_past[n] + bias_ref[:, 0:MOBA_BLOCK] + jnp.where(sel[n], 0.0, NEG_INF)
    m = jnp.max(s_own, axis=1, keepdims=True)
    for s in s_past:
        m = jnp.maximum(m, jnp.max(s, axis=1, keepdims=True))
    p_own = jnp.exp(s_own - m)
    l = jnp.sum(p_own, axis=1, keepdims=True)
    acc = jnp.dot(_bf(p_own), _bf(vpad[...]), preferred_element_type=F32)
    for n in range(N_PAST_BLK):
        p = jnp.exp(s_past[n] - m)
        l = l + jnp.sum(p, axis=1, keepdims=True)
        acc = acc + _dg(_bf(p), _bf(vt_buf[slot, :, blocks[n]]), 1, 1)
    o = jnp.where(own_head, acc * (1.0 / l), 0.0)
    o = jnp.sum(o.reshape(t_new, H_A, W_A), axis=1)
    o_ref[0] = o * _silu(ga_ref[0])


def _moba_sample(page_table, qrep, knew, vnew, ga, bias_s, bfar, cache_kt, cache_vt):
    nseq, t_new, _ = knew.shape
    tok = pl.BlockSpec((1, t_new, W_A), lambda b, pt: (b, 0, 0))
    grid_spec = pltpu.PrefetchScalarGridSpec(
        num_scalar_prefetch=1,
        grid=(nseq,),
        in_specs=[pl.BlockSpec((1, N_ROWS_S, W_A), lambda b, pt: (b, 0, 0)),
                  tok, tok, tok,
                  pl.BlockSpec(bias_s.shape, lambda b, pt: (0, 0)),
                  pl.BlockSpec((N_ROWS_S, 1), lambda b, pt: (0, 0)),
                  pl.BlockSpec(memory_space=pl.ANY),
                  pl.BlockSpec(memory_space=pl.ANY)],
        out_specs=tok,
        scratch_shapes=[pltpu.VMEM((2, W_A, PAST_LEN), F32),
                        pltpu.VMEM((2, W_A, PAST_LEN), F32),
                        pltpu.VMEM((PAGE_SIZE, W_A), F32),
                        pltpu.VMEM((PAGE_SIZE, W_A), F32),
                        pltpu.SemaphoreType.DMA((2, 2))],
    )
    return pl.pallas_call(
        _moba_s_kernel,
        grid_spec=grid_spec,
        out_shape=jax.ShapeDtypeStruct((nseq, t_new, W_A), F32),
        compiler_params=_cparams(1),
        name="moba_sample",
    )(page_table, qrep, knew, vnew, ga, bias_s, bfar, cache_kt, cache_vt)


def _pad_rows(ref, scratch, s, t):
    if t == CHUNK:
        return ref[s]
    scratch[s] = jnp.zeros(scratch.shape[1:], scratch.dtype)
    scratch[s, 0:t, :] = ref[s].astype(scratch.dtype)
    return scratch[s]


def _ret_kernel(q_ref, k_ref, v_ref, g_ref, st0_ref, dmat_ref, qdec_ref, kdec_ref, gl_ref,
                o_ref, st_ref, qpad, kpad, vpad, gpad, *, t, lq, nbs):
    c = pl.program_id(1)

    @pl.when(c == 0)
    def _():
        st_ref[...] = st0_ref[...]

    lane = lax.broadcasted_iota(jnp.int32, (1, 128), 1)
    rowsel = lax.broadcasted_iota(jnp.int32, (128, 1), 0) < DK_B
    work = []
    for s in range(nbs):
        q = _pad_rows(q_ref, qpad, s, t)[0:lq]
        k = _pad_rows(k_ref, kpad, s, t)
        v = _pad_rows(v_ref, vpad, s, t)
        for hp in range(H_B // 2):
            cols = slice(hp * 128, (hp + 1) * 128)
            kp = _bf(k[:, cols])
            st = st_ref[s, cols, :]
            kd = _bf(k[:, cols] * kdec_ref[:, cols])
            for hh in range(2):
                h = 2 * hp + hh
                qm = jnp.where((lane // DK_B) == hh, q[:, cols], 0.0)
                vh = _bf(v[:, h * DV_B:(h + 1) * DV_B])
                sc = _dg(_bf(qm), kp, 1, 1)
                so = _mm(qm * qdec_ref[0:lq, cols], st)
                upd = _dg(kd, vh, 0, 0)
                work.append((s, hp, hh, sc, so, upd, vh, st))
    outs = {}
    for (s, hp, hh, sc, so, upd, vh, st) in work:
        h = 2 * hp + hh
        o = jnp.dot(_bf(sc * dmat_ref[h, 0:lq, :]), vh, preferred_element_type=F32) + so
        outs[(s, h)] = o * lax.rsqrt(jnp.mean(o * o, axis=-1, keepdims=True) + EPS)
    for i in range(0, len(work), 2):
        s, hp, _, _, _, upd0, _, st = work[i]
        cols = slice(hp * 128, (hp + 1) * 128)
        st_ref[s, cols, :] = st * gl_ref[cols, :] + jnp.where(rowsel, upd0, work[i + 1][5])
    for s in range(nbs):
        g = _pad_rows(g_ref, gpad, s, t)[0:lq].astype(F32)
        o = jnp.concatenate([outs[(s, h)] for h in range(H_B)], axis=1)
        o_ref[s] = (o * _silu(g))[0:t].astype(o_ref.dtype)


def _retention(q, k, v, g, st0, dmat, qdec, kdec, gl, lq, nbs):
    nb, nc, t, _ = q.shape
    row = lambda c_: pl.BlockSpec((nbs, None, t, c_), lambda b, c: (b, c, 0, 0))
    const = lambda a: pl.BlockSpec(a.shape, lambda b, c: (0,) * a.ndim)
    st_spec = pl.BlockSpec((nbs, H_B * DK_B, DV_B), lambda b, c: (b, 0, 0))
    pad = lambda c_: pltpu.VMEM((nbs, CHUNK, c_), F32)
    return pl.pallas_call(
        functools.partial(_ret_kernel, t=t, lq=lq, nbs=nbs),
        grid=(nb // nbs, nc),
        in_specs=[row(256), row(256), row(512), row(512), st_spec,
                  const(dmat), const(qdec), const(kdec), const(gl)],
        out_specs=(row(512), st_spec),
        out_shape=(jax.ShapeDtypeStruct((nb, nc, t, 512), q.dtype),
                   jax.ShapeDtypeStruct((nb, H_B * DK_B, DV_B), F32)),
        scratch_shapes=[pad(256), pad(256), pad(512), pad(512)],
        compiler_params=_cparams(2),
        name="retention",
    )(q, k, v, g, st0, dmat, qdec, kdec, gl)


def _out_kernel(a_ref, b_ref, x_ref, g_ref, w_ref, o_ref):
    half = w_ref.shape[0] // 2
    y = (jnp.dot(_bf(a_ref[0]), w_ref[0:half, :], preferred_element_type=F32)
         + jnp.dot(_bf(b_ref[0]), w_ref[half:, :], preferred_element_type=F32))
    o_ref[0] = x_ref[0] + g_ref[0] * y


def _out_proj(a, b, x, gate, w_bf, per_row_mod):
    nb, s, d = x.shape
    tm = min(OUT_ROWS, s)
    if per_row_mod:
        g_spec = pl.BlockSpec((1, tm, d), lambda bb, i: (bb, i, 0))
    else:
        g_spec = pl.BlockSpec((1, 1, d), lambda bb, i: (bb, 0, 0))
    row = lambda c: pl.BlockSpec((1, tm, c), lambda bb, i: (bb, i, 0))
    return pl.pallas_call(
        _out_kernel,
        grid=(nb, s // tm),
        in_specs=[row(512), row(512), row(d), g_spec, pl.BlockSpec(w_bf.shape, lambda bb, i: (0, 0))],
        out_specs=row(d),
        out_shape=jax.ShapeDtypeStruct((nb, s, d), F32),
        compiler_params=_cparams(2),
        name="out_proj",
    )(a, b, x, gate, w_bf)


def _odd_in_kernel(x_ref, sc_ref, sh_ref, nw_ref, w_ref, sguw_ref, sgub_ref, dtb_ref,
                   oc_ref, zg_ref, xbc_ref, dt_ref, *maybe_v_ref):
    x = x_ref[0]
    tm = x.shape[0]
    ms = jnp.mean(x * x, axis=-1, keepdims=True)
    h = (x * lax.rsqrt(ms + EPS) * nw_ref[...]) * (1.0 + sc_ref[0]) + sh_ref[0]
    hb = _bf(h)

    def proj(lo, hi):
        return jnp.dot(hb, w_ref[:, lo:hi], preferred_element_type=F32)

    u = _gelu_tanh(proj(0, 512))
    v = _gelu_tanh(proj(512, 1024))
    mu = jnp.mean(v, axis=-1, keepdims=True)
    vc = v - mu
    v = vc * lax.rsqrt(jnp.mean(vc * vc, axis=-1, keepdims=True) + EPS)
    for v_ref in maybe_v_ref:
        v_ref[0] = v
    ii = lax.broadcasted_iota(jnp.int32, (CHUNK, CHUNK), 0)
    jj = lax.broadcasted_iota(jnp.int32, (CHUNK, CHUNK), 1)
    rows = []
    for ci in range(tm // CHUNK):
        cols = []
        for g in range(G_C):
            wg = jnp.where(ii >= jj, sguw_ref[g], 0.0)
            cols.append(_mm(wg, v[ci * CHUNK:(ci + 1) * CHUNK, g * 128:(g + 1) * 128]))
        rows.append(jnp.concatenate(cols, axis=1) + sgub_ref[...])
    sg = jnp.concatenate(rows, axis=0) if len(rows) > 1 else rows[0]
    oc_ref[0] = (u * sg * _silu(proj(1024, 1536))).astype(oc_ref.dtype)
    zg_ref[0] = proj(1536, 2048).astype(zg_ref.dtype)
    xbc_ref[0] = proj(2048, 3072)
    dt_ref[0] = _softplus(proj(3072, 3584) + dtb_ref[...])


def _odd_in(x, scale, shift, norm_w, w_bf, sgu_w, sgu_b_tab, dt_bias, per_row_mod, act, emit_v):
    nb, s, d = x.shape
    tm = PROJ_ROWS
    if per_row_mod:
        mod_spec = pl.BlockSpec((1, tm, d), lambda b, i: (b, i, 0))
    else:
        mod_spec = pl.BlockSpec((1, 1, d), lambda b, i: (b, 0, 0))
    row = lambda c: pl.BlockSpec((1, tm, c), lambda b, i: (b, i, 0))
    const = lambda shp: pl.BlockSpec(shp, lambda b, i: (0,) * len(shp))
    return pl.pallas_call(
        _odd_in_kernel,
        grid=(nb, s // tm),
        in_specs=[row(d), mod_spec, mod_spec, const((1, d)), const(w_bf.shape),
                  const(sgu_w.shape), const(sgu_b_tab.shape), const((1, 512))],
        out_specs=(row(512), row(512), row(1024), row(512)) + ((row(512),) if emit_v else ()),
        out_shape=(jax.ShapeDtypeStruct((nb, s, 512), act),
                   jax.ShapeDtypeStruct((nb, s, 512), act),
                   jax.ShapeDtypeStruct((nb, s, 1024), F32),
                   jax.ShapeDtypeStruct((nb, s, 512), F32),
                   ) + ((jax.ShapeDtypeStruct((nb, s, 512), F32),) if emit_v else ()),
        compiler_params=_cparams(2),
        name="odd_in",
    )(x, scale, shift, norm_w, w_bf, sgu_w, sgu_b_tab, dt_bias)


def _ssd_kernel(xbc_ref, dt_ref, zg_ref, tail_ref, st0_ref, cw_ref, cb_ref, alog_ref, dsk_ref, nw_ref,
                tri_ref, sel_ref, y_ref, st_ref, ext, dtpad, zpad, *, t, lq, nc, nbs):
    c = pl.program_id(1)

    @pl.when(c == 0)
    def _():
        st_ref[...] = st0_ref[...]
        ext[...] = jnp.zeros(ext.shape, F32)
        ext[:, 0:8, :] = tail_ref[...]

    ii = lax.broadcasted_iota(jnp.int32, (lq, CHUNK), 0)
    jj = lax.broadcasted_iota(jnp.int32, (lq, CHUNK), 1)
    lane = lax.broadcasted_iota(jnp.int32, (1, 128), 1)
    hpg = H_D // G_D
    neg_a = -jnp.exp(alog_ref[...])
    seqs = []
    for s in range(nbs):
        ext[s, 8:8 + t, :] = xbc_ref[s]
        conv = cb_ref[...]
        for w in range(CONV_W):
            conv = conv + ext[s, pl.ds(8 - (CONV_W - 1) + w, CHUNK), :] * cw_ref[w:w + 1, :]
        if nc > 1:
            ext[s, 0:8, :] = ext[s, CHUNK:CHUNK + 8, :]
        xc = _silu(conv)
        dt = _pad_rows(dt_ref, dtpad, s, t)
        cum = _mm_exact_lhs(tri_ref[...], dt * neg_a, 1, 0)
        seqs.append((xc, dt, cum))
    st1 = []
    for s in range(nbs):
        xc, dt, cum = seqs[s]
        xh = xc[:, 0:W_D]
        last = cum[CHUNK - 1:CHUNK, :]
        dtx = _bf(xh * dt)
        xw = _bf(xh * (jnp.exp(last - cum) * dt))
        cum_rows = _mm_exact_lhs(sel_ref[...], cum, 1, 1)
        per_g = []
        for g in range(G_D):
            bg = _bf(xc[:, W_D + g * N_D:W_D + (g + 1) * N_D])
            cg = _bf(xc[0:lq, W_D + G_D * N_D + g * N_D:W_D + G_D * N_D + (g + 1) * N_D])
            gr = slice(g * hpg * P_D, (g + 1) * hpg * P_D)
            cb = _dg(cg, bg, 1, 1)
            yoff = _dg(cg, _bf(st_ref[s, gr, :]), 1, 1)
            upd = _dg(xw[:, gr], bg, 0, 0)
            per_g.append((cb, yoff, upd))
        st1.append((dtx, cum_rows, per_g))
    for s in range(nbs):
        xc, dt, cum = seqs[s]
        dtx, cum_rows, per_g = st1[s]
        ecum = jnp.exp(cum[0:lq])
        elast = jnp.exp(cum[CHUNK - 1:CHUNK, :])
        ys = []
        for g in range(G_D):
            cb, yoff, upd = per_g[g]
            for pr in range(hpg // 2):
                l0 = g * hpg * P_D + pr * 128
                yh = []
                for hh in range(2):
                    h = g * hpg + pr * 2 + hh
                    col = jnp.broadcast_to(cum[0:lq, h * P_D:h * P_D + 1], (lq, CHUNK))
                    seg = jnp.minimum(col - cum_rows[h:h + 1, :], 0.0)
                    mh = jnp.where(ii >= jj, cb * jnp.exp(seg), 0.0)
                    yh.append(jnp.dot(_bf(mh), dtx[:, l0:l0 + 128], preferred_element_type=F32))
                ypair = jnp.where(lane < P_D, yh[0], yh[1])
                ys.append(ypair + yoff[:, pr * 128:(pr + 1) * 128] * ecum[:, l0:l0 + 128])
            for hl in range(hpg):
                h = g * hpg + hl
                r = slice(h * P_D, (h + 1) * P_D)
                dec = jnp.broadcast_to(elast[0:1, h * P_D:h * P_D + 1], (P_D, N_D))
                st_ref[s, r, :] = st_ref[s, r, :] * dec + upd[hl * P_D:(hl + 1) * P_D, :]
        y = jnp.concatenate(ys, axis=1)
        zg = _pad_rows(zg_ref, zpad, s, t)[0:lq].astype(F32)
        y = (y + xc[0:lq, 0:W_D] * dsk_ref[...]) * _silu(zg)
        gw = W_D // G_D
        outs = []
        for g in range(G_D):
            yg = y[:, g * gw:(g + 1) * gw]
            outs.append(yg * lax.rsqrt(jnp.mean(yg * yg, axis=-1, keepdims=True) + EPS))
        y_ref[s] = (jnp.concatenate(outs, axis=1) * nw_ref[...])[0:t].astype(y_ref.dtype)


def _ssd(xbc, dt, zg, tail, st0, conv_w, conv_b, a_log, d_skip, norm_w, tri, sel, lq, nbs):
    nb, nc, t, _ = xbc.shape
    row = lambda c_: pl.BlockSpec((nbs, None, t, c_), lambda b, c: (b, c, 0, 0))
    const = lambda a: pl.BlockSpec(a.shape, lambda b, c: (0,) * a.ndim)
    st_spec = pl.BlockSpec((nbs, H_D * P_D, N_D), lambda b, c: (b, 0, 0))
    return pl.pallas_call(
        functools.partial(_ssd_kernel, t=t, lq=lq, nc=nc, nbs=nbs),
        grid=(nb // nbs, nc),
        in_specs=[row(1024), row(512), row(512),
                  pl.BlockSpec((nbs, 8, CONV_DIM), lambda b, c: (b, 0, 0)), st_spec,
                  const(conv_w), const(conv_b), const(a_log), const(d_skip), const(norm_w),
                  const(tri), const(sel)],
        out_specs=(row(512), st_spec),
        out_shape=(jax.ShapeDtypeStruct((nb, nc, t, 512), zg.dtype),
                   jax.ShapeDtypeStruct((nb, H_D * P_D, N_D), F32)),
        scratch_shapes=[pltpu.VMEM((nbs, CHUNK + 8, CONV_DIM), F32),
                        pltpu.VMEM((nbs, CHUNK, 512), F32), pltpu.VMEM((nbs, CHUNK, 512), F32)],
        compiler_params=_cparams(2),
        name="ssd",
    )(xbc, dt, zg, tail, st0, conv_w, conv_b, a_log, d_skip, norm_w, tri, sel)


def _rotary_tables(pos):
    half = DK_B // 2
    inv = 1.0 / (10000.0 ** (jnp.arange(half, dtype=F32) / half))
    ang = pos.astype(F32)[:, None] * inv[None, :]
    cos, sin = jnp.cos(ang), jnp.sin(ang)
    cos_t = jnp.tile(jnp.concatenate([cos, cos], axis=1), (1, H_B))
    sin_t = jnp.tile(jnp.concatenate([-sin, sin], axis=1), (1, H_B))
    return cos_t, sin_t


def _retention_tables(chunk_len):
    log_g = np.log(1.0 - 2.0 ** (-5.0 - np.arange(H_B, dtype=np.float64)))
    idx = np.arange(CHUNK, dtype=np.float64)
    diff = idx[:, None] - idx[None, :]
    dmat = np.where(diff[None] >= 0, np.exp(np.maximum(diff, 0.0)[None] * log_g[:, None, None]), 0.0)
    qdec = np.exp((idx + 1.0)[:, None] * log_g[None, :])
    kdec = np.where(idx[:, None] < chunk_len, np.exp((chunk_len - 1.0 - idx)[:, None] * log_g[None, :]), 0.0)
    gl = np.exp(chunk_len * log_g)
    return (jnp.asarray(dmat, F32),
            jnp.asarray(np.repeat(qdec, DK_B, axis=1), F32),
            jnp.asarray(np.repeat(kdec, DK_B, axis=1), F32),
            jnp.asarray(np.repeat(np.repeat(gl, DK_B)[:, None], DV_B, axis=1), F32))


def _prompt_bias_idx():
    kk = np.arange(MOBA_BLOCK)[:, None]
    qq = np.arange(MOBA_BLOCK)[None, :]
    diag = np.where(qq >= kk, _t5_bucket_np(qq - kk), -1)
    sub = _t5_bucket_np(qq + MOBA_BLOCK - kk)
    return np.concatenate([diag, sub], axis=0).astype(np.int32)


def _sample_bias_idx(t_new):
    row_t = (np.arange(N_ROWS_S) // H_A)[:, None]
    qpos = PAST_LEN + row_t
    near = _t5_bucket_np(qpos - (PAST_LEN - MOBA_BLOCK + np.arange(MOBA_BLOCK))[None, :])
    own_k = np.arange(PAGE_SIZE)[None, :]
    own = np.where((own_k <= row_t) & (own_k < t_new), _t5_bucket_np(row_t - own_k), -1)
    return np.concatenate([near, own], axis=1).astype(np.int32)


def kernel(x_prompt, x_sample, cache_k, cache_v, state_ret, state_ssm, state_conv, page_table, c_prompt, c_sample, rel_bias, e_norm_w, e_ada_w, e_ada_b, e_in_w, e_q_norm_w, e_k_norm_w, e_out_w, o_norm_w, o_ada_w, o_ada_b, o_in_w, o_sgu_w, o_sgu_b, o_conv_w, o_conv_b, o_dt_bias, o_A_log, o_D, o_ssm_norm_w, o_out_w):
    bp, s_len, d = x_prompt.shape
    bs, t_len, _ = x_sample.shape
    n_s = bs * t_len

    c_all = jnp.concatenate([c_prompt, c_sample, jnp.zeros((8 - (bp + bs) % 8, d), F32)], axis=0)
    mods = []
    for ada_w, ada_b in ((e_ada_w[0], e_ada_b[0]), (o_ada_w[0], o_ada_b[0])):
        mod = _ada_mod(c_all, ada_w, ada_b)
        parts_p = [mod[:bp, i * d:(i + 1) * d].reshape(bp, 1, d) for i in range(3)]
        parts_s = [jnp.repeat(mod[bp:bp + bs, i * d:(i + 1) * d], t_len, axis=0).reshape(1, n_s, d) for i in range(3)]
        mods.append((parts_p, parts_s))
    (e_mod_p, e_mod_s), (o_mod_p, o_mod_s) = mods

    seg = jnp.asarray(np.kron(np.eye(H_A), np.ones((HD_A, HD_A))), BF16)
    qnw = jnp.tile(e_q_norm_w[0], H_A).reshape(1, W_A)
    knw = jnp.tile(e_k_norm_w[0], H_A).reshape(1, W_A)
    e_in_bf = _bf(e_in_w[0])
    e_out_bf = _bf(e_out_w[0])
    o_in_bf = _bf(jnp.concatenate([o_in_w[0][:, :3072], jnp.repeat(o_in_w[0][:, 3072:], P_D, axis=1)], axis=1))
    o_out_bf = _bf(o_out_w[0])
    x_s = x_sample.reshape(1, n_s, d)
    cos_p, sin_p = _rotary_tables(jnp.arange(s_len))
    cos_s, sin_s = _rotary_tables(PAST_LEN + (jnp.arange(n_s) % t_len))
    bias_p = _bias_tables(rel_bias, _prompt_bias_idx(), LOG2E).reshape(H_A, 2, MOBA_BLOCK, MOBA_BLOCK)
    bias_s_h = _bias_tables(rel_bias, _sample_bias_idx(t_len))
    row_h = jnp.arange(N_ROWS_S) % H_A
    bias_s = jnp.sum(jnp.where((jnp.arange(H_A)[:, None] == row_h[None, :])[:, :, None], bias_s_h, 0.0), axis=0)
    bfar = rel_bias[NUM_BUCKETS - 1, row_h].reshape(N_ROWS_S, 1)

    (qa, ka, va, kbf, vt, ga, qb, kb, vb, gb, kmean) = _even_in(
        x_prompt, e_mod_p[1], e_mod_p[0], e_norm_w[0].reshape(1, d), e_in_bf, qnw, knw, seg, cos_p, sin_p, False, BF16)
    oa = _moba_prompt(rel_bias, qa, kbf, vt, kmean.reshape(bp, s_len // MOBA_BLOCK, W_A), bias_p, ga)
    nc_p = s_len // CHUNK
    ch = lambda a: a.reshape(bp, nc_p, CHUNK, a.shape[-1])
    ob, ret_p = _retention(ch(qb), ch(kb), ch(vb), ch(gb), jnp.zeros((bp, H_B * DK_B, DV_B), F32),
                           *_retention_tables(CHUNK), lq=CHUNK, nbs=bp)
    xp1 = _out_proj(oa, ob.reshape(bp, s_len, W_B), x_prompt, e_mod_p[2], e_out_bf, False)
    k_prompt = ka.reshape(1, bp, s_len, H_A, HD_A)
    v_prompt = va.reshape(1, bp, s_len, H_A, HD_A)
    ret_state_prompt = ret_p.reshape(1, bp, H_B, DK_B, DV_B)

    (qa_s, ka_s, va_s, _, _, ga_s, qb_s, kb_s, vb_s, gb_s, _) = _even_in(
        x_s, e_mod_s[1], e_mod_s[0], e_norm_w[0].reshape(1, d), e_in_bf, qnw, knw, seg, cos_s, sin_s, True, F32)
    sq = lambda a: a.reshape(bs, t_len, a.shape[-1])
    qrep = jnp.repeat(sq(qa_s), H_A, axis=1)
    n_phys = cache_k.shape[1]
    page_t = lambda c: jnp.transpose(c[0], (0, 2, 3, 1)).reshape(n_phys, W_A, PAGE_SIZE)
    oa_s = _moba_sample(page_table, qrep, sq(ka_s), sq(va_s), sq(ga_s), bias_s, bfar,
                        page_t(cache_k), page_t(cache_v))
    sc = lambda a: a.reshape(bs, 1, t_len, a.shape[-1])
    ob_s, ret_s = _retention(sc(qb_s), sc(kb_s), sc(vb_s), sc(gb_s),
                             state_ret[0].reshape(bs, H_B * DK_B, DV_B), *_retention_tables(t_len), lq=8, nbs=SEQ_PER_STEP)
    xs1 = _out_proj(oa_s.reshape(1, n_s, W_A), ob_s.reshape(1, n_s, W_B), x_s, e_mod_s[2], e_out_bf, True)
    k_sample = ka_s.reshape(1, bs, t_len, H_A, HD_A)
    v_sample = va_s.reshape(1, bs, t_len, H_A, HD_A)
    ret_state_sample = ret_s.reshape(1, bs, H_B, DK_B, DV_B)

    tri = jnp.asarray(np.tril(np.ones((CHUNK, CHUNK))), BF16)
    sel = jnp.asarray(np.kron(np.eye(H_D), np.eye(1, P_D)), BF16)
    rep = lambda a: jnp.repeat(a, P_D).reshape(1, W_D)
    dt_bias, a_log, d_skip = rep(o_dt_bias[0]), rep(o_A_log[0]), rep(o_D[0])
    ssm_nw = o_ssm_norm_w[0].reshape(1, W_D)
    conv_b = o_conv_b[0].reshape(1, CONV_DIM)
    o_nw = o_norm_w[0].reshape(1, d)

    sgu_b_p = jnp.repeat(o_sgu_b[0].T, W_C // G_C, axis=1)
    oc, zg, xbc, dtp = _odd_in(xp1, o_mod_p[1], o_mod_p[0], o_nw, o_in_bf, o_sgu_w[0], sgu_b_p, dt_bias,
                               False, BF16, False)
    yn, ssm_p = _ssd(ch(xbc), ch(dtp), ch(zg), jnp.zeros((bp, 8, CONV_DIM), F32),
                     jnp.zeros((bp, H_D * P_D, N_D), F32), o_conv_w[0], conv_b, a_log, d_skip, ssm_nw,
                     tri, sel, lq=CHUNK, nbs=bp)
    y_prompt = _out_proj(oc, yn.reshape(bp, s_len, W_D), xp1, o_mod_p[2], o_out_bf, False)
    ssm_state_prompt = ssm_p.reshape(1, bp, H_D, P_D, N_D)
    conv_state_prompt = xbc[:, -(CONV_W - 1):][None]

    per_chunk = CHUNK // t_len
    w_small = o_sgu_w[0][:, :t_len, :t_len]
    same_seq = jnp.asarray(np.kron(np.eye(per_chunk), np.ones((t_len, t_len))), F32)
    sgu_w_s = jnp.tile(w_small, (1, per_chunk, per_chunk)) * same_seq
    sgu_b_s = jnp.repeat(jnp.tile(o_sgu_b[0][:, :t_len].T, (per_chunk, 1)), W_C // G_C, axis=1)
    oc_s, zg_s, xbc_s, dt_s, v_s = _odd_in(xs1, o_mod_s[1], o_mod_s[0], o_nw, o_in_bf, sgu_w_s, sgu_b_s, dt_bias,
                                           True, F32, True)
    tail_s = jnp.concatenate([jnp.zeros((bs, 8 - (CONV_W - 1), CONV_DIM), F32), state_conv[0]], axis=1)
    yn_s, ssm_s = _ssd(sc(xbc_s), sc(dt_s), sc(zg_s), tail_s, state_ssm[0].reshape(bs, H_D * P_D, N_D),
                       o_conv_w[0], conv_b, a_log, d_skip, ssm_nw, tri, sel, lq=8, nbs=SEQ_PER_STEP)
    xs2 = _out_proj(oc_s, yn_s.reshape(1, n_s, W_D), xs1, o_mod_s[2], o_out_bf, True)
    y_sample = xs2.reshape(bs, t_len, d)
    sgu_v_sample = v_s.reshape(1, bs, t_len, W_C)
    ssm_state_sample = ssm_s.reshape(1, bs, H_D, P_D, N_D)
    xin = jnp.concatenate([state_conv[0], xbc_s.reshape(bs, t_len, CONV_DIM)], axis=1)
    conv_state_sample = xin[:, -(CONV_W - 1):][None]

    return (y_prompt, y_sample, k_prompt, v_prompt, k_sample, v_sample, ret_state_prompt, ret_state_sample,
            sgu_v_sample, ssm_state_prompt, ssm_state_sample, conv_state_prompt, conv_state_sample)
```

```python
import functools
import math

import numpy as np
import jax
import jax.numpy as jnp
from jax import lax
from jax.experimental import pallas as pl
from jax.experimental.pallas import tpu as pltpu

F32 = jnp.float32
BF16 = jnp.bfloat16

D_MODEL = 1024
PAST_LEN = 2048
PAGE_SIZE = 128
H_A, HD_A, W_A = 8, 64, 512
MOBA_BLOCK = 256
MOBA_TOPK = 3
NUM_BUCKETS = 32
MAX_DISTANCE = 128
H_B, DK_B, DV_B, W_B = 4, 64, 128, 512
G_C, W_C = 4, 512
H_D, P_D, N_D, G_D, W_D = 8, 64, 128, 2, 512
CONV_W = 4
CONV_DIM = 1024
CHUNK = 128
SEQ_PER_STEP = 8
PROJ_ROWS = 512
OUT_ROWS = 1024
NEG_INF = -1e30
EPS = 1e-6
LOG2E = math.log2(math.e)
VMEM_LIMIT = 56 * 1024 * 1024


def _bf(x):
    return x.astype(BF16)


def _dg(a, b, ca, cb):
    return lax.dot_general(a, b, (((ca,), (cb,)), ((), ())), preferred_element_type=F32)


def _mm(a, b):
    return _dg(_bf(a), _bf(b), 1, 0)


def _mm_nt(a, b):
    return _dg(_bf(a), _bf(b), 1, 1)


def _mm_tn(a, b):
    return _dg(_bf(a), _bf(b), 0, 0)


def _split2(x):
    hi = _bf(x)
    return hi, _bf(x - hi.astype(F32))


def _split3(x):
    hi = _bf(x)
    r = x - hi.astype(F32)
    mid = _bf(r)
    return hi, mid, _bf(r - mid.astype(F32))


def _mm_hp(a, b, ca, cb):
    ah, al = _split2(a)
    bh, bl = _split2(b)
    return _dg(ah, bh, ca, cb) + (_dg(ah, bl, ca, cb) + _dg(al, bh, ca, cb))


def _mm_exact_lhs(e, x, ca, cb):
    h, m, l = _split3(x)
    return _dg(e, h, ca, cb) + (_dg(e, m, ca, cb) + _dg(e, l, ca, cb))


def _silu(x):
    return x * (1.0 / (1.0 + jnp.exp(-x)))


def _gelu_tanh(x):
    return 0.5 * x * (1.0 + jnp.tanh(math.sqrt(2.0 / math.pi) * (x + 0.044715 * (x * x * x))))


def _softplus(x):
    return jnp.maximum(x, 0.0) + jnp.log1p(jnp.exp(-jnp.abs(x)))


def _cparams(n_grid):
    return pltpu.CompilerParams(dimension_semantics=("arbitrary",) * n_grid,
                                vmem_limit_bytes=VMEM_LIMIT)


def _top3_rows(g, blk, nblk):
    sel = jnp.zeros(g.shape, jnp.bool_)
    for _ in range(MOBA_TOPK):
        m = jnp.max(g, axis=0, keepdims=True)
        idx = jnp.min(jnp.where(g == m, blk, nblk), axis=0, keepdims=True)
        pick = blk == idx
        sel = jnp.logical_or(sel, pick)
        g = jnp.where(pick, -jnp.inf, g)
    return sel


def _ada_kernel(c_ref, w_ref, b_ref, o_ref):
    s = _silu(c_ref[...])
    o_ref[...] = _mm_hp(s, w_ref[...], 1, 0) + b_ref[...]


def _ada_mod(c_all, w, b):
    m, d = c_all.shape
    n = w.shape[1]
    tn = 512
    return pl.pallas_call(
        _ada_kernel,
        grid=(n // tn,),
        in_specs=[pl.BlockSpec((m, d), lambda j: (0, 0)),
                  pl.BlockSpec((d, tn), lambda j: (0, j)),
                  pl.BlockSpec((1, tn), lambda j: (0, j))],
        out_specs=pl.BlockSpec((m, tn), lambda j: (0, j)),
        out_shape=jax.ShapeDtypeStruct((m, n), F32),
        compiler_params=_cparams(1),
        name="ada_mod",
    )(c_all, w, b.reshape(1, n))


def _t5_bucket_np(rel):
    n = np.maximum(rel, 0)
    max_exact = NUM_BUCKETS // 2
    nf = np.maximum(n, 1).astype(np.float64)
    large = max_exact + (np.log(nf / max_exact) / math.log(MAX_DISTANCE / max_exact)
                         * (NUM_BUCKETS - max_exact)).astype(np.int64)
    large = np.minimum(large, NUM_BUCKETS - 1)
    return np.where(n < max_exact, n, large).astype(np.int32)


def _bias_kernel(tab_ref, idx_ref, o_ref, *, scale):
    h = pl.program_id(0)
    idx = idx_ref[...]
    acc = jnp.zeros(idx.shape, F32)
    for b in range(NUM_BUCKETS):
        acc = jnp.where(idx == b, tab_ref[b, h], acc)
    o_ref[0] = jnp.where(idx == -1, NEG_INF, acc * scale)


def _bias_tables(rel_bias, idx, scale=1.0):
    r, c = idx.shape
    return pl.pallas_call(
        functools.partial(_bias_kernel, scale=scale),
        grid=(H_A,),
        in_specs=[pl.BlockSpec(memory_space=pltpu.SMEM),
                  pl.BlockSpec((r, c), lambda h: (0, 0))],
        out_specs=pl.BlockSpec((1, r, c), lambda h: (h, 0, 0)),
        out_shape=jax.ShapeDtypeStruct((H_A, r, c), F32),
        compiler_params=_cparams(1),
        name="t5_bias",
    )(rel_bias, jnp.asarray(idx))


def _even_in_kernel(x_ref, sc_ref, sh_ref, nw_ref, w_ref, qnw_ref, knw_ref, seg_ref, cos_ref, sin_ref,
                    qa_ref, ka_ref, va_ref, kbf_ref, vt_ref, ga_ref, qb_ref, kb_ref, vb_ref, gb_ref, km_ref):
    x = x_ref[0]
    ms = jnp.mean(x * x, axis=-1, keepdims=True)
    h = (x * lax.rsqrt(ms + EPS) * nw_ref[...]) * (1.0 + sc_ref[0]) + sh_ref[0]
    hb = _bf(h)

    def proj(lo, hi):
        return jnp.dot(hb, w_ref[:, lo:hi], preferred_element_type=F32)

    def head_rms(t, w_row):
        ss = jnp.dot(_bf(t * t), seg_ref[...], preferred_element_type=F32)
        return t * lax.rsqrt(ss * (1.0 / HD_A) + EPS) * w_row

    qa_ref[0] = head_rms(proj(0, 512), qnw_ref[...])
    ka = head_rms(proj(512, 1024), knw_ref[...])
    ka_ref[0] = ka
    kbf_ref[0] = _bf(ka)
    for j in range(ka.shape[0] // MOBA_BLOCK):
        km_ref[0, j] = jnp.mean(ka[j * MOBA_BLOCK:(j + 1) * MOBA_BLOCK], axis=0, keepdims=True)
    va = proj(1024, 1536)
    va_ref[0] = va
    vat = va.T
    ones_pad = (lax.broadcasted_iota(jnp.int32, (V_ROWS - HD_A, vat.shape[1]), 0) == 0).astype(F32)
    vt_ref[0] = _bf(jnp.concatenate(
        [piece for h in range(H_A) for piece in (vat[h * HD_A:(h + 1) * HD_A], ones_pad)], axis=0))
    ga_ref[0] = proj(1536, 2048).astype(ga_ref.dtype)

    lane = lax.broadcasted_iota(jnp.int32, (1, 256), 1) % DK_B
    first_half = lane < (DK_B // 2)
    cos = cos_ref[...]
    sin = sin_ref[...]

    def rotary(t):
        up = pltpu.roll(t, 256 - DK_B // 2, 1)
        dn = pltpu.roll(t, DK_B // 2, 1)
        return t * cos + jnp.where(first_half, up, dn) * sin

    qb_ref[0] = rotary(proj(2048, 2304)).astype(qb_ref.dtype)
    kb_ref[0] = (rotary(proj(2304, 2560)) * (DK_B ** -0.5)).astype(kb_ref.dtype)
    vb_ref[0] = proj(2560, 3072).astype(vb_ref.dtype)
    gb_ref[0] = proj(3072, 3584).astype(gb_ref.dtype)


def _even_in(x, scale, shift, norm_w, w_bf, qnw, knw, seg, cos, sin, per_row_mod, act):
    nb, s, d = x.shape
    tm = PROJ_ROWS
    ns = s // tm
    nkb = tm // MOBA_BLOCK
    if per_row_mod:
        mod_spec = pl.BlockSpec((1, tm, d), lambda b, i: (b, i, 0))
    else:
        mod_spec = pl.BlockSpec((1, 1, d), lambda b, i: (b, 0, 0))
    row = lambda c: pl.BlockSpec((1, tm, c), lambda b, i: (b, i, 0))
    const = lambda shp: pl.BlockSpec(shp, lambda b, i: (0,) * len(shp))
    out_shape = (
        jax.ShapeDtypeStruct((nb, s, 512), F32),
        jax.ShapeDtypeStruct((nb, s, 512), F32),
        jax.ShapeDtypeStruct((nb, s, 512), F32),
        jax.ShapeDtypeStruct((nb, s, 512), BF16),
        jax.ShapeDtypeStruct((nb, H_A * V_ROWS, s), BF16),
        jax.ShapeDtypeStruct((nb, s, 512), act),
        jax.ShapeDtypeStruct((nb, s, 256), act),
        jax.ShapeDtypeStruct((nb, s, 256), act),
        jax.ShapeDtypeStruct((nb, s, 512), act),
        jax.ShapeDtypeStruct((nb, s, 512), act),
        jax.ShapeDtypeStruct((nb, ns * nkb, 1, 512), F32),
    )
    out_specs = (row(512), row(512), row(512), row(512),
                 pl.BlockSpec((1, H_A * V_ROWS, tm), lambda b, i: (b, 0, i)),
                 row(512), row(256), row(256), row(512), row(512),
                 pl.BlockSpec((1, nkb, 1, 512), lambda b, i: (b, i, 0, 0)))
    return pl.pallas_call(
        _even_in_kernel,
        grid=(nb, ns),
        in_specs=[row(d), mod_spec, mod_spec, const((1, d)), const((d, 3584)),
                  const((1, 512)), const((1, 512)), const((512, 512)),
                  pl.BlockSpec((tm, 256), lambda b, i: (i, 0)),
                  pl.BlockSpec((tm, 256), lambda b, i: (i, 0))],
        out_specs=out_specs,
        out_shape=out_shape,
        compiler_params=_cparams(2),
        name="even_in",
    )(x, scale, shift, norm_w, w_bf, qnw, knw, seg, cos, sin)


MOBA_HS = 8
V_ROWS = HD_A + 8
FAR_KEYS = 2 * MOBA_BLOCK


def _moba_prompt_tile(tab_ref, q_ref, k_ref, vt_ref, km_ref, bias_ref, ga_ref, o_ref, rbf_ref, rbs_ref, qs_ref):
    hg = pl.program_id(1)
    qi = pl.program_id(2)
    nblk = km_ref.shape[1]
    lane = lax.broadcasted_iota(jnp.int32, (1, 128), 1)
    blk = lax.broadcasted_iota(jnp.int32, (nblk, MOBA_BLOCK), 0)
    for hl in range(MOBA_HS):
        pr, hh = divmod(hl, 2)
        pc = slice(pr * 128, (pr + 1) * 128)
        qm = jnp.where((lane // HD_A) == hh, q_ref[0, :, pc], 0.0)
        gate = _mm_hp(km_ref[0, :, pc], qm, 1, 1)
        gate = jnp.where(blk < qi, gate, NEG_INF)
        sel = jnp.logical_and(_top3_rows(gate, blk, nblk), blk < qi)
        far_c = tab_ref[NUM_BUCKETS - 1, MOBA_HS * hg + hl] * LOG2E
        rbf_ref[hl] = jnp.where(jnp.logical_and(sel, blk < qi - 1), far_c, NEG_INF)
        rbs_ref[hl] = jnp.where(jnp.logical_or(sel, blk == qi), 0.0, NEG_INF)
        qs_ref[hl] = _bf(qm * (HD_A ** -0.5 * LOG2E))

    def visit(carry, off, nkeys, extra_fn=None, block_rows=None):
        ss = []
        for hl in range(MOBA_HS):
            pr = hl // 2
            kj = k_ref[0, pl.ds(off, nkeys), pr * 128:(pr + 1) * 128]
            ss.append(_dg(kj, qs_ref[hl], 1, 1))
        stats, ps = [], []
        for hl in range(MOBA_HS):
            m = carry[hl][0]
            if block_rows is None:
                s = extra_fn(hl, ss[hl])
                mn = jnp.maximum(m, jnp.max(s, axis=0, keepdims=True))
                p = jnp.exp2(s - mn)
            else:
                halves = [ss[hl][i * MOBA_BLOCK:(i + 1) * MOBA_BLOCK] for i in range(nkeys // MOBA_BLOCK)]
                rows = block_rows(hl)
                mn = m
                for sh, r in zip(halves, rows):
                    mn = jnp.maximum(mn, jnp.max(sh, axis=0, keepdims=True) + r)
                p = jnp.concatenate([jnp.exp2(sh - (mn - r)) for sh, r in zip(halves, rows)], axis=0)
            stats.append((mn, jnp.exp2(m - mn)))
            ps.append(_bf(p))
        pvs = []
        for hl in range(MOBA_HS):
            vj = vt_ref[0, hl * V_ROWS:(hl + 1) * V_ROWS, pl.ds(off, nkeys)]
            pvs.append(jnp.dot(vj, ps[hl], preferred_element_type=F32))
        return tuple((stats[hl][0], stats[hl][1] * carry[hl][1] + pvs[hl]) for hl in range(MOBA_HS))

    def far_body(jp, carry):
        off = pl.multiple_of(jp * FAR_KEYS, FAR_KEYS)

        def rows(hl):
            return rbf_ref[hl, pl.ds(2 * jp, 1), :], rbf_ref[hl, pl.ds(2 * jp + 1, 1), :]

        return visit(carry, off, FAR_KEYS, block_rows=rows)

    init = tuple((jnp.full((1, MOBA_BLOCK), -jnp.inf, F32), jnp.zeros((V_ROWS, MOBA_BLOCK), F32))
                 for _ in range(MOBA_HS))
    carry = lax.fori_loop(0, qi // 2, far_body, init)
    js = jnp.maximum(qi - 1, 0)
    first = qi == 0
    top_tab = jnp.where(first, 0, 1)
    bot_mask = jnp.where(first, NEG_INF, 0.0)

    def near_extra(hl, s):
        top = s[:MOBA_BLOCK] + bias_ref[hl, top_tab] + rbs_ref[hl, pl.ds(js, 1), :]
        bot = s[MOBA_BLOCK:] + (bias_ref[hl, 0] + bot_mask)
        return jnp.concatenate([top, bot], axis=0)

    carry = visit(carry, pl.multiple_of(js * MOBA_BLOCK, MOBA_BLOCK), FAR_KEYS, near_extra)
    for pr in range(MOBA_HS // 2):
        accs = [carry[2 * pr + hh][1] for hh in range(2)]
        outs = [a[0:HD_A] * (1.0 / a[HD_A:HD_A + 1]) for a in accs]
        o = jnp.concatenate(outs, axis=0).T
        pc = slice(pr * 128, (pr + 1) * 128)
        o_ref[0, :, pc] = (o * _silu(ga_ref[0, :, pc].astype(F32))).astype(o_ref.dtype)


N_PAST_BLK = PAST_LEN // MOBA_BLOCK
N_PAGES = PAST_LEN // PAGE_SIZE
N_ROWS_S = 32
SEQ_PER_TILE = 2


def _page_copies(pt_ref, ck_hbm, cv_hbm, kt_buf, vt_buf, sems, seq, sl):
    cps = []
    for p in range(N_PAGES):
        dst = pl.ds(p * PAGE_SIZE, PAGE_SIZE)
        cps.append(pltpu.make_async_copy(ck_hbm.at[pt_ref[seq, p]], kt_buf.at[sl, :, dst], sems.at[0, sl]))
        cps.append(pltpu.make_async_copy(cv_hbm.at[pt_ref[seq, p]], vt_buf.at[sl, :, dst], sems.at[1, sl]))
    return cps


def _moba_sample_seq(j, slot, qrep_ref, knew_ref, vnew_ref, ga_ref, bias_ref, bfar_ref, kt_buf, vt_buf, kpad, vpad,
                     o_ref):
    t_new = knew_ref.shape[1]
    rowh = lax.broadcasted_iota(jnp.int32, (N_ROWS_S, W_A), 0) % H_A
    laneh = lax.broadcasted_iota(jnp.int32, (N_ROWS_S, W_A), 1) // HD_A
    own_head = rowh == laneh
    qf = jnp.where(own_head, qrep_ref[j], 0.0) * (HD_A ** -0.5)
    qbd = _bf(qf)
    q2 = jnp.concatenate([qbd, _bf(qf - qbd.astype(F32))], axis=0)
    kpad[0:t_new, :] = knew_ref[j]
    vpad[0:t_new, :] = vnew_ref[j]
    blocks = [slice(n * MOBA_BLOCK, (n + 1) * MOBA_BLOCK) for n in range(N_PAST_BLK)]
    s_past = []
    for n in range(N_PAST_BLK):
        s2 = jnp.dot(q2, _bf(kt_buf[slot, :, blocks[n]]), preferred_element_type=F32)
        s_past.append(s2[0:N_ROWS_S] + s2[N_ROWS_S:])
    s_own = _dg(qbd, _bf(kpad[...]), 1, 1) + bias_ref[:, MOBA_BLOCK:]
    g = [jnp.sum(s, axis=1, keepdims=True) for s in s_past]
    sel = [jnp.zeros((N_ROWS_S, 1), jnp.bool_) for _ in range(N_PAST_BLK)]
    for _ in range(MOBA_TOPK):
        m = functools.reduce(jnp.maximum, g)
        idx = functools.reduce(jnp.minimum, [jnp.where(g[n] == m, n, N_PAST_BLK) for n in range(N_PAST_BLK)])
        for n in range(N_PAST_BLK):
            pick = idx == n
            sel[n] = jnp.logical_or(sel[n], pick)
            g[n] = jnp.where(pick, -jnp.inf, g[n])
    bfar = bfar_ref[...]
    for n in range(N_PAST_BLK):
        if n < N_PAST_BLK - 1:
            s_past[n] = s_past[n] + jnp.where(sel[n], bfar, NEG_INF)
        else:
            s_past[n] = s_past[n] + bias_ref[:, 0:MOBA_BLOCK] + jnp.where(sel[n], 0.0, NEG_INF)
    m = jnp.max(s_own, axis=1, keepdims=True)
    for s in s_past:
        m = jnp.maximum(m, jnp.max(s, axis=1, keepdims=True))
    p_own = jnp.exp(s_own - m)
    l = jnp.sum(p_own, axis=1, keepdims=True)
    acc = jnp.dot(_bf(p_own), _bf(vpad[...]), preferred_element_type=F32)
    for n in range(N_PAST_BLK):
        p = jnp.exp(s_past[n] - m)
        l = l + jnp.sum(p, axis=1, keepdims=True)
        acc = acc + _dg(_bf(p), _bf(vt_buf[slot, :, blocks[n]]), 1, 1)
    o = jnp.where(own_head, acc * (1.0 / l), 0.0)
    o = jnp.sum(o.reshape(t_new, H_A, W_A), axis=1)
    o_ref[j] = o * _silu(ga_ref[j])


def _moba_kernel(pt_ref, tab_ref, q_ref, k_ref, vt_ref, km_ref, bias_ref, ga_ref,
                 qrep_ref, knew_ref, vnew_ref, gas_ref, bias_s_ref, bfar_ref, ck_hbm, cv_hbm,
                 o_ref, os_ref, rbf_ref, rbs_ref, qs_ref, kt_buf, vt_buf, kpad, vpad, sems):
    step = pl.program_id(0) * pl.num_programs(2) + pl.program_id(2)
    n_steps = pl.num_programs(0) * pl.num_programs(2)
    seq0 = SEQ_PER_TILE * step
    copies = functools.partial(_page_copies, pt_ref, ck_hbm, cv_hbm, kt_buf, vt_buf, sems)
    attend = functools.partial(_moba_sample_seq, qrep_ref=qrep_ref, knew_ref=knew_ref, vnew_ref=vnew_ref,
                               ga_ref=gas_ref, bias_ref=bias_s_ref, bfar_ref=bfar_ref, kt_buf=kt_buf,
                               vt_buf=vt_buf, kpad=kpad, vpad=vpad, o_ref=os_ref)

    @pl.when(step == 0)
    def _():
        kpad[...] = jnp.zeros(kpad.shape, F32)
        vpad[...] = jnp.zeros(vpad.shape, F32)
        for cp in copies(0, 0):
            cp.start()

    for cp in copies(seq0 + 1, 1):
        cp.start()
    for cp in copies(seq0, 0):
        cp.wait()
    attend(0, 0)

    @pl.when(step + 1 < n_steps)
    def _():
        for cp in copies(seq0 + SEQ_PER_TILE, 0):
            cp.start()

    _moba_prompt_tile(tab_ref, q_ref, k_ref, vt_ref, km_ref, bias_ref, ga_ref, o_ref, rbf_ref, rbs_ref, qs_ref)

    for cp in copies(seq0 + 1, 1):
        cp.wait()
    attend(1, 1)


def _moba(page_table, rel_bias, qa, kbf, vt, kmean, bias_t, ga,
          qrep, knew, vnew, ga_s, bias_s, bfar, cache_kt, cache_vt):
    nb, s, _ = qa.shape
    nq = s // MOBA_BLOCK
    nseq, t_new, _ = knew.shape
    assert H_A == MOBA_HS and nseq == SEQ_PER_TILE * nb * nq
    tile = pl.BlockSpec((1, MOBA_BLOCK, W_A), lambda b, hg, i, pt: (b, i, 0))
    per_b = lambda shp: pl.BlockSpec((1,) + shp, lambda b, hg, i, pt: (b, 0, 0))
    seqs = lambda r: pl.BlockSpec((SEQ_PER_TILE, r, W_A), lambda b, hg, i, pt: (b * nq + i, 0, 0))
    const = lambda a: pl.BlockSpec(a.shape, lambda b, hg, i, pt: (0,) * a.ndim)
    grid_spec = pltpu.PrefetchScalarGridSpec(
        num_scalar_prefetch=1,
        grid=(nb, 1, nq),
        in_specs=[pl.BlockSpec(memory_space=pltpu.SMEM),
                  tile, per_b((s, W_A)), per_b((H_A * V_ROWS, s)), per_b((nq, W_A)), const(bias_t), tile,
                  seqs(N_ROWS_S), seqs(t_new), seqs(t_new), seqs(t_new), const(bias_s), const(bfar),
                  pl.BlockSpec(memory_space=pl.ANY), pl.BlockSpec(memory_space=pl.ANY)],
        out_specs=(tile, seqs(t_new)),
        scratch_shapes=[pltpu.VMEM((MOBA_HS, nq, MOBA_BLOCK), F32),
                        pltpu.VMEM((MOBA_HS, nq, MOBA_BLOCK), F32),
                        pltpu.VMEM((MOBA_HS, MOBA_BLOCK, 128), BF16),
                        pltpu.VMEM((2, W_A, PAST_LEN), F32),
                        pltpu.VMEM((2, W_A, PAST_LEN), F32),
                        pltpu.VMEM((PAGE_SIZE, W_A), F32),
                        pltpu.VMEM((PAGE_SIZE, W_A), F32),
                        pltpu.SemaphoreType.DMA((2, 2))],
    )
    return pl.pallas_call(
        _moba_kernel,
        grid_spec=grid_spec,
        out_shape=(jax.ShapeDtypeStruct((nb, s, W_A), BF16),
                   jax.ShapeDtypeStruct((nseq, t_new, W_A), F32)),
        compiler_params=_cparams(3),
        name="moba",
    )(page_table, rel_bias, qa, kbf, vt, kmean, bias_t, ga, qrep, knew, vnew, ga_s, bias_s, bfar, cache_kt, cache_vt)


def _pad_rows(ref, scratch, s, t):
    if t == CHUNK:
        return ref[s]
    scratch[s] = jnp.zeros(scratch.shape[1:], scratch.dtype)
    scratch[s, 0:t, :] = ref[s].astype(scratch.dtype)
    return scratch[s]


def _ret_kernel(q_ref, k_ref, v_ref, g_ref, st0_ref, dmat_ref, qdec_ref, kdec_ref, gl_ref,
                o_ref, st_ref, qpad, kpad, vpad, gpad, *, t, lq, nbs):
    c = pl.program_id(1)

    @pl.when(c == 0)
    def _():
        st_ref[...] = st0_ref[...]

    lane = lax.broadcasted_iota(jnp.int32, (1, 128), 1)
    rowsel = lax.broadcasted_iota(jnp.int32, (128, 1), 0) < DK_B
    work = []
    for s in range(nbs):
        q = _pad_rows(q_ref, qpad, s, t)[0:lq]
        k = _pad_rows(k_ref, kpad, s, t)
        v = _pad_rows(v_ref, vpad, s, t)
        for hp in range(H_B // 2):
            cols = slice(hp * 128, (hp + 1) * 128)
            kp = _bf(k[:, cols])
            st = st_ref[s, cols, :]
            kd = _bf(k[:, cols] * kdec_ref[:, cols])
            for hh in range(2):
                h = 2 * hp + hh
                qm = jnp.where((lane // DK_B) == hh, q[:, cols], 0.0)
                vh = _bf(v[:, h * DV_B:(h + 1) * DV_B])
                sc = _dg(_bf(qm), kp, 1, 1)
                so = _mm(qm * qdec_ref[0:lq, cols], st)
                upd = _dg(kd, vh, 0, 0)
                work.append((s, hp, hh, sc, so, upd, vh, st))
    outs = {}
    for (s, hp, hh, sc, so, upd, vh, st) in work:
        h = 2 * hp + hh
        o = jnp.dot(_bf(sc * dmat_ref[h, 0:lq, :]), vh, preferred_element_type=F32) + so
        outs[(s, h)] = o * lax.rsqrt(jnp.mean(o * o, axis=-1, keepdims=True) + EPS)
    for i in range(0, len(work), 2):
        s, hp, _, _, _, upd0, _, st = work[i]
        cols = slice(hp * 128, (hp + 1) * 128)
        st_ref[s, cols, :] = st * gl_ref[cols, :] + jnp.where(rowsel, upd0, work[i + 1][5])
    for s in range(nbs):
        g = _pad_rows(g_ref, gpad, s, t)[0:lq].astype(F32)
        o = jnp.concatenate([outs[(s, h)] for h in range(H_B)], axis=1)
        o_ref[s] = (o * _silu(g))[0:t].astype(o_ref.dtype)


def _retention(q, k, v, g, st0, dmat, qdec, kdec, gl, lq, nbs):
    nb, nc, t, _ = q.shape
    row = lambda c_: pl.BlockSpec((nbs, None, t, c_), lambda b, c: (b, c, 0, 0))
    const = lambda a: pl.BlockSpec(a.shape, lambda b, c: (0,) * a.ndim)
    st_spec = pl.BlockSpec((nbs, H_B * DK_B, DV_B), lambda b, c: (b, 0, 0))
    pad = lambda c_: pltpu.VMEM((nbs, CHUNK, c_), F32)
    return pl.pallas_call(
        functools.partial(_ret_kernel, t=t, lq=lq, nbs=nbs),
        grid=(nb // nbs, nc),
        in_specs=[row(256), row(256), row(512), row(512), st_spec,
                  const(dmat), const(qdec), const(kdec), const(gl)],
        out_specs=(row(512), st_spec),
        out_shape=(jax.ShapeDtypeStruct((nb, nc, t, 512), q.dtype),
                   jax.ShapeDtypeStruct((nb, H_B * DK_B, DV_B), F32)),
        scratch_shapes=[pad(256), pad(256), pad(512), pad(512)],
        compiler_params=_cparams(2),
        name="retention",
    )(q, k, v, g, st0, dmat, qdec, kdec, gl)


def _out_kernel(a_ref, b_ref, x_ref, g_ref, w_ref, o_ref):
    half = w_ref.shape[0] // 2
    y = (jnp.dot(_bf(a_ref[0]), w_ref[0:half, :], preferred_element_type=F32)
         + jnp.dot(_bf(b_ref[0]), w_ref[half:, :], preferred_element_type=F32))
    o_ref[0] = x_ref[0] + g_ref[0] * y


def _out_proj(a, b, x, gate, w_bf, per_row_mod):
    nb, s, d = x.shape
    tm = min(OUT_ROWS, s)
    if per_row_mod:
        g_spec = pl.BlockSpec((1, tm, d), lambda bb, i: (bb, i, 0))
    else:
        g_spec = pl.BlockSpec((1, 1, d), lambda bb, i: (bb, 0, 0))
    row = lambda c: pl.BlockSpec((1, tm, c), lambda bb, i: (bb, i, 0))
    return pl.pallas_call(
        _out_kernel,
        grid=(nb, s // tm),
        in_specs=[row(512), row(512), row(d), g_spec, pl.BlockSpec(w_bf.shape, lambda bb, i: (0, 0))],
        out_specs=row(d),
        out_shape=jax.ShapeDtypeStruct((nb, s, d), F32),
        compiler_params=_cparams(2),
        name="out_proj",
    )(a, b, x, gate, w_bf)


def _odd_in_kernel(x_ref, sc_ref, sh_ref, nw_ref, w_ref, sguw_ref, sgub_ref, dtb_ref,
                   oc_ref, zg_ref, xbc_ref, dt_ref, *maybe_v_ref):
    x = x_ref[0]
    tm = x.shape[0]
    ms = jnp.mean(x * x, axis=-1, keepdims=True)
    h = (x * lax.rsqrt(ms + EPS) * nw_ref[...]) * (1.0 + sc_ref[0]) + sh_ref[0]
    hb = _bf(h)

    def proj(lo, hi):
        return jnp.dot(hb, w_ref[:, lo:hi], preferred_element_type=F32)

    u = _gelu_tanh(proj(0, 512))
    v = _gelu_tanh(proj(512, 1024))
    mu = jnp.mean(v, axis=-1, keepdims=True)
    vc = v - mu
    v = vc * lax.rsqrt(jnp.mean(vc * vc, axis=-1, keepdims=True) + EPS)
    for v_ref in maybe_v_ref:
        v_ref[0] = v
    ii = lax.broadcasted_iota(jnp.int32, (CHUNK, CHUNK), 0)
    jj = lax.broadcasted_iota(jnp.int32, (CHUNK, CHUNK), 1)
    rows = []
    for ci in range(tm // CHUNK):
        cols = []
        for g in range(G_C):
            wg = jnp.where(ii >= jj, sguw_ref[g], 0.0)
            cols.append(_mm(wg, v[ci * CHUNK:(ci + 1) * CHUNK, g * 128:(g + 1) * 128]))
        rows.append(jnp.concatenate(cols, axis=1) + sgub_ref[...])
    sg = jnp.concatenate(rows, axis=0) if len(rows) > 1 else rows[0]
    oc_ref[0] = (u * sg * _silu(proj(1024, 1536))).astype(oc_ref.dtype)
    zg_ref[0] = proj(1536, 2048).astype(zg_ref.dtype)
    xbc_ref[0] = proj(2048, 3072)
    dt_ref[0] = _softplus(proj(3072, 3584) + dtb_ref[...])


def _odd_in(x, scale, shift, norm_w, w_bf, sgu_w, sgu_b_tab, dt_bias, per_row_mod, act, emit_v):
    nb, s, d = x.shape
    tm = PROJ_ROWS
    if per_row_mod:
        mod_spec = pl.BlockSpec((1, tm, d), lambda b, i: (b, i, 0))
    else:
        mod_spec = pl.BlockSpec((1, 1, d), lambda b, i: (b, 0, 0))
    row = lambda c: pl.BlockSpec((1, tm, c), lambda b, i: (b, i, 0))
    const = lambda shp: pl.BlockSpec(shp, lambda b, i: (0,) * len(shp))
    return pl.pallas_call(
        _odd_in_kernel,
        grid=(nb, s // tm),
        in_specs=[row(d), mod_spec, mod_spec, const((1, d)), const(w_bf.shape),
                  const(sgu_w.shape), const(sgu_b_tab.shape), const((1, 512))],
        out_specs=(row(512), row(512), row(1024), row(512)) + ((row(512),) if emit_v else ()),
        out_shape=(jax.ShapeDtypeStruct((nb, s, 512), act),
                   jax.ShapeDtypeStruct((nb, s, 512), act),
                   jax.ShapeDtypeStruct((nb, s, 1024), F32),
                   jax.ShapeDtypeStruct((nb, s, 512), F32),
                   ) + ((jax.ShapeDtypeStruct((nb, s, 512), F32),) if emit_v else ()),
        compiler_params=_cparams(2),
        name="odd_in",
    )(x, scale, shift, norm_w, w_bf, sgu_w, sgu_b_tab, dt_bias)


def _ssd_kernel(xbc_ref, dt_ref, zg_ref, tail_ref, st0_ref, cw_ref, cb_ref, alog_ref, dsk_ref, nw_ref,
                tri_ref, sel_ref, y_ref, st_ref, ext, dtpad, zpad, *, t, lq, nc, nbs):
    c = pl.program_id(1)

    @pl.when(c == 0)
    def _():
        st_ref[...] = st0_ref[...]
        ext[...] = jnp.zeros(ext.shape, F32)
        ext[:, 0:8, :] = tail_ref[...]

    ii = lax.broadcasted_iota(jnp.int32, (lq, CHUNK), 0)
    jj = lax.broadcasted_iota(jnp.int32, (lq, CHUNK), 1)
    lane = lax.broadcasted_iota(jnp.int32, (1, 128), 1)
    hpg = H_D // G_D
    neg_a = -jnp.exp(alog_ref[...])
    seqs = []
    for s in range(nbs):
        ext[s, 8:8 + t, :] = xbc_ref[s]
        conv = cb_ref[...]
        for w in range(CONV_W):
            conv = conv + ext[s, pl.ds(8 - (CONV_W - 1) + w, CHUNK), :] * cw_ref[w:w + 1, :]
        if nc > 1:
            ext[s, 0:8, :] = ext[s, CHUNK:CHUNK + 8, :]
        xc = _silu(conv)
        dt = _pad_rows(dt_ref, dtpad, s, t)
        cum = _mm_exact_lhs(tri_ref[...], dt * neg_a, 1, 0)
        seqs.append((xc, dt, cum))
    st1 = []
    for s in range(nbs):
        xc, dt, cum = seqs[s]
        xh = xc[:, 0:W_D]
        last = cum[CHUNK - 1:CHUNK, :]
        dtx = _bf(xh * dt)
        xw = _bf(xh * (jnp.exp(last - cum) * dt))
        cum_rows = _mm_exact_lhs(sel_ref[...], cum, 1, 1)
        per_g = []
        for g in range(G_D):
            bg = _bf(xc[:, W_D + g * N_D:W_D + (g + 1) * N_D])
            cg = _bf(xc[0:lq, W_D + G_D * N_D + g * N_D:W_D + G_D * N_D + (g + 1) * N_D])
            gr = slice(g * hpg * P_D, (g + 1) * hpg * P_D)
            cb = _dg(cg, bg, 1, 1)
            yoff = _dg(cg, _bf(st_ref[s, gr, :]), 1, 1)
            upd = _dg(xw[:, gr], bg, 0, 0)
            per_g.append((cb, yoff, upd))
        st1.append((dtx, cum_rows, per_g))
    for s in range(nbs):
        xc, dt, cum = seqs[s]
        dtx, cum_rows, per_g = st1[s]
        ecum = jnp.exp(cum[0:lq])
        elast = jnp.exp(cum[CHUNK - 1:CHUNK, :])
        ys = []
        for g in range(G_D):
            cb, yoff, upd = per_g[g]
            for pr in range(hpg // 2):
                l0 = g * hpg * P_D + pr * 128
                yh = []
                for hh in range(2):
                    h = g * hpg + pr * 2 + hh
                    col = jnp.broadcast_to(cum[0:lq, h * P_D:h * P_D + 1], (lq, CHUNK))
                    seg = jnp.minimum(col - cum_rows[h:h + 1, :], 0.0)
                    mh = jnp.where(ii >= jj, cb * jnp.exp(seg), 0.0)
                    yh.append(jnp.dot(_bf(mh), dtx[:, l0:l0 + 128], preferred_element_type=F32))
                ypair = jnp.where(lane < P_D, yh[0], yh[1])
                ys.append(ypair + yoff[:, pr * 128:(pr + 1) * 128] * ecum[:, l0:l0 + 128])
            for hl in range(hpg):
                h = g * hpg + hl
                r = slice(h * P_D, (h + 1) * P_D)
                dec = jnp.broadcast_to(elast[0:1, h * P_D:h * P_D + 1], (P_D, N_D))
                st_ref[s, r, :] = st_ref[s, r, :] * dec + upd[hl * P_D:(hl + 1) * P_D, :]
        y = jnp.concatenate(ys, axis=1)
        zg = _pad_rows(zg_ref, zpad, s, t)[0:lq].astype(F32)
        y = (y + xc[0:lq, 0:W_D] * dsk_ref[...]) * _silu(zg)
        gw = W_D // G_D
        outs = []
        for g in range(G_D):
            yg = y[:, g * gw:(g + 1) * gw]
            outs.append(yg * lax.rsqrt(jnp.mean(yg * yg, axis=-1, keepdims=True) + EPS))
        y_ref[s] = (jnp.concatenate(outs, axis=1) * nw_ref[...])[0:t].astype(y_ref.dtype)


def _ssd(xbc, dt, zg, tail, st0, conv_w, conv_b, a_log, d_skip, norm_w, tri, sel, lq, nbs):
    nb, nc, t, _ = xbc.shape
    row = lambda c_: pl.BlockSpec((nbs, None, t, c_), lambda b, c: (b, c, 0, 0))
    const = lambda a: pl.BlockSpec(a.shape, lambda b, c: (0,) * a.ndim)
    st_spec = pl.BlockSpec((nbs, H_D * P_D, N_D), lambda b, c: (b, 0, 0))
    return pl.pallas_call(
        functools.partial(_ssd_kernel, t=t, lq=lq, nc=nc, nbs=nbs),
        grid=(nb // nbs, nc),
        in_specs=[row(1024), row(512), row(512),
                  pl.BlockSpec((nbs, 8, CONV_DIM), lambda b, c: (b, 0, 0)), st_spec,
                  const(conv_w), const(conv_b), const(a_log), const(d_skip), const(norm_w),
                  const(tri), const(sel)],
        out_specs=(row(512), st_spec),
        out_shape=(jax.ShapeDtypeStruct((nb, nc, t, 512), zg.dtype),
                   jax.ShapeDtypeStruct((nb, H_D * P_D, N_D), F32)),
        scratch_shapes=[pltpu.VMEM((nbs, CHUNK + 8, CONV_DIM), F32),
                        pltpu.VMEM((nbs, CHUNK, 512), F32), pltpu.VMEM((nbs, CHUNK, 512), F32)],
        compiler_params=_cparams(2),
        name="ssd",
    )(xbc, dt, zg, tail, st0, conv_w, conv_b, a_log, d_skip, norm_w, tri, sel)


def _rotary_tables(pos):
    half = DK_B // 2
    inv = 1.0 / (10000.0 ** (jnp.arange(half, dtype=F32) / half))
    ang = pos.astype(F32)[:, None] * inv[None, :]
    cos, sin = jnp.cos(ang), jnp.sin(ang)
    cos_t = jnp.tile(jnp.concatenate([cos, cos], axis=1), (1, H_B))
    sin_t = jnp.tile(jnp.concatenate([-sin, sin], axis=1), (1, H_B))
    return cos_t, sin_t


def _retention_tables(chunk_len):
    log_g = np.log(1.0 - 2.0 ** (-5.0 - np.arange(H_B, dtype=np.float64)))
    idx = np.arange(CHUNK, dtype=np.float64)
    diff = idx[:, None] - idx[None, :]
    dmat = np.where(diff[None] >= 0, np.exp(np.maximum(diff, 0.0)[None] * log_g[:, None, None]), 0.0)
    qdec = np.exp((idx + 1.0)[:, None] * log_g[None, :])
    kdec = np.where(idx[:, None] < chunk_len, np.exp((chunk_len - 1.0 - idx)[:, None] * log_g[None, :]), 0.0)
    gl = np.exp(chunk_len * log_g)
    return (jnp.asarray(dmat, F32),
            jnp.asarray(np.repeat(qdec, DK_B, axis=1), F32),
            jnp.asarray(np.repeat(kdec, DK_B, axis=1), F32),
            jnp.asarray(np.repeat(np.repeat(gl, DK_B)[:, None], DV_B, axis=1), F32))


def _prompt_bias_idx():
    kk = np.arange(MOBA_BLOCK)[:, None]
    qq = np.arange(MOBA_BLOCK)[None, :]
    diag = np.where(qq >= kk, _t5_bucket_np(qq - kk), -1)
    sub = _t5_bucket_np(qq + MOBA_BLOCK - kk)
    return np.concatenate([diag, sub], axis=0).astype(np.int32)


def _sample_bias_idx(t_new):
    row_t = (np.arange(N_ROWS_S) // H_A)[:, None]
    qpos = PAST_LEN + row_t
    near = _t5_bucket_np(qpos - (PAST_LEN - MOBA_BLOCK + np.arange(MOBA_BLOCK))[None, :])
    own_k = np.arange(PAGE_SIZE)[None, :]
    own = np.where((own_k <= row_t) & (own_k < t_new), _t5_bucket_np(row_t - own_k), -1)
    return np.concatenate([near, own], axis=1).astype(np.int32)


def kernel(x_prompt, x_sample, cache_k, cache_v, state_ret, state_ssm, state_conv, page_table, c_prompt, c_sample, rel_bias, e_norm_w, e_ada_w, e_ada_b, e_in_w, e_q_norm_w, e_k_norm_w, e_out_w, o_norm_w, o_ada_w, o_ada_b, o_in_w, o_sgu_w, o_sgu_b, o_conv_w, o_conv_b, o_dt_bias, o_A_log, o_D, o_ssm_norm_w, o_out_w):
    bp, s_len, d = x_prompt.shape
    bs, t_len, _ = x_sample.shape
    n_s = bs * t_len

    c_all = jnp.concatenate([c_prompt, c_sample, jnp.zeros((8 - (bp + bs) % 8, d), F32)], axis=0)
    mods = []
    for ada_w, ada_b in ((e_ada_w[0], e_ada_b[0]), (o_ada_w[0], o_ada_b[0])):
        mod = _ada_mod(c_all, ada_w, ada_b)
        parts_p = [mod[:bp, i * d:(i + 1) * d].reshape(bp, 1, d) for i in range(3)]
        parts_s = [jnp.repeat(mod[bp:bp + bs, i * d:(i + 1) * d], t_len, axis=0).reshape(1, n_s, d) for i in range(3)]
        mods.append((parts_p, parts_s))
    (e_mod_p, e_mod_s), (o_mod_p, o_mod_s) = mods

    seg = jnp.asarray(np.kron(np.eye(H_A), np.ones((HD_A, HD_A))), BF16)
    qnw = jnp.tile(e_q_norm_w[0], H_A).reshape(1, W_A)
    knw = jnp.tile(e_k_norm_w[0], H_A).reshape(1, W_A)
    e_in_bf = _bf(e_in_w[0])
    e_out_bf = _bf(e_out_w[0])
    o_in_bf = _bf(jnp.concatenate([o_in_w[0][:, :3072], jnp.repeat(o_in_w[0][:, 3072:], P_D, axis=1)], axis=1))
    o_out_bf = _bf(o_out_w[0])
    x_s = x_sample.reshape(1, n_s, d)
    cos_p, sin_p = _rotary_tables(jnp.arange(s_len))
    cos_s, sin_s = _rotary_tables(PAST_LEN + (jnp.arange(n_s) % t_len))
    bias_p = _bias_tables(rel_bias, _prompt_bias_idx(), LOG2E).reshape(H_A, 2, MOBA_BLOCK, MOBA_BLOCK)
    bias_s_h = _bias_tables(rel_bias, _sample_bias_idx(t_len))
    row_h = jnp.arange(N_ROWS_S) % H_A
    bias_s = jnp.sum(jnp.where((jnp.arange(H_A)[:, None] == row_h[None, :])[:, :, None], bias_s_h, 0.0), axis=0)
    bfar = rel_bias[NUM_BUCKETS - 1, row_h].reshape(N_ROWS_S, 1)

    (qa, ka, va, kbf, vt, ga, qb, kb, vb, gb, kmean) = _even_in(
        x_prompt, e_mod_p[1], e_mod_p[0], e_norm_w[0].reshape(1, d), e_in_bf, qnw, knw, seg, cos_p, sin_p, False, BF16)
    (qa_s, ka_s, va_s, _, _, ga_s, qb_s, kb_s, vb_s, gb_s, _) = _even_in(
        x_s, e_mod_s[1], e_mod_s[0], e_norm_w[0].reshape(1, d), e_in_bf, qnw, knw, seg, cos_s, sin_s, True, F32)
    sq = lambda a: a.reshape(bs, t_len, a.shape[-1])
    qrep = jnp.repeat(sq(qa_s), H_A, axis=1)
    n_phys = cache_k.shape[1]
    page_t = lambda c: jnp.transpose(c[0], (0, 2, 3, 1)).reshape(n_phys, W_A, PAGE_SIZE)
    oa, oa_s = _moba(page_table, rel_bias, qa, kbf, vt, kmean.reshape(bp, s_len // MOBA_BLOCK, W_A), bias_p, ga,
                     qrep, sq(ka_s), sq(va_s), sq(ga_s), bias_s, bfar, page_t(cache_k), page_t(cache_v))
    nc_p = s_len // CHUNK
    ch = lambda a: a.reshape(bp, nc_p, CHUNK, a.shape[-1])
    ob, ret_p = _retention(ch(qb), ch(kb), ch(vb), ch(gb), jnp.zeros((bp, H_B * DK_B, DV_B), F32),
                           *_retention_tables(CHUNK), lq=CHUNK, nbs=bp)
    xp1 = _out_proj(oa, ob.reshape(bp, s_len, W_B), x_prompt, e_mod_p[2], e_out_bf, False)
    k_prompt = ka.reshape(1, bp, s_len, H_A, HD_A)
    v_prompt = va.reshape(1, bp, s_len, H_A, HD_A)
    ret_state_prompt = ret_p.reshape(1, bp, H_B, DK_B, DV_B)

    sc = lambda a: a.reshape(bs, 1, t_len, a.shape[-1])
    ob_s, ret_s = _retention(sc(qb_s), sc(kb_s), sc(vb_s), sc(gb_s),
                             state_ret[0].reshape(bs, H_B * DK_B, DV_B), *_retention_tables(t_len), lq=8, nbs=SEQ_PER_STEP)
    xs1 = _out_proj(oa_s.reshape(1, n_s, W_A), ob_s.reshape(1, n_s, W_B), x_s, e_mod_s[2], e_out_bf, True)
    k_sample = ka_s.reshape(1, bs, t_len, H_A, HD_A)
    v_sample = va_s.reshape(1, bs, t_len, H_A, HD_A)
    ret_state_sample = ret_s.reshape(1, bs, H_B, DK_B, DV_B)

    tri = jnp.asarray(np.tril(np.ones((CHUNK, CHUNK))), BF16)
    sel = jnp.asarray(np.kron(np.eye(H_D), np.eye(1, P_D)), BF16)
    rep = lambda a: jnp.repeat(a, P_D).reshape(1, W_D)
    dt_bias, a_log, d_skip = rep(o_dt_bias[0]), rep(o_A_log[0]), rep(o_D[0])
    ssm_nw = o_ssm_norm_w[0].reshape(1, W_D)
    conv_b = o_conv_b[0].reshape(1, CONV_DIM)
    o_nw = o_norm_w[0].reshape(1, d)

    sgu_b_p = jnp.repeat(o_sgu_b[0].T, W_C // G_C, axis=1)
    oc, zg, xbc, dtp = _odd_in(xp1, o_mod_p[1], o_mod_p[0], o_nw, o_in_bf, o_sgu_w[0], sgu_b_p, dt_bias,
                               False, BF16, False)
    yn, ssm_p = _ssd(ch(xbc), ch(dtp), ch(zg), jnp.zeros((bp, 8, CONV_DIM), F32),
                     jnp.zeros((bp, H_D * P_D, N_D), F32), o_conv_w[0], conv_b, a_log, d_skip, ssm_nw,
                     tri, sel, lq=CHUNK, nbs=bp)
    y_prompt = _out_proj(oc, yn.reshape(bp, s_len, W_D), xp1, o_mod_p[2], o_out_bf, False)
    ssm_state_prompt = ssm_p.reshape(1, bp, H_D, P_D, N_D)
    conv_state_prompt = xbc[:, -(CONV_W - 1):][None]

    per_chunk = CHUNK // t_len
    w_small = o_sgu_w[0][:, :t_len, :t_len]
    same_seq = jnp.asarray(np.kron(np.eye(per_chunk), np.ones((t_len, t_len))), F32)
    sgu_w_s = jnp.tile(w_small, (1, per_chunk, per_chunk)) * same_seq
    sgu_b_s = jnp.repeat(jnp.tile(o_sgu_b[0][:, :t_len].T, (per_chunk, 1)), W_C // G_C, axis=1)
    oc_s, zg_s, xbc_s, dt_s, v_s = _odd_in(xs1, o_mod_s[1], o_mod_s[0], o_nw, o_in_bf, sgu_w_s, sgu_b_s, dt_bias,
                                           True, F32, True)
    tail_s = jnp.concatenate([jnp.zeros((bs, 8 - (CONV_W - 1), CONV_DIM), F32), state_conv[0]], axis=1)
    yn_s, ssm_s = _ssd(sc(xbc_s), sc(dt_s), sc(zg_s), tail_s, state_ssm[0].reshape(bs, H_D * P_D, N_D),
                       o_conv_w[0], conv_b, a_log, d_skip, ssm_nw, tri, sel, lq=8, nbs=SEQ_PER_STEP)
    xs2 = _out_proj(oc_s, yn_s.reshape(1, n_s, W_D), xs1, o_mod_s[2], o_out_bf, True)
    y_sample = xs2.reshape(bs, t_len, d)
    sgu_v_sample = v_s.reshape(1, bs, t_len, W_C)
    ssm_state_sample = ssm_s.reshape(1, bs, H_D, P_D, N_D)
    xin = jnp.concatenate([state_conv[0], xbc_s.reshape(bs, t_len, CONV_DIM)], axis=1)
    conv_state_sample = xin[:, -(CONV_W - 1):][None]

    return (y_prompt, y_sample, k_prompt, v_prompt, k_sample, v_sample, ret_state_prompt, ret_state_sample,
            sgu_v_sample, ssm_state_prompt, ssm_state_sample, conv_state_prompt, conv_state_sample)
```

```python
import functools
import math

import numpy as np
import jax
import jax.numpy as jnp
from jax import lax
from jax.experimental import pallas as pl
from jax.experimental.pallas import tpu as pltpu

F32 = jnp.float32
BF16 = jnp.bfloat16

D_MODEL = 1024
PAST_LEN = 2048
PAGE_SIZE = 128
H_A, HD_A, W_A = 8, 64, 512
MOBA_BLOCK = 256
MOBA_TOPK = 3
NUM_BUCKETS = 32
MAX_DISTANCE = 128
H_B, DK_B, DV_B, W_B = 4, 64, 128, 512
G_C, W_C = 4, 512
H_D, P_D, N_D, G_D, W_D = 8, 64, 128, 2, 512
CONV_W = 4
CONV_DIM = 1024
CHUNK = 128
SEQ_PER_STEP = 8
PROJ_ROWS = 512
OUT_ROWS = 1024
NEG_INF = -1e30
EPS = 1e-6
LOG2E = math.log2(math.e)
VMEM_LIMIT = 56 * 1024 * 1024


def _bf(x):
    return x.astype(BF16)


def _dg(a, b, ca, cb):
    return lax.dot_general(a, b, (((ca,), (cb,)), ((), ())), preferred_element_type=F32)


def _mm(a, b):
    return _dg(_bf(a), _bf(b), 1, 0)


def _mm_nt(a, b):
    return _dg(_bf(a), _bf(b), 1, 1)


def _mm_tn(a, b):
    return _dg(_bf(a), _bf(b), 0, 0)


def _split2(x):
    hi = _bf(x)
    return hi, _bf(x - hi.astype(F32))


def _split3(x):
    hi = _bf(x)
    r = x - hi.astype(F32)
    mid = _bf(r)
    return hi, mid, _bf(r - mid.astype(F32))


def _mm_hp(a, b, ca, cb):
    ah, al = _split2(a)
    bh, bl = _split2(b)
    return _dg(ah, bh, ca, cb) + (_dg(ah, bl, ca, cb) + _dg(al, bh, ca, cb))


def _mm_exact_lhs(e, x, ca, cb):
    h, m, l = _split3(x)
    return _dg(e, h, ca, cb) + (_dg(e, m, ca, cb) + _dg(e, l, ca, cb))


def _silu(x):
    return x * (1.0 / (1.0 + jnp.exp(-x)))


def _gelu_tanh(x):
    return 0.5 * x * (1.0 + jnp.tanh(math.sqrt(2.0 / math.pi) * (x + 0.044715 * (x * x * x))))


def _softplus(x):
    return jnp.maximum(x, 0.0) + jnp.log1p(jnp.exp(-jnp.abs(x)))


def _cparams(n_grid):
    return pltpu.CompilerParams(dimension_semantics=("arbitrary",) * n_grid,
                                vmem_limit_bytes=VMEM_LIMIT)


def _top3_rows(g, blk, nblk):
    sel = jnp.zeros(g.shape, jnp.bool_)
    for _ in range(MOBA_TOPK):
        m = jnp.max(g, axis=0, keepdims=True)
        idx = jnp.min(jnp.where(g == m, blk, nblk), axis=0, keepdims=True)
        pick = blk == idx
        sel = jnp.logical_or(sel, pick)
        g = jnp.where(pick, -jnp.inf, g)
    return sel


def _ada_kernel(c_ref, w_ref, b_ref, o_ref):
    s = _silu(c_ref[...])
    o_ref[...] = _mm_hp(s, w_ref[...], 1, 0) + b_ref[...]


def _ada_mod(c_all, w, b):
    m, d = c_all.shape
    n = w.shape[1]
    tn = 512
    return pl.pallas_call(
        _ada_kernel,
        grid=(n // tn,),
        in_specs=[pl.BlockSpec((m, d), lambda j: (0, 0)),
                  pl.BlockSpec((d, tn), lambda j: (0, j)),
                  pl.BlockSpec((1, tn), lambda j: (0, j))],
        out_specs=pl.BlockSpec((m, tn), lambda j: (0, j)),
        out_shape=jax.ShapeDtypeStruct((m, n), F32),
        compiler_params=_cparams(1),
        name="ada_mod",
    )(c_all, w, b.reshape(1, n))


def _t5_bucket_np(rel):
    n = np.maximum(rel, 0)
    max_exact = NUM_BUCKETS // 2
    nf = np.maximum(n, 1).astype(np.float64)
    large = max_exact + (np.log(nf / max_exact) / math.log(MAX_DISTANCE / max_exact)
                         * (NUM_BUCKETS - max_exact)).astype(np.int64)
    large = np.minimum(large, NUM_BUCKETS - 1)
    return np.where(n < max_exact, n, large).astype(np.int32)


def _bias_kernel(tab_ref, idx_ref, o_ref, *, scale):
    h = pl.program_id(0)
    idx = idx_ref[...]
    acc = jnp.zeros(idx.shape, F32)
    for b in range(NUM_BUCKETS):
        acc = jnp.where(idx == b, tab_ref[b, h], acc)
    o_ref[0] = jnp.where(idx == -1, NEG_INF, acc * scale)


def _bias_tables(rel_bias, idx, scale=1.0):
    r, c = idx.shape
    return pl.pallas_call(
        functools.partial(_bias_kernel, scale=scale),
        grid=(H_A,),
        in_specs=[pl.BlockSpec(memory_space=pltpu.SMEM),
                  pl.BlockSpec((r, c), lambda h: (0, 0))],
        out_specs=pl.BlockSpec((1, r, c), lambda h: (h, 0, 0)),
        out_shape=jax.ShapeDtypeStruct((H_A, r, c), F32),
        compiler_params=_cparams(1),
        name="t5_bias",
    )(rel_bias, jnp.asarray(idx))


def _even_in_kernel(x_ref, sc_ref, sh_ref, nw_ref, w_ref, qnw_ref, knw_ref, seg_ref, cos_ref, sin_ref,
                    qa_ref, ka_ref, va_ref, kbf_ref, vt_ref, ga_ref, qb_ref, kb_ref, vb_ref, gb_ref, km_ref):
    x = x_ref[0]
    ms = jnp.mean(x * x, axis=-1, keepdims=True)
    h = (x * lax.rsqrt(ms + EPS) * nw_ref[...]) * (1.0 + sc_ref[0]) + sh_ref[0]
    hb = _bf(h)

    def proj(lo, hi):
        return jnp.dot(hb, w_ref[:, lo:hi], preferred_element_type=F32)

    def head_rms(t, w_row):
        ss = jnp.dot(_bf(t * t), seg_ref[...], preferred_element_type=F32)
        return t * lax.rsqrt(ss * (1.0 / HD_A) + EPS) * w_row

    qa_ref[0] = head_rms(proj(0, 512), qnw_ref[...])
    ka = head_rms(proj(512, 1024), knw_ref[...])
    ka_ref[0] = ka
    kbf_ref[0] = _bf(ka)
    for j in range(ka.shape[0] // MOBA_BLOCK):
        km_ref[0, j] = jnp.mean(ka[j * MOBA_BLOCK:(j + 1) * MOBA_BLOCK], axis=0, keepdims=True)
    va = proj(1024, 1536)
    va_ref[0] = va
    vat = va.T
    ones_pad = (lax.broadcasted_iota(jnp.int32, (V_ROWS - HD_A, vat.shape[1]), 0) == 0).astype(F32)
    vt_ref[0] = _bf(jnp.concatenate(
        [piece for h in range(H_A) for piece in (vat[h * HD_A:(h + 1) * HD_A], ones_pad)], axis=0))
    ga_ref[0] = proj(1536, 2048).astype(ga_ref.dtype)

    lane = lax.broadcasted_iota(jnp.int32, (1, 256), 1) % DK_B
    first_half = lane < (DK_B // 2)
    cos = cos_ref[...]
    sin = sin_ref[...]

    def rotary(t):
        up = pltpu.roll(t, 256 - DK_B // 2, 1)
        dn = pltpu.roll(t, DK_B // 2, 1)
        return t * cos + jnp.where(first_half, up, dn) * sin

    qb_ref[0] = rotary(proj(2048, 2304)).astype(qb_ref.dtype)
    kb_ref[0] = (rotary(proj(2304, 2560)) * (DK_B ** -0.5)).astype(kb_ref.dtype)
    vb_ref[0] = proj(2560, 3072).astype(vb_ref.dtype)
    gb_ref[0] = proj(3072, 3584).astype(gb_ref.dtype)


def _even_in(x, scale, shift, norm_w, w_bf, qnw, knw, seg, cos, sin, per_row_mod, act):
    nb, s, d = x.shape
    tm = PROJ_ROWS
    ns = s // tm
    nkb = tm // MOBA_BLOCK
    if per_row_mod:
        mod_spec = pl.BlockSpec((1, tm, d), lambda b, i: (b, i, 0))
    else:
        mod_spec = pl.BlockSpec((1, 1, d), lambda b, i: (b, 0, 0))
    row = lambda c: pl.BlockSpec((1, tm, c), lambda b, i: (b, i, 0))
    const = lambda shp: pl.BlockSpec(shp, lambda b, i: (0,) * len(shp))
    out_shape = (
        jax.ShapeDtypeStruct((nb, s, 512), F32),
        jax.ShapeDtypeStruct((nb, s, 512), F32),
        jax.ShapeDtypeStruct((nb, s, 512), F32),
        jax.ShapeDtypeStruct((nb, s, 512), BF16),
        jax.ShapeDtypeStruct((nb, H_A * V_ROWS, s), BF16),
        jax.ShapeDtypeStruct((nb, s, 512), act),
        jax.ShapeDtypeStruct((nb, s, 256), act),
        jax.ShapeDtypeStruct((nb, s, 256), act),
        jax.ShapeDtypeStruct((nb, s, 512), act),
        jax.ShapeDtypeStruct((nb, s, 512), act),
        jax.ShapeDtypeStruct((nb, ns * nkb, 1, 512), F32),
    )
    out_specs = (row(512), row(512), row(512), row(512),
                 pl.BlockSpec((1, H_A * V_ROWS, tm), lambda b, i: (b, 0, i)),
                 row(512), row(256), row(256), row(512), row(512),
                 pl.BlockSpec((1, nkb, 1, 512), lambda b, i: (b, i, 0, 0)))
    return pl.pallas_call(
        _even_in_kernel,
        grid=(nb, ns),
        in_specs=[row(d), mod_spec, mod_spec, const((1, d)), const((d, 3584)),
                  const((1, 512)), const((1, 512)), const((512, 512)),
                  pl.BlockSpec((tm, 256), lambda b, i: (i, 0)),
                  pl.BlockSpec((tm, 256), lambda b, i: (i, 0))],
        out_specs=out_specs,
        out_shape=out_shape,
        compiler_params=_cparams(2),
        name="even_in",
    )(x, scale, shift, norm_w, w_bf, qnw, knw, seg, cos, sin)


MOBA_HS = 8
V_ROWS = HD_A + 8
FAR_KEYS = 2 * MOBA_BLOCK


def _moba_prompt_select(tab_ref, q_ref, km_ref, rbf_ref, rbs_ref, qs_ref):
    hg = pl.program_id(1)
    qi = pl.program_id(2)
    nblk = km_ref.shape[1]
    lane = lax.broadcasted_iota(jnp.int32, (1, 128), 1)
    blk = lax.broadcasted_iota(jnp.int32, (nblk, MOBA_BLOCK), 0)
    for hl in range(MOBA_HS):
        pr, hh = divmod(hl, 2)
        pc = slice(pr * 128, (pr + 1) * 128)
        qm = jnp.where((lane // HD_A) == hh, q_ref[0, :, pc], 0.0)
        gate = _mm_hp(km_ref[0, :, pc], qm, 1, 1)
        gate = jnp.where(blk < qi, gate, NEG_INF)
        sel = jnp.logical_and(_top3_rows(gate, blk, nblk), blk < qi)
        far_c = tab_ref[NUM_BUCKETS - 1, MOBA_HS * hg + hl] * LOG2E
        rbf_ref[hl] = jnp.where(jnp.logical_and(sel, blk < qi - 1), far_c, NEG_INF)
        rbs_ref[hl] = jnp.where(jnp.logical_or(sel, blk == qi), 0.0, NEG_INF)
        qs_ref[hl] = _bf(qm * (HD_A ** -0.5 * LOG2E))


def _moba_prompt_attend(k_ref, vt_ref, bias_ref, ga_ref, o_ref, rbf_ref, rbs_ref, qs_ref):
    qi = pl.program_id(2)

    def visit(carry, off, nkeys, extra_fn=None, block_rows=None):
        ss = []
        for hl in range(MOBA_HS):
            pr = hl // 2
            kj = k_ref[0, pl.ds(off, nkeys), pr * 128:(pr + 1) * 128]
            ss.append(_dg(kj, qs_ref[hl], 1, 1))
        stats, ps = [], []
        for hl in range(MOBA_HS):
            m = carry[hl][0]
            if block_rows is None:
                s = extra_fn(hl, ss[hl])
                mn = jnp.maximum(m, jnp.max(s, axis=0, keepdims=True))
                p = jnp.exp2(s - mn)
            else:
                halves = [ss[hl][i * MOBA_BLOCK:(i + 1) * MOBA_BLOCK] for i in range(nkeys // MOBA_BLOCK)]
                rows = block_rows(hl)
                mn = m
                for sh, r in zip(halves, rows):
                    mn = jnp.maximum(mn, jnp.max(sh, axis=0, keepdims=True) + r)
                p = jnp.concatenate([jnp.exp2(sh - (mn - r)) for sh, r in zip(halves, rows)], axis=0)
            stats.append((mn, jnp.exp2(m - mn)))
            ps.append(_bf(p))
        pvs = []
        for hl in range(MOBA_HS):
            vj = vt_ref[0, hl * V_ROWS:(hl + 1) * V_ROWS, pl.ds(off, nkeys)]
            pvs.append(jnp.dot(vj, ps[hl], preferred_element_type=F32))
        return tuple((stats[hl][0], stats[hl][1] * carry[hl][1] + pvs[hl]) for hl in range(MOBA_HS))

    def far_body(jp, carry):
        off = pl.multiple_of(jp * FAR_KEYS, FAR_KEYS)

        def rows(hl):
            return rbf_ref[hl, pl.ds(2 * jp, 1), :], rbf_ref[hl, pl.ds(2 * jp + 1, 1), :]

        return visit(carry, off, FAR_KEYS, block_rows=rows)

    init = tuple((jnp.full((1, MOBA_BLOCK), -jnp.inf, F32), jnp.zeros((V_ROWS, MOBA_BLOCK), F32))
                 for _ in range(MOBA_HS))
    carry = lax.fori_loop(0, qi // 2, far_body, init)
    js = jnp.maximum(qi - 1, 0)
    first = qi == 0
    top_tab = jnp.where(first, 0, 1)
    bot_mask = jnp.where(first, NEG_INF, 0.0)

    def near_extra(hl, s):
        top = s[:MOBA_BLOCK] + bias_ref[hl, top_tab] + rbs_ref[hl, pl.ds(js, 1), :]
        bot = s[MOBA_BLOCK:] + (bias_ref[hl, 0] + bot_mask)
        return jnp.concatenate([top, bot], axis=0)

    carry = visit(carry, pl.multiple_of(js * MOBA_BLOCK, MOBA_BLOCK), FAR_KEYS, near_extra)
    for pr in range(MOBA_HS // 2):
        accs = [carry[2 * pr + hh][1] for hh in range(2)]
        outs = [a[0:HD_A] * (1.0 / a[HD_A:HD_A + 1]) for a in accs]
        o = jnp.concatenate(outs, axis=0).T
        pc = slice(pr * 128, (pr + 1) * 128)
        o_ref[0, :, pc] = (o * _silu(ga_ref[0, :, pc].astype(F32))).astype(o_ref.dtype)


N_PAST_BLK = PAST_LEN // MOBA_BLOCK
N_PAGES = PAST_LEN // PAGE_SIZE
N_ROWS_S = 32
SEQ_PER_TILE = 2


def _page_copies(pt_ref, ck_hbm, cv_hbm, kt_buf, vt_buf, sems, seq, sl):
    cps = []
    for p in range(N_PAGES):
        dst = pl.ds(p * PAGE_SIZE, PAGE_SIZE)
        cps.append(pltpu.make_async_copy(ck_hbm.at[pt_ref[seq, p]], kt_buf.at[sl, :, dst], sems.at[0, sl]))
        cps.append(pltpu.make_async_copy(cv_hbm.at[pt_ref[seq, p]], vt_buf.at[sl, :, dst], sems.at[1, sl]))
    return cps


def _moba_sample_seqs(qrep_ref, knew_ref, vnew_ref, ga_ref, bias_ref, bfar_ref, kt_buf, vt_buf, kpad, vpad, o_ref):
    t_new = knew_ref.shape[1]
    rowh = lax.broadcasted_iota(jnp.int32, (N_ROWS_S, W_A), 0) % H_A
    laneh = lax.broadcasted_iota(jnp.int32, (N_ROWS_S, W_A), 1) // HD_A
    own_head = rowh == laneh
    blocks = [slice(n * MOBA_BLOCK, (n + 1) * MOBA_BLOCK) for n in range(N_PAST_BLK)]
    bfar = bfar_ref[...]
    scored = []
    for j in range(SEQ_PER_TILE):
        qf = jnp.where(own_head, qrep_ref[j], 0.0) * (HD_A ** -0.5)
        qbd = _bf(qf)
        q2 = jnp.concatenate([qbd, _bf(qf - qbd.astype(F32))], axis=0)
        kpad[j, 0:t_new, :] = knew_ref[j]
        vpad[j, 0:t_new, :] = vnew_ref[j]
        s_past = []
        for n in range(N_PAST_BLK):
            s2 = jnp.dot(q2, _bf(kt_buf[j, :, blocks[n]]), preferred_element_type=F32)
            s_past.append(s2[0:N_ROWS_S] + s2[N_ROWS_S:])
        s_own = _dg(qbd, _bf(kpad[j]), 1, 1) + bias_ref[:, MOBA_BLOCK:]
        scored.append((s_past, s_own))
    probs = []
    for s_past, s_own in scored:
        g = [jnp.sum(s, axis=1, keepdims=True) for s in s_past]
        sel = [jnp.zeros((N_ROWS_S, 1), jnp.bool_) for _ in range(N_PAST_BLK)]
        for _ in range(MOBA_TOPK):
            m = functools.reduce(jnp.maximum, g)
            idx = functools.reduce(jnp.minimum, [jnp.where(g[n] == m, n, N_PAST_BLK) for n in range(N_PAST_BLK)])
            for n in range(N_PAST_BLK):
                pick = idx == n
                sel[n] = jnp.logical_or(sel[n], pick)
                g[n] = jnp.where(pick, -jnp.inf, g[n])
        logits = [s_past[n] + jnp.where(sel[n], bfar, NEG_INF) for n in range(N_PAST_BLK - 1)]
        logits.append(s_past[-1] + bias_ref[:, 0:MOBA_BLOCK] + jnp.where(sel[-1], 0.0, NEG_INF))
        m = jnp.max(s_own, axis=1, keepdims=True)
        for s in logits:
            m = jnp.maximum(m, jnp.max(s, axis=1, keepdims=True))
        p_own = jnp.exp(s_own - m)
        ps = [jnp.exp(s - m) for s in logits]
        l = functools.reduce(jnp.add, [jnp.sum(p, axis=1, keepdims=True) for p in ps + [p_own]])
        probs.append(([_bf(p) for p in ps], _bf(p_own), l))
    for j, (ps, p_own, l) in enumerate(probs):
        acc = jnp.dot(p_own, _bf(vpad[j]), preferred_element_type=F32)
        for n in range(N_PAST_BLK):
            acc = acc + _dg(ps[n], _bf(vt_buf[j, :, blocks[n]]), 1, 1)
        o = jnp.where(own_head, acc * (1.0 / l), 0.0)
        o = jnp.sum(o.reshape(t_new, H_A, W_A), axis=1)
        o_ref[j] = o * _silu(ga_ref[j])


def _moba_kernel(pt_ref, tab_ref, q_ref, k_ref, vt_ref, km_ref, bias_ref, ga_ref,
                 qrep_ref, knew_ref, vnew_ref, gas_ref, bias_s_ref, bfar_ref, ck_hbm, cv_hbm,
                 o_ref, os_ref, rbf_ref, rbs_ref, qs_ref, kt_buf, vt_buf, kpad, vpad, sems):
    step = pl.program_id(0) * pl.num_programs(2) + pl.program_id(2)
    n_steps = pl.num_programs(0) * pl.num_programs(2)
    seq0 = SEQ_PER_TILE * step
    copies = functools.partial(_page_copies, pt_ref, ck_hbm, cv_hbm, kt_buf, vt_buf, sems)

    @pl.when(step == 0)
    def _():
        kpad[...] = jnp.zeros(kpad.shape, F32)
        vpad[...] = jnp.zeros(vpad.shape, F32)
        for j in range(SEQ_PER_TILE):
            for cp in copies(j, j):
                cp.start()

    for j in range(SEQ_PER_TILE):
        for cp in copies(seq0 + j, j):
            cp.wait()
    _moba_sample_seqs(qrep_ref, knew_ref, vnew_ref, gas_ref, bias_s_ref, bfar_ref, kt_buf, vt_buf, kpad, vpad, os_ref)
    _moba_prompt_select(tab_ref, q_ref, km_ref, rbf_ref, rbs_ref, qs_ref)

    @pl.when(step + 1 < n_steps)
    def _():
        for j in range(SEQ_PER_TILE):
            for cp in copies(seq0 + SEQ_PER_TILE + j, j):
                cp.start()

    _moba_prompt_attend(k_ref, vt_ref, bias_ref, ga_ref, o_ref, rbf_ref, rbs_ref, qs_ref)


def _moba(page_table, rel_bias, qa, kbf, vt, kmean, bias_t, ga,
          qrep, knew, vnew, ga_s, bias_s, bfar, cache_kt, cache_vt):
    nb, s, _ = qa.shape
    nq = s // MOBA_BLOCK
    nseq, t_new, _ = knew.shape
    assert H_A == MOBA_HS and nseq == SEQ_PER_TILE * nb * nq
    tile = pl.BlockSpec((1, MOBA_BLOCK, W_A), lambda b, hg, i, pt: (b, i, 0))
    per_b = lambda shp: pl.BlockSpec((1,) + shp, lambda b, hg, i, pt: (b, 0, 0))
    seqs = lambda r: pl.BlockSpec((SEQ_PER_TILE, r, W_A), lambda b, hg, i, pt: (b * nq + i, 0, 0))
    const = lambda a: pl.BlockSpec(a.shape, lambda b, hg, i, pt: (0,) * a.ndim)
    grid_spec = pltpu.PrefetchScalarGridSpec(
        num_scalar_prefetch=1,
        grid=(nb, 1, nq),
        in_specs=[pl.BlockSpec(memory_space=pltpu.SMEM),
                  tile, per_b((s, W_A)), per_b((H_A * V_ROWS, s)), per_b((nq, W_A)), const(bias_t), tile,
                  seqs(N_ROWS_S), seqs(t_new), seqs(t_new), seqs(t_new), const(bias_s), const(bfar),
                  pl.BlockSpec(memory_space=pl.ANY), pl.BlockSpec(memory_space=pl.ANY)],
        out_specs=(tile, seqs(t_new)),
        scratch_shapes=[pltpu.VMEM((MOBA_HS, nq, MOBA_BLOCK), F32),
                        pltpu.VMEM((MOBA_HS, nq, MOBA_BLOCK), F32),
                        pltpu.VMEM((MOBA_HS, MOBA_BLOCK, 128), BF16),
                        pltpu.VMEM((SEQ_PER_TILE, W_A, PAST_LEN), F32),
                        pltpu.VMEM((SEQ_PER_TILE, W_A, PAST_LEN), F32),
                        pltpu.VMEM((SEQ_PER_TILE, PAGE_SIZE, W_A), F32),
                        pltpu.VMEM((SEQ_PER_TILE, PAGE_SIZE, W_A), F32),
                        pltpu.SemaphoreType.DMA((2, SEQ_PER_TILE))],
    )
    return pl.pallas_call(
        _moba_kernel,
        grid_spec=grid_spec,
        out_shape=(jax.ShapeDtypeStruct((nb, s, W_A), BF16),
                   jax.ShapeDtypeStruct((nseq, t_new, W_A), F32)),
        compiler_params=_cparams(3),
        name="moba",
    )(page_table, rel_bias, qa, kbf, vt, kmean, bias_t, ga, qrep, knew, vnew, ga_s, bias_s, bfar, cache_kt, cache_vt)


def _pad_rows(ref, scratch, s, t):
    if t == CHUNK:
        return ref[s]
    scratch[s] = jnp.zeros(scratch.shape[1:], scratch.dtype)
    scratch[s, 0:t, :] = ref[s].astype(scratch.dtype)
    return scratch[s]


def _ret_kernel(q_ref, k_ref, v_ref, g_ref, st0_ref, dmat_ref, qdec_ref, kdec_ref, gl_ref,
                o_ref, st_ref, qpad, kpad, vpad, gpad, *, t, lq, nbs):
    c = pl.program_id(1)

    @pl.when(c == 0)
    def _():
        st_ref[...] = st0_ref[...]

    lane = lax.broadcasted_iota(jnp.int32, (1, 128), 1)
    rowsel = lax.broadcasted_iota(jnp.int32, (128, 1), 0) < DK_B
    work = []
    for s in range(nbs):
        q = _pad_rows(q_ref, qpad, s, t)[0:lq]
        k = _pad_rows(k_ref, kpad, s, t)
        v = _pad_rows(v_ref, vpad, s, t)
        for hp in range(H_B // 2):
            cols = slice(hp * 128, (hp + 1) * 128)
            kp = _bf(k[:, cols])
            st = st_ref[s, cols, :]
            kd = _bf(k[:, cols] * kdec_ref[:, cols])
            for hh in range(2):
                h = 2 * hp + hh
                qm = jnp.where((lane // DK_B) == hh, q[:, cols], 0.0)
                vh = _bf(v[:, h * DV_B:(h + 1) * DV_B])
                sc = _dg(_bf(qm), kp, 1, 1)
                so = _mm(qm * qdec_ref[0:lq, cols], st)
                upd = _dg(kd, vh, 0, 0)
                work.append((s, hp, hh, sc, so, upd, vh, st))
    outs = {}
    for (s, hp, hh, sc, so, upd, vh, st) in work:
        h = 2 * hp + hh
        o = jnp.dot(_bf(sc * dmat_ref[h, 0:lq, :]), vh, preferred_element_type=F32) + so
        outs[(s, h)] = o * lax.rsqrt(jnp.mean(o * o, axis=-1, keepdims=True) + EPS)
    for i in range(0, len(work), 2):
        s, hp, _, _, _, upd0, _, st = work[i]
        cols = slice(hp * 128, (hp + 1) * 128)
        st_ref[s, cols, :] = st * gl_ref[cols, :] + jnp.where(rowsel, upd0, work[i + 1][5])
    for s in range(nbs):
        g = _pad_rows(g_ref, gpad, s, t)[0:lq].astype(F32)
        o = jnp.concatenate([outs[(s, h)] for h in range(H_B)], axis=1)
        o_ref[s] = (o * _silu(g))[0:t].astype(o_ref.dtype)


def _retention(q, k, v, g, st0, dmat, qdec, kdec, gl, lq, nbs):
    nb, nc, t, _ = q.shape
    row = lambda c_: pl.BlockSpec((nbs, None, t, c_), lambda b, c: (b, c, 0, 0))
    const = lambda a: pl.BlockSpec(a.shape, lambda b, c: (0,) * a.ndim)
    st_spec = pl.BlockSpec((nbs, H_B * DK_B, DV_B), lambda b, c: (b, 0, 0))
    pad = lambda c_: pltpu.VMEM((nbs, CHUNK, c_), F32)
    return pl.pallas_call(
        functools.partial(_ret_kernel, t=t, lq=lq, nbs=nbs),
        grid=(nb // nbs, nc),
        in_specs=[row(256), row(256), row(512), row(512), st_spec,
                  const(dmat), const(qdec), const(kdec), const(gl)],
        out_specs=(row(512), st_spec),
        out_shape=(jax.ShapeDtypeStruct((nb, nc, t, 512), q.dtype),
                   jax.ShapeDtypeStruct((nb, H_B * DK_B, DV_B), F32)),
        scratch_shapes=[pad(256), pad(256), pad(512), pad(512)],
        compiler_params=_cparams(2),
        name="retention",
    )(q, k, v, g, st0, dmat, qdec, kdec, gl)


def _out_kernel(a_ref, b_ref, x_ref, g_ref, w_ref, o_ref):
    half = w_ref.shape[0] // 2
    y = (jnp.dot(_bf(a_ref[0]), w_ref[0:half, :], preferred_element_type=F32)
         + jnp.dot(_bf(b_ref[0]), w_ref[half:, :], preferred_element_type=F32))
    o_ref[0] = x_ref[0] + g_ref[0] * y


def _out_proj(a, b, x, gate, w_bf, per_row_mod):
    nb, s, d = x.shape
    tm = min(OUT_ROWS, s)
    if per_row_mod:
        g_spec = pl.BlockSpec((1, tm, d), lambda bb, i: (bb, i, 0))
    else:
        g_spec = pl.BlockSpec((1, 1, d), lambda bb, i: (bb, 0, 0))
    row = lambda c: pl.BlockSpec((1, tm, c), lambda bb, i: (bb, i, 0))
    return pl.pallas_call(
        _out_kernel,
        grid=(nb, s // tm),
        in_specs=[row(512), row(512), row(d), g_spec, pl.BlockSpec(w_bf.shape, lambda bb, i: (0, 0))],
        out_specs=row(d),
        out_shape=jax.ShapeDtypeStruct((nb, s, d), F32),
        compiler_params=_cparams(2),
        name="out_proj",
    )(a, b, x, gate, w_bf)


def _odd_in_kernel(x_ref, sc_ref, sh_ref, nw_ref, w_ref, sguw_ref, sgub_ref, dtb_ref,
                   oc_ref, zg_ref, xbc_ref, dt_ref, *maybe_v_ref):
    x = x_ref[0]
    tm = x.shape[0]
    ms = jnp.mean(x * x, axis=-1, keepdims=True)
    h = (x * lax.rsqrt(ms + EPS) * nw_ref[...]) * (1.0 + sc_ref[0]) + sh_ref[0]
    hb = _bf(h)

    def proj(lo, hi):
        return jnp.dot(hb, w_ref[:, lo:hi], preferred_element_type=F32)

    u = _gelu_tanh(proj(0, 512))
    v = _gelu_tanh(proj(512, 1024))
    mu = jnp.mean(v, axis=-1, keepdims=True)
    vc = v - mu
    v = vc * lax.rsqrt(jnp.mean(vc * vc, axis=-1, keepdims=True) + EPS)
    for v_ref in maybe_v_ref:
        v_ref[0] = v
    ii = lax.broadcasted_iota(jnp.int32, (CHUNK, CHUNK), 0)
    jj = lax.broadcasted_iota(jnp.int32, (CHUNK, CHUNK), 1)
    rows = []
    for ci in range(tm // CHUNK):
        cols = []
        for g in range(G_C):
            wg = jnp.where(ii >= jj, sguw_ref[g], 0.0)
            cols.append(_mm(wg, v[ci * CHUNK:(ci + 1) * CHUNK, g * 128:(g + 1) * 128]))
        rows.append(jnp.concatenate(cols, axis=1) + sgub_ref[...])
    sg = jnp.concatenate(rows, axis=0) if len(rows) > 1 else rows[0]
    oc_ref[0] = (u * sg * _silu(proj(1024, 1536))).astype(oc_ref.dtype)
    zg_ref[0] = proj(1536, 2048).astype(zg_ref.dtype)
    xbc_ref[0] = proj(2048, 3072)
    dt_ref[0] = _softplus(proj(3072, 3584) + dtb_ref[...])


def _odd_in(x, scale, shift, norm_w, w_bf, sgu_w, sgu_b_tab, dt_bias, per_row_mod, act, emit_v):
    nb, s, d = x.shape
    tm = PROJ_ROWS
    if per_row_mod:
        mod_spec = pl.BlockSpec((1, tm, d), lambda b, i: (b, i, 0))
    else:
        mod_spec = pl.BlockSpec((1, 1, d), lambda b, i: (b, 0, 0))
    row = lambda c: pl.BlockSpec((1, tm, c), lambda b, i: (b, i, 0))
    const = lambda shp: pl.BlockSpec(shp, lambda b, i: (0,) * len(shp))
    return pl.pallas_call(
        _odd_in_kernel,
        grid=(nb, s // tm),
        in_specs=[row(d), mod_spec, mod_spec, const((1, d)), const(w_bf.shape),
                  const(sgu_w.shape), const(sgu_b_tab.shape), const((1, 512))],
        out_specs=(row(512), row(512), row(1024), row(512)) + ((row(512),) if emit_v else ()),
        out_shape=(jax.ShapeDtypeStruct((nb, s, 512), act),
                   jax.ShapeDtypeStruct((nb, s, 512), act),
                   jax.ShapeDtypeStruct((nb, s, 1024), F32),
                   jax.ShapeDtypeStruct((nb, s, 512), F32),
                   ) + ((jax.ShapeDtypeStruct((nb, s, 512), F32),) if emit_v else ()),
        compiler_params=_cparams(2),
        name="odd_in",
    )(x, scale, shift, norm_w, w_bf, sgu_w, sgu_b_tab, dt_bias)


def _ssd_kernel(xbc_ref, dt_ref, zg_ref, tail_ref, st0_ref, cw_ref, cb_ref, alog_ref, dsk_ref, nw_ref,
                tri_ref, sel_ref, y_ref, st_ref, ext, dtpad, zpad, *, t, lq, nc, nbs):
    c = pl.program_id(1)

    @pl.when(c == 0)
    def _():
        st_ref[...] = st0_ref[...]
        ext[...] = jnp.zeros(ext.shape, F32)
        ext[:, 0:8, :] = tail_ref[...]

    ii = lax.broadcasted_iota(jnp.int32, (lq, CHUNK), 0)
    jj = lax.broadcasted_iota(jnp.int32, (lq, CHUNK), 1)
    lane = lax.broadcasted_iota(jnp.int32, (1, 128), 1)
    hpg = H_D // G_D
    neg_a = -jnp.exp(alog_ref[...])
    seqs = []
    for s in range(nbs):
        ext[s, 8:8 + t, :] = xbc_ref[s]
        conv = cb_ref[...]
        for w in range(CONV_W):
            conv = conv + ext[s, pl.ds(8 - (CONV_W - 1) + w, CHUNK), :] * cw_ref[w:w + 1, :]
        if nc > 1:
            ext[s, 0:8, :] = ext[s, CHUNK:CHUNK + 8, :]
        xc = _silu(conv)
        dt = _pad_rows(dt_ref, dtpad, s, t)
        cum = _mm_exact_lhs(tri_ref[...], dt * neg_a, 1, 0)
        seqs.append((xc, dt, cum))
    st1 = []
    for s in range(nbs):
        xc, dt, cum = seqs[s]
        xh = xc[:, 0:W_D]
        last = cum[CHUNK - 1:CHUNK, :]
        dtx = _bf(xh * dt)
        xw = _bf(xh * (jnp.exp(last - cum) * dt))
        cum_rows = _mm_exact_lhs(sel_ref[...], cum, 1, 1)
        per_g = []
        for g in range(G_D):
            bg = _bf(xc[:, W_D + g * N_D:W_D + (g + 1) * N_D])
            cg = _bf(xc[0:lq, W_D + G_D * N_D + g * N_D:W_D + G_D * N_D + (g + 1) * N_D])
            gr = slice(g * hpg * P_D, (g + 1) * hpg * P_D)
            cb = _dg(cg, bg, 1, 1)
            yoff = _dg(cg, _bf(st_ref[s, gr, :]), 1, 1)
            upd = _dg(xw[:, gr], bg, 0, 0)
            per_g.append((cb, yoff, upd))
        st1.append((dtx, cum_rows, per_g))
    for s in range(nbs):
        xc, dt, cum = seqs[s]
        dtx, cum_rows, per_g = st1[s]
        ecum = jnp.exp(cum[0:lq])
        elast = jnp.exp(cum[CHUNK - 1:CHUNK, :])
        ys = []
        for g in range(G_D):
            cb, yoff, upd = per_g[g]
            for pr in range(hpg // 2):
                l0 = g * hpg * P_D + pr * 128
                yh = []
                for hh in range(2):
                    h = g * hpg + pr * 2 + hh
                    col = jnp.broadcast_to(cum[0:lq, h * P_D:h * P_D + 1], (lq, CHUNK))
                    seg = jnp.minimum(col - cum_rows[h:h + 1, :], 0.0)
                    mh = jnp.where(ii >= jj, cb * jnp.exp(seg), 0.0)
                    yh.append(jnp.dot(_bf(mh), dtx[:, l0:l0 + 128], preferred_element_type=F32))
                ypair = jnp.where(lane < P_D, yh[0], yh[1])
                ys.append(ypair + yoff[:, pr * 128:(pr + 1) * 128] * ecum[:, l0:l0 + 128])
            for hl in range(hpg):
                h = g * hpg + hl
                r = slice(h * P_D, (h + 1) * P_D)
                dec = jnp.broadcast_to(elast[0:1, h * P_D:h * P_D + 1], (P_D, N_D))
                st_ref[s, r, :] = st_ref[s, r, :] * dec + upd[hl * P_D:(hl + 1) * P_D, :]
        y = jnp.concatenate(ys, axis=1)
        zg = _pad_rows(zg_ref, zpad, s, t)[0:lq].astype(F32)
        y = (y + xc[0:lq, 0:W_D] * dsk_ref[...]) * _silu(zg)
        gw = W_D // G_D
        outs = []
        for g in range(G_D):
            yg = y[:, g * gw:(g + 1) * gw]
            outs.append(yg * lax.rsqrt(jnp.mean(yg * yg, axis=-1, keepdims=True) + EPS))
        y_ref[s] = (jnp.concatenate(outs, axis=1) * nw_ref[...])[0:t].astype(y_ref.dtype)


def _ssd(xbc, dt, zg, tail, st0, conv_w, conv_b, a_log, d_skip, norm_w, tri, sel, lq, nbs):
    nb, nc, t, _ = xbc.shape
    row = lambda c_: pl.BlockSpec((nbs, None, t, c_), lambda b, c: (b, c, 0, 0))
    const = lambda a: pl.BlockSpec(a.shape, lambda b, c: (0,) * a.ndim)
    st_spec = pl.BlockSpec((nbs, H_D * P_D, N_D), lambda b, c: (b, 0, 0))
    return pl.pallas_call(
        functools.partial(_ssd_kernel, t=t, lq=lq, nc=nc, nbs=nbs),
        grid=(nb // nbs, nc),
        in_specs=[row(1024), row(512), row(512),
                  pl.BlockSpec((nbs, 8, CONV_DIM), lambda b, c: (b, 0, 0)), st_spec,
                  const(conv_w), const(conv_b), const(a_log), const(d_skip), const(norm_w),
                  const(tri), const(sel)],
        out_specs=(row(512), st_spec),
        out_shape=(jax.ShapeDtypeStruct((nb, nc, t, 512), zg.dtype),
                   jax.ShapeDtypeStruct((nb, H_D * P_D, N_D), F32)),
        scratch_shapes=[pltpu.VMEM((nbs, CHUNK + 8, CONV_DIM), F32),
                        pltpu.VMEM((nbs, CHUNK, 512), F32), pltpu.VMEM((nbs, CHUNK, 512), F32)],
        compiler_params=_cparams(2),
        name="ssd",
    )(xbc, dt, zg, tail, st0, conv_w, conv_b, a_log, d_skip, norm_w, tri, sel)


def _rotary_tables(pos):
    half = DK_B // 2
    inv = 1.0 / (10000.0 ** (jnp.arange(half, dtype=F32) / half))
    ang = pos.astype(F32)[:, None] * inv[None, :]
    cos, sin = jnp.cos(ang), jnp.sin(ang)
    cos_t = jnp.tile(jnp.concatenate([cos, cos], axis=1), (1, H_B))
    sin_t = jnp.tile(jnp.concatenate([-sin, sin], axis=1), (1, H_B))
    return cos_t, sin_t


def _retention_tables(chunk_len):
    log_g = np.log(1.0 - 2.0 ** (-5.0 - np.arange(H_B, dtype=np.float64)))
    idx = np.arange(CHUNK, dtype=np.float64)
    diff = idx[:, None] - idx[None, :]
    dmat = np.where(diff[None] >= 0, np.exp(np.maximum(diff, 0.0)[None] * log_g[:, None, None]), 0.0)
    qdec = np.exp((idx + 1.0)[:, None] * log_g[None, :])
    kdec = np.where(idx[:, None] < chunk_len, np.exp((chunk_len - 1.0 - idx)[:, None] * log_g[None, :]), 0.0)
    gl = np.exp(chunk_len * log_g)
    return (jnp.asarray(dmat, F32),
            jnp.asarray(np.repeat(qdec, DK_B, axis=1), F32),
            jnp.asarray(np.repeat(kdec, DK_B, axis=1), F32),
            jnp.asarray(np.repeat(np.repeat(gl, DK_B)[:, None], DV_B, axis=1), F32))


def _prompt_bias_idx():
    kk = np.arange(MOBA_BLOCK)[:, None]
    qq = np.arange(MOBA_BLOCK)[None, :]
    diag = np.where(qq >= kk, _t5_bucket_np(qq - kk), -1)
    sub = _t5_bucket_np(qq + MOBA_BLOCK - kk)
    return np.concatenate([diag, sub], axis=0).astype(np.int32)


def _sample_bias_idx(t_new):
    row_t = (np.arange(N_ROWS_S) // H_A)[:, None]
    qpos = PAST_LEN + row_t
    near = _t5_bucket_np(qpos - (PAST_LEN - MOBA_BLOCK + np.arange(MOBA_BLOCK))[None, :])
    own_k = np.arange(PAGE_SIZE)[None, :]
    own = np.where((own_k <= row_t) & (own_k < t_new), _t5_bucket_np(row_t - own_k), -1)
    return np.concatenate([near, own], axis=1).astype(np.int32)


def kernel(x_prompt, x_sample, cache_k, cache_v, state_ret, state_ssm, state_conv, page_table, c_prompt, c_sample, rel_bias, e_norm_w, e_ada_w, e_ada_b, e_in_w, e_q_norm_w, e_k_norm_w, e_out_w, o_norm_w, o_ada_w, o_ada_b, o_in_w, o_sgu_w, o_sgu_b, o_conv_w, o_conv_b, o_dt_bias, o_A_log, o_D, o_ssm_norm_w, o_out_w):
    bp, s_len, d = x_prompt.shape
    bs, t_len, _ = x_sample.shape
    n_s = bs * t_len

    c_all = jnp.concatenate([c_prompt, c_sample, jnp.zeros((8 - (bp + bs) % 8, d), F32)], axis=0)
    mods = []
    for ada_w, ada_b in ((e_ada_w[0], e_ada_b[0]), (o_ada_w[0], o_ada_b[0])):
        mod = _ada_mod(c_all, ada_w, ada_b)
        parts_p = [mod[:bp, i * d:(i + 1) * d].reshape(bp, 1, d) for i in range(3)]
        parts_s = [jnp.repeat(mod[bp:bp + bs, i * d:(i + 1) * d], t_len, axis=0).reshape(1, n_s, d) for i in range(3)]
        mods.append((parts_p, parts_s))
    (e_mod_p, e_mod_s), (o_mod_p, o_mod_s) = mods

    seg = jnp.asarray(np.kron(np.eye(H_A), np.ones((HD_A, HD_A))), BF16)
    qnw = jnp.tile(e_q_norm_w[0], H_A).reshape(1, W_A)
    knw = jnp.tile(e_k_norm_w[0], H_A).reshape(1, W_A)
    e_in_bf = _bf(e_in_w[0])
    e_out_bf = _bf(e_out_w[0])
    o_in_bf = _bf(jnp.concatenate([o_in_w[0][:, :3072], jnp.repeat(o_in_w[0][:, 3072:], P_D, axis=1)], axis=1))
    o_out_bf = _bf(o_out_w[0])
    x_s = x_sample.reshape(1, n_s, d)
    cos_p, sin_p = _rotary_tables(jnp.arange(s_len))
    cos_s, sin_s = _rotary_tables(PAST_LEN + (jnp.arange(n_s) % t_len))
    bias_p = _bias_tables(rel_bias, _prompt_bias_idx(), LOG2E).reshape(H_A, 2, MOBA_BLOCK, MOBA_BLOCK)
    bias_s_h = _bias_tables(rel_bias, _sample_bias_idx(t_len))
    row_h = jnp.arange(N_ROWS_S) % H_A
    bias_s = jnp.sum(jnp.where((jnp.arange(H_A)[:, None] == row_h[None, :])[:, :, None], bias_s_h, 0.0), axis=0)
    bfar = rel_bias[NUM_BUCKETS - 1, row_h].reshape(N_ROWS_S, 1)

    (qa, ka, va, kbf, vt, ga, qb, kb, vb, gb, kmean) = _even_in(
        x_prompt, e_mod_p[1], e_mod_p[0], e_norm_w[0].reshape(1, d), e_in_bf, qnw, knw, seg, cos_p, sin_p, False, BF16)
    (qa_s, ka_s, va_s, _, _, ga_s, qb_s, kb_s, vb_s, gb_s, _) = _even_in(
        x_s, e_mod_s[1], e_mod_s[0], e_norm_w[0].reshape(1, d), e_in_bf, qnw, knw, seg, cos_s, sin_s, True, F32)
    sq = lambda a: a.reshape(bs, t_len, a.shape[-1])
    qrep = jnp.repeat(sq(qa_s), H_A, axis=1)
    n_phys = cache_k.shape[1]
    page_t = lambda c: jnp.transpose(c[0], (0, 2, 3, 1)).reshape(n_phys, W_A, PAGE_SIZE)
    oa, oa_s = _moba(page_table, rel_bias, qa, kbf, vt, kmean.reshape(bp, s_len // MOBA_BLOCK, W_A), bias_p, ga,
                     qrep, sq(ka_s), sq(va_s), sq(ga_s), bias_s, bfar, page_t(cache_k), page_t(cache_v))
    nc_p = s_len // CHUNK
    ch = lambda a: a.reshape(bp, nc_p, CHUNK, a.shape[-1])
    ob, ret_p = _retention(ch(qb), ch(kb), ch(vb), ch(gb), jnp.zeros((bp, H_B * DK_B, DV_B), F32),
                           *_retention_tables(CHUNK), lq=CHUNK, nbs=bp)
    xp1 = _out_proj(oa, ob.reshape(bp, s_len, W_B), x_prompt, e_mod_p[2], e_out_bf, False)
    k_prompt = ka.reshape(1, bp, s_len, H_A, HD_A)
    v_prompt = va.reshape(1, bp, s_len, H_A, HD_A)
    ret_state_prompt = ret_p.reshape(1, bp, H_B, DK_B, DV_B)

    sc = lambda a: a.reshape(bs, 1, t_len, a.shape[-1])
    ob_s, ret_s = _retention(sc(qb_s), sc(kb_s), sc(vb_s), sc(gb_s),
                             state_ret[0].reshape(bs, H_B * DK_B, DV_B), *_retention_tables(t_len), lq=8, nbs=SEQ_PER_STEP)
    xs1 = _out_proj(oa_s.reshape(1, n_s, W_A), ob_s.reshape(1, n_s, W_B), x_s, e_mod_s[2], e_out_bf, True)
    k_sample = ka_s.reshape(1, bs, t_len, H_A, HD_A)
    v_sample = va_s.reshape(1, bs, t_len, H_A, HD_A)
    ret_state_sample = ret_s.reshape(1, bs, H_B, DK_B, DV_B)

    tri = jnp.asarray(np.tril(np.ones((CHUNK, CHUNK))), BF16)
    sel = jnp.asarray(np.kron(np.eye(H_D), np.eye(1, P_D)), BF16)
    rep = lambda a: jnp.repeat(a, P_D).reshape(1, W_D)
    dt_bias, a_log, d_skip = rep(o_dt_bias[0]), rep(o_A_log[0]), rep(o_D[0])
    ssm_nw = o_ssm_norm_w[0].reshape(1, W_D)
    conv_b = o_conv_b[0].reshape(1, CONV_DIM)
    o_nw = o_norm_w[0].reshape(1, d)

    sgu_b_p = jnp.repeat(o_sgu_b[0].T, W_C // G_C, axis=1)
    oc, zg, xbc, dtp = _odd_in(xp1, o_mod_p[1], o_mod_p[0], o_nw, o_in_bf, o_sgu_w[0], sgu_b_p, dt_bias,
                               False, BF16, False)
    yn, ssm_p = _ssd(ch(xbc), ch(dtp), ch(zg), jnp.zeros((bp, 8, CONV_DIM), F32),
                     jnp.zeros((bp, H_D * P_D, N_D), F32), o_conv_w[0], conv_b, a_log, d_skip, ssm_nw,
                     tri, sel, lq=CHUNK, nbs=bp)
    y_prompt = _out_proj(oc, yn.reshape(bp, s_len, W_D), xp1, o_mod_p[2], o_out_bf, False)
    ssm_state_prompt = ssm_p.reshape(1, bp, H_D, P_D, N_D)
    conv_state_prompt = xbc[:, -(CONV_W - 1):][None]

    per_chunk = CHUNK // t_len
    w_small = o_sgu_w[0][:, :t_len, :t_len]
    same_seq = jnp.asarray(np.kron(np.eye(per_chunk), np.ones((t_len, t_len))), F32)
    sgu_w_s = jnp.tile(w_small, (1, per_chunk, per_chunk)) * same_seq
    sgu_b_s = jnp.repeat(jnp.tile(o_sgu_b[0][:, :t_len].T, (per_chunk, 1)), W_C // G_C, axis=1)
    oc_s, zg_s, xbc_s, dt_s, v_s = _odd_in(xs1, o_mod_s[1], o_mod_s[0], o_nw, o_in_bf, sgu_w_s, sgu_b_s, dt_bias,
                                           True, F32, True)
    tail_s = jnp.concatenate([jnp.zeros((bs, 8 - (CONV_W - 1), CONV_DIM), F32), state_conv[0]], axis=1)
    yn_s, ssm_s = _ssd(sc(xbc_s), sc(dt_s), sc(zg_s), tail_s, state_ssm[0].reshape(bs, H_D * P_D, N_D),
                       o_conv_w[0], conv_b, a_log, d_skip, ssm_nw, tri, sel, lq=8, nbs=SEQ_PER_STEP)
    xs2 = _out_proj(oc_s, yn_s.reshape(1, n_s, W_D), xs1, o_mod_s[2], o_out_bf, True)
    y_sample = xs2.reshape(bs, t_len, d)
    sgu_v_sample = v_s.reshape(1, bs, t_len, W_C)
    ssm_state_sample = ssm_s.reshape(1, bs, H_D, P_D, N_D)
    xin = jnp.concatenate([state_conv[0], xbc_s.reshape(bs, t_len, CONV_DIM)], axis=1)
    conv_state_sample = xin[:, -(CONV_W - 1):][None]

    return (y_prompt, y_sample, k_prompt, v_prompt, k_sample, v_sample, ret_state_prompt, ret_state_sample,
            sgu_v_sample, ssm_state_prompt, ssm_state_sample, conv_state_prompt, conv_state_sample)
```

```python
import functools
import math

import numpy as np
import jax
import jax.numpy as jnp
from jax import lax
from jax.experimental import pallas as pl
from jax.experimental.pallas import tpu as pltpu

F32 = jnp.float32
BF16 = jnp.bfloat16

D_MODEL = 1024
PAST_LEN = 2048
PAGE_SIZE = 128
H_A, HD_A, W_A = 8, 64, 512
MOBA_BLOCK = 256
MOBA_TOPK = 3
NUM_BUCKETS = 32
MAX_DISTANCE = 128
H_B, DK_B, DV_B, W_B = 4, 64, 128, 512
G_C, W_C = 4, 512
H_D, P_D, N_D, G_D, W_D = 8, 64, 128, 2, 512
CONV_W = 4
CONV_DIM = 1024
CHUNK = 128
SEQ_PER_STEP = 8
PROJ_ROWS = 512
OUT_ROWS = 1024
NEG_INF = -1e30
EPS = 1e-6
LOG2E = math.log2(math.e)
VMEM_LIMIT = 56 * 1024 * 1024


def _bf(x):
    return x.astype(BF16)


def _dg(a, b, ca, cb):
    return lax.dot_general(a, b, (((ca,), (cb,)), ((), ())), preferred_element_type=F32)


def _mm(a, b):
    return _dg(_bf(a), _bf(b), 1, 0)


def _mm_nt(a, b):
    return _dg(_bf(a), _bf(b), 1, 1)


def _mm_tn(a, b):
    return _dg(_bf(a), _bf(b), 0, 0)


def _split2(x):
    hi = _bf(x)
    return hi, _bf(x - hi.astype(F32))


def _split3(x):
    hi = _bf(x)
    r = x - hi.astype(F32)
    mid = _bf(r)
    return hi, mid, _bf(r - mid.astype(F32))


def _mm_hp(a, b, ca, cb):
    ah, al = _split2(a)
    bh, bl = _split2(b)
    return _dg(ah, bh, ca, cb) + (_dg(ah, bl, ca, cb) + _dg(al, bh, ca, cb))


def _mm_exact_lhs(e, x, ca, cb):
    h, m, l = _split3(x)
    return _dg(e, h, ca, cb) + (_dg(e, m, ca, cb) + _dg(e, l, ca, cb))


def _silu(x):
    return x * (1.0 / (1.0 + jnp.exp(-x)))


def _gelu_tanh(x):
    return 0.5 * x * (1.0 + jnp.tanh(math.sqrt(2.0 / math.pi) * (x + 0.044715 * (x * x * x))))


def _softplus(x):
    return jnp.maximum(x, 0.0) + jnp.log1p(jnp.exp(-jnp.abs(x)))


def _cparams(n_grid):
    return pltpu.CompilerParams(dimension_semantics=("arbitrary",) * n_grid,
                                vmem_limit_bytes=VMEM_LIMIT)


def _top3_rows(g, blk, nblk):
    sel = jnp.zeros(g.shape, jnp.bool_)
    for _ in range(MOBA_TOPK):
        m = jnp.max(g, axis=0, keepdims=True)
        idx = jnp.min(jnp.where(g == m, blk, nblk), axis=0, keepdims=True)
        pick = blk == idx
        sel = jnp.logical_or(sel, pick)
        g = jnp.where(pick, -jnp.inf, g)
    return sel


def _ada_kernel(c_ref, w_ref, b_ref, o_ref):
    s = _silu(c_ref[...])
    o_ref[...] = _mm_hp(s, w_ref[...], 1, 0) + b_ref[...]


def _ada_mod(c_all, w, b):
    m, d = c_all.shape
    n = w.shape[1]
    tn = 512
    return pl.pallas_call(
        _ada_kernel,
        grid=(n // tn,),
        in_specs=[pl.BlockSpec((m, d), lambda j: (0, 0)),
                  pl.BlockSpec((d, tn), lambda j: (0, j)),
                  pl.BlockSpec((1, tn), lambda j: (0, j))],
        out_specs=pl.BlockSpec((m, tn), lambda j: (0, j)),
        out_shape=jax.ShapeDtypeStruct((m, n), F32),
        compiler_params=_cparams(1),
        name="ada_mod",
    )(c_all, w, b.reshape(1, n))


def _t5_bucket_np(rel):
    n = np.maximum(rel, 0)
    max_exact = NUM_BUCKETS // 2
    nf = np.maximum(n, 1).astype(np.float64)
    large = max_exact + (np.log(nf / max_exact) / math.log(MAX_DISTANCE / max_exact)
                         * (NUM_BUCKETS - max_exact)).astype(np.int64)
    large = np.minimum(large, NUM_BUCKETS - 1)
    return np.where(n < max_exact, n, large).astype(np.int32)


def _bias_kernel(tab_ref, idx_ref, o_ref, *, scale):
    h = pl.program_id(0)
    idx = idx_ref[...]
    acc = jnp.zeros(idx.shape, F32)
    for b in range(NUM_BUCKETS):
        acc = jnp.where(idx == b, tab_ref[b, h], acc)
    o_ref[0] = jnp.where(idx == -1, NEG_INF, acc * scale)


def _bias_tables(rel_bias, idx, scale=1.0):
    r, c = idx.shape
    return pl.pallas_call(
        functools.partial(_bias_kernel, scale=scale),
        grid=(H_A,),
        in_specs=[pl.BlockSpec(memory_space=pltpu.SMEM),
                  pl.BlockSpec((r, c), lambda h: (0, 0))],
        out_specs=pl.BlockSpec((1, r, c), lambda h: (h, 0, 0)),
        out_shape=jax.ShapeDtypeStruct((H_A, r, c), F32),
        compiler_params=_cparams(1),
        name="t5_bias",
    )(rel_bias, jnp.asarray(idx))


def _even_in_kernel(x_ref, sc_ref, sh_ref, nw_ref, w_ref, qnw_ref, knw_ref, seg_ref, cos_ref, sin_ref,
                    qa_ref, ka_ref, va_ref, kbf_ref, vt_ref, ga_ref, qb_ref, kb_ref, vb_ref, gb_ref, km_ref):
    x = x_ref[0]
    ms = jnp.mean(x * x, axis=-1, keepdims=True)
    h = (x * lax.rsqrt(ms + EPS) * nw_ref[...]) * (1.0 + sc_ref[0]) + sh_ref[0]
    hb = _bf(h)

    def proj(lo, hi):
        return jnp.dot(hb, w_ref[:, lo:hi], preferred_element_type=F32)

    def head_rms(t, w_row):
        ss = jnp.dot(_bf(t * t), seg_ref[...], preferred_element_type=F32)
        return t * lax.rsqrt(ss * (1.0 / HD_A) + EPS) * w_row

    qa_ref[0] = head_rms(proj(0, 512), qnw_ref[...])
    ka = head_rms(proj(512, 1024), knw_ref[...])
    ka_ref[0] = ka
    kbf_ref[0] = _bf(ka)
    for j in range(ka.shape[0] // MOBA_BLOCK):
        km_ref[0, j] = jnp.mean(ka[j * MOBA_BLOCK:(j + 1) * MOBA_BLOCK], axis=0, keepdims=True)
    va = proj(1024, 1536)
    va_ref[0] = va
    vat = va.T
    ones_pad = (lax.broadcasted_iota(jnp.int32, (V_ROWS - HD_A, vat.shape[1]), 0) == 0).astype(F32)
    vt_ref[0] = _bf(jnp.concatenate(
        [piece for h in range(H_A) for piece in (vat[h * HD_A:(h + 1) * HD_A], ones_pad)], axis=0))
    ga_ref[0] = proj(1536, 2048).astype(ga_ref.dtype)

    lane = lax.broadcasted_iota(jnp.int32, (1, 256), 1) % DK_B
    first_half = lane < (DK_B // 2)
    cos = cos_ref[...]
    sin = sin_ref[...]

    def rotary(t):
        up = pltpu.roll(t, 256 - DK_B // 2, 1)
        dn = pltpu.roll(t, DK_B // 2, 1)
        return t * cos + jnp.where(first_half, up, dn) * sin

    qb_ref[0] = rotary(proj(2048, 2304)).astype(qb_ref.dtype)
    kb_ref[0] = (rotary(proj(2304, 2560)) * (DK_B ** -0.5)).astype(kb_ref.dtype)
    vb_ref[0] = proj(2560, 3072).astype(vb_ref.dtype)
    gb_ref[0] = proj(3072, 3584).astype(gb_ref.dtype)


def _even_in(x, scale, shift, norm_w, w_bf, qnw, knw, seg, cos, sin, per_row_mod, act):
    nb, s, d = x.shape
    tm = PROJ_ROWS
    ns = s // tm
    nkb = tm // MOBA_BLOCK
    if per_row_mod:
        mod_spec = pl.BlockSpec((1, tm, d), lambda b, i: (b, i, 0))
    else:
        mod_spec = pl.BlockSpec((1, 1, d), lambda b, i: (b, 0, 0))
    row = lambda c: pl.BlockSpec((1, tm, c), lambda b, i: (b, i, 0))
    const = lambda shp: pl.BlockSpec(shp, lambda b, i: (0,) * len(shp))
    out_shape = (
        jax.ShapeDtypeStruct((nb, s, 512), F32),
        jax.ShapeDtypeStruct((nb, s, 512), F32),
        jax.ShapeDtypeStruct((nb, s, 512), F32),
        jax.ShapeDtypeStruct((nb, s, 512), BF16),
        jax.ShapeDtypeStruct((nb, H_A * V_ROWS, s), BF16),
        jax.ShapeDtypeStruct((nb, s, 512), act),
        jax.ShapeDtypeStruct((nb, s, 256), act),
        jax.ShapeDtypeStruct((nb, s, 256), act),
        jax.ShapeDtypeStruct((nb, s, 512), act),
        jax.ShapeDtypeStruct((nb, s, 512), act),
        jax.ShapeDtypeStruct((nb, ns * nkb, 1, 512), F32),
    )
    out_specs = (row(512), row(512), row(512), row(512),
                 pl.BlockSpec((1, H_A * V_ROWS, tm), lambda b, i: (b, 0, i)),
                 row(512), row(256), row(256), row(512), row(512),
                 pl.BlockSpec((1, nkb, 1, 512), lambda b, i: (b, i, 0, 0)))
    return pl.pallas_call(
        _even_in_kernel,
        grid=(nb, ns),
        in_specs=[row(d), mod_spec, mod_spec, const((1, d)), const((d, 3584)),
                  const((1, 512)), const((1, 512)), const((512, 512)),
                  pl.BlockSpec((tm, 256), lambda b, i: (i, 0)),
                  pl.BlockSpec((tm, 256), lambda b, i: (i, 0))],
        out_specs=out_specs,
        out_shape=out_shape,
        compiler_params=_cparams(2),
        name="even_in",
    )(x, scale, shift, norm_w, w_bf, qnw, knw, seg, cos, sin)


MOBA_HS = 8
V_ROWS = HD_A + 8
FAR_KEYS = 2 * MOBA_BLOCK


def _moba_prompt_select(tab_ref, q_ref, km_ref, rbf_ref, rbs_ref, qs_ref):
    hg = pl.program_id(1)
    qi = pl.program_id(2)
    nblk = km_ref.shape[1]
    lane = lax.broadcasted_iota(jnp.int32, (1, 128), 1)
    blk = lax.broadcasted_iota(jnp.int32, (nblk, MOBA_BLOCK), 0)
    for hl in range(MOBA_HS):
        pr, hh = divmod(hl, 2)
        pc = slice(pr * 128, (pr + 1) * 128)
        qm = jnp.where((lane // HD_A) == hh, q_ref[0, :, pc], 0.0)
        gate = _mm_hp(km_ref[0, :, pc], qm, 1, 1)
        gate = jnp.where(blk < qi, gate, NEG_INF)
        sel = jnp.logical_and(_top3_rows(gate, blk, nblk), blk < qi)
        far_c = tab_ref[NUM_BUCKETS - 1, MOBA_HS * hg + hl] * LOG2E
        rbf_ref[hl] = jnp.where(jnp.logical_and(sel, blk < qi - 1), far_c, NEG_INF)
        rbs_ref[hl] = jnp.where(jnp.logical_or(sel, blk == qi), 0.0, NEG_INF)
        qs_ref[hl] = _bf(qm * (HD_A ** -0.5 * LOG2E))


def _moba_prompt_attend(k_ref, vt_ref, bias_ref, ga_ref, o_ref, rbf_ref, rbs_ref, qs_ref):
    qi = pl.program_id(2)

    def visit(carry, off, nkeys, extra_fn=None, block_rows=None):
        ss = []
        for hl in range(MOBA_HS):
            pr = hl // 2
            kj = k_ref[0, pl.ds(off, nkeys), pr * 128:(pr + 1) * 128]
            ss.append(_dg(kj, qs_ref[hl], 1, 1))
        stats, ps = [], []
        for hl in range(MOBA_HS):
            m = carry[hl][0]
            if block_rows is None:
                s = extra_fn(hl, ss[hl])
                mn = jnp.maximum(m, jnp.max(s, axis=0, keepdims=True))
                p = jnp.exp2(s - mn)
            else:
                halves = [ss[hl][i * MOBA_BLOCK:(i + 1) * MOBA_BLOCK] for i in range(nkeys // MOBA_BLOCK)]
                rows = block_rows(hl)
                mn = m
                for sh, r in zip(halves, rows):
                    mn = jnp.maximum(mn, jnp.max(sh, axis=0, keepdims=True) + r)
                p = jnp.concatenate([jnp.exp2(sh - (mn - r)) for sh, r in zip(halves, rows)], axis=0)
            stats.append((mn, jnp.exp2(m - mn)))
            ps.append(_bf(p))
        pvs = []
        for hl in range(MOBA_HS):
            vj = vt_ref[0, hl * V_ROWS:(hl + 1) * V_ROWS, pl.ds(off, nkeys)]
            pvs.append(jnp.dot(vj, ps[hl], preferred_element_type=F32))
        return tuple((stats[hl][0], stats[hl][1] * carry[hl][1] + pvs[hl]) for hl in range(MOBA_HS))

    def far_body(jp, carry):
        off = pl.multiple_of(jp * FAR_KEYS, FAR_KEYS)

        def rows(hl):
            return rbf_ref[hl, pl.ds(2 * jp, 1), :], rbf_ref[hl, pl.ds(2 * jp + 1, 1), :]

        return visit(carry, off, FAR_KEYS, block_rows=rows)

    init = tuple((jnp.full((1, MOBA_BLOCK), -jnp.inf, F32), jnp.zeros((V_ROWS, MOBA_BLOCK), F32))
                 for _ in range(MOBA_HS))
    carry = lax.fori_loop(0, qi // 2, far_body, init)
    js = jnp.maximum(qi - 1, 0)
    first = qi == 0
    top_tab = jnp.where(first, 0, 1)
    bot_mask = jnp.where(first, NEG_INF, 0.0)

    def near_extra(hl, s):
        top = s[:MOBA_BLOCK] + bias_ref[hl, top_tab] + rbs_ref[hl, pl.ds(js, 1), :]
        bot = s[MOBA_BLOCK:] + (bias_ref[hl, 0] + bot_mask)
        return jnp.concatenate([top, bot], axis=0)

    carry = visit(carry, pl.multiple_of(js * MOBA_BLOCK, MOBA_BLOCK), FAR_KEYS, near_extra)
    for pr in range(MOBA_HS // 2):
        accs = [carry[2 * pr + hh][1] for hh in range(2)]
        outs = [a[0:HD_A] * (1.0 / a[HD_A:HD_A + 1]) for a in accs]
        o = jnp.concatenate(outs, axis=0).T
        pc = slice(pr * 128, (pr + 1) * 128)
        o_ref[0, :, pc] = (o * _silu(ga_ref[0, :, pc].astype(F32))).astype(o_ref.dtype)


N_PAST_BLK = PAST_LEN // MOBA_BLOCK
N_PAGES = PAST_LEN // PAGE_SIZE
N_ROWS_S = 32
SEQ_PER_TILE = 2


def _page_copies(pt_ref, ck_hbm, cv_hbm, kt_buf, vt_buf, sems, seq, sl):
    cps = []
    for p in range(N_PAGES):
        dst = pl.ds(p * PAGE_SIZE, PAGE_SIZE)
        cps.append(pltpu.make_async_copy(ck_hbm.at[pt_ref[seq, p]], kt_buf.at[sl, :, dst], sems.at[0, sl]))
        cps.append(pltpu.make_async_copy(cv_hbm.at[pt_ref[seq, p]], vt_buf.at[sl, :, dst], sems.at[1, sl]))
    return cps


def _moba_sample_seqs(qrep_ref, knew_ref, vnew_ref, ga_ref, bias_ref, bfar_ref, kt_buf, vt_buf, kpad, vpad, o_ref):
    t_new = knew_ref.shape[1]
    rowh = lax.broadcasted_iota(jnp.int32, (N_ROWS_S, W_A), 0) % H_A
    laneh = lax.broadcasted_iota(jnp.int32, (N_ROWS_S, W_A), 1) // HD_A
    own_head = rowh == laneh
    blocks = [slice(n * MOBA_BLOCK, (n + 1) * MOBA_BLOCK) for n in range(N_PAST_BLK)]
    bfar = bfar_ref[...]
    scored = []
    for j in range(SEQ_PER_TILE):
        qf = jnp.where(own_head, qrep_ref[j], 0.0) * (HD_A ** -0.5)
        qbd = _bf(qf)
        q2 = jnp.concatenate([qbd, _bf(qf - qbd.astype(F32))], axis=0)
        kpad[j, 0:t_new, :] = knew_ref[j]
        vpad[j, 0:t_new, :] = vnew_ref[j]
        s_past = []
        for n in range(N_PAST_BLK):
            s2 = jnp.dot(q2, _bf(kt_buf[j, :, blocks[n]]), preferred_element_type=F32)
            s_past.append(s2[0:N_ROWS_S] + s2[N_ROWS_S:])
        s_own = _dg(qbd, _bf(kpad[j]), 1, 1) + bias_ref[:, MOBA_BLOCK:]
        scored.append((s_past, s_own))
    probs = []
    for s_past, s_own in scored:
        g = [jnp.sum(s, axis=1, keepdims=True) for s in s_past]
        sel = [jnp.zeros((N_ROWS_S, 1), jnp.bool_) for _ in range(N_PAST_BLK)]
        for _ in range(MOBA_TOPK):
            m = functools.reduce(jnp.maximum, g)
            idx = functools.reduce(jnp.minimum, [jnp.where(g[n] == m, n, N_PAST_BLK) for n in range(N_PAST_BLK)])
            for n in range(N_PAST_BLK):
                pick = idx == n
                sel[n] = jnp.logical_or(sel[n], pick)
                g[n] = jnp.where(pick, -jnp.inf, g[n])
        logits = [s_past[n] + jnp.where(sel[n], bfar, NEG_INF) for n in range(N_PAST_BLK - 1)]
        logits.append(s_past[-1] + bias_ref[:, 0:MOBA_BLOCK] + jnp.where(sel[-1], 0.0, NEG_INF))
        m = jnp.max(s_own, axis=1, keepdims=True)
        for s in logits:
            m = jnp.maximum(m, jnp.max(s, axis=1, keepdims=True))
        p_own = jnp.exp(s_own - m)
        ps = [jnp.exp(s - m) for s in logits]
        l = functools.reduce(jnp.add, [jnp.sum(p, axis=1, keepdims=True) for p in ps + [p_own]])
        probs.append(([_bf(p) for p in ps], _bf(p_own), l))
    for j, (ps, p_own, l) in enumerate(probs):
        acc = jnp.dot(p_own, _bf(vpad[j]), preferred_element_type=F32)
        for n in range(N_PAST_BLK):
            acc = acc + _dg(ps[n], _bf(vt_buf[j, :, blocks[n]]), 1, 1)
        o = jnp.where(own_head, acc * (1.0 / l), 0.0)
        o = jnp.sum(o.reshape(t_new, H_A, W_A), axis=1)
        o_ref[j] = o * _silu(ga_ref[j])


def _moba_kernel(pt_ref, tab_ref, q_ref, k_ref, vt_ref, km_ref, bias_ref, ga_ref,
                 qrep_ref, knew_ref, vnew_ref, gas_ref, bias_s_ref, bfar_ref, ck_hbm, cv_hbm,
                 o_ref, os_ref, rbf_ref, rbs_ref, qs_ref, kt_buf, vt_buf, kpad, vpad, sems):
    step = pl.program_id(0) * pl.num_programs(2) + pl.program_id(2)
    n_steps = pl.num_programs(0) * pl.num_programs(2)
    seq0 = SEQ_PER_TILE * step
    copies = functools.partial(_page_copies, pt_ref, ck_hbm, cv_hbm, kt_buf, vt_buf, sems)

    @pl.when(step == 0)
    def _():
        kpad[...] = jnp.zeros(kpad.shape, F32)
        vpad[...] = jnp.zeros(vpad.shape, F32)
        for j in range(SEQ_PER_TILE):
            for cp in copies(j, j):
                cp.start()

    for j in range(SEQ_PER_TILE):
        for cp in copies(seq0 + j, j):
            cp.wait()
    _moba_sample_seqs(qrep_ref, knew_ref, vnew_ref, gas_ref, bias_s_ref, bfar_ref, kt_buf, vt_buf, kpad, vpad, os_ref)
    _moba_prompt_select(tab_ref, q_ref, km_ref, rbf_ref, rbs_ref, qs_ref)

    @pl.when(step + 1 < n_steps)
    def _():
        for j in range(SEQ_PER_TILE):
            for cp in copies(seq0 + SEQ_PER_TILE + j, j):
                cp.start()

    _moba_prompt_attend(k_ref, vt_ref, bias_ref, ga_ref, o_ref, rbf_ref, rbs_ref, qs_ref)


def _moba(page_table, rel_bias, qa, kbf, vt, kmean, bias_t, ga,
          qrep, knew, vnew, ga_s, bias_s, bfar, cache_kt, cache_vt):
    nb, s, _ = qa.shape
    nq = s // MOBA_BLOCK
    nseq, t_new, _ = knew.shape
    assert H_A == MOBA_HS and nseq == SEQ_PER_TILE * nb * nq
    tile = pl.BlockSpec((1, MOBA_BLOCK, W_A), lambda b, hg, i, pt: (b, i, 0))
    per_b = lambda shp: pl.BlockSpec((1,) + shp, lambda b, hg, i, pt: (b, 0, 0))
    seqs = lambda r: pl.BlockSpec((SEQ_PER_TILE, r, W_A), lambda b, hg, i, pt: (b * nq + i, 0, 0))
    const = lambda a: pl.BlockSpec(a.shape, lambda b, hg, i, pt: (0,) * a.ndim)
    grid_spec = pltpu.PrefetchScalarGridSpec(
        num_scalar_prefetch=1,
        grid=(nb, 1, nq),
        in_specs=[pl.BlockSpec(memory_space=pltpu.SMEM),
                  tile, per_b((s, W_A)), per_b((H_A * V_ROWS, s)), per_b((nq, W_A)), const(bias_t), tile,
                  seqs(N_ROWS_S), seqs(t_new), seqs(t_new), seqs(t_new), const(bias_s), const(bfar),
                  pl.BlockSpec(memory_space=pl.ANY), pl.BlockSpec(memory_space=pl.ANY)],
        out_specs=(tile, seqs(t_new)),
        scratch_shapes=[pltpu.VMEM((MOBA_HS, nq, MOBA_BLOCK), F32),
                        pltpu.VMEM((MOBA_HS, nq, MOBA_BLOCK), F32),
                        pltpu.VMEM((MOBA_HS, MOBA_BLOCK, 128), BF16),
                        pltpu.VMEM((SEQ_PER_TILE, W_A, PAST_LEN), F32),
                        pltpu.VMEM((SEQ_PER_TILE, W_A, PAST_LEN), F32),
                        pltpu.VMEM((SEQ_PER_TILE, PAGE_SIZE, W_A), F32),
                        pltpu.VMEM((SEQ_PER_TILE, PAGE_SIZE, W_A), F32),
                        pltpu.SemaphoreType.DMA((2, SEQ_PER_TILE))],
    )
    return pl.pallas_call(
        _moba_kernel,
        grid_spec=grid_spec,
        out_shape=(jax.ShapeDtypeStruct((nb, s, W_A), BF16),
                   jax.ShapeDtypeStruct((nseq, t_new, W_A), F32)),
        compiler_params=_cparams(3),
        name="moba",
    )(page_table, rel_bias, qa, kbf, vt, kmean, bias_t, ga, qrep, knew, vnew, ga_s, bias_s, bfar, cache_kt, cache_vt)


def _pad_rows(ref, scratch, s, t):
    if t == CHUNK:
        return ref[s]
    scratch[s] = jnp.zeros(scratch.shape[1:], scratch.dtype)
    scratch[s, 0:t, :] = ref[s].astype(scratch.dtype)
    return scratch[s]


def _ret_kernel(q_ref, k_ref, v_ref, g_ref, st0_ref, dmat_ref, qdec_ref, kdec_ref, gl_ref,
                o_ref, st_ref, qpad, kpad, vpad, gpad, *, t, lq, nbs):
    c = pl.program_id(1)

    @pl.when(c == 0)
    def _():
        st_ref[...] = st0_ref[...]

    lane = lax.broadcasted_iota(jnp.int32, (1, 128), 1)
    rowsel = lax.broadcasted_iota(jnp.int32, (128, 1), 0) < DK_B
    work = []
    for s in range(nbs):
        q = _pad_rows(q_ref, qpad, s, t)[0:lq]
        k = _pad_rows(k_ref, kpad, s, t)
        v = _pad_rows(v_ref, vpad, s, t)
        for hp in range(H_B // 2):
            cols = slice(hp * 128, (hp + 1) * 128)
            kp = _bf(k[:, cols])
            st = st_ref[s, cols, :]
            kd = _bf(k[:, cols] * kdec_ref[:, cols])
            for hh in range(2):
                h = 2 * hp + hh
                qm = jnp.where((lane // DK_B) == hh, q[:, cols], 0.0)
                vh = _bf(v[:, h * DV_B:(h + 1) * DV_B])
                sc = _dg(_bf(qm), kp, 1, 1)
                so = _mm(qm * qdec_ref[0:lq, cols], st)
                upd = _dg(kd, vh, 0, 0)
                work.append((s, hp, hh, sc, so, upd, vh, st))
    outs = {}
    for (s, hp, hh, sc, so, upd, vh, st) in work:
        h = 2 * hp + hh
        o = jnp.dot(_bf(sc * dmat_ref[h, 0:lq, :]), vh, preferred_element_type=F32) + so
        outs[(s, h)] = o * lax.rsqrt(jnp.mean(o * o, axis=-1, keepdims=True) + EPS)
    for i in range(0, len(work), 2):
        s, hp, _, _, _, upd0, _, st = work[i]
        cols = slice(hp * 128, (hp + 1) * 128)
        st_ref[s, cols, :] = st * gl_ref[cols, :] + jnp.where(rowsel, upd0, work[i + 1][5])
    for s in range(nbs):
        g = _pad_rows(g_ref, gpad, s, t)[0:lq].astype(F32)
        o = jnp.concatenate([outs[(s, h)] for h in range(H_B)], axis=1)
        o_ref[s] = (o * _silu(g))[0:t].astype(o_ref.dtype)


def _retention(q, k, v, g, st0, dmat, qdec, kdec, gl, lq, nbs):
    nb, nc, t, _ = q.shape
    row = lambda c_: pl.BlockSpec((nbs, None, t, c_), lambda b, c: (b, c, 0, 0))
    const = lambda a: pl.BlockSpec(a.shape, lambda b, c: (0,) * a.ndim)
    st_spec = pl.BlockSpec((nbs, H_B * DK_B, DV_B), lambda b, c: (b, 0, 0))
    pad = lambda c_: pltpu.VMEM((nbs, CHUNK, c_), F32)
    return pl.pallas_call(
        functools.partial(_ret_kernel, t=t, lq=lq, nbs=nbs),
        grid=(nb // nbs, nc),
        in_specs=[row(256), row(256), row(512), row(512), st_spec,
                  const(dmat), const(qdec), const(kdec), const(gl)],
        out_specs=(row(512), st_spec),
        out_shape=(jax.ShapeDtypeStruct((nb, nc, t, 512), q.dtype),
                   jax.ShapeDtypeStruct((nb, H_B * DK_B, DV_B), F32)),
        scratch_shapes=[pad(256), pad(256), pad(512), pad(512)],
        compiler_params=_cparams(2),
        name="retention",
    )(q, k, v, g, st0, dmat, qdec, kdec, gl)


def _out_kernel(a_ref, b_ref, x_ref, g_ref, w_ref, o_ref):
    half = w_ref.shape[0] // 2
    y = (jnp.dot(_bf(a_ref[0]), w_ref[0:half, :], preferred_element_type=F32)
         + jnp.dot(_bf(b_ref[0]), w_ref[half:, :], preferred_element_type=F32))
    o_ref[0] = x_ref[0] + g_ref[0] * y


def _out_proj(a, b, x, gate, w_bf, per_row_mod):
    nb, s, d = x.shape
    tm = min(OUT_ROWS, s)
    if per_row_mod:
        g_spec = pl.BlockSpec((1, tm, d), lambda bb, i: (bb, i, 0))
    else:
        g_spec = pl.BlockSpec((1, 1, d), lambda bb, i: (bb, 0, 0))
    row = lambda c: pl.BlockSpec((1, tm, c), lambda bb, i: (bb, i, 0))
    return pl.pallas_call(
        _out_kernel,
        grid=(nb, s // tm),
        in_specs=[row(512), row(512), row(d), g_spec, pl.BlockSpec(w_bf.shape, lambda bb, i: (0, 0))],
        out_specs=row(d),
        out_shape=jax.ShapeDtypeStruct((nb, s, d), F32),
        compiler_params=_cparams(2),
        name="out_proj",
    )(a, b, x, gate, w_bf)


def _odd_in_kernel(x_ref, sc_ref, sh_ref, nw_ref, w_ref, wdt_ref, sguw_ref, sgub_ref, dtb_ref,
                   oc_ref, zg_ref, xbc_ref, dt_ref, *maybe_v_ref):
    x = x_ref[0]
    tm = x.shape[0]
    ms = jnp.mean(x * x, axis=-1, keepdims=True)
    h = (x * lax.rsqrt(ms + EPS) * nw_ref[...]) * (1.0 + sc_ref[0]) + sh_ref[0]
    hb = _bf(h)

    def proj(lo, hi):
        return jnp.dot(hb, w_ref[:, lo:hi], preferred_element_type=F32)

    u = _gelu_tanh(proj(0, 512))
    v = _gelu_tanh(proj(512, 1024))
    mu = jnp.mean(v, axis=-1, keepdims=True)
    vc = v - mu
    v = vc * lax.rsqrt(jnp.mean(vc * vc, axis=-1, keepdims=True) + EPS)
    for v_ref in maybe_v_ref:
        v_ref[0] = v
    ii = lax.broadcasted_iota(jnp.int32, (CHUNK, CHUNK), 0)
    jj = lax.broadcasted_iota(jnp.int32, (CHUNK, CHUNK), 1)
    rows = []
    for ci in range(tm // CHUNK):
        cols = []
        for g in range(G_C):
            wg = jnp.where(ii >= jj, sguw_ref[g], 0.0)
            cols.append(_mm(wg, v[ci * CHUNK:(ci + 1) * CHUNK, g * 128:(g + 1) * 128]))
        rows.append(jnp.concatenate(cols, axis=1) + sgub_ref[...])
    sg = jnp.concatenate(rows, axis=0) if len(rows) > 1 else rows[0]
    oc_ref[0] = (u * sg * _silu(proj(1024, 1536))).astype(oc_ref.dtype)
    zg_ref[0] = proj(1536, 2048).astype(zg_ref.dtype)
    xbc_ref[0] = proj(2048, 3072)
    dt_ref[0] = _softplus(jnp.dot(hb, wdt_ref[...], preferred_element_type=F32) + dtb_ref[...])


def _odd_in(x, scale, shift, norm_w, w_bf, w_dt, sgu_w, sgu_b_tab, dt_bias, per_row_mod, act, emit_v):
    nb, s, d = x.shape
    tm = PROJ_ROWS
    if per_row_mod:
        mod_spec = pl.BlockSpec((1, tm, d), lambda b, i: (b, i, 0))
    else:
        mod_spec = pl.BlockSpec((1, 1, d), lambda b, i: (b, 0, 0))
    row = lambda c: pl.BlockSpec((1, tm, c), lambda b, i: (b, i, 0))
    const = lambda shp: pl.BlockSpec(shp, lambda b, i: (0,) * len(shp))
    return pl.pallas_call(
        _odd_in_kernel,
        grid=(nb, s // tm),
        in_specs=[row(d), mod_spec, mod_spec, const((1, d)), const((d, 3072)), const(w_dt.shape),
                  const(sgu_w.shape), const(sgu_b_tab.shape), const((1, 512))],
        out_specs=(row(512), row(512), row(1024), row(512)) + ((row(512),) if emit_v else ()),
        out_shape=(jax.ShapeDtypeStruct((nb, s, 512), act),
                   jax.ShapeDtypeStruct((nb, s, 512), act),
                   jax.ShapeDtypeStruct((nb, s, 1024), F32),
                   jax.ShapeDtypeStruct((nb, s, 512), F32),
                   ) + ((jax.ShapeDtypeStruct((nb, s, 512), F32),) if emit_v else ()),
        compiler_params=_cparams(2),
        name="odd_in",
    )(x, scale, shift, norm_w, w_bf, w_dt, sgu_w, sgu_b_tab, dt_bias)


def _ssd_kernel(xbc_ref, dt_ref, zg_ref, tail_ref, st0_ref, cw_ref, cb_ref, alog_ref, dsk_ref, nw_ref,
                tri_ref, sel_ref, y_ref, st_ref, ext, dtpad, zpad, *, t, lq, nc, nbs):
    c = pl.program_id(1)

    @pl.when(c == 0)
    def _():
        st_ref[...] = st0_ref[...]
        ext[...] = jnp.zeros(ext.shape, F32)
        ext[:, 0:8, :] = tail_ref[...]

    ii = lax.broadcasted_iota(jnp.int32, (lq, CHUNK), 0)
    jj = lax.broadcasted_iota(jnp.int32, (lq, CHUNK), 1)
    lane = lax.broadcasted_iota(jnp.int32, (1, 128), 1)
    hpg = H_D // G_D
    neg_a = -jnp.exp(alog_ref[...])
    seqs = []
    for s in range(nbs):
        ext[s, 8:8 + t, :] = xbc_ref[s]
        conv = cb_ref[...]
        for w in range(CONV_W):
            conv = conv + ext[s, pl.ds(8 - (CONV_W - 1) + w, CHUNK), :] * cw_ref[w:w + 1, :]
        if nc > 1:
            ext[s, 0:8, :] = ext[s, CHUNK:CHUNK + 8, :]
        xc = _silu(conv)
        dt = _pad_rows(dt_ref, dtpad, s, t)
        cum = _mm_exact_lhs(tri_ref[...], dt * neg_a, 1, 0)
        seqs.append((xc, dt, cum))
    st1 = []
    for s in range(nbs):
        xc, dt, cum = seqs[s]
        xh = xc[:, 0:W_D]
        last = cum[CHUNK - 1:CHUNK, :]
        dtx = _bf(xh * dt)
        xw = _bf(xh * (jnp.exp(last - cum) * dt))
        cum_rows = _mm_exact_lhs(sel_ref[...], cum, 1, 1)
        per_g = []
        for g in range(G_D):
            bg = _bf(xc[:, W_D + g * N_D:W_D + (g + 1) * N_D])
            cg = _bf(xc[0:lq, W_D + G_D * N_D + g * N_D:W_D + G_D * N_D + (g + 1) * N_D])
            gr = slice(g * hpg * P_D, (g + 1) * hpg * P_D)
            cb = _dg(cg, bg, 1, 1)
            yoff = _dg(cg, _bf(st_ref[s, gr, :]), 1, 1)
            upd = _dg(xw[:, gr], bg, 0, 0)
            per_g.append((cb, yoff, upd))
        st1.append((dtx, cum_rows, per_g))
    for s in range(nbs):
        xc, dt, cum = seqs[s]
        dtx, cum_rows, per_g = st1[s]
        ecum = jnp.exp(cum[0:lq])
        elast = jnp.exp(cum[CHUNK - 1:CHUNK, :])
        ys = []
        for g in range(G_D):
            cb, yoff, upd = per_g[g]
            for pr in range(hpg // 2):
                l0 = g * hpg * P_D + pr * 128
                yh = []
                for hh in range(2):
                    h = g * hpg + pr * 2 + hh
                    col = jnp.broadcast_to(cum[0:lq, h * P_D:h * P_D + 1], (lq, CHUNK))
                    seg = jnp.minimum(col - cum_rows[h:h + 1, :], 0.0)
                    mh = jnp.where(ii >= jj, cb * jnp.exp(seg), 0.0)
                    yh.append(jnp.dot(_bf(mh), dtx[:, l0:l0 + 128], preferred_element_type=F32))
                ypair = jnp.where(lane < P_D, yh[0], yh[1])
                ys.append(ypair + yoff[:, pr * 128:(pr + 1) * 128] * ecum[:, l0:l0 + 128])
            for hl in range(hpg):
                h = g * hpg + hl
                r = slice(h * P_D, (h + 1) * P_D)
                dec = jnp.broadcast_to(elast[0:1, h * P_D:h * P_D + 1], (P_D, N_D))
                st_ref[s, r, :] = st_ref[s, r, :] * dec + upd[hl * P_D:(hl + 1) * P_D, :]
        y = jnp.concatenate(ys, axis=1)
        zg = _pad_rows(zg_ref, zpad, s, t)[0:lq].astype(F32)
        y = (y + xc[0:lq, 0:W_D] * dsk_ref[...]) * _silu(zg)
        gw = W_D // G_D
        outs = []
        for g in range(G_D):
            yg = y[:, g * gw:(g + 1) * gw]
            outs.append(yg * lax.rsqrt(jnp.mean(yg * yg, axis=-1, keepdims=True) + EPS))
        y_ref[s] = (jnp.concatenate(outs, axis=1) * nw_ref[...])[0:t].astype(y_ref.dtype)


def _ssd(xbc, dt, zg, tail, st0, conv_w, conv_b, a_log, d_skip, norm_w, tri, sel, lq, nbs):
    nb, nc, t, _ = xbc.shape
    row = lambda c_: pl.BlockSpec((nbs, None, t, c_), lambda b, c: (b, c, 0, 0))
    const = lambda a: pl.BlockSpec(a.shape, lambda b, c: (0,) * a.ndim)
    st_spec = pl.BlockSpec((nbs, H_D * P_D, N_D), lambda b, c: (b, 0, 0))
    return pl.pallas_call(
        functools.partial(_ssd_kernel, t=t, lq=lq, nc=nc, nbs=nbs),
        grid=(nb // nbs, nc),
        in_specs=[row(1024), row(512), row(512),
                  pl.BlockSpec((nbs, 8, CONV_DIM), lambda b, c: (b, 0, 0)), st_spec,
                  const(conv_w), const(conv_b), const(a_log), const(d_skip), const(norm_w),
                  const(tri), const(sel)],
        out_specs=(row(512), st_spec),
        out_shape=(jax.ShapeDtypeStruct((nb, nc, t, 512), zg.dtype),
                   jax.ShapeDtypeStruct((nb, H_D * P_D, N_D), F32)),
        scratch_shapes=[pltpu.VMEM((nbs, CHUNK + 8, CONV_DIM), F32),
                        pltpu.VMEM((nbs, CHUNK, 512), F32), pltpu.VMEM((nbs, CHUNK, 512), F32)],
        compiler_params=_cparams(2),
        name="ssd",
    )(xbc, dt, zg, tail, st0, conv_w, conv_b, a_log, d_skip, norm_w, tri, sel)


def _rotary_tables(pos):
    half = DK_B // 2
    inv = (np.float32(1.0) / np.float32(10000.0) ** (np.arange(half, dtype=np.float32) / np.float32(half)))
    ang = (np.asarray(pos).astype(np.float32)[:, None] * inv.astype(np.float32)[None, :]).astype(np.float64)
    cos, sin = np.cos(ang), np.sin(ang)
    cos_t = np.tile(np.concatenate([cos, cos], axis=1), (1, H_B))
    sin_t = np.tile(np.concatenate([-sin, sin], axis=1), (1, H_B))
    return jnp.asarray(cos_t, F32), jnp.asarray(sin_t, F32)


def _retention_tables(chunk_len):
    log_g = np.log(1.0 - 2.0 ** (-5.0 - np.arange(H_B, dtype=np.float64)))
    idx = np.arange(CHUNK, dtype=np.float64)
    diff = idx[:, None] - idx[None, :]
    dmat = np.where(diff[None] >= 0, np.exp(np.maximum(diff, 0.0)[None] * log_g[:, None, None]), 0.0)
    qdec = np.exp((idx + 1.0)[:, None] * log_g[None, :])
    kdec = np.where(idx[:, None] < chunk_len, np.exp((chunk_len - 1.0 - idx)[:, None] * log_g[None, :]), 0.0)
    gl = np.exp(chunk_len * log_g)
    return (jnp.asarray(dmat, F32),
            jnp.asarray(np.repeat(qdec, DK_B, axis=1), F32),
            jnp.asarray(np.repeat(kdec, DK_B, axis=1), F32),
            jnp.asarray(np.repeat(np.repeat(gl, DK_B)[:, None], DV_B, axis=1), F32))


def _prompt_bias_idx():
    kk = np.arange(MOBA_BLOCK)[:, None]
    qq = np.arange(MOBA_BLOCK)[None, :]
    diag = np.where(qq >= kk, _t5_bucket_np(qq - kk), -1)
    sub = _t5_bucket_np(qq + MOBA_BLOCK - kk)
    return np.concatenate([diag, sub], axis=0).astype(np.int32)


def _sample_bias_idx(t_new):
    row_t = (np.arange(N_ROWS_S) // H_A)[:, None]
    qpos = PAST_LEN + row_t
    near = _t5_bucket_np(qpos - (PAST_LEN - MOBA_BLOCK + np.arange(MOBA_BLOCK))[None, :])
    own_k = np.arange(PAGE_SIZE)[None, :]
    own = np.where((own_k <= row_t) & (own_k < t_new), _t5_bucket_np(row_t - own_k), -1)
    return np.concatenate([near, own], axis=1).astype(np.int32)


def kernel(x_prompt, x_sample, cache_k, cache_v, state_ret, state_ssm, state_conv, page_table, c_prompt, c_sample, rel_bias, e_norm_w, e_ada_w, e_ada_b, e_in_w, e_q_norm_w, e_k_norm_w, e_out_w, o_norm_w, o_ada_w, o_ada_b, o_in_w, o_sgu_w, o_sgu_b, o_conv_w, o_conv_b, o_dt_bias, o_A_log, o_D, o_ssm_norm_w, o_out_w):
    bp, s_len, d = x_prompt.shape
    bs, t_len, _ = x_sample.shape
    n_s = bs * t_len

    c_all = jnp.concatenate([c_prompt, c_sample, jnp.zeros((8 - (bp + bs) % 8, d), F32)], axis=0)
    mods = []
    for ada_w, ada_b in ((e_ada_w[0], e_ada_b[0]), (o_ada_w[0], o_ada_b[0])):
        mod = _ada_mod(c_all, ada_w, ada_b)
        parts_p = [mod[:bp, i * d:(i + 1) * d].reshape(bp, 1, d) for i in range(3)]
        parts_s = [jnp.repeat(mod[bp:bp + bs, i * d:(i + 1) * d], t_len, axis=0).reshape(1, n_s, d) for i in range(3)]
        mods.append((parts_p, parts_s))
    (e_mod_p, e_mod_s), (o_mod_p, o_mod_s) = mods

    seg = jnp.asarray(np.kron(np.eye(H_A), np.ones((HD_A, HD_A))), BF16)
    qnw = jnp.tile(e_q_norm_w[0], H_A).reshape(1, W_A)
    knw = jnp.tile(e_k_norm_w[0], H_A).reshape(1, W_A)
    e_in_bf = _bf(e_in_w[0])
    e_out_bf = _bf(e_out_w[0])
    o_in_bf = _bf(o_in_w[0])
    o_dt_bf = _bf(jnp.repeat(o_in_w[0][:, 3072:], P_D, axis=1))
    o_out_bf = _bf(o_out_w[0])
    x_s = x_sample.reshape(1, n_s, d)
    cos_p, sin_p = _rotary_tables(np.arange(s_len))
    cos_s, sin_s = _rotary_tables(PAST_LEN + (np.arange(n_s) % t_len))
    bias_p = _bias_tables(rel_bias, _prompt_bias_idx(), LOG2E).reshape(H_A, 2, MOBA_BLOCK, MOBA_BLOCK)
    bias_s_h = _bias_tables(rel_bias, _sample_bias_idx(t_len))
    row_h = jnp.arange(N_ROWS_S) % H_A
    bias_s = jnp.sum(jnp.where((jnp.arange(H_A)[:, None] == row_h[None, :])[:, :, None], bias_s_h, 0.0), axis=0)
    bfar = rel_bias[NUM_BUCKETS - 1, row_h].reshape(N_ROWS_S, 1)

    (qa, ka, va, kbf, vt, ga, qb, kb, vb, gb, kmean) = _even_in(
        x_prompt, e_mod_p[1], e_mod_p[0], e_norm_w[0].reshape(1, d), e_in_bf, qnw, knw, seg, cos_p, sin_p, False, BF16)
    (qa_s, ka_s, va_s, _, _, ga_s, qb_s, kb_s, vb_s, gb_s, _) = _even_in(
        x_s, e_mod_s[1], e_mod_s[0], e_norm_w[0].reshape(1, d), e_in_bf, qnw, knw, seg, cos_s, sin_s, True, F32)
    sq = lambda a: a.reshape(bs, t_len, a.shape[-1])
    qrep = jnp.repeat(sq(qa_s), H_A, axis=1)
    n_phys = cache_k.shape[1]
    page_t = lambda c: jnp.transpose(c[0], (0, 2, 3, 1)).reshape(n_phys, W_A, PAGE_SIZE)
    oa, oa_s = _moba(page_table, rel_bias, qa, kbf, vt, kmean.reshape(bp, s_len // MOBA_BLOCK, W_A), bias_p, ga,
                     qrep, sq(ka_s), sq(va_s), sq(ga_s), bias_s, bfar, page_t(cache_k), page_t(cache_v))
    nc_p = s_len // CHUNK
    ch = lambda a: a.reshape(bp, nc_p, CHUNK, a.shape[-1])
    ob, ret_p = _retention(ch(qb), ch(kb), ch(vb), ch(gb), jnp.zeros((bp, H_B * DK_B, DV_B), F32),
                           *_retention_tables(CHUNK), lq=CHUNK, nbs=bp)
    xp1 = _out_proj(oa, ob.reshape(bp, s_len, W_B), x_prompt, e_mod_p[2], e_out_bf, False)
    k_prompt = ka.reshape(1, bp, s_len, H_A, HD_A)
    v_prompt = va.reshape(1, bp, s_len, H_A, HD_A)
    ret_state_prompt = ret_p.reshape(1, bp, H_B, DK_B, DV_B)

    sc = lambda a: a.reshape(bs, 1, t_len, a.shape[-1])
    ob_s, ret_s = _retention(sc(qb_s), sc(kb_s), sc(vb_s), sc(gb_s),
                             state_ret[0].reshape(bs, H_B * DK_B, DV_B), *_retention_tables(t_len), lq=8, nbs=SEQ_PER_STEP)
    xs1 = _out_proj(oa_s.reshape(1, n_s, W_A), ob_s.reshape(1, n_s, W_B), x_s, e_mod_s[2], e_out_bf, True)
    k_sample = ka_s.reshape(1, bs, t_len, H_A, HD_A)
    v_sample = va_s.reshape(1, bs, t_len, H_A, HD_A)
    ret_state_sample = ret_s.reshape(1, bs, H_B, DK_B, DV_B)

    tri = jnp.asarray(np.tril(np.ones((CHUNK, CHUNK))), BF16)
    sel = jnp.asarray(np.kron(np.eye(H_D), np.eye(1, P_D)), BF16)
    rep = lambda a: jnp.repeat(a, P_D).reshape(1, W_D)
    dt_bias, a_log, d_skip = rep(o_dt_bias[0]), rep(o_A_log[0]), rep(o_D[0])
    ssm_nw = o_ssm_norm_w[0].reshape(1, W_D)
    conv_b = o_conv_b[0].reshape(1, CONV_DIM)
    o_nw = o_norm_w[0].reshape(1, d)

    sgu_b_p = jnp.repeat(o_sgu_b[0].T, W_C // G_C, axis=1)
    oc, zg, xbc, dtp = _odd_in(xp1, o_mod_p[1], o_mod_p[0], o_nw, o_in_bf, o_dt_bf, o_sgu_w[0], sgu_b_p, dt_bias,
                               False, BF16, False)
    yn, ssm_p = _ssd(ch(xbc), ch(dtp), ch(zg), jnp.zeros((bp, 8, CONV_DIM), F32),
                     jnp.zeros((bp, H_D * P_D, N_D), F32), o_conv_w[0], conv_b, a_log, d_skip, ssm_nw,
                     tri, sel, lq=CHUNK, nbs=bp)
    y_prompt = _out_proj(oc, yn.reshape(bp, s_len, W_D), xp1, o_mod_p[2], o_out_bf, False)
    ssm_state_prompt = ssm_p.reshape(1, bp, H_D, P_D, N_D)
    conv_state_prompt = xbc[:, -(CONV_W - 1):][None]

    per_chunk = CHUNK // t_len
    w_small = o_sgu_w[0][:, :t_len, :t_len]
    same_seq = jnp.asarray(np.kron(np.eye(per_chunk), np.ones((t_len, t_len))), F32)
    sgu_w_s = jnp.tile(w_small, (1, per_chunk, per_chunk)) * same_seq
    sgu_b_s = jnp.repeat(jnp.tile(o_sgu_b[0][:, :t_len].T, (per_chunk, 1)), W_C // G_C, axis=1)
    oc_s, zg_s, xbc_s, dt_s, v_s = _odd_in(xs1, o_mod_s[1], o_mod_s[0], o_nw, o_in_bf, o_dt_bf, sgu_w_s, sgu_b_s, dt_bias,
                                           True, F32, True)
    tail_s = jnp.concatenate([jnp.zeros((bs, 8 - (CONV_W - 1), CONV_DIM), F32), state_conv[0]], axis=1)
    yn_s, ssm_s = _ssd(sc(xbc_s), sc(dt_s), sc(zg_s), tail_s, state_ssm[0].reshape(bs, H_D * P_D, N_D),
                       o_conv_w[0], conv_b, a_log, d_skip, ssm_nw, tri, sel, lq=8, nbs=SEQ_PER_STEP)
    xs2 = _out_proj(oc_s, yn_s.reshape(1, n_s, W_D), xs1, o_mod_s[2], o_out_bf, True)
    y_sample = xs2.reshape(bs, t_len, d)
    sgu_v_sample = v_s.reshape(1, bs, t_len, W_C)
    ssm_state_sample = ssm_s.reshape(1, bs, H_D, P_D, N_D)
    xin = jnp.concatenate([state_conv[0], xbc_s.reshape(bs, t_len, CONV_DIM)], axis=1)
    conv_state_sample = xin[:, -(CONV_W - 1):][None]

    return (y_prompt, y_sample, k_prompt, v_prompt, k_sample, v_sample, ret_state_prompt, ret_state_sample,
            sgu_v_sample, ssm_state_prompt, ssm_state_sample, conv_state_prompt, conv_state_sample)
```

```python
import functools
import math

import numpy as np
import jax
import jax.numpy as jnp
from jax import lax
from jax.experimental import pallas as pl
from jax.experimental.pallas import tpu as pltpu

F32 = jnp.float32
BF16 = jnp.bfloat16

D_MODEL = 1024
PAST_LEN = 2048
PAGE_SIZE = 128
H_A, HD_A, W_A = 8, 64, 512
MOBA_BLOCK = 256
MOBA_TOPK = 3
NUM_BUCKETS = 32
MAX_DISTANCE = 128
H_B, DK_B, DV_B, W_B = 4, 64, 128, 512
G_C, W_C = 4, 512
H_D, P_D, N_D, G_D, W_D = 8, 64, 128, 2, 512
CONV_W = 4
CONV_DIM = 1024
CHUNK = 128
SEQ_PER_STEP = 8
SHORT_ROWS = 16
PROJ_ROWS = 512
OUT_ROWS = 1024
NEG_INF = -1e30
EPS = 1e-6
LOG2E = math.log2(math.e)
VMEM_LIMIT = 56 * 1024 * 1024


def _bf(x):
    return x.astype(BF16)


def _dg(a, b, ca, cb):
    return lax.dot_general(a, b, (((ca,), (cb,)), ((), ())), preferred_element_type=F32)


def _mm(a, b):
    return _dg(_bf(a), _bf(b), 1, 0)


def _mm_nt(a, b):
    return _dg(_bf(a), _bf(b), 1, 1)


def _mm_tn(a, b):
    return _dg(_bf(a), _bf(b), 0, 0)


def _split2(x):
    hi = _bf(x)
    return hi, _bf(x - hi.astype(F32))


def _split3(x):
    hi = _bf(x)
    r = x - hi.astype(F32)
    mid = _bf(r)
    return hi, mid, _bf(r - mid.astype(F32))


def _mm_hp(a, b, ca, cb):
    ah, al = _split2(a)
    bh, bl = _split2(b)
    return _dg(ah, bh, ca, cb) + (_dg(ah, bl, ca, cb) + _dg(al, bh, ca, cb))


def _mm_exact_lhs(e, x, ca, cb):
    h, m, l = _split3(x)
    return _dg(e, h, ca, cb) + (_dg(e, m, ca, cb) + _dg(e, l, ca, cb))


def _silu(x):
    return x * (1.0 / (1.0 + jnp.exp(-x)))


def _gelu_tanh(x):
    return 0.5 * x * (1.0 + jnp.tanh(math.sqrt(2.0 / math.pi) * (x + 0.044715 * (x * x * x))))


def _softplus(x):
    return jnp.maximum(x, 0.0) + jnp.log1p(jnp.exp(-jnp.abs(x)))


def _cparams(n_grid):
    return pltpu.CompilerParams(dimension_semantics=("arbitrary",) * n_grid,
                                vmem_limit_bytes=VMEM_LIMIT)


def _top3_rows(g, blk, nblk):
    sel = jnp.zeros(g.shape, jnp.bool_)
    for _ in range(MOBA_TOPK):
        m = jnp.max(g, axis=0, keepdims=True)
        idx = jnp.min(jnp.where(g == m, blk, nblk), axis=0, keepdims=True)
        pick = blk == idx
        sel = jnp.logical_or(sel, pick)
        g = jnp.where(pick, -jnp.inf, g)
    return sel


def _ada_kernel(c_ref, w_ref, b_ref, o_ref):
    s = _silu(c_ref[...])
    o_ref[...] = _mm_hp(s, w_ref[...], 1, 0) + b_ref[...]


def _ada_mod(c_all, w, b):
    m, d = c_all.shape
    n = w.shape[1]
    tn = 512
    return pl.pallas_call(
        _ada_kernel,
        grid=(n // tn,),
        in_specs=[pl.BlockSpec((m, d), lambda j: (0, 0)),
                  pl.BlockSpec((d, tn), lambda j: (0, j)),
                  pl.BlockSpec((1, tn), lambda j: (0, j))],
        out_specs=pl.BlockSpec((m, tn), lambda j: (0, j)),
        out_shape=jax.ShapeDtypeStruct((m, n), F32),
        compiler_params=_cparams(1),
        name="ada_mod",
    )(c_all, w, b.reshape(1, n))


def _t5_bucket_np(rel):
    n = np.maximum(rel, 0)
    max_exact = NUM_BUCKETS // 2
    nf = np.maximum(n, 1).astype(np.float64)
    large = max_exact + (np.log(nf / max_exact) / math.log(MAX_DISTANCE / max_exact)
                         * (NUM_BUCKETS - max_exact)).astype(np.int64)
    large = np.minimum(large, NUM_BUCKETS - 1)
    return np.where(n < max_exact, n, large).astype(np.int32)


def _bias_kernel(tab_ref, idx_ref, o_ref, *, scale):
    h = pl.program_id(0)
    idx = idx_ref[...]
    acc = jnp.zeros(idx.shape, F32)
    for b in range(NUM_BUCKETS):
        acc = jnp.where(idx == b, tab_ref[b, h], acc)
    o_ref[0] = jnp.where(idx == -1, NEG_INF, acc * scale)


def _bias_tables(rel_bias, idx, scale=1.0):
    r, c = idx.shape
    return pl.pallas_call(
        functools.partial(_bias_kernel, scale=scale),
        grid=(H_A,),
        in_specs=[pl.BlockSpec(memory_space=pltpu.SMEM),
                  pl.BlockSpec((r, c), lambda h: (0, 0))],
        out_specs=pl.BlockSpec((1, r, c), lambda h: (h, 0, 0)),
        out_shape=jax.ShapeDtypeStruct((H_A, r, c), F32),
        compiler_params=_cparams(1),
        name="t5_bias",
    )(rel_bias, jnp.asarray(idx))


def _even_in_kernel(x_ref, sc_ref, sh_ref, nw_ref, w_ref, qnw_ref, knw_ref, seg_ref, cos_ref, sin_ref,
                    qa_ref, ka_ref, va_ref, kbf_ref, vt_ref, ga_ref, qb_ref, kb_ref, vb_ref, gb_ref, km_ref):
    x = x_ref[0]
    ms = jnp.mean(x * x, axis=-1, keepdims=True)
    h = (x * lax.rsqrt(ms + EPS) * nw_ref[...]) * (1.0 + sc_ref[0]) + sh_ref[0]
    hb = _bf(h)

    def proj(lo, hi):
        return jnp.dot(hb, w_ref[:, lo:hi], preferred_element_type=F32)

    def head_rms(t, w_row):
        ss = jnp.dot(_bf(t * t), seg_ref[...], preferred_element_type=F32)
        return t * lax.rsqrt(ss * (1.0 / HD_A) + EPS) * w_row

    qa_ref[0] = head_rms(proj(0, 512), qnw_ref[...])
    ka = head_rms(proj(512, 1024), knw_ref[...])
    ka_ref[0] = ka
    kbf_ref[0] = _bf(ka)
    for j in range(ka.shape[0] // MOBA_BLOCK):
        km_ref[0, j] = jnp.mean(ka[j * MOBA_BLOCK:(j + 1) * MOBA_BLOCK], axis=0, keepdims=True)
    va = proj(1024, 1536)
    va_ref[0] = va
    vat = va.T
    ones_pad = (lax.broadcasted_iota(jnp.int32, (V_ROWS - HD_A, vat.shape[1]), 0) == 0).astype(F32)
    vt_ref[0] = _bf(jnp.concatenate(
        [piece for h in range(H_A) for piece in (vat[h * HD_A:(h + 1) * HD_A], ones_pad)], axis=0))
    ga_ref[0] = proj(1536, 2048).astype(ga_ref.dtype)

    lane = lax.broadcasted_iota(jnp.int32, (1, 256), 1) % DK_B
    first_half = lane < (DK_B // 2)
    cos = cos_ref[...]
    sin = sin_ref[...]

    def rotary(t):
        up = pltpu.roll(t, 256 - DK_B // 2, 1)
        dn = pltpu.roll(t, DK_B // 2, 1)
        return t * cos + jnp.where(first_half, up, dn) * sin

    qb_ref[0] = rotary(proj(2048, 2304)).astype(qb_ref.dtype)
    kb_ref[0] = (rotary(proj(2304, 2560)) * (DK_B ** -0.5)).astype(kb_ref.dtype)
    vb_ref[0] = proj(2560, 3072).astype(vb_ref.dtype)
    gb_ref[0] = proj(3072, 3584).astype(gb_ref.dtype)


def _even_in(x, scale, shift, norm_w, w_bf, qnw, knw, seg, cos, sin, per_row_mod, act):
    nb, s, d = x.shape
    tm = PROJ_ROWS
    ns = s // tm
    nkb = tm // MOBA_BLOCK
    if per_row_mod:
        mod_spec = pl.BlockSpec((1, tm, d), lambda b, i: (b, i, 0))
    else:
        mod_spec = pl.BlockSpec((1, 1, d), lambda b, i: (b, 0, 0))
    row = lambda c: pl.BlockSpec((1, tm, c), lambda b, i: (b, i, 0))
    const = lambda shp: pl.BlockSpec(shp, lambda b, i: (0,) * len(shp))
    out_shape = (
        jax.ShapeDtypeStruct((nb, s, 512), F32),
        jax.ShapeDtypeStruct((nb, s, 512), F32),
        jax.ShapeDtypeStruct((nb, s, 512), F32),
        jax.ShapeDtypeStruct((nb, s, 512), BF16),
        jax.ShapeDtypeStruct((nb, H_A * V_ROWS, s), BF16),
        jax.ShapeDtypeStruct((nb, s, 512), act),
        jax.ShapeDtypeStruct((nb, s, 256), act),
        jax.ShapeDtypeStruct((nb, s, 256), act),
        jax.ShapeDtypeStruct((nb, s, 512), act),
        jax.ShapeDtypeStruct((nb, s, 512), act),
        jax.ShapeDtypeStruct((nb, ns * nkb, 1, 512), F32),
    )
    out_specs = (row(512), row(512), row(512), row(512),
                 pl.BlockSpec((1, H_A * V_ROWS, tm), lambda b, i: (b, 0, i)),
                 row(512), row(256), row(256), row(512), row(512),
                 pl.BlockSpec((1, nkb, 1, 512), lambda b, i: (b, i, 0, 0)))
    return pl.pallas_call(
        _even_in_kernel,
        grid=(nb, ns),
        in_specs=[row(d), mod_spec, mod_spec, const((1, d)), const((d, 3584)),
                  const((1, 512)), const((1, 512)), const((512, 512)),
                  pl.BlockSpec((tm, 256), lambda b, i: (i, 0)),
                  pl.BlockSpec((tm, 256), lambda b, i: (i, 0))],
        out_specs=out_specs,
        out_shape=out_shape,
        compiler_params=_cparams(2),
        name="even_in",
    )(x, scale, shift, norm_w, w_bf, qnw, knw, seg, cos, sin)


MOBA_HS = 8
V_ROWS = HD_A + 8
FAR_KEYS = 2 * MOBA_BLOCK


def _moba_prompt_select(tab_ref, q_ref, km_ref, rbf_ref, rbs_ref, qs_ref):
    hg = pl.program_id(1)
    qi = pl.program_id(2)
    nblk = km_ref.shape[1]
    lane = lax.broadcasted_iota(jnp.int32, (1, 128), 1)
    blk = lax.broadcasted_iota(jnp.int32, (nblk, MOBA_BLOCK), 0)
    for hl in range(MOBA_HS):
        pr, hh = divmod(hl, 2)
        pc = slice(pr * 128, (pr + 1) * 128)
        qm = jnp.where((lane // HD_A) == hh, q_ref[0, :, pc], 0.0)
        gate = _mm_hp(km_ref[0, :, pc], qm, 1, 1)
        gate = jnp.where(blk < qi, gate, NEG_INF)
        sel = jnp.logical_and(_top3_rows(gate, blk, nblk), blk < qi)
        far_c = tab_ref[NUM_BUCKETS - 1, MOBA_HS * hg + hl] * LOG2E
        rbf_ref[hl] = jnp.where(jnp.logical_and(sel, blk < qi - 1), far_c, NEG_INF)
        rbs_ref[hl] = jnp.where(jnp.logical_or(sel, blk == qi), 0.0, NEG_INF)
        qs_ref[hl] = _bf(qm * (HD_A ** -0.5 * LOG2E))


def _moba_prompt_attend(k_ref, vt_ref, bias_ref, ga_ref, o_ref, rbf_ref, rbs_ref, qs_ref):
    qi = pl.program_id(2)

    def visit(carry, off, nkeys, extra_fn=None, block_rows=None):
        ss = []
        for hl in range(MOBA_HS):
            pr = hl // 2
            kj = k_ref[0, pl.ds(off, nkeys), pr * 128:(pr + 1) * 128]
            ss.append(_dg(kj, qs_ref[hl], 1, 1))
        stats, ps = [], []
        for hl in range(MOBA_HS):
            m = carry[hl][0]
            if block_rows is None:
                s = extra_fn(hl, ss[hl])
                mn = jnp.maximum(m, jnp.max(s, axis=0, keepdims=True))
                p = jnp.exp2(s - mn)
            else:
                halves = [ss[hl][i * MOBA_BLOCK:(i + 1) * MOBA_BLOCK] for i in range(nkeys // MOBA_BLOCK)]
                rows = block_rows(hl)
                mn = m
                for sh, r in zip(halves, rows):
                    mn = jnp.maximum(mn, jnp.max(sh, axis=0, keepdims=True) + r)
                p = jnp.concatenate([jnp.exp2(sh - (mn - r)) for sh, r in zip(halves, rows)], axis=0)
            stats.append((mn, jnp.exp2(m - mn)))
            ps.append(_bf(p))
        pvs = []
        for hl in range(MOBA_HS):
            vj = vt_ref[0, hl * V_ROWS:(hl + 1) * V_ROWS, pl.ds(off, nkeys)]
            pvs.append(jnp.dot(vj, ps[hl], preferred_element_type=F32))
        return tuple((stats[hl][0], stats[hl][1] * carry[hl][1] + pvs[hl]) for hl in range(MOBA_HS))

    def far_body(jp, carry):
        off = pl.multiple_of(jp * FAR_KEYS, FAR_KEYS)

        def rows(hl):
            return rbf_ref[hl, pl.ds(2 * jp, 1), :], rbf_ref[hl, pl.ds(2 * jp + 1, 1), :]

        return visit(carry, off, FAR_KEYS, block_rows=rows)

    init = tuple((jnp.full((1, MOBA_BLOCK), -jnp.inf, F32), jnp.zeros((V_ROWS, MOBA_BLOCK), F32))
                 for _ in range(MOBA_HS))
    carry = lax.fori_loop(0, qi // 2, far_body, init)
    js = jnp.maximum(qi - 1, 0)
    first = qi == 0
    top_tab = jnp.where(first, 0, 1)
    bot_mask = jnp.where(first, NEG_INF, 0.0)

    def near_extra(hl, s):
        top = s[:MOBA_BLOCK] + bias_ref[hl, top_tab] + rbs_ref[hl, pl.ds(js, 1), :]
        bot = s[MOBA_BLOCK:] + (bias_ref[hl, 0] + bot_mask)
        return jnp.concatenate([top, bot], axis=0)

    carry = visit(carry, pl.multiple_of(js * MOBA_BLOCK, MOBA_BLOCK), FAR_KEYS, near_extra)
    for pr in range(MOBA_HS // 2):
        accs = [carry[2 * pr + hh][1] for hh in range(2)]
        outs = [a[0:HD_A] * (1.0 / a[HD_A:HD_A + 1]) for a in accs]
        o = jnp.concatenate(outs, axis=0).T
        pc = slice(pr * 128, (pr + 1) * 128)
        o_ref[0, :, pc] = (o * _silu(ga_ref[0, :, pc].astype(F32))).astype(o_ref.dtype)


N_PAST_BLK = PAST_LEN // MOBA_BLOCK
N_PAGES = PAST_LEN // PAGE_SIZE
N_ROWS_S = 32
SEQ_PER_TILE = 2


def _page_copies(pt_ref, ck_hbm, cv_hbm, kt_buf, vt_buf, sems, seq, sl):
    cps = []
    for p in range(N_PAGES):
        dst = pl.ds(p * PAGE_SIZE, PAGE_SIZE)
        cps.append(pltpu.make_async_copy(ck_hbm.at[pt_ref[seq, p]], kt_buf.at[sl, :, dst], sems.at[0, sl]))
        cps.append(pltpu.make_async_copy(cv_hbm.at[pt_ref[seq, p]], vt_buf.at[sl, :, dst], sems.at[1, sl]))
    return cps


def _moba_sample_seqs(qrep_ref, knew_ref, vnew_ref, ga_ref, bias_ref, bfar_ref, kt_buf, vt_buf, kpad, vpad, o_ref):
    t_new = knew_ref.shape[1]
    rowh = lax.broadcasted_iota(jnp.int32, (N_ROWS_S, W_A), 0) % H_A
    laneh = lax.broadcasted_iota(jnp.int32, (N_ROWS_S, W_A), 1) // HD_A
    own_head = rowh == laneh
    blocks = [slice(n * MOBA_BLOCK, (n + 1) * MOBA_BLOCK) for n in range(N_PAST_BLK)]
    bfar = bfar_ref[...]
    scored = []
    for j in range(SEQ_PER_TILE):
        qf = jnp.where(own_head, qrep_ref[j], 0.0) * (HD_A ** -0.5)
        qbd = _bf(qf)
        q2 = jnp.concatenate([qbd, _bf(qf - qbd.astype(F32))], axis=0)
        kpad[j, 0:t_new, :] = knew_ref[j]
        vpad[j, 0:t_new, :] = vnew_ref[j]
        s_past = []
        for n in range(N_PAST_BLK):
            s2 = jnp.dot(q2, _bf(kt_buf[j, :, blocks[n]]), preferred_element_type=F32)
            s_past.append(s2[0:N_ROWS_S] + s2[N_ROWS_S:])
        s_own = _dg(qbd, _bf(kpad[j]), 1, 1) + bias_ref[:, MOBA_BLOCK:]
        scored.append((s_past, s_own))
    probs = []
    for s_past, s_own in scored:
        g = [jnp.sum(s, axis=1, keepdims=True) for s in s_past]
        sel = [jnp.zeros((N_ROWS_S, 1), jnp.bool_) for _ in range(N_PAST_BLK)]
        for _ in range(MOBA_TOPK):
            m = functools.reduce(jnp.maximum, g)
            idx = functools.reduce(jnp.minimum, [jnp.where(g[n] == m, n, N_PAST_BLK) for n in range(N_PAST_BLK)])
            for n in range(N_PAST_BLK):
                pick = idx == n
                sel[n] = jnp.logical_or(sel[n], pick)
                g[n] = jnp.where(pick, -jnp.inf, g[n])
        logits = [s_past[n] + jnp.where(sel[n], bfar, NEG_INF) for n in range(N_PAST_BLK - 1)]
        logits.append(s_past[-1] + bias_ref[:, 0:MOBA_BLOCK] + jnp.where(sel[-1], 0.0, NEG_INF))
        m = jnp.max(s_own, axis=1, keepdims=True)
        for s in logits:
            m = jnp.maximum(m, jnp.max(s, axis=1, keepdims=True))
        p_own = jnp.exp(s_own - m)
        ps = [jnp.exp(s - m) for s in logits]
        l = functools.reduce(jnp.add, [jnp.sum(p, axis=1, keepdims=True) for p in ps + [p_own]])
        probs.append(([_bf(p) for p in ps], _bf(p_own), l))
    for j, (ps, p_own, l) in enumerate(probs):
        acc = jnp.dot(p_own, _bf(vpad[j]), preferred_element_type=F32)
        for n in range(N_PAST_BLK):
            acc = acc + _dg(ps[n], _bf(vt_buf[j, :, blocks[n]]), 1, 1)
        o = jnp.where(own_head, acc * (1.0 / l), 0.0)
        o = jnp.sum(o.reshape(t_new, H_A, W_A), axis=1)
        o_ref[j] = o * _silu(ga_ref[j])


def _moba_kernel(pt_ref, tab_ref, q_ref, k_ref, vt_ref, km_ref, bias_ref, ga_ref,
                 qrep_ref, knew_ref, vnew_ref, gas_ref, bias_s_ref, bfar_ref, ck_hbm, cv_hbm,
                 o_ref, os_ref, rbf_ref, rbs_ref, qs_ref, kt_buf, vt_buf, kpad, vpad, sems):
    step = pl.program_id(0) * pl.num_programs(2) + pl.program_id(2)
    n_steps = pl.num_programs(0) * pl.num_programs(2)
    seq0 = SEQ_PER_TILE * step
    copies = functools.partial(_page_copies, pt_ref, ck_hbm, cv_hbm, kt_buf, vt_buf, sems)

    @pl.when(step == 0)
    def _():
        kpad[...] = jnp.zeros(kpad.shape, F32)
        vpad[...] = jnp.zeros(vpad.shape, F32)
        for j in range(SEQ_PER_TILE):
            for cp in copies(j, j):
                cp.start()

    for j in range(SEQ_PER_TILE):
        for cp in copies(seq0 + j, j):
            cp.wait()
    _moba_sample_seqs(qrep_ref, knew_ref, vnew_ref, gas_ref, bias_s_ref, bfar_ref, kt_buf, vt_buf, kpad, vpad, os_ref)
    _moba_prompt_select(tab_ref, q_ref, km_ref, rbf_ref, rbs_ref, qs_ref)

    @pl.when(step + 1 < n_steps)
    def _():
        for j in range(SEQ_PER_TILE):
            for cp in copies(seq0 + SEQ_PER_TILE + j, j):
                cp.start()

    _moba_prompt_attend(k_ref, vt_ref, bias_ref, ga_ref, o_ref, rbf_ref, rbs_ref, qs_ref)


def _moba(page_table, rel_bias, qa, kbf, vt, kmean, bias_t, ga,
          qrep, knew, vnew, ga_s, bias_s, bfar, cache_kt, cache_vt):
    nb, s, _ = qa.shape
    nq = s // MOBA_BLOCK
    nseq, t_new, _ = knew.shape
    assert H_A == MOBA_HS and nseq == SEQ_PER_TILE * nb * nq
    tile = pl.BlockSpec((1, MOBA_BLOCK, W_A), lambda b, hg, i, pt: (b, i, 0))
    per_b = lambda shp: pl.BlockSpec((1,) + shp, lambda b, hg, i, pt: (b, 0, 0))
    seqs = lambda r: pl.BlockSpec((SEQ_PER_TILE, r, W_A), lambda b, hg, i, pt: (b * nq + i, 0, 0))
    const = lambda a: pl.BlockSpec(a.shape, lambda b, hg, i, pt: (0,) * a.ndim)
    grid_spec = pltpu.PrefetchScalarGridSpec(
        num_scalar_prefetch=1,
        grid=(nb, 1, nq),
        in_specs=[pl.BlockSpec(memory_space=pltpu.SMEM),
                  tile, per_b((s, W_A)), per_b((H_A * V_ROWS, s)), per_b((nq, W_A)), const(bias_t), tile,
                  seqs(N_ROWS_S), seqs(t_new), seqs(t_new), seqs(t_new), const(bias_s), const(bfar),
                  pl.BlockSpec(memory_space=pl.ANY), pl.BlockSpec(memory_space=pl.ANY)],
        out_specs=(tile, seqs(t_new)),
        scratch_shapes=[pltpu.VMEM((MOBA_HS, nq, MOBA_BLOCK), F32),
                        pltpu.VMEM((MOBA_HS, nq, MOBA_BLOCK), F32),
                        pltpu.VMEM((MOBA_HS, MOBA_BLOCK, 128), BF16),
                        pltpu.VMEM((SEQ_PER_TILE, W_A, PAST_LEN), F32),
                        pltpu.VMEM((SEQ_PER_TILE, W_A, PAST_LEN), F32),
                        pltpu.VMEM((SEQ_PER_TILE, PAGE_SIZE, W_A), F32),
                        pltpu.VMEM((SEQ_PER_TILE, PAGE_SIZE, W_A), F32),
                        pltpu.SemaphoreType.DMA((2, SEQ_PER_TILE))],
    )
    return pl.pallas_call(
        _moba_kernel,
        grid_spec=grid_spec,
        out_shape=(jax.ShapeDtypeStruct((nb, s, W_A), BF16),
                   jax.ShapeDtypeStruct((nseq, t_new, W_A), F32)),
        compiler_params=_cparams(3),
        name="moba",
    )(page_table, rel_bias, qa, kbf, vt, kmean, bias_t, ga, qrep, knew, vnew, ga_s, bias_s, bfar, cache_kt, cache_vt)


def _pad_rows(ref, scratch, s, t):
    if t == CHUNK:
        return ref[s]
    scratch[s] = jnp.zeros(scratch.shape[1:], scratch.dtype)
    scratch[s, 0:t, :] = ref[s].astype(scratch.dtype)
    return scratch[s]


def _ret_kernel(q_ref, k_ref, v_ref, g_ref, st0_ref, dmat_ref, qdec_ref, kdec_ref, gl_ref,
                o_ref, st_ref, qpad, kpad, vpad, gpad, *, t, lq, nbs):
    c = pl.program_id(1)

    @pl.when(c == 0)
    def _():
        st_ref[...] = st0_ref[...]

    lane = lax.broadcasted_iota(jnp.int32, (1, 128), 1)
    rowsel = lax.broadcasted_iota(jnp.int32, (128, 1), 0) < DK_B
    work = []
    for s in range(nbs):
        q = _pad_rows(q_ref, qpad, s, t)[0:lq]
        k = _pad_rows(k_ref, kpad, s, t)
        v = _pad_rows(v_ref, vpad, s, t)
        for hp in range(H_B // 2):
            cols = slice(hp * 128, (hp + 1) * 128)
            kp = _bf(k[:, cols])
            st = st_ref[s, cols, :]
            kd = _bf(k[:, cols] * kdec_ref[0:k.shape[0], cols])
            for hh in range(2):
                h = 2 * hp + hh
                qm = jnp.where((lane // DK_B) == hh, q[:, cols], 0.0)
                vh = _bf(v[:, h * DV_B:(h + 1) * DV_B])
                sc = _dg(_bf(qm), kp, 1, 1)
                so = _mm(qm * qdec_ref[0:lq, cols], st)
                upd = _dg(kd, vh, 0, 0)
                work.append((s, hp, hh, sc, so, upd, vh, st))
    outs = {}
    for (s, hp, hh, sc, so, upd, vh, st) in work:
        h = 2 * hp + hh
        o = jnp.dot(_bf(sc * dmat_ref[h, 0:lq, 0:sc.shape[1]]), vh, preferred_element_type=F32) + so
        outs[(s, h)] = o * lax.rsqrt(jnp.mean(o * o, axis=-1, keepdims=True) + EPS)
    for i in range(0, len(work), 2):
        s, hp, _, _, _, upd0, _, st = work[i]
        cols = slice(hp * 128, (hp + 1) * 128)
        st_ref[s, cols, :] = st * gl_ref[cols, :] + jnp.where(rowsel, upd0, work[i + 1][5])
    for s in range(nbs):
        g = _pad_rows(g_ref, gpad, s, t)[0:lq].astype(F32)
        o = jnp.concatenate([outs[(s, h)] for h in range(H_B)], axis=1)
        o_ref[s] = (o * _silu(g))[0:t].astype(o_ref.dtype)


def _retention(q, k, v, g, st0, dmat, qdec, kdec, gl, lq, nbs):
    nb, nc, t, _ = q.shape
    row = lambda c_: pl.BlockSpec((nbs, None, t, c_), lambda b, c: (b, c, 0, 0))
    const = lambda a: pl.BlockSpec(a.shape, lambda b, c: (0,) * a.ndim)
    st_spec = pl.BlockSpec((nbs, H_B * DK_B, DV_B), lambda b, c: (b, 0, 0))
    pad = lambda c_: pltpu.VMEM((nbs, t if t == CHUNK else SHORT_ROWS, c_), F32)
    return pl.pallas_call(
        functools.partial(_ret_kernel, t=t, lq=lq, nbs=nbs),
        grid=(nb // nbs, nc),
        in_specs=[row(256), row(256), row(512), row(512), st_spec,
                  const(dmat), const(qdec), const(kdec), const(gl)],
        out_specs=(row(512), st_spec),
        out_shape=(jax.ShapeDtypeStruct((nb, nc, t, 512), q.dtype),
                   jax.ShapeDtypeStruct((nb, H_B * DK_B, DV_B), F32)),
        scratch_shapes=[pad(256), pad(256), pad(512), pad(512)],
        compiler_params=_cparams(2),
        name="retention",
    )(q, k, v, g, st0, dmat, qdec, kdec, gl)


def _out_kernel(a_ref, b_ref, x_ref, g_ref, w_ref, o_ref):
    half = w_ref.shape[0] // 2
    y = (jnp.dot(_bf(a_ref[0]), w_ref[0:half, :], preferred_element_type=F32)
         + jnp.dot(_bf(b_ref[0]), w_ref[half:, :], preferred_element_type=F32))
    o_ref[0] = x_ref[0] + g_ref[0] * y


def _out_proj(a, b, x, gate, w_bf, per_row_mod):
    nb, s, d = x.shape
    tm = min(OUT_ROWS, s)
    if per_row_mod:
        g_spec = pl.BlockSpec((1, tm, d), lambda bb, i: (bb, i, 0))
    else:
        g_spec = pl.BlockSpec((1, 1, d), lambda bb, i: (bb, 0, 0))
    row = lambda c: pl.BlockSpec((1, tm, c), lambda bb, i: (bb, i, 0))
    return pl.pallas_call(
        _out_kernel,
        grid=(nb, s // tm),
        in_specs=[row(512), row(512), row(d), g_spec, pl.BlockSpec(w_bf.shape, lambda bb, i: (0, 0))],
        out_specs=row(d),
        out_shape=jax.ShapeDtypeStruct((nb, s, d), F32),
        compiler_params=_cparams(2),
        name="out_proj",
    )(a, b, x, gate, w_bf)


def _odd_in_kernel(x_ref, sc_ref, sh_ref, nw_ref, w_ref, wdt_ref, sguw_ref, sgub_ref, dtb_ref,
                   oc_ref, zg_ref, xbc_ref, dt_ref, *maybe_v_ref):
    x = x_ref[0]
    tm = x.shape[0]
    ms = jnp.mean(x * x, axis=-1, keepdims=True)
    h = (x * lax.rsqrt(ms + EPS) * nw_ref[...]) * (1.0 + sc_ref[0]) + sh_ref[0]
    hb = _bf(h)

    def proj(lo, hi):
        return jnp.dot(hb, w_ref[:, lo:hi], preferred_element_type=F32)

    u = _gelu_tanh(proj(0, 512))
    v = _gelu_tanh(proj(512, 1024))
    mu = jnp.mean(v, axis=-1, keepdims=True)
    vc = v - mu
    v = vc * lax.rsqrt(jnp.mean(vc * vc, axis=-1, keepdims=True) + EPS)
    for v_ref in maybe_v_ref:
        v_ref[0] = v
    ii = lax.broadcasted_iota(jnp.int32, (CHUNK, CHUNK), 0)
    jj = lax.broadcasted_iota(jnp.int32, (CHUNK, CHUNK), 1)
    rows = []
    for ci in range(tm // CHUNK):
        cols = []
        for g in range(G_C):
            wg = jnp.where(ii >= jj, sguw_ref[g], 0.0)
            cols.append(_mm(wg, v[ci * CHUNK:(ci + 1) * CHUNK, g * 128:(g + 1) * 128]))
        rows.append(jnp.concatenate(cols, axis=1) + sgub_ref[...])
    sg = jnp.concatenate(rows, axis=0) if len(rows) > 1 else rows[0]
    oc_ref[0] = (u * sg * _silu(proj(1024, 1536))).astype(oc_ref.dtype)
    zg_ref[0] = proj(1536, 2048).astype(zg_ref.dtype)
    xbc_ref[0] = proj(2048, 3072)
    dt_ref[0] = _softplus(jnp.dot(hb, wdt_ref[...], preferred_element_type=F32) + dtb_ref[...])


def _odd_in(x, scale, shift, norm_w, w_bf, w_dt, sgu_w, sgu_b_tab, dt_bias, per_row_mod, act, emit_v):
    nb, s, d = x.shape
    tm = PROJ_ROWS
    if per_row_mod:
        mod_spec = pl.BlockSpec((1, tm, d), lambda b, i: (b, i, 0))
    else:
        mod_spec = pl.BlockSpec((1, 1, d), lambda b, i: (b, 0, 0))
    row = lambda c: pl.BlockSpec((1, tm, c), lambda b, i: (b, i, 0))
    const = lambda shp: pl.BlockSpec(shp, lambda b, i: (0,) * len(shp))
    return pl.pallas_call(
        _odd_in_kernel,
        grid=(nb, s // tm),
        in_specs=[row(d), mod_spec, mod_spec, const((1, d)), const((d, 3072)), const(w_dt.shape),
                  const(sgu_w.shape), const(sgu_b_tab.shape), const((1, 512))],
        out_specs=(row(512), row(512), row(1024), row(512)) + ((row(512),) if emit_v else ()),
        out_shape=(jax.ShapeDtypeStruct((nb, s, 512), act),
                   jax.ShapeDtypeStruct((nb, s, 512), act),
                   jax.ShapeDtypeStruct((nb, s, 1024), F32),
                   jax.ShapeDtypeStruct((nb, s, 512), F32),
                   ) + ((jax.ShapeDtypeStruct((nb, s, 512), F32),) if emit_v else ()),
        compiler_params=_cparams(2),
        name="odd_in",
    )(x, scale, shift, norm_w, w_bf, w_dt, sgu_w, sgu_b_tab, dt_bias)


def _ssd_kernel(xbc_ref, dt_ref, zg_ref, tail_ref, st0_ref, cw_ref, cb_ref, alog_ref, dsk_ref, nw_ref,
                tri_ref, sel_ref, y_ref, st_ref, ext, dtpad, zpad, *, t, lq, nc, nbs):
    rows = dtpad.shape[1]
    c = pl.program_id(1)

    @pl.when(c == 0)
    def _():
        st_ref[...] = st0_ref[...]
        ext[...] = jnp.zeros(ext.shape, F32)
        ext[:, 0:8, :] = tail_ref[...]

    ii = lax.broadcasted_iota(jnp.int32, (lq, rows), 0)
    jj = lax.broadcasted_iota(jnp.int32, (lq, rows), 1)
    lane = lax.broadcasted_iota(jnp.int32, (1, 128), 1)
    hpg = H_D // G_D
    neg_a = -jnp.exp(alog_ref[...])
    seqs = []
    for s in range(nbs):
        ext[s, 8:8 + t, :] = xbc_ref[s]
        conv = cb_ref[...]
        for w in range(CONV_W):
            conv = conv + ext[s, pl.ds(8 - (CONV_W - 1) + w, rows), :] * cw_ref[w:w + 1, :]
        if nc > 1:
            ext[s, 0:8, :] = ext[s, rows:rows + 8, :]
        xc = _silu(conv)
        dt = _pad_rows(dt_ref, dtpad, s, t)
        cum = _mm_exact_lhs(tri_ref[0:rows, 0:rows], dt * neg_a, 1, 0)
        seqs.append((xc, dt, cum))
    st1 = []
    for s in range(nbs):
        xc, dt, cum = seqs[s]
        xh = xc[:, 0:W_D]
        last = cum[rows - 1:rows, :]
        dtx = _bf(xh * dt)
        xw = _bf(xh * (jnp.exp(last - cum) * dt))
        cum_rows = _mm_exact_lhs(sel_ref[...], cum, 1, 1)
        per_g = []
        for g in range(G_D):
            bg = _bf(xc[:, W_D + g * N_D:W_D + (g + 1) * N_D])
            cg = _bf(xc[0:lq, W_D + G_D * N_D + g * N_D:W_D + G_D * N_D + (g + 1) * N_D])
            gr = slice(g * hpg * P_D, (g + 1) * hpg * P_D)
            cb = _dg(cg, bg, 1, 1)
            yoff = _dg(cg, _bf(st_ref[s, gr, :]), 1, 1)
            upd = _dg(xw[:, gr], bg, 0, 0)
            per_g.append((cb, yoff, upd))
        st1.append((dtx, cum_rows, per_g))
    for s in range(nbs):
        xc, dt, cum = seqs[s]
        dtx, cum_rows, per_g = st1[s]
        ecum = jnp.exp(cum[0:lq])
        elast = jnp.exp(cum[rows - 1:rows, :])
        ys = []
        for g in range(G_D):
            cb, yoff, upd = per_g[g]
            for pr in range(hpg // 2):
                l0 = g * hpg * P_D + pr * 128
                yh = []
                for hh in range(2):
                    h = g * hpg + pr * 2 + hh
                    col = jnp.broadcast_to(cum[0:lq, h * P_D:h * P_D + 1], (lq, rows))
                    seg = jnp.minimum(col - cum_rows[h:h + 1, :], 0.0)
                    mh = jnp.where(ii >= jj, cb * jnp.exp(seg), 0.0)
                    yh.append(jnp.dot(_bf(mh), dtx[:, l0:l0 + 128], preferred_element_type=F32))
                ypair = jnp.where(lane < P_D, yh[0], yh[1])
                ys.append(ypair + yoff[:, pr * 128:(pr + 1) * 128] * ecum[:, l0:l0 + 128])
            for hl in range(hpg):
                h = g * hpg + hl
                r = slice(h * P_D, (h + 1) * P_D)
                dec = jnp.broadcast_to(elast[0:1, h * P_D:h * P_D + 1], (P_D, N_D))
                st_ref[s, r, :] = st_ref[s, r, :] * dec + upd[hl * P_D:(hl + 1) * P_D, :]
        y = jnp.concatenate(ys, axis=1)
        zg = _pad_rows(zg_ref, zpad, s, t)[0:lq].astype(F32)
        y = (y + xc[0:lq, 0:W_D] * dsk_ref[...]) * _silu(zg)
        gw = W_D // G_D
        outs = []
        for g in range(G_D):
            yg = y[:, g * gw:(g + 1) * gw]
            outs.append(yg * lax.rsqrt(jnp.mean(yg * yg, axis=-1, keepdims=True) + EPS))
        y_ref[s] = (jnp.concatenate(outs, axis=1) * nw_ref[...])[0:t].astype(y_ref.dtype)


def _ssd(xbc, dt, zg, tail, st0, conv_w, conv_b, a_log, d_skip, norm_w, tri, sel, lq, nbs):
    nb, nc, t, _ = xbc.shape
    rows = t if t == CHUNK else SHORT_ROWS
    row = lambda c_: pl.BlockSpec((nbs, None, t, c_), lambda b, c: (b, c, 0, 0))
    const = lambda a: pl.BlockSpec(a.shape, lambda b, c: (0,) * a.ndim)
    st_spec = pl.BlockSpec((nbs, H_D * P_D, N_D), lambda b, c: (b, 0, 0))
    return pl.pallas_call(
        functools.partial(_ssd_kernel, t=t, lq=lq, nc=nc, nbs=nbs),
        grid=(nb // nbs, nc),
        in_specs=[row(1024), row(512), row(512),
                  pl.BlockSpec((nbs, 8, CONV_DIM), lambda b, c: (b, 0, 0)), st_spec,
                  const(conv_w), const(conv_b), const(a_log), const(d_skip), const(norm_w),
                  const(tri), const(sel)],
        out_specs=(row(512), st_spec),
        out_shape=(jax.ShapeDtypeStruct((nb, nc, t, 512), zg.dtype),
                   jax.ShapeDtypeStruct((nb, H_D * P_D, N_D), F32)),
        scratch_shapes=[pltpu.VMEM((nbs, rows + 8, CONV_DIM), F32),
                        pltpu.VMEM((nbs, rows, 512), F32), pltpu.VMEM((nbs, rows, 512), F32)],
        compiler_params=_cparams(2),
        name="ssd",
    )(xbc, dt, zg, tail, st0, conv_w, conv_b, a_log, d_skip, norm_w, tri, sel)


def _rotary_tables(pos):
    half = DK_B // 2
    inv = (np.float32(1.0) / np.float32(10000.0) ** (np.arange(half, dtype=np.float32) / np.float32(half)))
    ang = (np.asarray(pos).astype(np.float32)[:, None] * inv.astype(np.float32)[None, :]).astype(np.float64)
    cos, sin = np.cos(ang), np.sin(ang)
    cos_t = np.tile(np.concatenate([cos, cos], axis=1), (1, H_B))
    sin_t = np.tile(np.concatenate([-sin, sin], axis=1), (1, H_B))
    return jnp.asarray(cos_t, F32), jnp.asarray(sin_t, F32)


def _retention_tables(chunk_len):
    log_g = np.log(1.0 - 2.0 ** (-5.0 - np.arange(H_B, dtype=np.float64)))
    idx = np.arange(CHUNK, dtype=np.float64)
    diff = idx[:, None] - idx[None, :]
    dmat = np.where(diff[None] >= 0, np.exp(np.maximum(diff, 0.0)[None] * log_g[:, None, None]), 0.0)
    qdec = np.exp((idx + 1.0)[:, None] * log_g[None, :])
    kdec = np.where(idx[:, None] < chunk_len, np.exp((chunk_len - 1.0 - idx)[:, None] * log_g[None, :]), 0.0)
    gl = np.exp(chunk_len * log_g)
    return (jnp.asarray(dmat, F32),
            jnp.asarray(np.repeat(qdec, DK_B, axis=1), F32),
            jnp.asarray(np.repeat(kdec, DK_B, axis=1), F32),
            jnp.asarray(np.repeat(np.repeat(gl, DK_B)[:, None], DV_B, axis=1), F32))


def _prompt_bias_idx():
    kk = np.arange(MOBA_BLOCK)[:, None]
    qq = np.arange(MOBA_BLOCK)[None, :]
    diag = np.where(qq >= kk, _t5_bucket_np(qq - kk), -1)
    sub = _t5_bucket_np(qq + MOBA_BLOCK - kk)
    return np.concatenate([diag, sub], axis=0).astype(np.int32)


def _sample_bias_idx(t_new):
    row_t = (np.arange(N_ROWS_S) // H_A)[:, None]
    qpos = PAST_LEN + row_t
    near = _t5_bucket_np(qpos - (PAST_LEN - MOBA_BLOCK + np.arange(MOBA_BLOCK))[None, :])
    own_k = np.arange(PAGE_SIZE)[None, :]
    own = np.where((own_k <= row_t) & (own_k < t_new), _t5_bucket_np(row_t - own_k), -1)
    return np.concatenate([near, own], axis=1).astype(np.int32)


def kernel(x_prompt, x_sample, cache_k, cache_v, state_ret, state_ssm, state_conv, page_table, c_prompt, c_sample, rel_bias, e_norm_w, e_ada_w, e_ada_b, e_in_w, e_q_norm_w, e_k_norm_w, e_out_w, o_norm_w, o_ada_w, o_ada_b, o_in_w, o_sgu_w, o_sgu_b, o_conv_w, o_conv_b, o_dt_bias, o_A_log, o_D, o_ssm_norm_w, o_out_w):
    bp, s_len, d = x_prompt.shape
    bs, t_len, _ = x_sample.shape
    n_s = bs * t_len

    c_all = jnp.concatenate([c_prompt, c_sample, jnp.zeros((8 - (bp + bs) % 8, d), F32)], axis=0)
    mods = []
    for ada_w, ada_b in ((e_ada_w[0], e_ada_b[0]), (o_ada_w[0], o_ada_b[0])):
        mod = _ada_mod(c_all, ada_w, ada_b)
        parts_p = [mod[:bp, i * d:(i + 1) * d].reshape(bp, 1, d) for i in range(3)]
        parts_s = [jnp.repeat(mod[bp:bp + bs, i * d:(i + 1) * d], t_len, axis=0).reshape(1, n_s, d) for i in range(3)]
        mods.append((parts_p, parts_s))
    (e_mod_p, e_mod_s), (o_mod_p, o_mod_s) = mods

    seg = jnp.asarray(np.kron(np.eye(H_A), np.ones((HD_A, HD_A))), BF16)
    qnw = jnp.tile(e_q_norm_w[0], H_A).reshape(1, W_A)
    knw = jnp.tile(e_k_norm_w[0], H_A).reshape(1, W_A)
    e_in_bf = _bf(e_in_w[0])
    e_out_bf = _bf(e_out_w[0])
    o_in_bf = _bf(o_in_w[0])
    o_dt_bf = _bf(jnp.repeat(o_in_w[0][:, 3072:], P_D, axis=1))
    o_out_bf = _bf(o_out_w[0])
    x_s = x_sample.reshape(1, n_s, d)
    cos_p, sin_p = _rotary_tables(np.arange(s_len))
    cos_s, sin_s = _rotary_tables(PAST_LEN + (np.arange(n_s) % t_len))
    bias_p = _bias_tables(rel_bias, _prompt_bias_idx(), LOG2E).reshape(H_A, 2, MOBA_BLOCK, MOBA_BLOCK)
    bias_s_h = _bias_tables(rel_bias, _sample_bias_idx(t_len))
    row_h = jnp.arange(N_ROWS_S) % H_A
    bias_s = jnp.sum(jnp.where((jnp.arange(H_A)[:, None] == row_h[None, :])[:, :, None], bias_s_h, 0.0), axis=0)
    bfar = rel_bias[NUM_BUCKETS - 1, row_h].reshape(N_ROWS_S, 1)

    (qa, ka, va, kbf, vt, ga, qb, kb, vb, gb, kmean) = _even_in(
        x_prompt, e_mod_p[1], e_mod_p[0], e_norm_w[0].reshape(1, d), e_in_bf, qnw, knw, seg, cos_p, sin_p, False, BF16)
    (qa_s, ka_s, va_s, _, _, ga_s, qb_s, kb_s, vb_s, gb_s, _) = _even_in(
        x_s, e_mod_s[1], e_mod_s[0], e_norm_w[0].reshape(1, d), e_in_bf, qnw, knw, seg, cos_s, sin_s, True, F32)
    sq = lambda a: a.reshape(bs, t_len, a.shape[-1])
    qrep = jnp.repeat(sq(qa_s), H_A, axis=1)
    n_phys = cache_k.shape[1]
    page_t = lambda c: jnp.transpose(c[0], (0, 2, 3, 1)).reshape(n_phys, W_A, PAGE_SIZE)
    oa, oa_s = _moba(page_table, rel_bias, qa, kbf, vt, kmean.reshape(bp, s_len // MOBA_BLOCK, W_A), bias_p, ga,
                     qrep, sq(ka_s), sq(va_s), sq(ga_s), bias_s, bfar, page_t(cache_k), page_t(cache_v))
    nc_p = s_len // CHUNK
    ch = lambda a: a.reshape(bp, nc_p, CHUNK, a.shape[-1])
    ob, ret_p = _retention(ch(qb), ch(kb), ch(vb), ch(gb), jnp.zeros((bp, H_B * DK_B, DV_B), F32),
                           *_retention_tables(CHUNK), lq=CHUNK, nbs=bp)
    xp1 = _out_proj(oa, ob.reshape(bp, s_len, W_B), x_prompt, e_mod_p[2], e_out_bf, False)
    k_prompt = ka.reshape(1, bp, s_len, H_A, HD_A)
    v_prompt = va.reshape(1, bp, s_len, H_A, HD_A)
    ret_state_prompt = ret_p.reshape(1, bp, H_B, DK_B, DV_B)

    sc = lambda a: a.reshape(bs, 1, t_len, a.shape[-1])
    ob_s, ret_s = _retention(sc(qb_s), sc(kb_s), sc(vb_s), sc(gb_s),
                             state_ret[0].reshape(bs, H_B * DK_B, DV_B), *_retention_tables(t_len), lq=8, nbs=SEQ_PER_STEP)
    xs1 = _out_proj(oa_s.reshape(1, n_s, W_A), ob_s.reshape(1, n_s, W_B), x_s, e_mod_s[2], e_out_bf, True)
    k_sample = ka_s.reshape(1, bs, t_len, H_A, HD_A)
    v_sample = va_s.reshape(1, bs, t_len, H_A, HD_A)
    ret_state_sample = ret_s.reshape(1, bs, H_B, DK_B, DV_B)

    tri = jnp.asarray(np.tril(np.ones((CHUNK, CHUNK))), BF16)
    sel = jnp.asarray(np.kron(np.eye(H_D), np.eye(1, P_D)), BF16)
    rep = lambda a: jnp.repeat(a, P_D).reshape(1, W_D)
    dt_bias, a_log, d_skip = rep(o_dt_bias[0]), rep(o_A_log[0]), rep(o_D[0])
    ssm_nw = o_ssm_norm_w[0].reshape(1, W_D)
    conv_b = o_conv_b[0].reshape(1, CONV_DIM)
    o_nw = o_norm_w[0].reshape(1, d)

    sgu_b_p = jnp.repeat(o_sgu_b[0].T, W_C // G_C, axis=1)
    oc, zg, xbc, dtp = _odd_in(xp1, o_mod_p[1], o_mod_p[0], o_nw, o_in_bf, o_dt_bf, o_sgu_w[0], sgu_b_p, dt_bias,
                               False, BF16, False)
    yn, ssm_p = _ssd(ch(xbc), ch(dtp), ch(zg), jnp.zeros((bp, 8, CONV_DIM), F32),
                     jnp.zeros((bp, H_D * P_D, N_D), F32), o_conv_w[0], conv_b, a_log, d_skip, ssm_nw,
                     tri, sel, lq=CHUNK, nbs=bp)
    y_prompt = _out_proj(oc, yn.reshape(bp, s_len, W_D), xp1, o_mod_p[2], o_out_bf, False)
    ssm_state_prompt = ssm_p.reshape(1, bp, H_D, P_D, N_D)
    conv_state_prompt = xbc[:, -(CONV_W - 1):][None]

    per_chunk = CHUNK // t_len
    w_small = o_sgu_w[0][:, :t_len, :t_len]
    same_seq = jnp.asarray(np.kron(np.eye(per_chunk), np.ones((t_len, t_len))), F32)
    sgu_w_s = jnp.tile(w_small, (1, per_chunk, per_chunk)) * same_seq
    sgu_b_s = jnp.repeat(jnp.tile(o_sgu_b[0][:, :t_len].T, (per_chunk, 1)), W_C // G_C, axis=1)
    oc_s, zg_s, xbc_s, dt_s, v_s = _odd_in(xs1, o_mod_s[1], o_mod_s[0], o_nw, o_in_bf, o_dt_bf, sgu_w_s, sgu_b_s, dt_bias,
                                           True, F32, True)
    tail_s = jnp.concatenate([jnp.zeros((bs, 8 - (CONV_W - 1), CONV_DIM), F32), state_conv[0]], axis=1)
    yn_s, ssm_s = _ssd(sc(xbc_s), sc(dt_s), sc(zg_s), tail_s, state_ssm[0].reshape(bs, H_D * P_D, N_D),
                       o_conv_w[0], conv_b, a_log, d_skip, ssm_nw, tri, sel, lq=8, nbs=SEQ_PER_STEP)
    xs2 = _out_proj(oc_s, yn_s.reshape(1, n_s, W_D), xs1, o_mod_s[2], o_out_bf, True)
    y_sample = xs2.reshape(bs, t_len, d)
    sgu_v_sample = v_s.reshape(1, bs, t_len, W_C)
    ssm_state_sample = ssm_s.reshape(1, bs, H_D, P_D, N_D)
    xin = jnp.concatenate([state_conv[0], xbc_s.reshape(bs, t_len, CONV_DIM)], axis=1)
    conv_state_sample = xin[:, -(CONV_W - 1):][None]

    return (y_prompt, y_sample, k_prompt, v_prompt, k_sample, v_sample, ret_state_prompt, ret_state_sample,
            sgu_v_sample, ssm_state_prompt, ssm_state_sample, conv_state_prompt, conv_state_sample)
```

```python
import functools
import math

import numpy as np
import jax
import jax.numpy as jnp
from jax import lax
from jax.experimental import pallas as pl
from jax.experimental.pallas import tpu as pltpu

F32 = jnp.float32
BF16 = jnp.bfloat16

D_MODEL = 1024
PAST_LEN = 2048
PAGE_SIZE = 128
H_A, HD_A, W_A = 8, 64, 512
MOBA_BLOCK = 256
MOBA_TOPK = 3
NUM_BUCKETS = 32
MAX_DISTANCE = 128
H_B, DK_B, DV_B, W_B = 4, 64, 128, 512
G_C, W_C = 4, 512
H_D, P_D, N_D, G_D, W_D = 8, 64, 128, 2, 512
CONV_W = 4
CONV_DIM = 1024
CHUNK = 128
SEQ_PER_STEP = 8
SHORT_ROWS = 16
PROJ_ROWS = 512
OUT_ROWS = 1024
NEG_INF = -1e30
EPS = 1e-6
LOG2E = math.log2(math.e)
VMEM_LIMIT = 56 * 1024 * 1024


def _bf(x):
    return x.astype(BF16)


def _dg(a, b, ca, cb):
    return lax.dot_general(a, b, (((ca,), (cb,)), ((), ())), preferred_element_type=F32)


def _mm(a, b):
    return _dg(_bf(a), _bf(b), 1, 0)


def _mm_nt(a, b):
    return _dg(_bf(a), _bf(b), 1, 1)


def _mm_tn(a, b):
    return _dg(_bf(a), _bf(b), 0, 0)


def _split2(x):
    hi = _bf(x)
    return hi, _bf(x - hi.astype(F32))


def _split3(x):
    hi = _bf(x)
    r = x - hi.astype(F32)
    mid = _bf(r)
    return hi, mid, _bf(r - mid.astype(F32))


def _mm_hp(a, b, ca, cb):
    ah, al = _split2(a)
    bh, bl = _split2(b)
    return _dg(ah, bh, ca, cb) + (_dg(ah, bl, ca, cb) + _dg(al, bh, ca, cb))


def _mm_exact_lhs(e, x, ca, cb):
    h, m, l = _split3(x)
    return _dg(e, h, ca, cb) + (_dg(e, m, ca, cb) + _dg(e, l, ca, cb))


def _silu(x):
    return x * (1.0 / (1.0 + jnp.exp(-x)))


def _gelu_tanh(x):
    return 0.5 * x * (1.0 + jnp.tanh(math.sqrt(2.0 / math.pi) * (x + 0.044715 * (x * x * x))))


def _softplus(x):
    return jnp.maximum(x, 0.0) + jnp.log1p(jnp.exp(-jnp.abs(x)))


def _cparams(n_grid):
    return pltpu.CompilerParams(dimension_semantics=("arbitrary",) * n_grid,
                                vmem_limit_bytes=VMEM_LIMIT)


def _top3_rows(g, blk, nblk):
    sel = jnp.zeros(g.shape, jnp.bool_)
    for _ in range(MOBA_TOPK):
        m = jnp.max(g, axis=0, keepdims=True)
        idx = jnp.min(jnp.where(g == m, blk, nblk), axis=0, keepdims=True)
        pick = blk == idx
        sel = jnp.logical_or(sel, pick)
        g = jnp.where(pick, -jnp.inf, g)
    return sel


def _ada_kernel(c_ref, w_ref, b_ref, o_ref):
    s = _silu(c_ref[...])
    o_ref[...] = _mm_hp(s, w_ref[...], 1, 0) + b_ref[...]


def _ada_mod(c_all, w, b):
    m, d = c_all.shape
    n = w.shape[1]
    tn = 512
    return pl.pallas_call(
        _ada_kernel,
        grid=(n // tn,),
        in_specs=[pl.BlockSpec((m, d), lambda j: (0, 0)),
                  pl.BlockSpec((d, tn), lambda j: (0, j)),
                  pl.BlockSpec((1, tn), lambda j: (0, j))],
        out_specs=pl.BlockSpec((m, tn), lambda j: (0, j)),
        out_shape=jax.ShapeDtypeStruct((m, n), F32),
        compiler_params=_cparams(1),
        name="ada_mod",
    )(c_all, w, b.reshape(1, n))


def _t5_bucket_np(rel):
    n = np.maximum(rel, 0)
    max_exact = NUM_BUCKETS // 2
    nf = np.maximum(n, 1).astype(np.float64)
    large = max_exact + (np.log(nf / max_exact) / math.log(MAX_DISTANCE / max_exact)
                         * (NUM_BUCKETS - max_exact)).astype(np.int64)
    large = np.minimum(large, NUM_BUCKETS - 1)
    return np.where(n < max_exact, n, large).astype(np.int32)


def _bias_kernel(tab_ref, idx_ref, o_ref, *, scale):
    h = pl.program_id(0)
    idx = idx_ref[...]
    acc = jnp.zeros(idx.shape, F32)
    for b in range(NUM_BUCKETS):
        acc = jnp.where(idx == b, tab_ref[b, h], acc)
    o_ref[0] = jnp.where(idx == -1, NEG_INF, acc * scale)


def _bias_tables(rel_bias, idx, scale=1.0):
    r, c = idx.shape
    return pl.pallas_call(
        functools.partial(_bias_kernel, scale=scale),
        grid=(H_A,),
        in_specs=[pl.BlockSpec(memory_space=pltpu.SMEM),
                  pl.BlockSpec((r, c), lambda h: (0, 0))],
        out_specs=pl.BlockSpec((1, r, c), lambda h: (h, 0, 0)),
        out_shape=jax.ShapeDtypeStruct((H_A, r, c), F32),
        compiler_params=_cparams(1),
        name="t5_bias",
    )(rel_bias, jnp.asarray(idx))


def _even_in_kernel(x_ref, sc_ref, sh_ref, nw_ref, w_ref, qnw_ref, knw_ref, seg_ref, cos_ref, sin_ref,
                    qa_ref, ka_ref, va_ref, kbf_ref, vt_ref, ga_ref, qb_ref, kb_ref, vb_ref, gb_ref, km_ref):
    x = x_ref[0]
    ms = jnp.mean(x * x, axis=-1, keepdims=True)
    h = (x * lax.rsqrt(ms + EPS) * nw_ref[...]) * (1.0 + sc_ref[0]) + sh_ref[0]
    hb = _bf(h)

    def proj(lo, hi):
        return jnp.dot(hb, w_ref[:, lo:hi], preferred_element_type=F32)

    def head_rms(t, w_row):
        ss = jnp.dot(_bf(t * t), seg_ref[...], preferred_element_type=F32)
        return t * lax.rsqrt(ss * (1.0 / HD_A) + EPS) * w_row

    qa_ref[0] = head_rms(proj(0, 512), qnw_ref[...])
    ka = head_rms(proj(512, 1024), knw_ref[...])
    ka_ref[0] = ka
    kbf_ref[0] = _bf(ka)
    for j in range(ka.shape[0] // MOBA_BLOCK):
        km_ref[0, j] = jnp.mean(ka[j * MOBA_BLOCK:(j + 1) * MOBA_BLOCK], axis=0, keepdims=True)
    va = proj(1024, 1536)
    va_ref[0] = va
    vat = va.T
    ones_pad = (lax.broadcasted_iota(jnp.int32, (V_ROWS - HD_A, vat.shape[1]), 0) == 0).astype(F32)
    vt_ref[0] = _bf(jnp.concatenate(
        [piece for h in range(H_A) for piece in (vat[h * HD_A:(h + 1) * HD_A], ones_pad)], axis=0))
    ga_ref[0] = proj(1536, 2048).astype(ga_ref.dtype)

    lane = lax.broadcasted_iota(jnp.int32, (1, 256), 1) % DK_B
    first_half = lane < (DK_B // 2)
    cos = cos_ref[...]
    sin = sin_ref[...]

    def rotary(t):
        up = pltpu.roll(t, 256 - DK_B // 2, 1)
        dn = pltpu.roll(t, DK_B // 2, 1)
        return t * cos + jnp.where(first_half, up, dn) * sin

    qb_ref[0] = rotary(proj(2048, 2304)).astype(qb_ref.dtype)
    kb_ref[0] = (rotary(proj(2304, 2560)) * (DK_B ** -0.5)).astype(kb_ref.dtype)
    vb_ref[0] = proj(2560, 3072).astype(vb_ref.dtype)
    gb_ref[0] = proj(3072, 3584).astype(gb_ref.dtype)


def _even_in(x, scale, shift, norm_w, w_bf, qnw, knw, seg, cos, sin, per_row_mod, act):
    nb, s, d = x.shape
    tm = PROJ_ROWS
    ns = s // tm
    nkb = tm // MOBA_BLOCK
    if per_row_mod:
        mod_spec = pl.BlockSpec((1, tm, d), lambda b, i: (b, i, 0))
    else:
        mod_spec = pl.BlockSpec((1, 1, d), lambda b, i: (b, 0, 0))
    row = lambda c: pl.BlockSpec((1, tm, c), lambda b, i: (b, i, 0))
    const = lambda shp: pl.BlockSpec(shp, lambda b, i: (0,) * len(shp))
    out_shape = (
        jax.ShapeDtypeStruct((nb, s, 512), F32),
        jax.ShapeDtypeStruct((nb, s, 512), F32),
        jax.ShapeDtypeStruct((nb, s, 512), F32),
        jax.ShapeDtypeStruct((nb, s, 512), BF16),
        jax.ShapeDtypeStruct((nb, H_A * V_ROWS, s), BF16),
        jax.ShapeDtypeStruct((nb, s, 512), act),
        jax.ShapeDtypeStruct((nb, s, 256), act),
        jax.ShapeDtypeStruct((nb, s, 256), act),
        jax.ShapeDtypeStruct((nb, s, 512), act),
        jax.ShapeDtypeStruct((nb, s, 512), act),
        jax.ShapeDtypeStruct((nb, ns * nkb, 1, 512), F32),
    )
    out_specs = (row(512), row(512), row(512), row(512),
                 pl.BlockSpec((1, H_A * V_ROWS, tm), lambda b, i: (b, 0, i)),
                 row(512), row(256), row(256), row(512), row(512),
                 pl.BlockSpec((1, nkb, 1, 512), lambda b, i: (b, i, 0, 0)))
    return pl.pallas_call(
        _even_in_kernel,
        grid=(nb, ns),
        in_specs=[row(d), mod_spec, mod_spec, const((1, d)), const((d, 3584)),
                  const((1, 512)), const((1, 512)), const((512, 512)),
                  pl.BlockSpec((tm, 256), lambda b, i: (i, 0)),
                  pl.BlockSpec((tm, 256), lambda b, i: (i, 0))],
        out_specs=out_specs,
        out_shape=out_shape,
        compiler_params=_cparams(2),
        name="even_in",
    )(x, scale, shift, norm_w, w_bf, qnw, knw, seg, cos, sin)


MOBA_HS = 8
V_ROWS = HD_A + 8
FAR_KEYS = 2 * MOBA_BLOCK


def _moba_prompt_select(tab_ref, q_ref, km_ref, rbf_ref, rbs_ref, qs_ref):
    hg = pl.program_id(1)
    qi = pl.program_id(2)
    nblk = km_ref.shape[1]
    lane = lax.broadcasted_iota(jnp.int32, (1, 128), 1)
    blk = lax.broadcasted_iota(jnp.int32, (nblk, MOBA_BLOCK), 0)
    for hl in range(MOBA_HS):
        pr, hh = divmod(hl, 2)
        pc = slice(pr * 128, (pr + 1) * 128)
        qm = jnp.where((lane // HD_A) == hh, q_ref[0, :, pc], 0.0)
        gate = _mm_hp(km_ref[0, :, pc], qm, 1, 1)
        gate = jnp.where(blk < qi, gate, NEG_INF)
        sel = jnp.logical_and(_top3_rows(gate, blk, nblk), blk < qi)
        far_c = tab_ref[NUM_BUCKETS - 1, MOBA_HS * hg + hl] * LOG2E
        rbf_ref[hl] = jnp.where(jnp.logical_and(sel, blk < qi - 1), far_c, NEG_INF)
        rbs_ref[hl] = jnp.where(jnp.logical_or(sel, blk == qi), 0.0, NEG_INF)
        qs_ref[hl] = _bf(qm * (HD_A ** -0.5 * LOG2E))


def _moba_prompt_attend(k_ref, vt_ref, bias_ref, ga_ref, o_ref, rbf_ref, rbs_ref, qs_ref):
    qi = pl.program_id(2)

    def visit(carry, off, nkeys, extra_fn=None, block_rows=None):
        ss = []
        for hl in range(MOBA_HS):
            pr = hl // 2
            kj = k_ref[0, pl.ds(off, nkeys), pr * 128:(pr + 1) * 128]
            ss.append(_dg(kj, qs_ref[hl], 1, 1))
        stats, ps = [], []
        for hl in range(MOBA_HS):
            m = carry[hl][0]
            if block_rows is None:
                s = extra_fn(hl, ss[hl])
                mn = jnp.maximum(m, jnp.max(s, axis=0, keepdims=True))
                p = jnp.exp2(s - mn)
            else:
                halves = [ss[hl][i * MOBA_BLOCK:(i + 1) * MOBA_BLOCK] for i in range(nkeys // MOBA_BLOCK)]
                rows = block_rows(hl)
                mn = m
                for sh, r in zip(halves, rows):
                    mn = jnp.maximum(mn, jnp.max(sh, axis=0, keepdims=True) + r)
                p = jnp.concatenate([jnp.exp2(sh - (mn - r)) for sh, r in zip(halves, rows)], axis=0)
            stats.append((mn, jnp.exp2(m - mn)))
            ps.append(_bf(p))
        pvs = []
        for hl in range(MOBA_HS):
            vj = vt_ref[0, hl * V_ROWS:(hl + 1) * V_ROWS, pl.ds(off, nkeys)]
            pvs.append(jnp.dot(vj, ps[hl], preferred_element_type=F32))
        return tuple((stats[hl][0], stats[hl][1] * carry[hl][1] + pvs[hl]) for hl in range(MOBA_HS))

    def far_body(jp, carry):
        off = pl.multiple_of(jp * FAR_KEYS, FAR_KEYS)

        def rows(hl):
            return rbf_ref[hl, pl.ds(2 * jp, 1), :], rbf_ref[hl, pl.ds(2 * jp + 1, 1), :]

        return visit(carry, off, FAR_KEYS, block_rows=rows)

    init = tuple((jnp.full((1, MOBA_BLOCK), -jnp.inf, F32), jnp.zeros((V_ROWS, MOBA_BLOCK), F32))
                 for _ in range(MOBA_HS))
    carry = lax.fori_loop(0, qi // 2, far_body, init)
    js = jnp.maximum(qi - 1, 0)
    first = qi == 0
    top_tab = jnp.where(first, 0, 1)
    bot_mask = jnp.where(first, NEG_INF, 0.0)

    def near_extra(hl, s):
        top = s[:MOBA_BLOCK] + bias_ref[hl, top_tab] + rbs_ref[hl, pl.ds(js, 1), :]
        bot = s[MOBA_BLOCK:] + (bias_ref[hl, 0] + bot_mask)
        return jnp.concatenate([top, bot], axis=0)

    carry = visit(carry, pl.multiple_of(js * MOBA_BLOCK, MOBA_BLOCK), FAR_KEYS, near_extra)
    for pr in range(MOBA_HS // 2):
        accs = [carry[2 * pr + hh][1] for hh in range(2)]
        outs = [a[0:HD_A] * (1.0 / a[HD_A:HD_A + 1]) for a in accs]
        o = jnp.concatenate(outs, axis=0).T
        pc = slice(pr * 128, (pr + 1) * 128)
        o_ref[0, :, pc] = (o * _silu(ga_ref[0, :, pc].astype(F32))).astype(o_ref.dtype)


N_PAST_BLK = PAST_LEN // MOBA_BLOCK
N_PAGES = PAST_LEN // PAGE_SIZE
N_ROWS_S = 32
SEQ_PER_TILE = 2


def _page_copies(pt_ref, ck_hbm, cv_hbm, kt_buf, vt_buf, sems, seq, sl):
    cps = []
    for p in range(N_PAGES):
        dst = pl.ds(p * PAGE_SIZE, PAGE_SIZE)
        cps.append(pltpu.make_async_copy(ck_hbm.at[pt_ref[seq, p]], kt_buf.at[sl, :, dst], sems.at[0, sl]))
        cps.append(pltpu.make_async_copy(cv_hbm.at[pt_ref[seq, p]], vt_buf.at[sl, :, dst], sems.at[1, sl]))
    return cps


def _moba_sample_seqs(qrep_ref, knew_ref, vnew_ref, ga_ref, bias_ref, bfar_ref, kt_buf, vt_buf, kpad, vpad, o_ref):
    t_new = knew_ref.shape[1]
    rowh = lax.broadcasted_iota(jnp.int32, (N_ROWS_S, W_A), 0) % H_A
    laneh = lax.broadcasted_iota(jnp.int32, (N_ROWS_S, W_A), 1) // HD_A
    own_head = rowh == laneh
    blocks = [slice(n * MOBA_BLOCK, (n + 1) * MOBA_BLOCK) for n in range(N_PAST_BLK)]
    bfar = bfar_ref[...]
    scored = []
    for j in range(SEQ_PER_TILE):
        qf = jnp.where(own_head, qrep_ref[j], 0.0) * (HD_A ** -0.5)
        qbd = _bf(qf)
        q2 = jnp.concatenate([qbd, _bf(qf - qbd.astype(F32))], axis=0)
        kpad[j, 0:t_new, :] = knew_ref[j]
        vpad[j, 0:t_new, :] = vnew_ref[j]
        s_past = []
        for n in range(N_PAST_BLK):
            s2 = jnp.dot(q2, _bf(kt_buf[j, :, blocks[n]]), preferred_element_type=F32)
            s_past.append(s2[0:N_ROWS_S] + s2[N_ROWS_S:])
        s_own = _dg(qbd, _bf(kpad[j]), 1, 1) + bias_ref[:, MOBA_BLOCK:]
        scored.append((s_past, s_own))
    probs = []
    for s_past, s_own in scored:
        g = [jnp.sum(s, axis=1, keepdims=True) for s in s_past]
        sel = [jnp.zeros((N_ROWS_S, 1), jnp.bool_) for _ in range(N_PAST_BLK)]
        for _ in range(MOBA_TOPK):
            m = functools.reduce(jnp.maximum, g)
            idx = functools.reduce(jnp.minimum, [jnp.where(g[n] == m, n, N_PAST_BLK) for n in range(N_PAST_BLK)])
            for n in range(N_PAST_BLK):
                pick = idx == n
                sel[n] = jnp.logical_or(sel[n], pick)
                g[n] = jnp.where(pick, -jnp.inf, g[n])
        logits = [s_past[n] + jnp.where(sel[n], bfar, NEG_INF) for n in range(N_PAST_BLK - 1)]
        logits.append(s_past[-1] + bias_ref[:, 0:MOBA_BLOCK] + jnp.where(sel[-1], 0.0, NEG_INF))
        m = jnp.max(s_own, axis=1, keepdims=True)
        for s in logits:
            m = jnp.maximum(m, jnp.max(s, axis=1, keepdims=True))
        p_own = jnp.exp(s_own - m)
        ps = [jnp.exp(s - m) for s in logits]
        l = functools.reduce(jnp.add, [jnp.sum(p, axis=1, keepdims=True) for p in ps + [p_own]])
        probs.append(([_bf(p) for p in ps], _bf(p_own), l))
    for j, (ps, p_own, l) in enumerate(probs):
        acc = jnp.dot(p_own, _bf(vpad[j]), preferred_element_type=F32)
        for n in range(N_PAST_BLK):
            acc = acc + _dg(ps[n], _bf(vt_buf[j, :, blocks[n]]), 1, 1)
        o = jnp.where(own_head, acc * (1.0 / l), 0.0)
        o = jnp.sum(o.reshape(t_new, H_A, W_A), axis=1)
        o_ref[j] = o * _silu(ga_ref[j])


def _moba_kernel(pt_ref, tab_ref, q_ref, k_ref, vt_ref, km_ref, bias_ref, ga_ref,
                 qrep_ref, knew_ref, vnew_ref, gas_ref, bias_s_ref, bfar_ref, ck_hbm, cv_hbm,
                 o_ref, os_ref, rbf_ref, rbs_ref, qs_ref, kt_buf, vt_buf, kpad, vpad, sems):
    step = pl.program_id(0) * pl.num_programs(2) + pl.program_id(2)
    n_steps = pl.num_programs(0) * pl.num_programs(2)
    seq0 = SEQ_PER_TILE * step
    copies = functools.partial(_page_copies, pt_ref, ck_hbm, cv_hbm, kt_buf, vt_buf, sems)

    @pl.when(step == 0)
    def _():
        kpad[...] = jnp.zeros(kpad.shape, F32)
        vpad[...] = jnp.zeros(vpad.shape, F32)
        for j in range(SEQ_PER_TILE):
            for cp in copies(j, j):
                cp.start()

    for j in range(SEQ_PER_TILE):
        for cp in copies(seq0 + j, j):
            cp.wait()
    _moba_sample_seqs(qrep_ref, knew_ref, vnew_ref, gas_ref, bias_s_ref, bfar_ref, kt_buf, vt_buf, kpad, vpad, os_ref)
    _moba_prompt_select(tab_ref, q_ref, km_ref, rbf_ref, rbs_ref, qs_ref)

    @pl.when(step + 1 < n_steps)
    def _():
        for j in range(SEQ_PER_TILE):
            for cp in copies(seq0 + SEQ_PER_TILE + j, j):
                cp.start()

    _moba_prompt_attend(k_ref, vt_ref, bias_ref, ga_ref, o_ref, rbf_ref, rbs_ref, qs_ref)


def _moba(page_table, rel_bias, qa, kbf, vt, kmean, bias_t, ga,
          qrep, knew, vnew, ga_s, bias_s, bfar, cache_kt, cache_vt):
    nb, s, _ = qa.shape
    nq = s // MOBA_BLOCK
    nseq, t_new, _ = knew.shape
    assert H_A == MOBA_HS and nseq == SEQ_PER_TILE * nb * nq
    tile = pl.BlockSpec((1, MOBA_BLOCK, W_A), lambda b, hg, i, pt: (b, i, 0))
    per_b = lambda shp: pl.BlockSpec((1,) + shp, lambda b, hg, i, pt: (b, 0, 0))
    seqs = lambda r: pl.BlockSpec((SEQ_PER_TILE, r, W_A), lambda b, hg, i, pt: (b * nq + i, 0, 0))
    const = lambda a: pl.BlockSpec(a.shape, lambda b, hg, i, pt: (0,) * a.ndim)
    grid_spec = pltpu.PrefetchScalarGridSpec(
        num_scalar_prefetch=1,
        grid=(nb, 1, nq),
        in_specs=[pl.BlockSpec(memory_space=pltpu.SMEM),
                  tile, per_b((s, W_A)), per_b((H_A * V_ROWS, s)), per_b((nq, W_A)), const(bias_t), tile,
                  seqs(N_ROWS_S), seqs(t_new), seqs(t_new), seqs(t_new), const(bias_s), const(bfar),
                  pl.BlockSpec(memory_space=pl.ANY), pl.BlockSpec(memory_space=pl.ANY)],
        out_specs=(tile, seqs(t_new)),
        scratch_shapes=[pltpu.VMEM((MOBA_HS, nq, MOBA_BLOCK), F32),
                        pltpu.VMEM((MOBA_HS, nq, MOBA_BLOCK), F32),
                        pltpu.VMEM((MOBA_HS, MOBA_BLOCK, 128), BF16),
                        pltpu.VMEM((SEQ_PER_TILE, W_A, PAST_LEN), F32),
                        pltpu.VMEM((SEQ_PER_TILE, W_A, PAST_LEN), F32),
                        pltpu.VMEM((SEQ_PER_TILE, PAGE_SIZE, W_A), F32),
                        pltpu.VMEM((SEQ_PER_TILE, PAGE_SIZE, W_A), F32),
                        pltpu.SemaphoreType.DMA((2, SEQ_PER_TILE))],
    )
    return pl.pallas_call(
        _moba_kernel,
        grid_spec=grid_spec,
        out_shape=(jax.ShapeDtypeStruct((nb, s, W_A), BF16),
                   jax.ShapeDtypeStruct((nseq, t_new, W_A), F32)),
        compiler_params=_cparams(3),
        name="moba",
    )(page_table, rel_bias, qa, kbf, vt, kmean, bias_t, ga, qrep, knew, vnew, ga_s, bias_s, bfar, cache_kt, cache_vt)


def _pad_rows(ref, scratch, s, t):
    if t == CHUNK:
        return ref[s]
    scratch[s] = jnp.zeros(scratch.shape[1:], scratch.dtype)
    scratch[s, 0:t, :] = ref[s].astype(scratch.dtype)
    return scratch[s]


def _ret_kernel(q_ref, k_ref, v_ref, g_ref, st0_ref, dmat_ref, qdec_ref, kdec_ref, gl_ref,
                o_ref, st_ref, qpad, kpad, vpad, gpad, *, t, lq, nbs):
    c = pl.program_id(1)

    @pl.when(c == 0)
    def _():
        st_ref[...] = st0_ref[...]

    lane = lax.broadcasted_iota(jnp.int32, (1, 128), 1)
    rowsel = lax.broadcasted_iota(jnp.int32, (128, 1), 0) < DK_B
    work = []
    for s in range(nbs):
        q = _pad_rows(q_ref, qpad, s, t)[0:lq]
        k = _pad_rows(k_ref, kpad, s, t)
        v = _pad_rows(v_ref, vpad, s, t)
        for hp in range(H_B // 2):
            cols = slice(hp * 128, (hp + 1) * 128)
            kp = _bf(k[:, cols])
            st = st_ref[s, cols, :]
            kd = _bf(k[:, cols] * kdec_ref[0:k.shape[0], cols])
            for hh in range(2):
                h = 2 * hp + hh
                qm = jnp.where((lane // DK_B) == hh, q[:, cols], 0.0)
                vh = _bf(v[:, h * DV_B:(h + 1) * DV_B])
                sc = _dg(_bf(qm), kp, 1, 1)
                so = _mm(qm * qdec_ref[0:lq, cols], st)
                upd = _dg(kd, vh, 0, 0)
                work.append((s, hp, hh, sc, so, upd, vh, st))
    outs = {}
    for (s, hp, hh, sc, so, upd, vh, st) in work:
        h = 2 * hp + hh
        o = jnp.dot(_bf(sc * dmat_ref[h, 0:lq, 0:sc.shape[1]]), vh, preferred_element_type=F32) + so
        outs[(s, h)] = o * lax.rsqrt(jnp.mean(o * o, axis=-1, keepdims=True) + EPS)
    for i in range(0, len(work), 2):
        s, hp, _, _, _, upd0, _, st = work[i]
        cols = slice(hp * 128, (hp + 1) * 128)
        st_ref[s, cols, :] = st * gl_ref[cols, :] + jnp.where(rowsel, upd0, work[i + 1][5])
    for s in range(nbs):
        g = _pad_rows(g_ref, gpad, s, t)[0:lq].astype(F32)
        o = jnp.concatenate([outs[(s, h)] for h in range(H_B)], axis=1)
        o_ref[s] = (o * _silu(g))[0:t].astype(o_ref.dtype)


def _retention(q, k, v, g, st0, dmat, qdec, kdec, gl, lq, nbs):
    nb, nc, t, _ = q.shape
    row = lambda c_: pl.BlockSpec((nbs, None, t, c_), lambda b, c: (b, c, 0, 0))
    const = lambda a: pl.BlockSpec(a.shape, lambda b, c: (0,) * a.ndim)
    st_spec = pl.BlockSpec((nbs, H_B * DK_B, DV_B), lambda b, c: (b, 0, 0))
    pad = lambda c_: pltpu.VMEM((nbs, t if t == CHUNK else SHORT_ROWS, c_), F32)
    return pl.pallas_call(
        functools.partial(_ret_kernel, t=t, lq=lq, nbs=nbs),
        grid=(nb // nbs, nc),
        in_specs=[row(256), row(256), row(512), row(512), st_spec,
                  const(dmat), const(qdec), const(kdec), const(gl)],
        out_specs=(row(512), st_spec),
        out_shape=(jax.ShapeDtypeStruct((nb, nc, t, 512), q.dtype),
                   jax.ShapeDtypeStruct((nb, H_B * DK_B, DV_B), F32)),
        scratch_shapes=[pad(256), pad(256), pad(512), pad(512)],
        compiler_params=_cparams(2),
        name="retention",
    )(q, k, v, g, st0, dmat, qdec, kdec, gl)


def _out_kernel(a_ref, b_ref, x_ref, g_ref, w_ref, o_ref):
    half = w_ref.shape[0] // 2
    y = (jnp.dot(_bf(a_ref[0]), w_ref[0:half, :], preferred_element_type=F32)
         + jnp.dot(_bf(b_ref[0]), w_ref[half:, :], preferred_element_type=F32))
    o_ref[0] = x_ref[0] + g_ref[0] * y


def _out_proj(a, b, x, gate, w_bf, per_row_mod):
    nb, s, d = x.shape
    tm = min(OUT_ROWS, s)
    if per_row_mod:
        g_spec = pl.BlockSpec((1, tm, d), lambda bb, i: (bb, i, 0))
    else:
        g_spec = pl.BlockSpec((1, 1, d), lambda bb, i: (bb, 0, 0))
    row = lambda c: pl.BlockSpec((1, tm, c), lambda bb, i: (bb, i, 0))
    return pl.pallas_call(
        _out_kernel,
        grid=(nb, s // tm),
        in_specs=[row(512), row(512), row(d), g_spec, pl.BlockSpec(w_bf.shape, lambda bb, i: (0, 0))],
        out_specs=row(d),
        out_shape=jax.ShapeDtypeStruct((nb, s, d), F32),
        compiler_params=_cparams(2),
        name="out_proj",
    )(a, b, x, gate, w_bf)


def _odd_in_kernel(x_ref, sc_ref, sh_ref, nw_ref, w_ref, wdt_ref, sguw_ref, sgub_ref, dtb_ref,
                   oc_ref, zg_ref, xbc_ref, dt_ref, *maybe_v_ref):
    x = x_ref[0]
    tm = x.shape[0]
    ms = jnp.mean(x * x, axis=-1, keepdims=True)
    h = (x * lax.rsqrt(ms + EPS) * nw_ref[...]) * (1.0 + sc_ref[0]) + sh_ref[0]
    hb = _bf(h)

    def proj(lo, hi):
        return jnp.dot(hb, w_ref[:, lo:hi], preferred_element_type=F32)

    u = _gelu_tanh(proj(0, 512))
    v = _gelu_tanh(proj(512, 1024))
    mu = jnp.mean(v, axis=-1, keepdims=True)
    vc = v - mu
    v = vc * lax.rsqrt(jnp.mean(vc * vc, axis=-1, keepdims=True) + EPS)
    for v_ref in maybe_v_ref:
        v_ref[0] = v
    ii = lax.broadcasted_iota(jnp.int32, (CHUNK, CHUNK), 0)
    jj = lax.broadcasted_iota(jnp.int32, (CHUNK, CHUNK), 1)
    rows = []
    for ci in range(tm // CHUNK):
        cols = []
        for g in range(G_C):
            wg = jnp.where(ii >= jj, sguw_ref[g], 0.0)
            cols.append(_mm(wg, v[ci * CHUNK:(ci + 1) * CHUNK, g * 128:(g + 1) * 128]))
        rows.append(jnp.concatenate(cols, axis=1) + sgub_ref[...])
    sg = jnp.concatenate(rows, axis=0) if len(rows) > 1 else rows[0]
    oc_ref[0] = (u * sg * _silu(proj(1024, 1536))).astype(oc_ref.dtype)
    zg_ref[0] = proj(1536, 2048).astype(zg_ref.dtype)
    xbc_ref[0] = proj(2048, 3072)
    dt_ref[0] = _softplus(jnp.dot(hb, wdt_ref[...], preferred_element_type=F32) + dtb_ref[...])


def _odd_in(x, scale, shift, norm_w, w_bf, w_dt, sgu_w, sgu_b_tab, dt_bias, per_row_mod, act, emit_v):
    nb, s, d = x.shape
    tm = PROJ_ROWS
    if per_row_mod:
        mod_spec = pl.BlockSpec((1, tm, d), lambda b, i: (b, i, 0))
    else:
        mod_spec = pl.BlockSpec((1, 1, d), lambda b, i: (b, 0, 0))
    row = lambda c: pl.BlockSpec((1, tm, c), lambda b, i: (b, i, 0))
    const = lambda shp: pl.BlockSpec(shp, lambda b, i: (0,) * len(shp))
    return pl.pallas_call(
        _odd_in_kernel,
        grid=(nb, s // tm),
        in_specs=[row(d), mod_spec, mod_spec, const((1, d)), const((d, 3072)), const(w_dt.shape),
                  const(sgu_w.shape), const(sgu_b_tab.shape), const((1, 512))],
        out_specs=(row(512), row(512), row(1024), row(512)) + ((row(512),) if emit_v else ()),
        out_shape=(jax.ShapeDtypeStruct((nb, s, 512), act),
                   jax.ShapeDtypeStruct((nb, s, 512), act),
                   jax.ShapeDtypeStruct((nb, s, 1024), F32),
                   jax.ShapeDtypeStruct((nb, s, 512), F32),
                   ) + ((jax.ShapeDtypeStruct((nb, s, 512), F32),) if emit_v else ()),
        compiler_params=_cparams(2),
        name="odd_in",
    )(x, scale, shift, norm_w, w_bf, w_dt, sgu_w, sgu_b_tab, dt_bias)


def _ssd_kernel(xbc_ref, dt_ref, zg_ref, tail_ref, st0_ref, cw_ref, cb_ref, alog_ref, dsk_ref, nw_ref,
                tri_ref, sel_ref, y_ref, st_ref, ext, dtpad, zpad, *, t, lq, nc, nbs):
    rows = dtpad.shape[1]
    c = pl.program_id(1)

    @pl.when(c == 0)
    def _():
        st_ref[...] = st0_ref[...]
        ext[...] = jnp.zeros(ext.shape, F32)
        ext[:, 0:8, :] = tail_ref[...]

    ii = lax.broadcasted_iota(jnp.int32, (lq, rows), 0)
    jj = lax.broadcasted_iota(jnp.int32, (lq, rows), 1)
    lane = lax.broadcasted_iota(jnp.int32, (1, 128), 1)
    hpg = H_D // G_D
    neg_a = -jnp.exp(alog_ref[...])
    seqs = []
    for s in range(nbs):
        ext[s, 8:8 + t, :] = xbc_ref[s]
        conv = cb_ref[...]
        for w in range(CONV_W):
            conv = conv + ext[s, pl.ds(8 - (CONV_W - 1) + w, rows), :] * cw_ref[w:w + 1, :]
        if nc > 1:
            ext[s, 0:8, :] = ext[s, rows:rows + 8, :]
        xc = _silu(conv)
        dt = _pad_rows(dt_ref, dtpad, s, t)
        cum = _mm_exact_lhs(tri_ref[0:rows, 0:rows], dt * neg_a, 1, 0)
        seqs.append((xc, dt, cum))
    st1 = []
    for s in range(nbs):
        xc, dt, cum = seqs[s]
        xh = xc[:, 0:W_D]
        last = cum[rows - 1:rows, :]
        dtx = _bf(xh * dt)
        xw = _bf(xh * (jnp.exp(last - cum) * dt))
        cum_rows = _mm_exact_lhs(sel_ref[...], cum, 1, 1)
        per_g = []
        for g in range(G_D):
            bg = _bf(xc[:, W_D + g * N_D:W_D + (g + 1) * N_D])
            cg = _bf(xc[0:lq, W_D + G_D * N_D + g * N_D:W_D + G_D * N_D + (g + 1) * N_D])
            gr = slice(g * hpg * P_D, (g + 1) * hpg * P_D)
            cb = _dg(cg, bg, 1, 1)
            yoff = _dg(cg, _bf(st_ref[s, gr, :]), 1, 1)
            upd = _dg(xw[:, gr], bg, 0, 0)
            per_g.append((cb, yoff, upd))
        st1.append((dtx, cum_rows, per_g))
    for s in range(nbs):
        xc, dt, cum = seqs[s]
        dtx, cum_rows, per_g = st1[s]
        ecum = jnp.exp(cum[0:lq])
        elast = jnp.exp(cum[rows - 1:rows, :])
        ys = []
        for g in range(G_D):
            cb, yoff, upd = per_g[g]
            for pr in range(hpg // 2):
                l0 = g * hpg * P_D + pr * 128
                yh = []
                for hh in range(2):
                    h = g * hpg + pr * 2 + hh
                    col = jnp.broadcast_to(cum[0:lq, h * P_D:h * P_D + 1], (lq, rows))
                    seg = jnp.minimum(col - cum_rows[h:h + 1, :], 0.0)
                    mh = jnp.where(ii >= jj, cb * jnp.exp(seg), 0.0)
                    yh.append(jnp.dot(_bf(mh), dtx[:, l0:l0 + 128], preferred_element_type=F32))
                ypair = jnp.where(lane < P_D, yh[0], yh[1])
                ys.append(ypair + yoff[:, pr * 128:(pr + 1) * 128] * ecum[:, l0:l0 + 128])
            for hl in range(hpg):
                h = g * hpg + hl
                r = slice(h * P_D, (h + 1) * P_D)
                dec = jnp.broadcast_to(elast[0:1, h * P_D:h * P_D + 1], (P_D, N_D))
                st_ref[s, r, :] = st_ref[s, r, :] * dec + upd[hl * P_D:(hl + 1) * P_D, :]
        y = jnp.concatenate(ys, axis=1)
        zg = _pad_rows(zg_ref, zpad, s, t)[0:lq].astype(F32)
        y = (y + xc[0:lq, 0:W_D] * dsk_ref[...]) * _silu(zg)
        gw = W_D // G_D
        outs = []
        for g in range(G_D):
            yg = y[:, g * gw:(g + 1) * gw]
            outs.append(yg * lax.rsqrt(jnp.mean(yg * yg, axis=-1, keepdims=True) + EPS))
        y_ref[s] = (jnp.concatenate(outs, axis=1) * nw_ref[...])[0:t].astype(y_ref.dtype)


def _ssd(xbc, dt, zg, tail, st0, conv_w, conv_b, a_log, d_skip, norm_w, tri, sel, lq, nbs):
    nb, nc, t, _ = xbc.shape
    rows = t if t == CHUNK else SHORT_ROWS
    row = lambda c_: pl.BlockSpec((nbs, None, t, c_), lambda b, c: (b, c, 0, 0))
    const = lambda a: pl.BlockSpec(a.shape, lambda b, c: (0,) * a.ndim)
    st_spec = pl.BlockSpec((nbs, H_D * P_D, N_D), lambda b, c: (b, 0, 0))
    return pl.pallas_call(
        functools.partial(_ssd_kernel, t=t, lq=lq, nc=nc, nbs=nbs),
        grid=(nb // nbs, nc),
        in_specs=[row(1024), row(512), row(512),
                  pl.BlockSpec((nbs, 8, CONV_DIM), lambda b, c: (b, 0, 0)), st_spec,
                  const(conv_w), const(conv_b), const(a_log), const(d_skip), const(norm_w),
                  const(tri), const(sel)],
        out_specs=(row(512), st_spec),
        out_shape=(jax.ShapeDtypeStruct((nb, nc, t, 512), zg.dtype),
                   jax.ShapeDtypeStruct((nb, H_D * P_D, N_D), F32)),
        scratch_shapes=[pltpu.VMEM((nbs, rows + 8, CONV_DIM), F32),
                        pltpu.VMEM((nbs, rows, 512), F32), pltpu.VMEM((nbs, rows, 512), F32)],
        compiler_params=_cparams(2),
        name="ssd",
    )(xbc, dt, zg, tail, st0, conv_w, conv_b, a_log, d_skip, norm_w, tri, sel)


def _rotary_tables(pos):
    half = DK_B // 2
    inv = (np.float32(1.0) / np.float32(10000.0) ** (np.arange(half, dtype=np.float32) / np.float32(half)))
    ang = (np.asarray(pos).astype(np.float32)[:, None] * inv.astype(np.float32)[None, :]).astype(np.float64)
    cos, sin = np.cos(ang), np.sin(ang)
    cos_t = np.tile(np.concatenate([cos, cos], axis=1), (1, H_B))
    sin_t = np.tile(np.concatenate([-sin, sin], axis=1), (1, H_B))
    return jnp.asarray(cos_t, F32), jnp.asarray(sin_t, F32)


def _retention_tables(chunk_len):
    log_g = np.log(1.0 - 2.0 ** (-5.0 - np.arange(H_B, dtype=np.float64)))
    idx = np.arange(CHUNK, dtype=np.float64)
    diff = idx[:, None] - idx[None, :]
    dmat = np.where(diff[None] >= 0, np.exp(np.maximum(diff, 0.0)[None] * log_g[:, None, None]), 0.0)
    qdec = np.exp((idx + 1.0)[:, None] * log_g[None, :])
    kdec = np.where(idx[:, None] < chunk_len, np.exp((chunk_len - 1.0 - idx)[:, None] * log_g[None, :]), 0.0)
    gl = np.exp(chunk_len * log_g)
    return (jnp.asarray(dmat, F32),
            jnp.asarray(np.repeat(qdec, DK_B, axis=1), F32),
            jnp.asarray(np.repeat(kdec, DK_B, axis=1), F32),
            jnp.asarray(np.repeat(np.repeat(gl, DK_B)[:, None], DV_B, axis=1), F32))


def _prompt_bias_idx():
    kk = np.arange(MOBA_BLOCK)[:, None]
    qq = np.arange(MOBA_BLOCK)[None, :]
    diag = np.where(qq >= kk, _t5_bucket_np(qq - kk), -1)
    sub = _t5_bucket_np(qq + MOBA_BLOCK - kk)
    return np.concatenate([diag, sub], axis=0).astype(np.int32)


def _sample_bias_idx(t_new):
    row_t = (np.arange(N_ROWS_S) // H_A)[:, None]
    qpos = PAST_LEN + row_t
    near = _t5_bucket_np(qpos - (PAST_LEN - MOBA_BLOCK + np.arange(MOBA_BLOCK))[None, :])
    own_k = np.arange(PAGE_SIZE)[None, :]
    own = np.where((own_k <= row_t) & (own_k < t_new), _t5_bucket_np(row_t - own_k), -1)
    return np.concatenate([near, own], axis=1).astype(np.int32)


def kernel(x_prompt, x_sample, cache_k, cache_v, state_ret, state_ssm, state_conv, page_table, c_prompt, c_sample, rel_bias, e_norm_w, e_ada_w, e_ada_b, e_in_w, e_q_norm_w, e_k_norm_w, e_out_w, o_norm_w, o_ada_w, o_ada_b, o_in_w, o_sgu_w, o_sgu_b, o_conv_w, o_conv_b, o_dt_bias, o_A_log, o_D, o_ssm_norm_w, o_out_w):
    bp, s_len, d = x_prompt.shape
    bs, t_len, _ = x_sample.shape
    n_s = bs * t_len

    c_all = jnp.concatenate([c_prompt, jnp.repeat(c_sample, t_len, axis=0),
                             jnp.zeros((8 - (bp + n_s) % 8, d), F32)], axis=0)
    mods = []
    for ada_w, ada_b in ((e_ada_w[0], e_ada_b[0]), (o_ada_w[0], o_ada_b[0])):
        mod = _ada_mod(c_all, ada_w, ada_b)
        parts_p = [mod[:bp, i * d:(i + 1) * d].reshape(bp, 1, d) for i in range(3)]
        parts_s = [mod[bp:bp + n_s, i * d:(i + 1) * d].reshape(1, n_s, d) for i in range(3)]
        mods.append((parts_p, parts_s))
    (e_mod_p, e_mod_s), (o_mod_p, o_mod_s) = mods

    seg = jnp.asarray(np.kron(np.eye(H_A), np.ones((HD_A, HD_A))), BF16)
    qnw = jnp.tile(e_q_norm_w[0], H_A).reshape(1, W_A)
    knw = jnp.tile(e_k_norm_w[0], H_A).reshape(1, W_A)
    e_in_bf = _bf(e_in_w[0])
    e_out_bf = _bf(e_out_w[0])
    o_in_bf = _bf(o_in_w[0])
    o_dt_bf = _bf(jnp.repeat(o_in_w[0][:, 3072:], P_D, axis=1))
    o_out_bf = _bf(o_out_w[0])
    x_s = x_sample.reshape(1, n_s, d)
    cos_p, sin_p = _rotary_tables(np.arange(s_len))
    cos_s, sin_s = _rotary_tables(PAST_LEN + (np.arange(n_s) % t_len))
    bias_p = _bias_tables(rel_bias, _prompt_bias_idx(), LOG2E).reshape(H_A, 2, MOBA_BLOCK, MOBA_BLOCK)
    bias_s_h = _bias_tables(rel_bias, _sample_bias_idx(t_len))
    row_h = jnp.arange(N_ROWS_S) % H_A
    bias_s = jnp.sum(jnp.where((jnp.arange(H_A)[:, None] == row_h[None, :])[:, :, None], bias_s_h, 0.0), axis=0)
    bfar = rel_bias[NUM_BUCKETS - 1, row_h].reshape(N_ROWS_S, 1)

    (qa, ka, va, kbf, vt, ga, qb, kb, vb, gb, kmean) = _even_in(
        x_prompt, e_mod_p[1], e_mod_p[0], e_norm_w[0].reshape(1, d), e_in_bf, qnw, knw, seg, cos_p, sin_p, False, BF16)
    (qa_s, ka_s, va_s, _, _, ga_s, qb_s, kb_s, vb_s, gb_s, _) = _even_in(
        x_s, e_mod_s[1], e_mod_s[0], e_norm_w[0].reshape(1, d), e_in_bf, qnw, knw, seg, cos_s, sin_s, True, F32)
    sq = lambda a: a.reshape(bs, t_len, a.shape[-1])
    qrep = jnp.repeat(sq(qa_s), H_A, axis=1)
    n_phys = cache_k.shape[1]
    page_t = lambda c: jnp.transpose(c[0], (0, 2, 3, 1)).reshape(n_phys, W_A, PAGE_SIZE)
    oa, oa_s = _moba(page_table, rel_bias, qa, kbf, vt, kmean.reshape(bp, s_len // MOBA_BLOCK, W_A), bias_p, ga,
                     qrep, sq(ka_s), sq(va_s), sq(ga_s), bias_s, bfar, page_t(cache_k), page_t(cache_v))
    nc_p = s_len // CHUNK
    ch = lambda a: a.reshape(bp, nc_p, CHUNK, a.shape[-1])
    ob, ret_p = _retention(ch(qb), ch(kb), ch(vb), ch(gb), jnp.zeros((bp, H_B * DK_B, DV_B), F32),
                           *_retention_tables(CHUNK), lq=CHUNK, nbs=bp)
    xp1 = _out_proj(oa, ob.reshape(bp, s_len, W_B), x_prompt, e_mod_p[2], e_out_bf, False)
    k_prompt = ka.reshape(1, bp, s_len, H_A, HD_A)
    v_prompt = va.reshape(1, bp, s_len, H_A, HD_A)
    ret_state_prompt = ret_p.reshape(1, bp, H_B, DK_B, DV_B)

    sc = lambda a: a.reshape(bs, 1, t_len, a.shape[-1])
    ob_s, ret_s = _retention(sc(qb_s), sc(kb_s), sc(vb_s), sc(gb_s),
                             state_ret[0].reshape(bs, H_B * DK_B, DV_B), *_retention_tables(t_len), lq=8, nbs=SEQ_PER_STEP)
    xs1 = _out_proj(oa_s.reshape(1, n_s, W_A), ob_s.reshape(1, n_s, W_B), x_s, e_mod_s[2], e_out_bf, True)
    k_sample = ka_s.reshape(1, bs, t_len, H_A, HD_A)
    v_sample = va_s.reshape(1, bs, t_len, H_A, HD_A)
    ret_state_sample = ret_s.reshape(1, bs, H_B, DK_B, DV_B)

    tri = jnp.asarray(np.tril(np.ones((CHUNK, CHUNK))), BF16)
    sel = jnp.asarray(np.kron(np.eye(H_D), np.eye(1, P_D)), BF16)
    rep = lambda a: jnp.repeat(a, P_D).reshape(1, W_D)
    dt_bias, a_log, d_skip = rep(o_dt_bias[0]), rep(o_A_log[0]), rep(o_D[0])
    ssm_nw = o_ssm_norm_w[0].reshape(1, W_D)
    conv_b = o_conv_b[0].reshape(1, CONV_DIM)
    o_nw = o_norm_w[0].reshape(1, d)

    sgu_b_p = jnp.repeat(o_sgu_b[0].T, W_C // G_C, axis=1)
    oc, zg, xbc, dtp = _odd_in(xp1, o_mod_p[1], o_mod_p[0], o_nw, o_in_bf, o_dt_bf, o_sgu_w[0], sgu_b_p, dt_bias,
                               False, BF16, False)
    yn, ssm_p = _ssd(ch(xbc), ch(dtp), ch(zg), jnp.zeros((bp, 8, CONV_DIM), F32),
                     jnp.zeros((bp, H_D * P_D, N_D), F32), o_conv_w[0], conv_b, a_log, d_skip, ssm_nw,
                     tri, sel, lq=CHUNK, nbs=bp)
    y_prompt = _out_proj(oc, yn.reshape(bp, s_len, W_D), xp1, o_mod_p[2], o_out_bf, False)
    ssm_state_prompt = ssm_p.reshape(1, bp, H_D, P_D, N_D)
    conv_state_prompt = xbc[:, -(CONV_W - 1):][None]

    per_chunk = CHUNK // t_len
    w_small = o_sgu_w[0][:, :t_len, :t_len]
    same_seq = jnp.asarray(np.kron(np.eye(per_chunk), np.ones((t_len, t_len))), F32)
    tok_of_row = jnp.asarray(np.tile(np.eye(t_len), (per_chunk, 1)), F32)
    sgu_w_s = jnp.einsum('ia,gab,jb->gij', tok_of_row, w_small, tok_of_row,
                         precision=lax.Precision.HIGHEST) * same_seq
    sgu_b_s = jnp.repeat(jnp.tile(o_sgu_b[0][:, :t_len].T, (per_chunk, 1)), W_C // G_C, axis=1)
    oc_s, zg_s, xbc_s, dt_s, v_s = _odd_in(xs1, o_mod_s[1], o_mod_s[0], o_nw, o_in_bf, o_dt_bf, sgu_w_s, sgu_b_s, dt_bias,
                                           True, F32, True)
    tail_s = jnp.concatenate([jnp.zeros((bs, 8 - (CONV_W - 1), CONV_DIM), F32), state_conv[0]], axis=1)
    yn_s, ssm_s = _ssd(sc(xbc_s), sc(dt_s), sc(zg_s), tail_s, state_ssm[0].reshape(bs, H_D * P_D, N_D),
                       o_conv_w[0], conv_b, a_log, d_skip, ssm_nw, tri, sel, lq=8, nbs=SEQ_PER_STEP)
    xs2 = _out_proj(oc_s, yn_s.reshape(1, n_s, W_D), xs1, o_mod_s[2], o_out_bf, True)
    y_sample = xs2.reshape(bs, t_len, d)
    sgu_v_sample = v_s.reshape(1, bs, t_len, W_C)
    ssm_state_sample = ssm_s.reshape(1, bs, H_D, P_D, N_D)
    xin = jnp.concatenate([state_conv[0], xbc_s.reshape(bs, t_len, CONV_DIM)], axis=1)
    conv_state_sample = xin[:, -(CONV_W - 1):][None]

    return (y_prompt, y_sample, k_prompt, v_prompt, k_sample, v_sample, ret_state_prompt, ret_state_sample,
            sgu_v_sample, ssm_state_prompt, ssm_state_sample, conv_state_prompt, conv_state_sample)
```

```python
import functools
import math

import numpy as np
import jax
import jax.numpy as jnp
from jax import lax
from jax.experimental import pallas as pl
from jax.experimental.pallas import tpu as pltpu

F32 = jnp.float32
BF16 = jnp.bfloat16

D_MODEL = 1024
PAST_LEN = 2048
PAGE_SIZE = 128
H_A, HD_A, W_A = 8, 64, 512
MOBA_BLOCK = 256
MOBA_TOPK = 3
NUM_BUCKETS = 32
MAX_DISTANCE = 128
H_B, DK_B, DV_B, W_B = 4, 64, 128, 512
G_C, W_C = 4, 512
H_D, P_D, N_D, G_D, W_D = 8, 64, 128, 2, 512
CONV_W = 4
CONV_DIM = 1024
CHUNK = 128
SEQ_PER_STEP = 8
SHORT_ROWS = 16
PROJ_ROWS = 512
OUT_ROWS = 1024
NEG_INF = -1e30
EPS = 1e-6
LOG2E = math.log2(math.e)
VMEM_LIMIT = 56 * 1024 * 1024


def _bf(x):
    return x.astype(BF16)


def _dg(a, b, ca, cb):
    return lax.dot_general(a, b, (((ca,), (cb,)), ((), ())), preferred_element_type=F32)


def _mm(a, b):
    return _dg(_bf(a), _bf(b), 1, 0)


def _mm_nt(a, b):
    return _dg(_bf(a), _bf(b), 1, 1)


def _mm_tn(a, b):
    return _dg(_bf(a), _bf(b), 0, 0)


def _split2(x):
    hi = _bf(x)
    return hi, _bf(x - hi.astype(F32))


def _split3(x):
    hi = _bf(x)
    r = x - hi.astype(F32)
    mid = _bf(r)
    return hi, mid, _bf(r - mid.astype(F32))


def _mm_hp(a, b, ca, cb):
    ah, al = _split2(a)
    bh, bl = _split2(b)
    return _dg(ah, bh, ca, cb) + (_dg(ah, bl, ca, cb) + _dg(al, bh, ca, cb))


def _mm_exact_lhs(e, x, ca, cb):
    h, m, l = _split3(x)
    return _dg(e, h, ca, cb) + (_dg(e, m, ca, cb) + _dg(e, l, ca, cb))


def _silu(x):
    return x * (1.0 / (1.0 + jnp.exp(-x)))


def _gelu_tanh(x):
    return 0.5 * x * (1.0 + jnp.tanh(math.sqrt(2.0 / math.pi) * (x + 0.044715 * (x * x * x))))


def _softplus(x):
    return jnp.maximum(x, 0.0) + jnp.log1p(jnp.exp(-jnp.abs(x)))


def _cparams(n_grid):
    return pltpu.CompilerParams(dimension_semantics=("arbitrary",) * n_grid,
                                vmem_limit_bytes=VMEM_LIMIT)


def _top3_rows(g, blk, nblk):
    sel = jnp.zeros(g.shape, jnp.bool_)
    for _ in range(MOBA_TOPK):
        m = jnp.max(g, axis=0, keepdims=True)
        idx = jnp.min(jnp.where(g == m, blk, nblk), axis=0, keepdims=True)
        pick = blk == idx
        sel = jnp.logical_or(sel, pick)
        g = jnp.where(pick, -jnp.inf, g)
    return sel


def _ada_kernel(c_ref, w_ref, b_ref, o_ref):
    s = _silu(c_ref[...])
    o_ref[0] = _mm_hp(s, w_ref[...], 1, 0) + b_ref[...]


def _ada_mod(c_all, w, b):
    m, d = c_all.shape
    n_parts = w.shape[1] // d
    return pl.pallas_call(
        _ada_kernel,
        grid=(n_parts,),
        in_specs=[pl.BlockSpec((m, d), lambda j: (0, 0)),
                  pl.BlockSpec((d, d), lambda j: (0, j)),
                  pl.BlockSpec((1, d), lambda j: (0, j))],
        out_specs=pl.BlockSpec((1, m, d), lambda j: (j, 0, 0)),
        out_shape=jax.ShapeDtypeStruct((n_parts, m, d), F32),
        compiler_params=_cparams(1),
        name="ada_mod",
    )(c_all, w, b.reshape(1, n_parts * d))


def _t5_bucket_np(rel):
    n = np.maximum(rel, 0)
    max_exact = NUM_BUCKETS // 2
    nf = np.maximum(n, 1).astype(np.float64)
    large = max_exact + (np.log(nf / max_exact) / math.log(MAX_DISTANCE / max_exact)
                         * (NUM_BUCKETS - max_exact)).astype(np.int64)
    large = np.minimum(large, NUM_BUCKETS - 1)
    return np.where(n < max_exact, n, large).astype(np.int32)


def _bias_kernel(tab_ref, idx_ref, o_ref, *, scale):
    h = pl.program_id(0)
    idx = idx_ref[...]
    acc = jnp.zeros(idx.shape, F32)
    for b in range(NUM_BUCKETS):
        acc = jnp.where(idx == b, tab_ref[b, h], acc)
    o_ref[0] = jnp.where(idx == -1, NEG_INF, acc * scale)


def _bias_tables(rel_bias, idx, scale=1.0):
    r, c = idx.shape
    return pl.pallas_call(
        functools.partial(_bias_kernel, scale=scale),
        grid=(H_A,),
        in_specs=[pl.BlockSpec(memory_space=pltpu.SMEM),
                  pl.BlockSpec((r, c), lambda h: (0, 0))],
        out_specs=pl.BlockSpec((1, r, c), lambda h: (h, 0, 0)),
        out_shape=jax.ShapeDtypeStruct((H_A, r, c), F32),
        compiler_params=_cparams(1),
        name="t5_bias",
    )(rel_bias, jnp.asarray(idx))


def _even_in_kernel(x_ref, sc_ref, sh_ref, nw_ref, w_ref, qnw_ref, knw_ref, seg_ref, cos_ref, sin_ref,
                    qa_ref, ka_ref, va_ref, kbf_ref, vt_ref, ga_ref, qb_ref, kb_ref, vb_ref, gb_ref, km_ref):
    x = x_ref[0]
    ms = jnp.mean(x * x, axis=-1, keepdims=True)
    h = (x * lax.rsqrt(ms + EPS) * nw_ref[...]) * (1.0 + sc_ref[0]) + sh_ref[0]
    hb = _bf(h)

    def proj(lo, hi):
        return jnp.dot(hb, w_ref[:, lo:hi], preferred_element_type=F32)

    def head_rms(t, w_row):
        ss = jnp.dot(_bf(t * t), seg_ref[...], preferred_element_type=F32)
        return t * lax.rsqrt(ss * (1.0 / HD_A) + EPS) * w_row

    qa_ref[0] = head_rms(proj(0, 512), qnw_ref[...])
    ka = head_rms(proj(512, 1024), knw_ref[...])
    ka_ref[0] = ka
    kbf_ref[0] = _bf(ka)
    for j in range(ka.shape[0] // MOBA_BLOCK):
        km_ref[0, j] = jnp.mean(ka[j * MOBA_BLOCK:(j + 1) * MOBA_BLOCK], axis=0, keepdims=True)
    va = proj(1024, 1536)
    va_ref[0] = va
    vat = va.T
    ones_pad = (lax.broadcasted_iota(jnp.int32, (V_ROWS - HD_A, vat.shape[1]), 0) == 0).astype(F32)
    vt_ref[0] = _bf(jnp.concatenate(
        [piece for h in range(H_A) for piece in (vat[h * HD_A:(h + 1) * HD_A], ones_pad)], axis=0))
    ga_ref[0] = proj(1536, 2048).astype(ga_ref.dtype)

    lane = lax.broadcasted_iota(jnp.int32, (1, 256), 1) % DK_B
    first_half = lane < (DK_B // 2)
    cos = cos_ref[...]
    sin = sin_ref[...]

    def rotary(t):
        up = pltpu.roll(t, 256 - DK_B // 2, 1)
        dn = pltpu.roll(t, DK_B // 2, 1)
        return t * cos + jnp.where(first_half, up, dn) * sin

    qb_ref[0] = rotary(proj(2048, 2304)).astype(qb_ref.dtype)
    kb_ref[0] = (rotary(proj(2304, 2560)) * (DK_B ** -0.5)).astype(kb_ref.dtype)
    vb_ref[0] = proj(2560, 3072).astype(vb_ref.dtype)
    gb_ref[0] = proj(3072, 3584).astype(gb_ref.dtype)


def _even_in(x, scale, shift, norm_w, w_bf, qnw, knw, seg, cos, sin, per_row_mod, act):
    nb, s, d = x.shape
    tm = PROJ_ROWS
    ns = s // tm
    nkb = tm // MOBA_BLOCK
    if per_row_mod:
        mod_spec = pl.BlockSpec((1, tm, d), lambda b, i: (b, i, 0))
    else:
        mod_spec = pl.BlockSpec((1, 1, d), lambda b, i: (b, 0, 0))
    row = lambda c: pl.BlockSpec((1, tm, c), lambda b, i: (b, i, 0))
    const = lambda shp: pl.BlockSpec(shp, lambda b, i: (0,) * len(shp))
    out_shape = (
        jax.ShapeDtypeStruct((nb, s, 512), F32),
        jax.ShapeDtypeStruct((nb, s, 512), F32),
        jax.ShapeDtypeStruct((nb, s, 512), F32),
        jax.ShapeDtypeStruct((nb, s, 512), BF16),
        jax.ShapeDtypeStruct((nb, H_A * V_ROWS, s), BF16),
        jax.ShapeDtypeStruct((nb, s, 512), act),
        jax.ShapeDtypeStruct((nb, s, 256), act),
        jax.ShapeDtypeStruct((nb, s, 256), act),
        jax.ShapeDtypeStruct((nb, s, 512), act),
        jax.ShapeDtypeStruct((nb, s, 512), act),
        jax.ShapeDtypeStruct((nb, ns * nkb, 1, 512), F32),
    )
    out_specs = (row(512), row(512), row(512), row(512),
                 pl.BlockSpec((1, H_A * V_ROWS, tm), lambda b, i: (b, 0, i)),
                 row(512), row(256), row(256), row(512), row(512),
                 pl.BlockSpec((1, nkb, 1, 512), lambda b, i: (b, i, 0, 0)))
    return pl.pallas_call(
        _even_in_kernel,
        grid=(nb, ns),
        in_specs=[row(d), mod_spec, mod_spec, const((1, d)), const((d, 3584)),
                  const((1, 512)), const((1, 512)), const((512, 512)),
                  pl.BlockSpec((tm, 256), lambda b, i: (i, 0)),
                  pl.BlockSpec((tm, 256), lambda b, i: (i, 0))],
        out_specs=out_specs,
        out_shape=out_shape,
        compiler_params=_cparams(2),
        name="even_in",
    )(x, scale, shift, norm_w, w_bf, qnw, knw, seg, cos, sin)


MOBA_HS = 8
V_ROWS = HD_A + 8
FAR_KEYS = 2 * MOBA_BLOCK


def _moba_prompt_select(tab_ref, q_ref, km_ref, rbf_ref, rbs_ref, qs_ref):
    hg = pl.program_id(1)
    qi = pl.program_id(2)
    nblk = km_ref.shape[1]
    lane = lax.broadcasted_iota(jnp.int32, (1, 128), 1)
    blk = lax.broadcasted_iota(jnp.int32, (nblk, MOBA_BLOCK), 0)
    for hl in range(MOBA_HS):
        pr, hh = divmod(hl, 2)
        pc = slice(pr * 128, (pr + 1) * 128)
        qm = jnp.where((lane // HD_A) == hh, q_ref[0, :, pc], 0.0)
        gate = _mm_hp(km_ref[0, :, pc], qm, 1, 1)
        gate = jnp.where(blk < qi, gate, NEG_INF)
        sel = jnp.logical_and(_top3_rows(gate, blk, nblk), blk < qi)
        far_c = tab_ref[NUM_BUCKETS - 1, MOBA_HS * hg + hl] * LOG2E
        rbf_ref[hl] = jnp.where(jnp.logical_and(sel, blk < qi - 1), far_c, NEG_INF)
        rbs_ref[hl] = jnp.where(jnp.logical_or(sel, blk == qi), 0.0, NEG_INF)
        qs_ref[hl] = _bf(qm * (HD_A ** -0.5 * LOG2E))


def _moba_prompt_attend(k_ref, vt_ref, bias_ref, ga_ref, o_ref, rbf_ref, rbs_ref, qs_ref):
    qi = pl.program_id(2)

    def visit(carry, off, nkeys, extra_fn=None, block_rows=None):
        ss = []
        for hl in range(MOBA_HS):
            pr = hl // 2
            kj = k_ref[0, pl.ds(off, nkeys), pr * 128:(pr + 1) * 128]
            ss.append(_dg(kj, qs_ref[hl], 1, 1))
        stats, ps = [], []
        for hl in range(MOBA_HS):
            m = carry[hl][0]
            if block_rows is None:
                s = extra_fn(hl, ss[hl])
                mn = jnp.maximum(m, jnp.max(s, axis=0, keepdims=True))
                p = jnp.exp2(s - mn)
            else:
                halves = [ss[hl][i * MOBA_BLOCK:(i + 1) * MOBA_BLOCK] for i in range(nkeys // MOBA_BLOCK)]
                rows = block_rows(hl)
                mn = m
                for sh, r in zip(halves, rows):
                    mn = jnp.maximum(mn, jnp.max(sh, axis=0, keepdims=True) + r)
                p = jnp.concatenate([jnp.exp2(sh - (mn - r)) for sh, r in zip(halves, rows)], axis=0)
            stats.append((mn, jnp.exp2(m - mn)))
            ps.append(_bf(p))
        pvs = []
        for hl in range(MOBA_HS):
            vj = vt_ref[0, hl * V_ROWS:(hl + 1) * V_ROWS, pl.ds(off, nkeys)]
            pvs.append(jnp.dot(vj, ps[hl], preferred_element_type=F32))
        return tuple((stats[hl][0], stats[hl][1] * carry[hl][1] + pvs[hl]) for hl in range(MOBA_HS))

    def far_body(jp, carry):
        off = pl.multiple_of(jp * FAR_KEYS, FAR_KEYS)

        def rows(hl):
            return rbf_ref[hl, pl.ds(2 * jp, 1), :], rbf_ref[hl, pl.ds(2 * jp + 1, 1), :]

        return visit(carry, off, FAR_KEYS, block_rows=rows)

    init = tuple((jnp.full((1, MOBA_BLOCK), -jnp.inf, F32), jnp.zeros((V_ROWS, MOBA_BLOCK), F32))
                 for _ in range(MOBA_HS))
    carry = lax.fori_loop(0, qi // 2, far_body, init)
    js = jnp.maximum(qi - 1, 0)
    first = qi == 0
    top_tab = jnp.where(first, 0, 1)
    bot_mask = jnp.where(first, NEG_INF, 0.0)

    def near_extra(hl, s):
        top = s[:MOBA_BLOCK] + bias_ref[hl, top_tab] + rbs_ref[hl, pl.ds(js, 1), :]
        bot = s[MOBA_BLOCK:] + (bias_ref[hl, 0] + bot_mask)
        return jnp.concatenate([top, bot], axis=0)

    carry = visit(carry, pl.multiple_of(js * MOBA_BLOCK, MOBA_BLOCK), FAR_KEYS, near_extra)
    for pr in range(MOBA_HS // 2):
        accs = [carry[2 * pr + hh][1] for hh in range(2)]
        outs = [a[0:HD_A] * (1.0 / a[HD_A:HD_A + 1]) for a in accs]
        o = jnp.concatenate(outs, axis=0).T
        pc = slice(pr * 128, (pr + 1) * 128)
        o_ref[0, :, pc] = (o * _silu(ga_ref[0, :, pc].astype(F32))).astype(o_ref.dtype)


N_PAST_BLK = PAST_LEN // MOBA_BLOCK
N_PAGES = PAST_LEN // PAGE_SIZE
N_ROWS_S = 32
SEQ_PER_TILE = 2


def _page_copies(pt_ref, ck_hbm, cv_hbm, kt_buf, vt_buf, sems, seq, sl):
    cps = []
    for p in range(N_PAGES):
        dst = pl.ds(p * PAGE_SIZE, PAGE_SIZE)
        cps.append(pltpu.make_async_copy(ck_hbm.at[pt_ref[seq, p]], kt_buf.at[sl, :, dst], sems.at[0, sl]))
        cps.append(pltpu.make_async_copy(cv_hbm.at[pt_ref[seq, p]], vt_buf.at[sl, :, dst], sems.at[1, sl]))
    return cps


def _moba_sample_seqs(qrep_ref, knew_ref, vnew_ref, ga_ref, bias_ref, bfar_ref, kt_buf, vt_buf, kpad, vpad, o_ref):
    t_new = knew_ref.shape[1]
    rowh = lax.broadcasted_iota(jnp.int32, (N_ROWS_S, W_A), 0) % H_A
    laneh = lax.broadcasted_iota(jnp.int32, (N_ROWS_S, W_A), 1) // HD_A
    own_head = rowh == laneh
    blocks = [slice(n * MOBA_BLOCK, (n + 1) * MOBA_BLOCK) for n in range(N_PAST_BLK)]
    bfar = bfar_ref[...]
    scored = []
    for j in range(SEQ_PER_TILE):
        qf = jnp.where(own_head, qrep_ref[j], 0.0) * (HD_A ** -0.5)
        qbd = _bf(qf)
        q2 = jnp.concatenate([qbd, _bf(qf - qbd.astype(F32))], axis=0)
        kpad[j, 0:t_new, :] = knew_ref[j]
        vpad[j, 0:t_new, :] = vnew_ref[j]
        s_past = []
        for n in range(N_PAST_BLK):
            s2 = jnp.dot(q2, _bf(kt_buf[j, :, blocks[n]]), preferred_element_type=F32)
            s_past.append(s2[0:N_ROWS_S] + s2[N_ROWS_S:])
        s_own = _dg(qbd, _bf(kpad[j]), 1, 1) + bias_ref[:, MOBA_BLOCK:]
        scored.append((s_past, s_own))
    probs = []
    for s_past, s_own in scored:
        g = [jnp.sum(s, axis=1, keepdims=True) for s in s_past]
        sel = [jnp.zeros((N_ROWS_S, 1), jnp.bool_) for _ in range(N_PAST_BLK)]
        for _ in range(MOBA_TOPK):
            m = functools.reduce(jnp.maximum, g)
            idx = functools.reduce(jnp.minimum, [jnp.where(g[n] == m, n, N_PAST_BLK) for n in range(N_PAST_BLK)])
            for n in range(N_PAST_BLK):
                pick = idx == n
                sel[n] = jnp.logical_or(sel[n], pick)
                g[n] = jnp.where(pick, -jnp.inf, g[n])
        logits = [s_past[n] + jnp.where(sel[n], bfar, NEG_INF) for n in range(N_PAST_BLK - 1)]
        logits.append(s_past[-1] + bias_ref[:, 0:MOBA_BLOCK] + jnp.where(sel[-1], 0.0, NEG_INF))
        m = jnp.max(s_own, axis=1, keepdims=True)
        for s in logits:
            m = jnp.maximum(m, jnp.max(s, axis=1, keepdims=True))
        p_own = jnp.exp(s_own - m)
        ps = [jnp.exp(s - m) for s in logits]
        l = functools.reduce(jnp.add, [jnp.sum(p, axis=1, keepdims=True) for p in ps + [p_own]])
        probs.append(([_bf(p) for p in ps], _bf(p_own), l))
    for j, (ps, p_own, l) in enumerate(probs):
        acc = jnp.dot(p_own, _bf(vpad[j]), preferred_element_type=F32)
        for n in range(N_PAST_BLK):
            acc = acc + _dg(ps[n], _bf(vt_buf[j, :, blocks[n]]), 1, 1)
        o = jnp.where(own_head, acc * (1.0 / l), 0.0)
        o = jnp.sum(o.reshape(t_new, H_A, W_A), axis=1)
        o_ref[j] = o * _silu(ga_ref[j])


def _moba_kernel(pt_ref, tab_ref, q_ref, k_ref, vt_ref, km_ref, bias_ref, ga_ref,
                 qrep_ref, knew_ref, vnew_ref, gas_ref, bias_s_ref, bfar_ref, ck_hbm, cv_hbm,
                 o_ref, os_ref, rbf_ref, rbs_ref, qs_ref, kt_buf, vt_buf, kpad, vpad, sems):
    step = pl.program_id(0) * pl.num_programs(2) + pl.program_id(2)
    n_steps = pl.num_programs(0) * pl.num_programs(2)
    seq0 = SEQ_PER_TILE * step
    copies = functools.partial(_page_copies, pt_ref, ck_hbm, cv_hbm, kt_buf, vt_buf, sems)

    @pl.when(step == 0)
    def _():
        kpad[...] = jnp.zeros(kpad.shape, F32)
        vpad[...] = jnp.zeros(vpad.shape, F32)
        for j in range(SEQ_PER_TILE):
            for cp in copies(j, j):
                cp.start()

    for j in range(SEQ_PER_TILE):
        for cp in copies(seq0 + j, j):
            cp.wait()
    _moba_sample_seqs(qrep_ref, knew_ref, vnew_ref, gas_ref, bias_s_ref, bfar_ref, kt_buf, vt_buf, kpad, vpad, os_ref)
    _moba_prompt_select(tab_ref, q_ref, km_ref, rbf_ref, rbs_ref, qs_ref)

    @pl.when(step + 1 < n_steps)
    def _():
        for j in range(SEQ_PER_TILE):
            for cp in copies(seq0 + SEQ_PER_TILE + j, j):
                cp.start()

    _moba_prompt_attend(k_ref, vt_ref, bias_ref, ga_ref, o_ref, rbf_ref, rbs_ref, qs_ref)


def _moba(page_table, rel_bias, qa, kbf, vt, kmean, bias_t, ga,
          qrep, knew, vnew, ga_s, bias_s, bfar, cache_kt, cache_vt):
    nb, s, _ = qa.shape
    nq = s // MOBA_BLOCK
    nseq, t_new, _ = knew.shape
    assert H_A == MOBA_HS and nseq == SEQ_PER_TILE * nb * nq
    tile = pl.BlockSpec((1, MOBA_BLOCK, W_A), lambda b, hg, i, pt: (b, i, 0))
    per_b = lambda shp: pl.BlockSpec((1,) + shp, lambda b, hg, i, pt: (b, 0, 0))
    seqs = lambda r: pl.BlockSpec((SEQ_PER_TILE, r, W_A), lambda b, hg, i, pt: (b * nq + i, 0, 0))
    const = lambda a: pl.BlockSpec(a.shape, lambda b, hg, i, pt: (0,) * a.ndim)
    grid_spec = pltpu.PrefetchScalarGridSpec(
        num_scalar_prefetch=1,
        grid=(nb, 1, nq),
        in_specs=[pl.BlockSpec(memory_space=pltpu.SMEM),
                  tile, per_b((s, W_A)), per_b((H_A * V_ROWS, s)), per_b((nq, W_A)), const(bias_t), tile,
                  seqs(N_ROWS_S), seqs(t_new), seqs(t_new), seqs(t_new), const(bias_s), const(bfar),
                  pl.BlockSpec(memory_space=pl.ANY), pl.BlockSpec(memory_space=pl.ANY)],
        out_specs=(tile, seqs(t_new)),
        scratch_shapes=[pltpu.VMEM((MOBA_HS, nq, MOBA_BLOCK), F32),
                        pltpu.VMEM((MOBA_HS, nq, MOBA_BLOCK), F32),
                        pltpu.VMEM((MOBA_HS, MOBA_BLOCK, 128), BF16),
                        pltpu.VMEM((SEQ_PER_TILE, W_A, PAST_LEN), F32),
                        pltpu.VMEM((SEQ_PER_TILE, W_A, PAST_LEN), F32),
                        pltpu.VMEM((SEQ_PER_TILE, PAGE_SIZE, W_A), F32),
                        pltpu.VMEM((SEQ_PER_TILE, PAGE_SIZE, W_A), F32),
                        pltpu.SemaphoreType.DMA((2, SEQ_PER_TILE))],
    )
    return pl.pallas_call(
        _moba_kernel,
        grid_spec=grid_spec,
        out_shape=(jax.ShapeDtypeStruct((nb, s, W_A), BF16),
                   jax.ShapeDtypeStruct((nseq, t_new, W_A), F32)),
        compiler_params=_cparams(3),
        name="moba",
    )(page_table, rel_bias, qa, kbf, vt, kmean, bias_t, ga, qrep, knew, vnew, ga_s, bias_s, bfar, cache_kt, cache_vt)


def _pad_rows(ref, scratch, s, t):
    if t == CHUNK:
        return ref[s]
    scratch[s] = jnp.zeros(scratch.shape[1:], scratch.dtype)
    scratch[s, 0:t, :] = ref[s].astype(scratch.dtype)
    return scratch[s]


def _ret_kernel(q_ref, k_ref, v_ref, g_ref, st0_ref, dmat_ref, qdec_ref, kdec_ref, gl_ref,
                o_ref, st_ref, qpad, kpad, vpad, gpad, *, t, lq, nbs):
    c = pl.program_id(1)

    @pl.when(c == 0)
    def _():
        st_ref[...] = st0_ref[...]

    lane = lax.broadcasted_iota(jnp.int32, (1, 128), 1)
    rowsel = lax.broadcasted_iota(jnp.int32, (128, 1), 0) < DK_B
    work = []
    for s in range(nbs):
        q = _pad_rows(q_ref, qpad, s, t)[0:lq]
        k = _pad_rows(k_ref, kpad, s, t)
        v = _pad_rows(v_ref, vpad, s, t)
        for hp in range(H_B // 2):
            cols = slice(hp * 128, (hp + 1) * 128)
            kp = _bf(k[:, cols])
            st = st_ref[s, cols, :]
            kd = _bf(k[:, cols] * kdec_ref[0:k.shape[0], cols])
            for hh in range(2):
                h = 2 * hp + hh
                qm = jnp.where((lane // DK_B) == hh, q[:, cols], 0.0)
                vh = _bf(v[:, h * DV_B:(h + 1) * DV_B])
                sc = _dg(_bf(qm), kp, 1, 1)
                so = _mm(qm * qdec_ref[0:lq, cols], st)
                upd = _dg(kd, vh, 0, 0)
                work.append((s, hp, hh, sc, so, upd, vh, st))
    outs = {}
    for (s, hp, hh, sc, so, upd, vh, st) in work:
        h = 2 * hp + hh
        o = jnp.dot(_bf(sc * dmat_ref[h, 0:lq, 0:sc.shape[1]]), vh, preferred_element_type=F32) + so
        outs[(s, h)] = o * lax.rsqrt(jnp.mean(o * o, axis=-1, keepdims=True) + EPS)
    for i in range(0, len(work), 2):
        s, hp, _, _, _, upd0, _, st = work[i]
        cols = slice(hp * 128, (hp + 1) * 128)
        st_ref[s, cols, :] = st * gl_ref[cols, :] + jnp.where(rowsel, upd0, work[i + 1][5])
    for s in range(nbs):
        g = _pad_rows(g_ref, gpad, s, t)[0:lq].astype(F32)
        o = jnp.concatenate([outs[(s, h)] for h in range(H_B)], axis=1)
        o_ref[s] = (o * _silu(g))[0:t].astype(o_ref.dtype)


def _retention(q, k, v, g, st0, dmat, qdec, kdec, gl, lq, nbs):
    nb, nc, t, _ = q.shape
    row = lambda c_: pl.BlockSpec((nbs, None, t, c_), lambda b, c: (b, c, 0, 0))
    const = lambda a: pl.BlockSpec(a.shape, lambda b, c: (0,) * a.ndim)
    st_spec = pl.BlockSpec((nbs, H_B * DK_B, DV_B), lambda b, c: (b, 0, 0))
    pad = lambda c_: pltpu.VMEM((nbs, t if t == CHUNK else SHORT_ROWS, c_), F32)
    return pl.pallas_call(
        functools.partial(_ret_kernel, t=t, lq=lq, nbs=nbs),
        grid=(nb // nbs, nc),
        in_specs=[row(256), row(256), row(512), row(512), st_spec,
                  const(dmat), const(qdec), const(kdec), const(gl)],
        out_specs=(row(512), st_spec),
        out_shape=(jax.ShapeDtypeStruct((nb, nc, t, 512), q.dtype),
                   jax.ShapeDtypeStruct((nb, H_B * DK_B, DV_B), F32)),
        scratch_shapes=[pad(256), pad(256), pad(512), pad(512)],
        compiler_params=_cparams(2),
        name="retention",
    )(q, k, v, g, st0, dmat, qdec, kdec, gl)


def _out_kernel(a_ref, b_ref, x_ref, g_ref, w_ref, o_ref):
    half = w_ref.shape[0] // 2
    y = (jnp.dot(_bf(a_ref[0]), w_ref[0:half, :], preferred_element_type=F32)
         + jnp.dot(_bf(b_ref[0]), w_ref[half:, :], preferred_element_type=F32))
    o_ref[0] = x_ref[0] + g_ref[0] * y


def _out_proj(a, b, x, gate, w_bf, per_row_mod):
    nb, s, d = x.shape
    tm = min(OUT_ROWS, s)
    if per_row_mod:
        g_spec = pl.BlockSpec((1, tm, d), lambda bb, i: (bb, i, 0))
    else:
        g_spec = pl.BlockSpec((1, 1, d), lambda bb, i: (bb, 0, 0))
    row = lambda c: pl.BlockSpec((1, tm, c), lambda bb, i: (bb, i, 0))
    return pl.pallas_call(
        _out_kernel,
        grid=(nb, s // tm),
        in_specs=[row(512), row(512), row(d), g_spec, pl.BlockSpec(w_bf.shape, lambda bb, i: (0, 0))],
        out_specs=row(d),
        out_shape=jax.ShapeDtypeStruct((nb, s, d), F32),
        compiler_params=_cparams(2),
        name="out_proj",
    )(a, b, x, gate, w_bf)


def _odd_in_kernel(x_ref, sc_ref, sh_ref, nw_ref, w_ref, wdt_ref, sguw_ref, sgub_ref, dtb_ref,
                   oc_ref, zg_ref, xbc_ref, dt_ref, *maybe_v_ref):
    x = x_ref[0]
    tm = x.shape[0]
    ms = jnp.mean(x * x, axis=-1, keepdims=True)
    h = (x * lax.rsqrt(ms + EPS) * nw_ref[...]) * (1.0 + sc_ref[0]) + sh_ref[0]
    hb = _bf(h)

    def proj(lo, hi):
        return jnp.dot(hb, w_ref[:, lo:hi], preferred_element_type=F32)

    u = _gelu_tanh(proj(0, 512))
    v = _gelu_tanh(proj(512, 1024))
    mu = jnp.mean(v, axis=-1, keepdims=True)
    vc = v - mu
    v = vc * lax.rsqrt(jnp.mean(vc * vc, axis=-1, keepdims=True) + EPS)
    for v_ref in maybe_v_ref:
        v_ref[0] = v
    ii = lax.broadcasted_iota(jnp.int32, (CHUNK, CHUNK), 0)
    jj = lax.broadcasted_iota(jnp.int32, (CHUNK, CHUNK), 1)
    rows = []
    for ci in range(tm // CHUNK):
        cols = []
        for g in range(G_C):
            wg = jnp.where(ii >= jj, sguw_ref[g], 0.0)
            cols.append(_mm(wg, v[ci * CHUNK:(ci + 1) * CHUNK, g * 128:(g + 1) * 128]))
        rows.append(jnp.concatenate(cols, axis=1) + sgub_ref[...])
    sg = jnp.concatenate(rows, axis=0) if len(rows) > 1 else rows[0]
    oc_ref[0] = (u * sg * _silu(proj(1024, 1536))).astype(oc_ref.dtype)
    zg_ref[0] = proj(1536, 2048).astype(zg_ref.dtype)
    xbc_ref[0] = proj(2048, 3072)
    dt_ref[0] = _softplus(jnp.dot(hb, wdt_ref[...], preferred_element_type=F32) + dtb_ref[...])


def _odd_in(x, scale, shift, norm_w, w_bf, w_dt, sgu_w, sgu_b_tab, dt_bias, per_row_mod, act, emit_v):
    nb, s, d = x.shape
    tm = PROJ_ROWS
    if per_row_mod:
        mod_spec = pl.BlockSpec((1, tm, d), lambda b, i: (b, i, 0))
    else:
        mod_spec = pl.BlockSpec((1, 1, d), lambda b, i: (b, 0, 0))
    row = lambda c: pl.BlockSpec((1, tm, c), lambda b, i: (b, i, 0))
    const = lambda shp: pl.BlockSpec(shp, lambda b, i: (0,) * len(shp))
    return pl.pallas_call(
        _odd_in_kernel,
        grid=(nb, s // tm),
        in_specs=[row(d), mod_spec, mod_spec, const((1, d)), const((d, 3072)), const(w_dt.shape),
                  const(sgu_w.shape), const(sgu_b_tab.shape), const((1, 512))],
        out_specs=(row(512), row(512), row(1024), row(512)) + ((row(512),) if emit_v else ()),
        out_shape=(jax.ShapeDtypeStruct((nb, s, 512), act),
                   jax.ShapeDtypeStruct((nb, s, 512), act),
                   jax.ShapeDtypeStruct((nb, s, 1024), F32),
                   jax.ShapeDtypeStruct((nb, s, 512), F32),
                   ) + ((jax.ShapeDtypeStruct((nb, s, 512), F32),) if emit_v else ()),
        compiler_params=_cparams(2),
        name="odd_in",
    )(x, scale, shift, norm_w, w_bf, w_dt, sgu_w, sgu_b_tab, dt_bias)


def _ssd_kernel(xbc_ref, dt_ref, zg_ref, tail_ref, st0_ref, cw_ref, cb_ref, alog_ref, dsk_ref, nw_ref,
                tri_ref, sel_ref, y_ref, st_ref, ext, dtpad, zpad, *, t, lq, nc, nbs):
    rows = dtpad.shape[1]
    c = pl.program_id(1)

    @pl.when(c == 0)
    def _():
        st_ref[...] = st0_ref[...]
        ext[...] = jnp.zeros(ext.shape, F32)
        ext[:, 0:8, :] = tail_ref[...]

    ii = lax.broadcasted_iota(jnp.int32, (lq, rows), 0)
    jj = lax.broadcasted_iota(jnp.int32, (lq, rows), 1)
    lane = lax.broadcasted_iota(jnp.int32, (1, 128), 1)
    hpg = H_D // G_D
    neg_a = -jnp.exp(alog_ref[...])
    seqs = []
    for s in range(nbs):
        ext[s, 8:8 + t, :] = xbc_ref[s]
        conv = cb_ref[...]
        for w in range(CONV_W):
            conv = conv + ext[s, pl.ds(8 - (CONV_W - 1) + w, rows), :] * cw_ref[w:w + 1, :]
        if nc > 1:
            ext[s, 0:8, :] = ext[s, rows:rows + 8, :]
        xc = _silu(conv)
        dt = _pad_rows(dt_ref, dtpad, s, t)
        cum = _mm_exact_lhs(tri_ref[0:rows, 0:rows], dt * neg_a, 1, 0)
        seqs.append((xc, dt, cum))
    st1 = []
    for s in range(nbs):
        xc, dt, cum = seqs[s]
        xh = xc[:, 0:W_D]
        last = cum[rows - 1:rows, :]
        dtx = _bf(xh * dt)
        xw = _bf(xh * (jnp.exp(last - cum) * dt))
        cum_rows = _mm_exact_lhs(sel_ref[...], cum, 1, 1)
        per_g = []
        for g in range(G_D):
            bg = _bf(xc[:, W_D + g * N_D:W_D + (g + 1) * N_D])
            cg = _bf(xc[0:lq, W_D + G_D * N_D + g * N_D:W_D + G_D * N_D + (g + 1) * N_D])
            gr = slice(g * hpg * P_D, (g + 1) * hpg * P_D)
            cb = _dg(cg, bg, 1, 1)
            yoff = _dg(cg, _bf(st_ref[s, gr, :]), 1, 1)
            upd = _dg(xw[:, gr], bg, 0, 0)
            per_g.append((cb, yoff, upd))
        st1.append((dtx, cum_rows, per_g))
    for s in range(nbs):
        xc, dt, cum = seqs[s]
        dtx, cum_rows, per_g = st1[s]
        ecum = jnp.exp(cum[0:lq])
        elast = jnp.exp(cum[rows - 1:rows, :])
        ys = []
        for g in range(G_D):
            cb, yoff, upd = per_g[g]
            for pr in range(hpg // 2):
                l0 = g * hpg * P_D + pr * 128
                yh = []
                for hh in range(2):
                    h = g * hpg + pr * 2 + hh
                    col = jnp.broadcast_to(cum[0:lq, h * P_D:h * P_D + 1], (lq, rows))
                    seg = jnp.minimum(col - cum_rows[h:h + 1, :], 0.0)
                    mh = jnp.where(ii >= jj, cb * jnp.exp(seg), 0.0)
                    yh.append(jnp.dot(_bf(mh), dtx[:, l0:l0 + 128], preferred_element_type=F32))
                ypair = jnp.where(lane < P_D, yh[0], yh[1])
                ys.append(ypair + yoff[:, pr * 128:(pr + 1) * 128] * ecum[:, l0:l0 + 128])
            for hl in range(hpg):
                h = g * hpg + hl
                r = slice(h * P_D, (h + 1) * P_D)
                dec = jnp.broadcast_to(elast[0:1, h * P_D:h * P_D + 1], (P_D, N_D))
                st_ref[s, r, :] = st_ref[s, r, :] * dec + upd[hl * P_D:(hl + 1) * P_D, :]
        y = jnp.concatenate(ys, axis=1)
        zg = _pad_rows(zg_ref, zpad, s, t)[0:lq].astype(F32)
        y = (y + xc[0:lq, 0:W_D] * dsk_ref[...]) * _silu(zg)
        gw = W_D // G_D
        outs = []
        for g in range(G_D):
            yg = y[:, g * gw:(g + 1) * gw]
            outs.append(yg * lax.rsqrt(jnp.mean(yg * yg, axis=-1, keepdims=True) + EPS))
        y_ref[s] = (jnp.concatenate(outs, axis=1) * nw_ref[...])[0:t].astype(y_ref.dtype)


def _ssd(xbc, dt, zg, tail, st0, conv_w, conv_b, a_log, d_skip, norm_w, tri, sel, lq, nbs):
    nb, nc, t, _ = xbc.shape
    rows = t if t == CHUNK else SHORT_ROWS
    row = lambda c_: pl.BlockSpec((nbs, None, t, c_), lambda b, c: (b, c, 0, 0))
    const = lambda a: pl.BlockSpec(a.shape, lambda b, c: (0,) * a.ndim)
    st_spec = pl.BlockSpec((nbs, H_D * P_D, N_D), lambda b, c: (b, 0, 0))
    return pl.pallas_call(
        functools.partial(_ssd_kernel, t=t, lq=lq, nc=nc, nbs=nbs),
        grid=(nb // nbs, nc),
        in_specs=[row(1024), row(512), row(512),
                  pl.BlockSpec((nbs, 8, CONV_DIM), lambda b, c: (b, 0, 0)), st_spec,
                  const(conv_w), const(conv_b), const(a_log), const(d_skip), const(norm_w),
                  const(tri), const(sel)],
        out_specs=(row(512), st_spec),
        out_shape=(jax.ShapeDtypeStruct((nb, nc, t, 512), zg.dtype),
                   jax.ShapeDtypeStruct((nb, H_D * P_D, N_D), F32)),
        scratch_shapes=[pltpu.VMEM((nbs, rows + 8, CONV_DIM), F32),
                        pltpu.VMEM((nbs, rows, 512), F32), pltpu.VMEM((nbs, rows, 512), F32)],
        compiler_params=_cparams(2),
        name="ssd",
    )(xbc, dt, zg, tail, st0, conv_w, conv_b, a_log, d_skip, norm_w, tri, sel)


def _rotary_tables(pos):
    half = DK_B // 2
    inv = (np.float32(1.0) / np.float32(10000.0) ** (np.arange(half, dtype=np.float32) / np.float32(half)))
    ang = (np.asarray(pos).astype(np.float32)[:, None] * inv.astype(np.float32)[None, :]).astype(np.float64)
    cos, sin = np.cos(ang), np.sin(ang)
    cos_t = np.tile(np.concatenate([cos, cos], axis=1), (1, H_B))
    sin_t = np.tile(np.concatenate([-sin, sin], axis=1), (1, H_B))
    return jnp.asarray(cos_t, F32), jnp.asarray(sin_t, F32)


def _retention_tables(chunk_len):
    log_g = np.log(1.0 - 2.0 ** (-5.0 - np.arange(H_B, dtype=np.float64)))
    idx = np.arange(CHUNK, dtype=np.float64)
    diff = idx[:, None] - idx[None, :]
    dmat = np.where(diff[None] >= 0, np.exp(np.maximum(diff, 0.0)[None] * log_g[:, None, None]), 0.0)
    qdec = np.exp((idx + 1.0)[:, None] * log_g[None, :])
    kdec = np.where(idx[:, None] < chunk_len, np.exp((chunk_len - 1.0 - idx)[:, None] * log_g[None, :]), 0.0)
    gl = np.exp(chunk_len * log_g)
    return (jnp.asarray(dmat, F32),
            jnp.asarray(np.repeat(qdec, DK_B, axis=1), F32),
            jnp.asarray(np.repeat(kdec, DK_B, axis=1), F32),
            jnp.asarray(np.repeat(np.repeat(gl, DK_B)[:, None], DV_B, axis=1), F32))


def _prompt_bias_idx():
    kk = np.arange(MOBA_BLOCK)[:, None]
    qq = np.arange(MOBA_BLOCK)[None, :]
    diag = np.where(qq >= kk, _t5_bucket_np(qq - kk), -1)
    sub = _t5_bucket_np(qq + MOBA_BLOCK - kk)
    return np.concatenate([diag, sub], axis=0).astype(np.int32)


def _sample_bias_idx(t_new):
    row_t = (np.arange(N_ROWS_S) // H_A)[:, None]
    qpos = PAST_LEN + row_t
    near = _t5_bucket_np(qpos - (PAST_LEN - MOBA_BLOCK + np.arange(MOBA_BLOCK))[None, :])
    own_k = np.arange(PAGE_SIZE)[None, :]
    own = np.where((own_k <= row_t) & (own_k < t_new), _t5_bucket_np(row_t - own_k), -1)
    return np.concatenate([near, own], axis=1).astype(np.int32)


def kernel(x_prompt, x_sample, cache_k, cache_v, state_ret, state_ssm, state_conv, page_table, c_prompt, c_sample, rel_bias, e_norm_w, e_ada_w, e_ada_b, e_in_w, e_q_norm_w, e_k_norm_w, e_out_w, o_norm_w, o_ada_w, o_ada_b, o_in_w, o_sgu_w, o_sgu_b, o_conv_w, o_conv_b, o_dt_bias, o_A_log, o_D, o_ssm_norm_w, o_out_w):
    bp, s_len, d = x_prompt.shape
    bs, t_len, _ = x_sample.shape
    n_s = bs * t_len

    c_all = jnp.concatenate([jnp.repeat(c_sample, t_len, axis=0), c_prompt,
                             jnp.zeros((8 - (bp + n_s) % 8, d), F32)], axis=0)
    mods = []
    for ada_w, ada_b in ((e_ada_w[0], e_ada_b[0]), (o_ada_w[0], o_ada_b[0])):
        mod = _ada_mod(c_all, ada_w, ada_b)
        parts_p = [mod[i, n_s:n_s + bp].reshape(bp, 1, d) for i in range(3)]
        parts_s = [mod[i, :n_s].reshape(1, n_s, d) for i in range(3)]
        mods.append((parts_p, parts_s))
    (e_mod_p, e_mod_s), (o_mod_p, o_mod_s) = mods

    seg = jnp.asarray(np.kron(np.eye(H_A), np.ones((HD_A, HD_A))), BF16)
    qnw = jnp.tile(e_q_norm_w[0], H_A).reshape(1, W_A)
    knw = jnp.tile(e_k_norm_w[0], H_A).reshape(1, W_A)
    e_in_bf = _bf(e_in_w[0])
    e_out_bf = _bf(e_out_w[0])
    o_in_bf = _bf(o_in_w[0])
    o_dt_bf = _bf(jnp.repeat(o_in_w[0][:, 3072:], P_D, axis=1))
    o_out_bf = _bf(o_out_w[0])
    x_s = x_sample.reshape(1, n_s, d)
    cos_p, sin_p = _rotary_tables(np.arange(s_len))
    cos_s, sin_s = _rotary_tables(PAST_LEN + (np.arange(n_s) % t_len))
    bias_p = _bias_tables(rel_bias, _prompt_bias_idx(), LOG2E).reshape(H_A, 2, MOBA_BLOCK, MOBA_BLOCK)
    bias_s_h = _bias_tables(rel_bias, _sample_bias_idx(t_len))
    row_h = jnp.arange(N_ROWS_S) % H_A
    bias_s = jnp.sum(jnp.where((jnp.arange(H_A)[:, None] == row_h[None, :])[:, :, None], bias_s_h, 0.0), axis=0)
    bfar = rel_bias[NUM_BUCKETS - 1, row_h].reshape(N_ROWS_S, 1)

    (qa, ka, va, kbf, vt, ga, qb, kb, vb, gb, kmean) = _even_in(
        x_prompt, e_mod_p[1], e_mod_p[0], e_norm_w[0].reshape(1, d), e_in_bf, qnw, knw, seg, cos_p, sin_p, False, BF16)
    (qa_s, ka_s, va_s, _, _, ga_s, qb_s, kb_s, vb_s, gb_s, _) = _even_in(
        x_s, e_mod_s[1], e_mod_s[0], e_norm_w[0].reshape(1, d), e_in_bf, qnw, knw, seg, cos_s, sin_s, True, F32)
    sq = lambda a: a.reshape(bs, t_len, a.shape[-1])
    qrep = jnp.repeat(sq(qa_s), H_A, axis=1)
    n_phys = cache_k.shape[1]
    page_t = lambda c: jnp.transpose(c[0], (0, 2, 3, 1)).reshape(n_phys, W_A, PAGE_SIZE)
    oa, oa_s = _moba(page_table, rel_bias, qa, kbf, vt, kmean.reshape(bp, s_len // MOBA_BLOCK, W_A), bias_p, ga,
                     qrep, sq(ka_s), sq(va_s), sq(ga_s), bias_s, bfar, page_t(cache_k), page_t(cache_v))
    nc_p = s_len // CHUNK
    ch = lambda a: a.reshape(bp, nc_p, CHUNK, a.shape[-1])
    ob, ret_p = _retention(ch(qb), ch(kb), ch(vb), ch(gb), jnp.zeros((bp, H_B * DK_B, DV_B), F32),
                           *_retention_tables(CHUNK), lq=CHUNK, nbs=bp)
    xp1 = _out_proj(oa, ob.reshape(bp, s_len, W_B), x_prompt, e_mod_p[2], e_out_bf, False)
    k_prompt = ka.reshape(1, bp, s_len, H_A, HD_A)
    v_prompt = va.reshape(1, bp, s_len, H_A, HD_A)
    ret_state_prompt = ret_p.reshape(1, bp, H_B, DK_B, DV_B)

    sc = lambda a: a.reshape(bs, 1, t_len, a.shape[-1])
    ob_s, ret_s = _retention(sc(qb_s), sc(kb_s), sc(vb_s), sc(gb_s),
                             state_ret[0].reshape(bs, H_B * DK_B, DV_B), *_retention_tables(t_len), lq=8, nbs=SEQ_PER_STEP)
    xs1 = _out_proj(oa_s.reshape(1, n_s, W_A), ob_s.reshape(1, n_s, W_B), x_s, e_mod_s[2], e_out_bf, True)
    k_sample = ka_s.reshape(1, bs, t_len, H_A, HD_A)
    v_sample = va_s.reshape(1, bs, t_len, H_A, HD_A)
    ret_state_sample = ret_s.reshape(1, bs, H_B, DK_B, DV_B)

    tri = jnp.asarray(np.tril(np.ones((CHUNK, CHUNK))), BF16)
    sel = jnp.asarray(np.kron(np.eye(H_D), np.eye(1, P_D)), BF16)
    rep = lambda a: jnp.repeat(a, P_D).reshape(1, W_D)
    dt_bias, a_log, d_skip = rep(o_dt_bias[0]), rep(o_A_log[0]), rep(o_D[0])
    ssm_nw = o_ssm_norm_w[0].reshape(1, W_D)
    conv_b = o_conv_b[0].reshape(1, CONV_DIM)
    o_nw = o_norm_w[0].reshape(1, d)

    sgu_b_p = jnp.repeat(o_sgu_b[0].T, W_C // G_C, axis=1)
    oc, zg, xbc, dtp = _odd_in(xp1, o_mod_p[1], o_mod_p[0], o_nw, o_in_bf, o_dt_bf, o_sgu_w[0], sgu_b_p, dt_bias,
                               False, BF16, False)
    yn, ssm_p = _ssd(ch(xbc), ch(dtp), ch(zg), jnp.zeros((bp, 8, CONV_DIM), F32),
                     jnp.zeros((bp, H_D * P_D, N_D), F32), o_conv_w[0], conv_b, a_log, d_skip, ssm_nw,
                     tri, sel, lq=CHUNK, nbs=bp)
    y_prompt = _out_proj(oc, yn.reshape(bp, s_len, W_D), xp1, o_mod_p[2], o_out_bf, False)
    ssm_state_prompt = ssm_p.reshape(1, bp, H_D, P_D, N_D)
    conv_state_prompt = xbc[:, -(CONV_W - 1):][None]

    per_chunk = CHUNK // t_len
    w_small = o_sgu_w[0][:, :t_len, :t_len]
    same_seq = jnp.asarray(np.kron(np.eye(per_chunk), np.ones((t_len, t_len))), F32)
    tok_of_row = jnp.asarray(np.tile(np.eye(t_len), (per_chunk, 1)), F32)
    sgu_w_s = jnp.einsum('ia,gab,jb->gij', tok_of_row, w_small, tok_of_row,
                         precision=lax.Precision.HIGHEST) * same_seq
    sgu_b_s = jnp.repeat(jnp.tile(o_sgu_b[0][:, :t_len].T, (per_chunk, 1)), W_C // G_C, axis=1)
    oc_s, zg_s, xbc_s, dt_s, v_s = _odd_in(xs1, o_mod_s[1], o_mod_s[0], o_nw, o_in_bf, o_dt_bf, sgu_w_s, sgu_b_s, dt_bias,
                                           True, F32, True)
    tail_s = jnp.concatenate([jnp.zeros((bs, 8 - (CONV_W - 1), CONV_DIM), F32), state_conv[0]], axis=1)
    yn_s, ssm_s = _ssd(sc(xbc_s), sc(dt_s), sc(zg_s), tail_s, state_ssm[0].reshape(bs, H_D * P_D, N_D),
                       o_conv_w[0], conv_b, a_log, d_skip, ssm_nw, tri, sel, lq=8, nbs=SEQ_PER_STEP)
    xs2 = _out_proj(oc_s, yn_s.reshape(1, n_s, W_D), xs1, o_mod_s[2], o_out_bf, True)
    y_sample = xs2.reshape(bs, t_len, d)
    sgu_v_sample = v_s.reshape(1, bs, t_len, W_C)
    ssm_state_sample = ssm_s.reshape(1, bs, H_D, P_D, N_D)
    xin = jnp.concatenate([state_conv[0], xbc_s.reshape(bs, t_len, CONV_DIM)], axis=1)
    conv_state_sample = xin[:, -(CONV_W - 1):][None]

    return (y_prompt, y_sample, k_prompt, v_prompt, k_sample, v_sample, ret_state_prompt, ret_state_sample,
            sgu_v_sample, ssm_state_prompt, ssm_state_sample, conv_state_prompt, conv_state_sample)
```

```python
import functools
import math

import numpy as np
import jax
import jax.numpy as jnp
from jax import lax
from jax.experimental import pallas as pl
from jax.experimental.pallas import tpu as pltpu

F32 = jnp.float32
BF16 = jnp.bfloat16

PAST_LEN = 2048
PAGE_SIZE = 128
H_A, HD_A, W_A = 8, 64, 512
MOBA_BLOCK = 256
MOBA_TOPK = 3
NUM_BUCKETS = 32
MAX_DISTANCE = 128
H_B, DK_B, DV_B, W_B = 4, 64, 128, 512
G_C, W_C = 4, 512
H_D, P_D, N_D, G_D, W_D = 8, 64, 128, 2, 512
CONV_W = 4
CONV_DIM = 1024
CHUNK = 128
SEQ_PER_STEP = 8
SHORT_ROWS = 16
PROJ_ROWS = 512
OUT_ROWS = 1024
NEG_INF = -1e30
EPS = 1e-6
LOG2E = math.log2(math.e)
VMEM_LIMIT = 56 * 1024 * 1024


def _bf(x):
    return x.astype(BF16)


def _dg(a, b, ca, cb):
    return lax.dot_general(a, b, (((ca,), (cb,)), ((), ())), preferred_element_type=F32)


def _mm(a, b):
    return _dg(_bf(a), _bf(b), 1, 0)


def _split2(x):
    hi = _bf(x)
    return hi, _bf(x - hi.astype(F32))


def _split3(x):
    hi = _bf(x)
    r = x - hi.astype(F32)
    mid = _bf(r)
    return hi, mid, _bf(r - mid.astype(F32))


def _mm_hp(a, b, ca, cb):
    ah, al = _split2(a)
    bh, bl = _split2(b)
    return _dg(ah, bh, ca, cb) + (_dg(ah, bl, ca, cb) + _dg(al, bh, ca, cb))


def _mm_exact_lhs(e, x, ca, cb):
    h, m, l = _split3(x)
    return _dg(e, h, ca, cb) + (_dg(e, m, ca, cb) + _dg(e, l, ca, cb))


def _silu(x):
    return x * (1.0 / (1.0 + jnp.exp(-x)))


def _gelu_tanh(x):
    return 0.5 * x * (1.0 + jnp.tanh(math.sqrt(2.0 / math.pi) * (x + 0.044715 * (x * x * x))))


def _softplus(x):
    return jnp.maximum(x, 0.0) + jnp.log1p(jnp.exp(-jnp.abs(x)))


def _cparams(n_grid):
    return pltpu.CompilerParams(dimension_semantics=("arbitrary",) * n_grid,
                                vmem_limit_bytes=VMEM_LIMIT)


def _top3_rows(g, blk, nblk):
    sel = jnp.zeros(g.shape, jnp.bool_)
    for _ in range(MOBA_TOPK):
        m = jnp.max(g, axis=0, keepdims=True)
        idx = jnp.min(jnp.where(g == m, blk, nblk), axis=0, keepdims=True)
        pick = blk == idx
        sel = jnp.logical_or(sel, pick)
        g = jnp.where(pick, -jnp.inf, g)
    return sel


def _ada_kernel(c_ref, w_ref, b_ref, o_ref):
    s = _silu(c_ref[...])
    o_ref[0] = _mm_hp(s, w_ref[...], 1, 0) + b_ref[...]


def _ada_mod(c_all, w, b):
    m, d = c_all.shape
    n_parts = w.shape[1] // d
    return pl.pallas_call(
        _ada_kernel,
        grid=(n_parts,),
        in_specs=[pl.BlockSpec((m, d), lambda j: (0, 0)),
                  pl.BlockSpec((d, d), lambda j: (0, j)),
                  pl.BlockSpec((1, d), lambda j: (0, j))],
        out_specs=pl.BlockSpec((1, m, d), lambda j: (j, 0, 0)),
        out_shape=jax.ShapeDtypeStruct((n_parts, m, d), F32),
        compiler_params=_cparams(1),
        name="ada_mod",
    )(c_all, w, b.reshape(1, n_parts * d))


def _t5_bucket_np(rel):
    n = np.maximum(rel, 0)
    max_exact = NUM_BUCKETS // 2
    nf = np.maximum(n, 1).astype(np.float64)
    large = max_exact + (np.log(nf / max_exact) / math.log(MAX_DISTANCE / max_exact)
                         * (NUM_BUCKETS - max_exact)).astype(np.int64)
    large = np.minimum(large, NUM_BUCKETS - 1)
    return np.where(n < max_exact, n, large).astype(np.int32)


def _bias_kernel(tab_ref, idx_ref, o_ref, *, scale):
    h = pl.program_id(0)
    idx = idx_ref[...]
    acc = jnp.zeros(idx.shape, F32)
    for b in range(NUM_BUCKETS):
        acc = jnp.where(idx == b, tab_ref[b, h], acc)
    o_ref[0] = jnp.where(idx == -1, NEG_INF, acc * scale)


def _bias_tables(rel_bias, idx, scale=1.0):
    r, c = idx.shape
    return pl.pallas_call(
        functools.partial(_bias_kernel, scale=scale),
        grid=(H_A,),
        in_specs=[pl.BlockSpec(memory_space=pltpu.SMEM),
                  pl.BlockSpec((r, c), lambda h: (0, 0))],
        out_specs=pl.BlockSpec((1, r, c), lambda h: (h, 0, 0)),
        out_shape=jax.ShapeDtypeStruct((H_A, r, c), F32),
        compiler_params=_cparams(1),
        name="t5_bias",
    )(rel_bias, jnp.asarray(idx))


def _even_in_kernel(x_ref, sc_ref, sh_ref, nw_ref, w_ref, qnw_ref, knw_ref, seg_ref, cos_ref, sin_ref,
                    qa_ref, ka_ref, va_ref, kbf_ref, vt_ref, ga_ref, qb_ref, kb_ref, vb_ref, gb_ref, km_ref):
    x = x_ref[0]
    ms = jnp.mean(x * x, axis=-1, keepdims=True)
    h = (x * lax.rsqrt(ms + EPS) * nw_ref[...]) * (1.0 + sc_ref[0]) + sh_ref[0]
    hb = _bf(h)

    def proj(lo, hi):
        return jnp.dot(hb, w_ref[:, lo:hi], preferred_element_type=F32)

    def head_rms(t, w_row):
        ss = jnp.dot(_bf(t * t), seg_ref[...], preferred_element_type=F32)
        return t * lax.rsqrt(ss * (1.0 / HD_A) + EPS) * w_row

    qa_ref[0] = head_rms(proj(0, 512), qnw_ref[...])
    ka = head_rms(proj(512, 1024), knw_ref[...])
    ka_ref[0] = ka
    kbf_ref[0] = _bf(ka)
    for j in range(ka.shape[0] // MOBA_BLOCK):
        km_ref[0, j] = jnp.mean(ka[j * MOBA_BLOCK:(j + 1) * MOBA_BLOCK], axis=0, keepdims=True)
    va = proj(1024, 1536)
    va_ref[0] = va
    vat = va.T
    ones_pad = (lax.broadcasted_iota(jnp.int32, (V_ROWS - HD_A, vat.shape[1]), 0) == 0).astype(F32)
    vt_ref[0] = _bf(jnp.concatenate(
        [piece for h in range(H_A) for piece in (vat[h * HD_A:(h + 1) * HD_A], ones_pad)], axis=0))
    ga_ref[0] = proj(1536, 2048).astype(ga_ref.dtype)

    lane = lax.broadcasted_iota(jnp.int32, (1, 256), 1) % DK_B
    first_half = lane < (DK_B // 2)
    cos = cos_ref[...]
    sin = sin_ref[...]

    def rotary(t):
        up = pltpu.roll(t, 256 - DK_B // 2, 1)
        dn = pltpu.roll(t, DK_B // 2, 1)
        return t * cos + jnp.where(first_half, up, dn) * sin

    qb_ref[0] = rotary(proj(2048, 2304)).astype(qb_ref.dtype)
    kb_ref[0] = (rotary(proj(2304, 2560)) * (DK_B ** -0.5)).astype(kb_ref.dtype)
    vb_ref[0] = proj(2560, 3072).astype(vb_ref.dtype)
    gb_ref[0] = proj(3072, 3584).astype(gb_ref.dtype)


def _mod_spec(tm, d, slab):
    if slab is None:
        return pl.BlockSpec((1, 1, d), lambda b, i: (b, 0, 0))
    return pl.BlockSpec((1, tm, d), lambda b, i: (slab, i, 0))


def _even_in(x, scale, shift, norm_w, w_bf, qnw, knw, seg, cos, sin, slabs, act):
    nb, s, d = x.shape
    tm = min(PROJ_ROWS, s // 2)
    ns = s // tm
    nkb = tm // MOBA_BLOCK
    sc_spec, sh_spec = (_mod_spec(tm, d, p) for p in (slabs or (None, None)))
    row = lambda c: pl.BlockSpec((1, tm, c), lambda b, i: (b, i, 0))
    const = lambda shp: pl.BlockSpec(shp, lambda b, i: (0,) * len(shp))
    out_shape = (
        jax.ShapeDtypeStruct((nb, s, 512), F32),
        jax.ShapeDtypeStruct((nb, s, 512), F32),
        jax.ShapeDtypeStruct((nb, s, 512), F32),
        jax.ShapeDtypeStruct((nb, s, 512), BF16),
        jax.ShapeDtypeStruct((nb, H_A * V_ROWS, s), BF16),
        jax.ShapeDtypeStruct((nb, s, 512), act),
        jax.ShapeDtypeStruct((nb, s, 256), act),
        jax.ShapeDtypeStruct((nb, s, 256), act),
        jax.ShapeDtypeStruct((nb, s, 512), act),
        jax.ShapeDtypeStruct((nb, s, 512), act),
        jax.ShapeDtypeStruct((nb, ns * nkb, 1, 512), F32),
    )
    out_specs = (row(512), row(512), row(512), row(512),
                 pl.BlockSpec((1, H_A * V_ROWS, tm), lambda b, i: (b, 0, i)),
                 row(512), row(256), row(256), row(512), row(512),
                 pl.BlockSpec((1, nkb, 1, 512), lambda b, i: (b, i, 0, 0)))
    return pl.pallas_call(
        _even_in_kernel,
        grid=(nb, ns),
        in_specs=[row(d), sc_spec, sh_spec, const((1, d)), const((d, 3584)),
                  const((1, 512)), const((1, 512)), const((512, 512)),
                  pl.BlockSpec((tm, 256), lambda b, i: (i, 0)),
                  pl.BlockSpec((tm, 256), lambda b, i: (i, 0))],
        out_specs=out_specs,
        out_shape=out_shape,
        compiler_params=_cparams(2),
        name="even_in",
    )(x, scale, shift, norm_w, w_bf, qnw, knw, seg, cos, sin)


MOBA_HS = 8
V_ROWS = HD_A + 8
FAR_KEYS = 2 * MOBA_BLOCK


def _moba_prompt_select(tab_ref, q_ref, km_ref, rbf_ref, rbs_ref, qs_ref):
    hg = pl.program_id(1)
    qi = pl.program_id(2)
    nblk = km_ref.shape[1]
    lane = lax.broadcasted_iota(jnp.int32, (1, 128), 1)
    blk = lax.broadcasted_iota(jnp.int32, (nblk, MOBA_BLOCK), 0)
    for hl in range(MOBA_HS):
        pr, hh = divmod(hl, 2)
        pc = slice(pr * 128, (pr + 1) * 128)
        qm = jnp.where((lane // HD_A) == hh, q_ref[0, :, pc], 0.0)
        gate = _mm_hp(km_ref[0, :, pc], qm, 1, 1)
        gate = jnp.where(blk < qi, gate, NEG_INF)
        sel = jnp.logical_and(_top3_rows(gate, blk, nblk), blk < qi)
        far_c = tab_ref[NUM_BUCKETS - 1, MOBA_HS * hg + hl] * LOG2E
        rbf_ref[hl] = jnp.where(jnp.logical_and(sel, blk < qi - 1), far_c, NEG_INF)
        rbs_ref[hl] = jnp.where(jnp.logical_or(sel, blk == qi), 0.0, NEG_INF)
        qs_ref[hl] = _bf(qm * (HD_A ** -0.5 * LOG2E))


def _moba_prompt_attend(k_ref, vt_ref, bias_ref, ga_ref, o_ref, rbf_ref, rbs_ref, qs_ref):
    qi = pl.program_id(2)

    def visit(carry, off, nkeys, extra_fn=None, block_rows=None):
        ss = []
        for hl in range(MOBA_HS):
            pr = hl // 2
            kj = k_ref[0, pl.ds(off, nkeys), pr * 128:(pr + 1) * 128]
            ss.append(_dg(kj, qs_ref[hl], 1, 1))
        stats, ps = [], []
        for hl in range(MOBA_HS):
            m = carry[hl][0]
            if block_rows is None:
                s = extra_fn(hl, ss[hl])
                mn = jnp.maximum(m, jnp.max(s, axis=0, keepdims=True))
                p = jnp.exp2(s - mn)
            else:
                halves = [ss[hl][i * MOBA_BLOCK:(i + 1) * MOBA_BLOCK] for i in range(nkeys // MOBA_BLOCK)]
                rows = block_rows(hl)
                mn = m
                for sh, r in zip(halves, rows):
                    mn = jnp.maximum(mn, jnp.max(sh, axis=0, keepdims=True) + r)
                p = jnp.concatenate([jnp.exp2(sh - (mn - r)) for sh, r in zip(halves, rows)], axis=0)
            stats.append((mn, jnp.exp2(m - mn)))
            ps.append(_bf(p))
        pvs = []
        for hl in range(MOBA_HS):
            vj = vt_ref[0, hl * V_ROWS:(hl + 1) * V_ROWS, pl.ds(off, nkeys)]
            pvs.append(jnp.dot(vj, ps[hl], preferred_element_type=F32))
        return tuple((stats[hl][0], stats[hl][1] * carry[hl][1] + pvs[hl]) for hl in range(MOBA_HS))

    def far_body(jp, carry):
        off = pl.multiple_of(jp * FAR_KEYS, FAR_KEYS)

        def rows(hl):
            return rbf_ref[hl, pl.ds(2 * jp, 1), :], rbf_ref[hl, pl.ds(2 * jp + 1, 1), :]

        return visit(carry, off, FAR_KEYS, block_rows=rows)

    init = tuple((jnp.full((1, MOBA_BLOCK), -jnp.inf, F32), jnp.zeros((V_ROWS, MOBA_BLOCK), F32))
                 for _ in range(MOBA_HS))
    carry = lax.fori_loop(0, qi // 2, far_body, init)
    js = jnp.maximum(qi - 1, 0)
    first = qi == 0
    top_tab = jnp.where(first, 0, 1)
    bot_mask = jnp.where(first, NEG_INF, 0.0)

    def near_extra(hl, s):
        top = s[:MOBA_BLOCK] + bias_ref[hl, top_tab] + rbs_ref[hl, pl.ds(js, 1), :]
        bot = s[MOBA_BLOCK:] + (bias_ref[hl, 0] + bot_mask)
        return jnp.concatenate([top, bot], axis=0)

    carry = visit(carry, pl.multiple_of(js * MOBA_BLOCK, MOBA_BLOCK), FAR_KEYS, near_extra)
    for pr in range(MOBA_HS // 2):
        accs = [carry[2 * pr + hh][1] for hh in range(2)]
        outs = [a[0:HD_A] * (1.0 / a[HD_A:HD_A + 1]) for a in accs]
        o = jnp.concatenate(outs, axis=0).T
        pc = slice(pr * 128, (pr + 1) * 128)
        o_ref[0, :, pc] = (o * _silu(ga_ref[0, :, pc].astype(F32))).astype(o_ref.dtype)


N_PAST_BLK = PAST_LEN // MOBA_BLOCK
N_PAGES = PAST_LEN // PAGE_SIZE
N_ROWS_S = 32
SEQ_PER_TILE = 2


def _page_copies(pt_ref, ck_hbm, cv_hbm, kt_buf, vt_buf, sems, seq, sl):
    cps = []
    for p in range(N_PAGES):
        dst = pl.ds(p * PAGE_SIZE, PAGE_SIZE)
        cps.append(pltpu.make_async_copy(ck_hbm.at[pt_ref[seq, p]], kt_buf.at[sl, :, dst], sems.at[0, sl]))
        cps.append(pltpu.make_async_copy(cv_hbm.at[pt_ref[seq, p]], vt_buf.at[sl, :, dst], sems.at[1, sl]))
    return cps


def _moba_sample_seqs(qrep_ref, knew_ref, vnew_ref, ga_ref, bias_ref, bfar_ref, kt_buf, vt_buf, kpad, vpad, o_ref):
    t_new = knew_ref.shape[1]
    rowh = lax.broadcasted_iota(jnp.int32, (N_ROWS_S, W_A), 0) % H_A
    laneh = lax.broadcasted_iota(jnp.int32, (N_ROWS_S, W_A), 1) // HD_A
    own_head = rowh == laneh
    blocks = [slice(n * MOBA_BLOCK, (n + 1) * MOBA_BLOCK) for n in range(N_PAST_BLK)]
    bfar = bfar_ref[...]
    scored = []
    for j in range(SEQ_PER_TILE):
        qf = jnp.where(own_head, qrep_ref[j], 0.0) * (HD_A ** -0.5)
        qbd = _bf(qf)
        q2 = jnp.concatenate([qbd, _bf(qf - qbd.astype(F32))], axis=0)
        kpad[j, 0:t_new, :] = knew_ref[j]
        vpad[j, 0:t_new, :] = vnew_ref[j]
        s_past = []
        for n in range(N_PAST_BLK):
            s2 = jnp.dot(q2, _bf(kt_buf[j, :, blocks[n]]), preferred_element_type=F32)
            s_past.append(s2[0:N_ROWS_S] + s2[N_ROWS_S:])
        s_own = _dg(qbd, _bf(kpad[j]), 1, 1) + bias_ref[:, MOBA_BLOCK:]
        scored.append((s_past, s_own))
    probs = []
    for s_past, s_own in scored:
        g = [jnp.sum(s, axis=1, keepdims=True) for s in s_past]
        sel = [jnp.zeros((N_ROWS_S, 1), jnp.bool_) for _ in range(N_PAST_BLK)]
        for _ in range(MOBA_TOPK):
            m = functools.reduce(jnp.maximum, g)
            idx = functools.reduce(jnp.minimum, [jnp.where(g[n] == m, n, N_PAST_BLK) for n in range(N_PAST_BLK)])
            for n in range(N_PAST_BLK):
                pick = idx == n
                sel[n] = jnp.logical_or(sel[n], pick)
                g[n] = jnp.where(pick, -jnp.inf, g[n])
        logits = [s_past[n] + jnp.where(sel[n], bfar, NEG_INF) for n in range(N_PAST_BLK - 1)]
        logits.append(s_past[-1] + bias_ref[:, 0:MOBA_BLOCK] + jnp.where(sel[-1], 0.0, NEG_INF))
        m = jnp.max(s_own, axis=1, keepdims=True)
        for s in logits:
            m = jnp.maximum(m, jnp.max(s, axis=1, keepdims=True))
        p_own = jnp.exp(s_own - m)
        ps = [jnp.exp(s - m) for s in logits]
        l = functools.reduce(jnp.add, [jnp.sum(p, axis=1, keepdims=True) for p in ps + [p_own]])
        probs.append(([_bf(p) for p in ps], _bf(p_own), l))
    for j, (ps, p_own, l) in enumerate(probs):
        acc = jnp.dot(p_own, _bf(vpad[j]), preferred_element_type=F32)
        for n in range(N_PAST_BLK):
            acc = acc + _dg(ps[n], _bf(vt_buf[j, :, blocks[n]]), 1, 1)
        o = jnp.where(own_head, acc * (1.0 / l), 0.0)
        o = jnp.sum(o.reshape(t_new, H_A, W_A), axis=1)
        o_ref[j] = o * _silu(ga_ref[j])


def _moba_kernel(pt_ref, tab_ref, q_ref, k_ref, vt_ref, km_ref, bias_ref, ga_ref,
                 qrep_ref, knew_ref, vnew_ref, gas_ref, bias_s_ref, bfar_ref, ck_hbm, cv_hbm,
                 o_ref, os_ref, rbf_ref, rbs_ref, qs_ref, kt_buf, vt_buf, kpad, vpad, sems):
    step = pl.program_id(0) * pl.num_programs(2) + pl.program_id(2)
    n_steps = pl.num_programs(0) * pl.num_programs(2)
    seq0 = SEQ_PER_TILE * step
    copies = functools.partial(_page_copies, pt_ref, ck_hbm, cv_hbm, kt_buf, vt_buf, sems)

    @pl.when(step == 0)
    def _():
        kpad[...] = jnp.zeros(kpad.shape, F32)
        vpad[...] = jnp.zeros(vpad.shape, F32)
        for j in range(SEQ_PER_TILE):
            for cp in copies(j, j):
                cp.start()

    for j in range(SEQ_PER_TILE):
        for cp in copies(seq0 + j, j):
            cp.wait()
    _moba_sample_seqs(qrep_ref, knew_ref, vnew_ref, gas_ref, bias_s_ref, bfar_ref, kt_buf, vt_buf, kpad, vpad, os_ref)
    _moba_prompt_select(tab_ref, q_ref, km_ref, rbf_ref, rbs_ref, qs_ref)

    @pl.when(step + 1 < n_steps)
    def _():
        for j in range(SEQ_PER_TILE):
            for cp in copies(seq0 + SEQ_PER_TILE + j, j):
                cp.start()

    _moba_prompt_attend(k_ref, vt_ref, bias_ref, ga_ref, o_ref, rbf_ref, rbs_ref, qs_ref)


def _moba(page_table, rel_bias, qa, kbf, vt, kmean, bias_t, ga,
          qrep, knew, vnew, ga_s, bias_s, bfar, cache_kt, cache_vt):
    nb, s, _ = qa.shape
    nq = s // MOBA_BLOCK
    nseq, t_new, _ = knew.shape
    assert H_A == MOBA_HS and nseq == SEQ_PER_TILE * nb * nq
    tile = pl.BlockSpec((1, MOBA_BLOCK, W_A), lambda b, hg, i, pt: (b, i, 0))
    per_b = lambda shp: pl.BlockSpec((1,) + shp, lambda b, hg, i, pt: (b, 0, 0))
    seqs = lambda r: pl.BlockSpec((SEQ_PER_TILE, r, W_A), lambda b, hg, i, pt: (b * nq + i, 0, 0))
    const = lambda a: pl.BlockSpec(a.shape, lambda b, hg, i, pt: (0,) * a.ndim)
    grid_spec = pltpu.PrefetchScalarGridSpec(
        num_scalar_prefetch=1,
        grid=(nb, 1, nq),
        in_specs=[pl.BlockSpec(memory_space=pltpu.SMEM),
                  tile, per_b((s, W_A)), per_b((H_A * V_ROWS, s)), per_b((nq, W_A)), const(bias_t), tile,
                  seqs(N_ROWS_S), seqs(t_new), seqs(t_new), seqs(t_new), const(bias_s), const(bfar),
                  pl.BlockSpec(memory_space=pl.ANY), pl.BlockSpec(memory_space=pl.ANY)],
        out_specs=(tile, seqs(t_new)),
        scratch_shapes=[pltpu.VMEM((MOBA_HS, nq, MOBA_BLOCK), F32),
                        pltpu.VMEM((MOBA_HS, nq, MOBA_BLOCK), F32),
                        pltpu.VMEM((MOBA_HS, MOBA_BLOCK, 128), BF16),
                        pltpu.VMEM((SEQ_PER_TILE, W_A, PAST_LEN), F32),
                        pltpu.VMEM((SEQ_PER_TILE, W_A, PAST_LEN), F32),
                        pltpu.VMEM((SEQ_PER_TILE, PAGE_SIZE, W_A), F32),
                        pltpu.VMEM((SEQ_PER_TILE, PAGE_SIZE, W_A), F32),
                        pltpu.SemaphoreType.DMA((2, SEQ_PER_TILE))],
    )
    return pl.pallas_call(
        _moba_kernel,
        grid_spec=grid_spec,
        out_shape=(jax.ShapeDtypeStruct((nb, s, W_A), BF16),
                   jax.ShapeDtypeStruct((nseq, t_new, W_A), F32)),
        compiler_params=_cparams(3),
        name="moba",
    )(page_table, rel_bias, qa, kbf, vt, kmean, bias_t, ga, qrep, knew, vnew, ga_s, bias_s, bfar, cache_kt, cache_vt)


def _pad_rows(ref, scratch, s, t):
    if t == CHUNK:
        return ref[s]
    scratch[s] = jnp.zeros(scratch.shape[1:], scratch.dtype)
    scratch[s, 0:t, :] = ref[s].astype(scratch.dtype)
    return scratch[s]


def _ret_kernel(q_ref, k_ref, v_ref, g_ref, st0_ref, dmat_ref, qdec_ref, kdec_ref, gl_ref,
                o_ref, st_ref, qpad, kpad, vpad, gpad, *, t, lq, nbs):
    c = pl.program_id(1)

    @pl.when(c == 0)
    def _():
        st_ref[...] = st0_ref[...]

    lane = lax.broadcasted_iota(jnp.int32, (1, 128), 1)
    rowsel = lax.broadcasted_iota(jnp.int32, (128, 1), 0) < DK_B
    work = []
    for s in range(nbs):
        q = _pad_rows(q_ref, qpad, s, t)[0:lq]
        k = _pad_rows(k_ref, kpad, s, t)
        v = _pad_rows(v_ref, vpad, s, t)
        for hp in range(H_B // 2):
            cols = slice(hp * 128, (hp + 1) * 128)
            kp = _bf(k[:, cols])
            st = st_ref[s, cols, :]
            kd = _bf(k[:, cols] * kdec_ref[0:k.shape[0], cols])
            for hh in range(2):
                h = 2 * hp + hh
                qm = jnp.where((lane // DK_B) == hh, q[:, cols], 0.0)
                vh = _bf(v[:, h * DV_B:(h + 1) * DV_B])
                sc = _dg(_bf(qm), kp, 1, 1)
                so = _mm(qm * qdec_ref[0:lq, cols], st)
                upd = _dg(kd, vh, 0, 0)
                work.append((s, hp, hh, sc, so, upd, vh, st))
    outs = {}
    for (s, hp, hh, sc, so, upd, vh, st) in work:
        h = 2 * hp + hh
        o = jnp.dot(_bf(sc * dmat_ref[h, 0:lq, 0:sc.shape[1]]), vh, preferred_element_type=F32) + so
        outs[(s, h)] = o * lax.rsqrt(jnp.mean(o * o, axis=-1, keepdims=True) + EPS)
    for i in range(0, len(work), 2):
        s, hp, _, _, _, upd0, _, st = work[i]
        cols = slice(hp * 128, (hp + 1) * 128)
        st_ref[s, cols, :] = st * gl_ref[cols, :] + jnp.where(rowsel, upd0, work[i + 1][5])
    for s in range(nbs):
        g = _pad_rows(g_ref, gpad, s, t)[0:lq].astype(F32)
        o = jnp.concatenate([outs[(s, h)] for h in range(H_B)], axis=1)
        o_ref[s] = (o * _silu(g))[0:t].astype(o_ref.dtype)


def _retention(q, k, v, g, st0, dmat, qdec, kdec, gl, lq, nbs):
    nb, nc, t, _ = q.shape
    row = lambda c_: pl.BlockSpec((nbs, None, t, c_), lambda b, c: (b, c, 0, 0))
    const = lambda a: pl.BlockSpec(a.shape, lambda b, c: (0,) * a.ndim)
    st_spec = pl.BlockSpec((nbs, H_B * DK_B, DV_B), lambda b, c: (b, 0, 0))
    pad = lambda c_: pltpu.VMEM((nbs, t if t == CHUNK else SHORT_ROWS, c_), F32)
    return pl.pallas_call(
        functools.partial(_ret_kernel, t=t, lq=lq, nbs=nbs),
        grid=(nb // nbs, nc),
        in_specs=[row(256), row(256), row(512), row(512), st_spec,
                  const(dmat), const(qdec), const(kdec), const(gl)],
        out_specs=(row(512), st_spec),
        out_shape=(jax.ShapeDtypeStruct((nb, nc, t, 512), q.dtype),
                   jax.ShapeDtypeStruct((nb, H_B * DK_B, DV_B), F32)),
        scratch_shapes=[pad(256), pad(256), pad(512), pad(512)],
        compiler_params=_cparams(2),
        name="retention",
    )(q, k, v, g, st0, dmat, qdec, kdec, gl)


def _out_kernel(a_ref, b_ref, x_ref, g_ref, w_ref, o_ref):
    half = w_ref.shape[0] // 2
    y = (jnp.dot(_bf(a_ref[0]), w_ref[0:half, :], preferred_element_type=F32)
         + jnp.dot(_bf(b_ref[0]), w_ref[half:, :], preferred_element_type=F32))
    o_ref[0] = x_ref[0] + g_ref[0] * y


def _out_proj(a, b, x, gate, w_bf, slab):
    nb, s, d = x.shape
    tm = min(OUT_ROWS, s)
    g_spec = _mod_spec(tm, d, slab)
    row = lambda c: pl.BlockSpec((1, tm, c), lambda bb, i: (bb, i, 0))
    return pl.pallas_call(
        _out_kernel,
        grid=(nb, s // tm),
        in_specs=[row(512), row(512), row(d), g_spec, pl.BlockSpec(w_bf.shape, lambda bb, i: (0, 0))],
        out_specs=row(d),
        out_shape=jax.ShapeDtypeStruct((nb, s, d), F32),
        compiler_params=_cparams(2),
        name="out_proj",
    )(a, b, x, gate, w_bf)


def _odd_in_kernel(x_ref, sc_ref, sh_ref, nw_ref, w_ref, wdt_ref, sguw_ref, sgub_ref, dtb_ref,
                   oc_ref, zg_ref, xbc_ref, dt_ref, *maybe_v_ref):
    x = x_ref[0]
    tm = x.shape[0]
    ms = jnp.mean(x * x, axis=-1, keepdims=True)
    h = (x * lax.rsqrt(ms + EPS) * nw_ref[...]) * (1.0 + sc_ref[0]) + sh_ref[0]
    hb = _bf(h)

    def proj(lo, hi):
        return jnp.dot(hb, w_ref[:, lo:hi], preferred_element_type=F32)

    u = _gelu_tanh(proj(0, 512))
    v = _gelu_tanh(proj(512, 1024))
    mu = jnp.mean(v, axis=-1, keepdims=True)
    vc = v - mu
    v = vc * lax.rsqrt(jnp.mean(vc * vc, axis=-1, keepdims=True) + EPS)
    for v_ref in maybe_v_ref:
        v_ref[0] = v
    ii = lax.broadcasted_iota(jnp.int32, (CHUNK, CHUNK), 0)
    jj = lax.broadcasted_iota(jnp.int32, (CHUNK, CHUNK), 1)
    rows = []
    for ci in range(tm // CHUNK):
        cols = []
        for g in range(G_C):
            wg = jnp.where(ii >= jj, sguw_ref[g], 0.0)
            cols.append(_mm(wg, v[ci * CHUNK:(ci + 1) * CHUNK, g * 128:(g + 1) * 128]))
        rows.append(jnp.concatenate(cols, axis=1) + sgub_ref[...])
    sg = jnp.concatenate(rows, axis=0) if len(rows) > 1 else rows[0]
    oc_ref[0] = (u * sg * _silu(proj(1024, 1536))).astype(oc_ref.dtype)
    zg_ref[0] = proj(1536, 2048).astype(zg_ref.dtype)
    xbc_ref[0] = proj(2048, 3072)
    dt_ref[0] = _softplus(jnp.dot(hb, wdt_ref[...], preferred_element_type=F32) + dtb_ref[...])


def _odd_in(x, scale, shift, norm_w, w_bf, w_dt, sgu_w, sgu_b_tab, dt_bias, slabs, act, emit_v):
    nb, s, d = x.shape
    tm = min(PROJ_ROWS, s // 2)
    sc_spec, sh_spec = (_mod_spec(tm, d, p) for p in (slabs or (None, None)))
    row = lambda c: pl.BlockSpec((1, tm, c), lambda b, i: (b, i, 0))
    const = lambda shp: pl.BlockSpec(shp, lambda b, i: (0,) * len(shp))
    return pl.pallas_call(
        _odd_in_kernel,
        grid=(nb, s // tm),
        in_specs=[row(d), sc_spec, sh_spec, const((1, d)), const((d, 3072)), const(w_dt.shape),
                  const(sgu_w.shape), const(sgu_b_tab.shape), const((1, 512))],
        out_specs=(row(512), row(512), row(1024), row(512)) + ((row(512),) if emit_v else ()),
        out_shape=(jax.ShapeDtypeStruct((nb, s, 512), act),
                   jax.ShapeDtypeStruct((nb, s, 512), act),
                   jax.ShapeDtypeStruct((nb, s, 1024), F32),
                   jax.ShapeDtypeStruct((nb, s, 512), F32),
                   ) + ((jax.ShapeDtypeStruct((nb, s, 512), F32),) if emit_v else ()),
        compiler_params=_cparams(2),
        name="odd_in",
    )(x, scale, shift, norm_w, w_bf, w_dt, sgu_w, sgu_b_tab, dt_bias)


def _ssd_kernel(xbc_ref, dt_ref, zg_ref, tail_ref, st0_ref, cw_ref, cb_ref, alog_ref, dsk_ref, nw_ref,
                tri_ref, sel_ref, y_ref, st_ref, ext, dtpad, zpad, *, t, lq, nc, nbs):
    rows = dtpad.shape[1]
    c = pl.program_id(1)

    @pl.when(c == 0)
    def _():
        st_ref[...] = st0_ref[...]
        ext[...] = jnp.zeros(ext.shape, F32)
        ext[:, 0:8, :] = tail_ref[...]

    ii = lax.broadcasted_iota(jnp.int32, (lq, rows), 0)
    jj = lax.broadcasted_iota(jnp.int32, (lq, rows), 1)
    lane = lax.broadcasted_iota(jnp.int32, (1, 128), 1)
    hpg = H_D // G_D
    neg_a = -jnp.exp(alog_ref[...])
    seqs = []
    for s in range(nbs):
        ext[s, 8:8 + t, :] = xbc_ref[s]
        conv = cb_ref[...]
        for w in range(CONV_W):
            conv = conv + ext[s, pl.ds(8 - (CONV_W - 1) + w, rows), :] * cw_ref[w:w + 1, :]
        if nc > 1:
            ext[s, 0:8, :] = ext[s, rows:rows + 8, :]
        xc = _silu(conv)
        dt = _pad_rows(dt_ref, dtpad, s, t)
        cum = _mm_exact_lhs(tri_ref[0:rows, 0:rows], dt * neg_a, 1, 0)
        seqs.append((xc, dt, cum))
    st1 = []
    for s in range(nbs):
        xc, dt, cum = seqs[s]
        xh = xc[:, 0:W_D]
        last = cum[rows - 1:rows, :]
        dtx = _bf(xh * dt)
        xw = _bf(xh * (jnp.exp(last - cum) * dt))
        cum_rows = _mm_exact_lhs(sel_ref[...], cum, 1, 1)
        per_g = []
        for g in range(G_D):
            bg = _bf(xc[:, W_D + g * N_D:W_D + (g + 1) * N_D])
            cg = _bf(xc[0:lq, W_D + G_D * N_D + g * N_D:W_D + G_D * N_D + (g + 1) * N_D])
            gr = slice(g * hpg * P_D, (g + 1) * hpg * P_D)
            cb = _dg(cg, bg, 1, 1)
            yoff = _dg(cg, _bf(st_ref[s, gr, :]), 1, 1)
            upd = _dg(xw[:, gr], bg, 0, 0)
            per_g.append((cb, yoff, upd))
        st1.append((dtx, cum_rows, per_g))
    for s in range(nbs):
        xc, dt, cum = seqs[s]
        dtx, cum_rows, per_g = st1[s]
        ecum = jnp.exp(cum[0:lq])
        elast = jnp.exp(cum[rows - 1:rows, :])
        ys = []
        for g in range(G_D):
            cb, yoff, upd = per_g[g]
            for pr in range(hpg // 2):
                l0 = g * hpg * P_D + pr * 128
                yh = []
                for hh in range(2):
                    h = g * hpg + pr * 2 + hh
                    col = jnp.broadcast_to(cum[0:lq, h * P_D:h * P_D + 1], (lq, rows))
                    seg = jnp.minimum(col - cum_rows[h:h + 1, :], 0.0)
                    mh = jnp.where(ii >= jj, cb * jnp.exp(seg), 0.0)
                    yh.append(jnp.dot(_bf(mh), dtx[:, l0:l0 + 128], preferred_element_type=F32))
                ypair = jnp.where(lane < P_D, yh[0], yh[1])
                ys.append(ypair + yoff[:, pr * 128:(pr + 1) * 128] * ecum[:, l0:l0 + 128])
            for hl in range(hpg):
                h = g * hpg + hl
                r = slice(h * P_D, (h + 1) * P_D)
                dec = jnp.broadcast_to(elast[0:1, h * P_D:h * P_D + 1], (P_D, N_D))
                st_ref[s, r, :] = st_ref[s, r, :] * dec + upd[hl * P_D:(hl + 1) * P_D, :]
        y = jnp.concatenate(ys, axis=1)
        zg = _pad_rows(zg_ref, zpad, s, t)[0:lq].astype(F32)
        y = (y + xc[0:lq, 0:W_D] * dsk_ref[...]) * _silu(zg)
        gw = W_D // G_D
        outs = []
        for g in range(G_D):
            yg = y[:, g * gw:(g + 1) * gw]
            outs.append(yg * lax.rsqrt(jnp.mean(yg * yg, axis=-1, keepdims=True) + EPS))
        y_ref[s] = (jnp.concatenate(outs, axis=1) * nw_ref[...])[0:t].astype(y_ref.dtype)


def _ssd(xbc, dt, zg, tail, st0, conv_w, conv_b, a_log, d_skip, norm_w, tri, sel, lq, nbs):
    nb, nc, t, _ = xbc.shape
    rows = t if t == CHUNK else SHORT_ROWS
    row = lambda c_: pl.BlockSpec((nbs, None, t, c_), lambda b, c: (b, c, 0, 0))
    const = lambda a: pl.BlockSpec(a.shape, lambda b, c: (0,) * a.ndim)
    st_spec = pl.BlockSpec((nbs, H_D * P_D, N_D), lambda b, c: (b, 0, 0))
    return pl.pallas_call(
        functools.partial(_ssd_kernel, t=t, lq=lq, nc=nc, nbs=nbs),
        grid=(nb // nbs, nc),
        in_specs=[row(1024), row(512), row(512),
                  pl.BlockSpec((nbs, 8, CONV_DIM), lambda b, c: (b, 0, 0)), st_spec,
                  const(conv_w), const(conv_b), const(a_log), const(d_skip), const(norm_w),
                  const(tri), const(sel)],
        out_specs=(row(512), st_spec),
        out_shape=(jax.ShapeDtypeStruct((nb, nc, t, 512), zg.dtype),
                   jax.ShapeDtypeStruct((nb, H_D * P_D, N_D), F32)),
        scratch_shapes=[pltpu.VMEM((nbs, rows + 8, CONV_DIM), F32),
                        pltpu.VMEM((nbs, rows, 512), F32), pltpu.VMEM((nbs, rows, 512), F32)],
        compiler_params=_cparams(2),
        name="ssd",
    )(xbc, dt, zg, tail, st0, conv_w, conv_b, a_log, d_skip, norm_w, tri, sel)


def _rotary_tables(pos):
    half = DK_B // 2
    inv = (np.float32(1.0) / np.float32(10000.0) ** (np.arange(half, dtype=np.float32) / np.float32(half)))
    ang = (np.asarray(pos).astype(np.float32)[:, None] * inv.astype(np.float32)[None, :]).astype(np.float64)
    cos, sin = np.cos(ang), np.sin(ang)
    cos_t = np.tile(np.concatenate([cos, cos], axis=1), (1, H_B))
    sin_t = np.tile(np.concatenate([-sin, sin], axis=1), (1, H_B))
    return jnp.asarray(cos_t, F32), jnp.asarray(sin_t, F32)


def _retention_tables(chunk_len):
    log_g = np.log(1.0 - 2.0 ** (-5.0 - np.arange(H_B, dtype=np.float64)))
    idx = np.arange(CHUNK, dtype=np.float64)
    diff = idx[:, None] - idx[None, :]
    dmat = np.where(diff[None] >= 0, np.exp(np.maximum(diff, 0.0)[None] * log_g[:, None, None]), 0.0)
    qdec = np.exp((idx + 1.0)[:, None] * log_g[None, :])
    kdec = np.where(idx[:, None] < chunk_len, np.exp((chunk_len - 1.0 - idx)[:, None] * log_g[None, :]), 0.0)
    gl = np.exp(chunk_len * log_g)
    return (jnp.asarray(dmat, F32),
            jnp.asarray(np.repeat(qdec, DK_B, axis=1), F32),
            jnp.asarray(np.repeat(kdec, DK_B, axis=1), F32),
            jnp.asarray(np.repeat(np.repeat(gl, DK_B)[:, None], DV_B, axis=1), F32))


def _prompt_bias_idx():
    kk = np.arange(MOBA_BLOCK)[:, None]
    qq = np.arange(MOBA_BLOCK)[None, :]
    diag = np.where(qq >= kk, _t5_bucket_np(qq - kk), -1)
    sub = _t5_bucket_np(qq + MOBA_BLOCK - kk)
    return np.concatenate([diag, sub], axis=0).astype(np.int32)


def _sample_bias_idx(t_new):
    row_t = (np.arange(N_ROWS_S) // H_A)[:, None]
    qpos = PAST_LEN + row_t
    near = _t5_bucket_np(qpos - (PAST_LEN - MOBA_BLOCK + np.arange(MOBA_BLOCK))[None, :])
    own_k = np.arange(PAGE_SIZE)[None, :]
    own = np.where((own_k <= row_t) & (own_k < t_new), _t5_bucket_np(row_t - own_k), -1)
    return np.concatenate([near, own], axis=1).astype(np.int32)


def kernel(x_prompt, x_sample, cache_k, cache_v, state_ret, state_ssm, state_conv, page_table, c_prompt, c_sample, rel_bias, e_norm_w, e_ada_w, e_ada_b, e_in_w, e_q_norm_w, e_k_norm_w, e_out_w, o_norm_w, o_ada_w, o_ada_b, o_in_w, o_sgu_w, o_sgu_b, o_conv_w, o_conv_b, o_dt_bias, o_A_log, o_D, o_ssm_norm_w, o_out_w):
    bp, s_len, d = x_prompt.shape
    bs, t_len, _ = x_sample.shape
    n_s = bs * t_len

    c_all = jnp.concatenate([jnp.repeat(c_sample, t_len, axis=0), c_prompt,
                             jnp.zeros((8 - (bp + n_s) % 8, d), F32)], axis=0)
    mods = []
    for ada_w, ada_b in ((e_ada_w[0], e_ada_b[0]), (o_ada_w[0], o_ada_b[0])):
        mod = _ada_mod(c_all, ada_w, ada_b)
        parts_p = [mod[i, n_s:n_s + bp].reshape(bp, 1, d) for i in range(3)]
        mods.append((parts_p, mod))
    (e_mod_p, e_mod_s), (o_mod_p, o_mod_s) = mods

    seg = jnp.asarray(np.kron(np.eye(H_A), np.ones((HD_A, HD_A))), BF16)
    qnw = jnp.tile(e_q_norm_w[0], H_A).reshape(1, W_A)
    knw = jnp.tile(e_k_norm_w[0], H_A).reshape(1, W_A)
    e_in_bf = _bf(e_in_w[0])
    e_out_bf = _bf(e_out_w[0])
    o_in_bf = _bf(o_in_w[0])
    o_dt_bf = _bf(jnp.repeat(o_in_w[0][:, 3072:], P_D, axis=1))
    o_out_bf = _bf(o_out_w[0])
    x_s = x_sample.reshape(1, n_s, d)
    cos_p, sin_p = _rotary_tables(np.arange(s_len))
    cos_s, sin_s = _rotary_tables(PAST_LEN + (np.arange(n_s) % t_len))
    bias_p = _bias_tables(rel_bias, _prompt_bias_idx(), LOG2E).reshape(H_A, 2, MOBA_BLOCK, MOBA_BLOCK)
    bias_s_h = _bias_tables(rel_bias, _sample_bias_idx(t_len))
    row_h = jnp.arange(N_ROWS_S) % H_A
    bias_s = jnp.sum(jnp.where((jnp.arange(H_A)[:, None] == row_h[None, :])[:, :, None], bias_s_h, 0.0), axis=0)
    bfar = rel_bias[NUM_BUCKETS - 1, row_h].reshape(N_ROWS_S, 1)

    (qa, ka, va, kbf, vt, ga, qb, kb, vb, gb, kmean) = _even_in(
        x_prompt, e_mod_p[1], e_mod_p[0], e_norm_w[0].reshape(1, d), e_in_bf, qnw, knw, seg, cos_p, sin_p, None, BF16)
    (qa_s, ka_s, va_s, _, _, ga_s, qb_s, kb_s, vb_s, gb_s, _) = _even_in(
        x_s, e_mod_s, e_mod_s, e_norm_w[0].reshape(1, d), e_in_bf, qnw, knw, seg, cos_s, sin_s, (1, 0), F32)
    sq = lambda a: a.reshape(bs, t_len, a.shape[-1])
    qrep = jnp.repeat(sq(qa_s), H_A, axis=1)
    n_phys = cache_k.shape[1]
    page_t = lambda c: jnp.transpose(c[0], (0, 2, 3, 1)).reshape(n_phys, W_A, PAGE_SIZE)
    oa, oa_s = _moba(page_table, rel_bias, qa, kbf, vt, kmean.reshape(bp, s_len // MOBA_BLOCK, W_A), bias_p, ga,
                     qrep, sq(ka_s), sq(va_s), sq(ga_s), bias_s, bfar, page_t(cache_k), page_t(cache_v))
    nc_p = s_len // CHUNK
    ch = lambda a: a.reshape(bp, nc_p, CHUNK, a.shape[-1])
    ob, ret_p = _retention(ch(qb), ch(kb), ch(vb), ch(gb), jnp.zeros((bp, H_B * DK_B, DV_B), F32),
                           *_retention_tables(CHUNK), lq=CHUNK, nbs=bp)
    xp1 = _out_proj(oa, ob.reshape(bp, s_len, W_B), x_prompt, e_mod_p[2], e_out_bf, None)
    k_prompt = ka.reshape(1, bp, s_len, H_A, HD_A)
    v_prompt = va.reshape(1, bp, s_len, H_A, HD_A)
    ret_state_prompt = ret_p.reshape(1, bp, H_B, DK_B, DV_B)

    sc = lambda a: a.reshape(bs, 1, t_len, a.shape[-1])
    ob_s, ret_s = _retention(sc(qb_s), sc(kb_s), sc(vb_s), sc(gb_s),
                             state_ret[0].reshape(bs, H_B * DK_B, DV_B), *_retention_tables(t_len), lq=8, nbs=SEQ_PER_STEP)
    xs1 = _out_proj(oa_s.reshape(1, n_s, W_A), ob_s.reshape(1, n_s, W_B), x_s, e_mod_s, e_out_bf, 2)
    k_sample = ka_s.reshape(1, bs, t_len, H_A, HD_A)
    v_sample = va_s.reshape(1, bs, t_len, H_A, HD_A)
    ret_state_sample = ret_s.reshape(1, bs, H_B, DK_B, DV_B)

    tri = jnp.asarray(np.tril(np.ones((CHUNK, CHUNK))), BF16)
    sel = jnp.asarray(np.kron(np.eye(H_D), np.eye(1, P_D)), BF16)
    rep = lambda a: jnp.repeat(a, P_D).reshape(1, W_D)
    dt_bias, a_log, d_skip = rep(o_dt_bias[0]), rep(o_A_log[0]), rep(o_D[0])
    ssm_nw = o_ssm_norm_w[0].reshape(1, W_D)
    conv_b = o_conv_b[0].reshape(1, CONV_DIM)
    o_nw = o_norm_w[0].reshape(1, d)

    sgu_b_p = jnp.repeat(o_sgu_b[0].T, W_C // G_C, axis=1)
    oc, zg, xbc, dtp = _odd_in(xp1, o_mod_p[1], o_mod_p[0], o_nw, o_in_bf, o_dt_bf, o_sgu_w[0], sgu_b_p, dt_bias,
                               None, BF16, False)
    yn, ssm_p = _ssd(ch(xbc), ch(dtp), ch(zg), jnp.zeros((bp, 8, CONV_DIM), F32),
                     jnp.zeros((bp, H_D * P_D, N_D), F32), o_conv_w[0], conv_b, a_log, d_skip, ssm_nw,
                     tri, sel, lq=CHUNK, nbs=bp)
    y_prompt = _out_proj(oc, yn.reshape(bp, s_len, W_D), xp1, o_mod_p[2], o_out_bf, None)
    ssm_state_prompt = ssm_p.reshape(1, bp, H_D, P_D, N_D)
    conv_state_prompt = xbc[:, -(CONV_W - 1):][None]

    per_chunk = CHUNK // t_len
    w_small = o_sgu_w[0][:, :t_len, :t_len]
    same_seq = jnp.asarray(np.kron(np.eye(per_chunk), np.ones((t_len, t_len))), F32)
    tok_of_row = jnp.asarray(np.tile(np.eye(t_len), (per_chunk, 1)), F32)
    sgu_w_s = jnp.einsum('ia,gab,jb->gij', tok_of_row, w_small, tok_of_row,
                         precision=lax.Precision.HIGHEST) * same_seq
    sgu_b_s = jnp.repeat(jnp.tile(o_sgu_b[0][:, :t_len].T, (per_chunk, 1)), W_C // G_C, axis=1)
    oc_s, zg_s, xbc_s, dt_s, v_s = _odd_in(xs1, o_mod_s, o_mod_s, o_nw, o_in_bf, o_dt_bf, sgu_w_s, sgu_b_s, dt_bias,
                                           (1, 0), F32, True)
    tail_s = jnp.concatenate([jnp.zeros((bs, 8 - (CONV_W - 1), CONV_DIM), F32), state_conv[0]], axis=1)
    yn_s, ssm_s = _ssd(sc(xbc_s), sc(dt_s), sc(zg_s), tail_s, state_ssm[0].reshape(bs, H_D * P_D, N_D),
                       o_conv_w[0], conv_b, a_log, d_skip, ssm_nw, tri, sel, lq=8, nbs=SEQ_PER_STEP)
    xs2 = _out_proj(oc_s, yn_s.reshape(1, n_s, W_D), xs1, o_mod_s, o_out_bf, 2)
    y_sample = xs2.reshape(bs, t_len, d)
    sgu_v_sample = v_s.reshape(1, bs, t_len, W_C)
    ssm_state_sample = ssm_s.reshape(1, bs, H_D, P_D, N_D)
    xin = jnp.concatenate([state_conv[0], xbc_s.reshape(bs, t_len, CONV_DIM)], axis=1)
    conv_state_sample = xin[:, -(CONV_W - 1):][None]

    return (y_prompt, y_sample, k_prompt, v_prompt, k_sample, v_sample, ret_state_prompt, ret_state_sample,
            sgu_v_sample, ssm_state_prompt, ssm_state_sample, conv_state_prompt, conv_state_sample)
```

```python
import functools
import math

import numpy as np
import jax
import jax.numpy as jnp
from jax import lax
from jax.experimental import pallas as pl
from jax.experimental.pallas import tpu as pltpu

F32 = jnp.float32
BF16 = jnp.bfloat16

PAST_LEN = 2048
PAGE_SIZE = 128
H_A, HD_A, W_A = 8, 64, 512
MOBA_BLOCK = 256
MOBA_TOPK = 3
NUM_BUCKETS = 32
MAX_DISTANCE = 128
H_B, DK_B, DV_B, W_B = 4, 64, 128, 512
G_C, W_C = 4, 512
H_D, P_D, N_D, G_D, W_D = 8, 64, 128, 2, 512
CONV_W = 4
CONV_DIM = 1024
CHUNK = 128
SEQ_PER_STEP = 8
SHORT_ROWS = 16
PROJ_ROWS = 512
OUT_ROWS = 1024
NEG_INF = -1e30
EPS = 1e-6
LOG2E = math.log2(math.e)
VMEM_LIMIT = 56 * 1024 * 1024


def _bf(x):
    return x.astype(BF16)


def _dg(a, b, ca, cb):
    return lax.dot_general(a, b, (((ca,), (cb,)), ((), ())), preferred_element_type=F32)


def _mm(a, b):
    return _dg(_bf(a), _bf(b), 1, 0)


def _split2(x):
    hi = _bf(x)
    return hi, _bf(x - hi.astype(F32))


def _split3(x):
    hi = _bf(x)
    r = x - hi.astype(F32)
    mid = _bf(r)
    return hi, mid, _bf(r - mid.astype(F32))


def _mm_hp(a, b, ca, cb):
    ah, al = _split2(a)
    bh, bl = _split2(b)
    return _dg(ah, bh, ca, cb) + (_dg(ah, bl, ca, cb) + _dg(al, bh, ca, cb))


def _mm_exact_lhs(e, x, ca, cb):
    h, m, l = _split3(x)
    return _dg(e, h, ca, cb) + (_dg(e, m, ca, cb) + _dg(e, l, ca, cb))


def _silu(x):
    return x * (1.0 / (1.0 + jnp.exp(-x)))


def _gelu_tanh(x):
    return 0.5 * x * (1.0 + jnp.tanh(math.sqrt(2.0 / math.pi) * (x + 0.044715 * (x * x * x))))


def _softplus(x):
    return jnp.maximum(x, 0.0) + jnp.log1p(jnp.exp(-jnp.abs(x)))


def _cparams(n_grid):
    return pltpu.CompilerParams(dimension_semantics=("arbitrary",) * n_grid,
                                vmem_limit_bytes=VMEM_LIMIT)


def _top3_rows(g, blk, nblk):
    sel = jnp.zeros(g.shape, jnp.bool_)
    for _ in range(MOBA_TOPK):
        m = jnp.max(g, axis=0, keepdims=True)
        idx = jnp.min(jnp.where(g == m, blk, nblk), axis=0, keepdims=True)
        pick = blk == idx
        sel = jnp.logical_or(sel, pick)
        g = jnp.where(pick, -jnp.inf, g)
    return sel


def _ada_kernel(c_ref, w_ref, b_ref, o_ref):
    s = _silu(c_ref[...])
    o_ref[0] = _mm_hp(s, w_ref[...], 1, 0) + b_ref[...]


def _ada_mod(c_all, w, b):
    m, d = c_all.shape
    n_parts = w.shape[1] // d
    return pl.pallas_call(
        _ada_kernel,
        grid=(n_parts,),
        in_specs=[pl.BlockSpec((m, d), lambda j: (0, 0)),
                  pl.BlockSpec((d, d), lambda j: (0, j)),
                  pl.BlockSpec((1, d), lambda j: (0, j))],
        out_specs=pl.BlockSpec((1, m, d), lambda j: (j, 0, 0)),
        out_shape=jax.ShapeDtypeStruct((n_parts, m, d), F32),
        compiler_params=_cparams(1),
        name="ada_mod",
    )(c_all, w, b.reshape(1, n_parts * d))


def _t5_bucket_np(rel):
    n = np.maximum(rel, 0)
    max_exact = NUM_BUCKETS // 2
    nf = np.maximum(n, 1).astype(np.float64)
    large = max_exact + (np.log(nf / max_exact) / math.log(MAX_DISTANCE / max_exact)
                         * (NUM_BUCKETS - max_exact)).astype(np.int64)
    large = np.minimum(large, NUM_BUCKETS - 1)
    return np.where(n < max_exact, n, large).astype(np.int32)


def _bias_kernel(tab_ref, idx_ref, o_ref, *, scale):
    h = pl.program_id(0)
    idx = idx_ref[...]
    acc = jnp.zeros(idx.shape, F32)
    for b in range(NUM_BUCKETS):
        acc = jnp.where(idx == b, tab_ref[b, h], acc)
    o_ref[0] = jnp.where(idx == -1, NEG_INF, acc * scale)


def _bias_tables(rel_bias, idx, scale=1.0):
    r, c = idx.shape
    return pl.pallas_call(
        functools.partial(_bias_kernel, scale=scale),
        grid=(H_A,),
        in_specs=[pl.BlockSpec(memory_space=pltpu.SMEM),
                  pl.BlockSpec((r, c), lambda h: (0, 0))],
        out_specs=pl.BlockSpec((1, r, c), lambda h: (h, 0, 0)),
        out_shape=jax.ShapeDtypeStruct((H_A, r, c), F32),
        compiler_params=_cparams(1),
        name="t5_bias",
    )(rel_bias, jnp.asarray(idx))


def _even_in_kernel(x_ref, sc_ref, sh_ref, nw_ref, w_ref, qnw_ref, knw_ref, seg_ref, cos_ref, sin_ref,
                    qa_ref, ka_ref, va_ref, kbf_ref, vt_ref, ga_ref, qb_ref, kb_ref, vb_ref, gb_ref, km_ref):
    x = x_ref[0]
    ms = jnp.mean(x * x, axis=-1, keepdims=True)
    h = (x * lax.rsqrt(ms + EPS) * nw_ref[...]) * (1.0 + sc_ref[0]) + sh_ref[0]
    hb = _bf(h)

    def proj(lo, hi):
        return jnp.dot(hb, w_ref[:, lo:hi], preferred_element_type=F32)

    def head_rms(t, w_row):
        ss = jnp.dot(_bf(t * t), seg_ref[...], preferred_element_type=F32)
        return t * lax.rsqrt(ss * (1.0 / HD_A) + EPS) * w_row

    qa_ref[0] = head_rms(proj(0, 512), qnw_ref[...])
    ka = head_rms(proj(512, 1024), knw_ref[...])
    ka_ref[0] = ka
    kbf_ref[0] = _bf(ka)
    for j in range(ka.shape[0] // MOBA_BLOCK):
        km_ref[0, j] = jnp.mean(ka[j * MOBA_BLOCK:(j + 1) * MOBA_BLOCK], axis=0, keepdims=True)
    va = proj(1024, 1536)
    va_ref[0] = va
    vat = va.T
    ones_pad = (lax.broadcasted_iota(jnp.int32, (V_ROWS - HD_A, vat.shape[1]), 0) == 0).astype(F32)
    vt_ref[0] = _bf(jnp.concatenate(
        [piece for h in range(H_A) for piece in (vat[h * HD_A:(h + 1) * HD_A], ones_pad)], axis=0))
    ga_ref[0] = proj(1536, 2048).astype(ga_ref.dtype)

    lane = lax.broadcasted_iota(jnp.int32, (1, 256), 1) % DK_B
    first_half = lane < (DK_B // 2)
    cos = cos_ref[...]
    sin = sin_ref[...]

    def rotary(t):
        up = pltpu.roll(t, 256 - DK_B // 2, 1)
        dn = pltpu.roll(t, DK_B // 2, 1)
        return t * cos + jnp.where(first_half, up, dn) * sin

    qb_ref[0] = rotary(proj(2048, 2304)).astype(qb_ref.dtype)
    kb_ref[0] = (rotary(proj(2304, 2560)) * (DK_B ** -0.5)).astype(kb_ref.dtype)
    vb_ref[0] = proj(2560, 3072).astype(vb_ref.dtype)
    gb_ref[0] = proj(3072, 3584).astype(gb_ref.dtype)


def _mod_spec(tm, d, slab):
    if slab is None:
        return pl.BlockSpec((1, 1, d), lambda b, i: (b, 0, 0))
    return pl.BlockSpec((1, tm, d), lambda b, i: (slab, i, 0))


def _even_in(x, scale, shift, norm_w, w_bf, qnw, knw, seg, cos, sin, slabs, act):
    nb, s, d = x.shape
    tm = PROJ_ROWS
    ns = s // tm
    nkb = tm // MOBA_BLOCK
    sc_spec, sh_spec = (_mod_spec(tm, d, p) for p in (slabs or (None, None)))
    row = lambda c: pl.BlockSpec((1, tm, c), lambda b, i: (b, i, 0))
    const = lambda shp: pl.BlockSpec(shp, lambda b, i: (0,) * len(shp))
    out_shape = (
        jax.ShapeDtypeStruct((nb, s, 512), F32),
        jax.ShapeDtypeStruct((nb, s, 512), F32),
        jax.ShapeDtypeStruct((nb, s, 512), F32),
        jax.ShapeDtypeStruct((nb, s, 512), BF16),
        jax.ShapeDtypeStruct((nb, H_A * V_ROWS, s), BF16),
        jax.ShapeDtypeStruct((nb, s, 512), act),
        jax.ShapeDtypeStruct((nb, s, 256), act),
        jax.ShapeDtypeStruct((nb, s, 256), act),
        jax.ShapeDtypeStruct((nb, s, 512), act),
        jax.ShapeDtypeStruct((nb, s, 512), act),
        jax.ShapeDtypeStruct((nb, ns * nkb, 1, 512), F32),
    )
    out_specs = (row(512), row(512), row(512), row(512),
                 pl.BlockSpec((1, H_A * V_ROWS, tm), lambda b, i: (b, 0, i)),
                 row(512), row(256), row(256), row(512), row(512),
                 pl.BlockSpec((1, nkb, 1, 512), lambda b, i: (b, i, 0, 0)))
    return pl.pallas_call(
        _even_in_kernel,
        grid=(nb, ns),
        in_specs=[row(d), sc_spec, sh_spec, const((1, d)), const((d, 3584)),
                  const((1, 512)), const((1, 512)), const((512, 512)),
                  pl.BlockSpec((tm, 256), lambda b, i: (i, 0)),
                  pl.BlockSpec((tm, 256), lambda b, i: (i, 0))],
        out_specs=out_specs,
        out_shape=out_shape,
        compiler_params=_cparams(2),
        name="even_in",
    )(x, scale, shift, norm_w, w_bf, qnw, knw, seg, cos, sin)


MOBA_HS = 8
V_ROWS = HD_A + 8
FAR_KEYS = 2 * MOBA_BLOCK


def _moba_prompt_select(tab_ref, q_ref, km_ref, rbf_ref, rbs_ref, qs_ref):
    hg = pl.program_id(1)
    qi = pl.program_id(2)
    nblk = km_ref.shape[1]
    lane = lax.broadcasted_iota(jnp.int32, (1, 128), 1)
    blk = lax.broadcasted_iota(jnp.int32, (nblk, MOBA_BLOCK), 0)
    for hl in range(MOBA_HS):
        pr, hh = divmod(hl, 2)
        pc = slice(pr * 128, (pr + 1) * 128)
        qm = jnp.where((lane // HD_A) == hh, q_ref[0, :, pc], 0.0)
        gate = _mm_hp(km_ref[0, :, pc], qm, 1, 1)
        gate = jnp.where(blk < qi, gate, NEG_INF)
        sel = jnp.logical_and(_top3_rows(gate, blk, nblk), blk < qi)
        far_c = tab_ref[NUM_BUCKETS - 1, MOBA_HS * hg + hl] * LOG2E
        rbf_ref[hl] = jnp.where(jnp.logical_and(sel, blk < qi - 1), far_c, NEG_INF)
        rbs_ref[hl] = jnp.where(jnp.logical_or(sel, blk == qi), 0.0, NEG_INF)
        qs_ref[hl] = _bf(qm * (HD_A ** -0.5 * LOG2E))


def _moba_prompt_attend(k_ref, vt_ref, bias_ref, ga_ref, o_ref, rbf_ref, rbs_ref, qs_ref):
    qi = pl.program_id(2)

    def visit(carry, off, nkeys, extra_fn=None, block_rows=None):
        ss = []
        for hl in range(MOBA_HS):
            pr = hl // 2
            kj = k_ref[0, pl.ds(off, nkeys), pr * 128:(pr + 1) * 128]
            ss.append(_dg(kj, qs_ref[hl], 1, 1))
        stats, ps = [], []
        for hl in range(MOBA_HS):
            m = carry[hl][0]
            if block_rows is None:
                s = extra_fn(hl, ss[hl])
                mn = jnp.maximum(m, jnp.max(s, axis=0, keepdims=True))
                p = jnp.exp2(s - mn)
            else:
                halves = [ss[hl][i * MOBA_BLOCK:(i + 1) * MOBA_BLOCK] for i in range(nkeys // MOBA_BLOCK)]
                rows = block_rows(hl)
                mn = m
                for sh, r in zip(halves, rows):
                    mn = jnp.maximum(mn, jnp.max(sh, axis=0, keepdims=True) + r)
                p = jnp.concatenate([jnp.exp2(sh - (mn - r)) for sh, r in zip(halves, rows)], axis=0)
            stats.append((mn, jnp.exp2(m - mn)))
            ps.append(_bf(p))
        pvs = []
        for hl in range(MOBA_HS):
            vj = vt_ref[0, hl * V_ROWS:(hl + 1) * V_ROWS, pl.ds(off, nkeys)]
            pvs.append(jnp.dot(vj, ps[hl], preferred_element_type=F32))
        return tuple((stats[hl][0], stats[hl][1] * carry[hl][1] + pvs[hl]) for hl in range(MOBA_HS))

    def far_body(jp, carry):
        off = pl.multiple_of(jp * FAR_KEYS, FAR_KEYS)

        def rows(hl):
            return rbf_ref[hl, pl.ds(2 * jp, 1), :], rbf_ref[hl, pl.ds(2 * jp + 1, 1), :]

        return visit(carry, off, FAR_KEYS, block_rows=rows)

    init = tuple((jnp.full((1, MOBA_BLOCK), -jnp.inf, F32), jnp.zeros((V_ROWS, MOBA_BLOCK), F32))
                 for _ in range(MOBA_HS))
    carry = lax.fori_loop(0, qi // 2, far_body, init)
    js = jnp.maximum(qi - 1, 0)
    first = qi == 0
    top_tab = jnp.where(first, 0, 1)
    bot_mask = jnp.where(first, NEG_INF, 0.0)

    def near_extra(hl, s):
        top = s[:MOBA_BLOCK] + bias_ref[hl, top_tab] + rbs_ref[hl, pl.ds(js, 1), :]
        bot = s[MOBA_BLOCK:] + (bias_ref[hl, 0] + bot_mask)
        return jnp.concatenate([top, bot], axis=0)

    carry = visit(carry, pl.multiple_of(js * MOBA_BLOCK, MOBA_BLOCK), FAR_KEYS, near_extra)
    for pr in range(MOBA_HS // 2):
        accs = [carry[2 * pr + hh][1] for hh in range(2)]
        outs = [a[0:HD_A] * (1.0 / a[HD_A:HD_A + 1]) for a in accs]
        o = jnp.concatenate(outs, axis=0).T
        pc = slice(pr * 128, (pr + 1) * 128)
        o_ref[0, :, pc] = (o * _silu(ga_ref[0, :, pc].astype(F32))).astype(o_ref.dtype)


N_PAST_BLK = PAST_LEN // MOBA_BLOCK
N_PAGES = PAST_LEN // PAGE_SIZE
N_ROWS_S = 32
SEQ_PER_TILE = 2


def _page_copies(pt_ref, ck_hbm, cv_hbm, kt_buf, vt_buf, sems, seq, sl):
    cps = []
    for p in range(N_PAGES):
        dst = pl.ds(p * PAGE_SIZE, PAGE_SIZE)
        cps.append(pltpu.make_async_copy(ck_hbm.at[pt_ref[seq, p]], kt_buf.at[sl, :, dst], sems.at[0, sl]))
        cps.append(pltpu.make_async_copy(cv_hbm.at[pt_ref[seq, p]], vt_buf.at[sl, :, dst], sems.at[1, sl]))
    return cps


def _moba_sample_seqs(qrep_ref, knew_ref, vnew_ref, ga_ref, bias_ref, bfar_ref, kt_buf, vt_buf, kpad, vpad, o_ref):
    t_new = knew_ref.shape[1]
    rowh = lax.broadcasted_iota(jnp.int32, (N_ROWS_S, W_A), 0) % H_A
    laneh = lax.broadcasted_iota(jnp.int32, (N_ROWS_S, W_A), 1) // HD_A
    own_head = rowh == laneh
    blocks = [slice(n * MOBA_BLOCK, (n + 1) * MOBA_BLOCK) for n in range(N_PAST_BLK)]
    bfar = bfar_ref[...]
    scored = []
    for j in range(SEQ_PER_TILE):
        qf = jnp.where(own_head, qrep_ref[j], 0.0) * (HD_A ** -0.5)
        qbd = _bf(qf)
        q2 = jnp.concatenate([qbd, _bf(qf - qbd.astype(F32))], axis=0)
        kpad[j, 0:t_new, :] = knew_ref[j]
        vpad[j, 0:t_new, :] = vnew_ref[j]
        s_past = []
        for n in range(N_PAST_BLK):
            s2 = jnp.dot(q2, _bf(kt_buf[j, :, blocks[n]]), preferred_element_type=F32)
            s_past.append(s2[0:N_ROWS_S] + s2[N_ROWS_S:])
        s_own = _dg(qbd, _bf(kpad[j]), 1, 1) + bias_ref[:, MOBA_BLOCK:]
        scored.append((s_past, s_own))
    probs = []
    for s_past, s_own in scored:
        g = [jnp.sum(s, axis=1, keepdims=True) for s in s_past]
        sel = [jnp.zeros((N_ROWS_S, 1), jnp.bool_) for _ in range(N_PAST_BLK)]
        for _ in range(MOBA_TOPK):
            m = functools.reduce(jnp.maximum, g)
            idx = functools.reduce(jnp.minimum, [jnp.where(g[n] == m, n, N_PAST_BLK) for n in range(N_PAST_BLK)])
            for n in range(N_PAST_BLK):
                pick = idx == n
                sel[n] = jnp.logical_or(sel[n], pick)
                g[n] = jnp.where(pick, -jnp.inf, g[n])
        logits = [s_past[n] + jnp.where(sel[n], bfar, NEG_INF) for n in range(N_PAST_BLK - 1)]
        logits.append(s_past[-1] + bias_ref[:, 0:MOBA_BLOCK] + jnp.where(sel[-1], 0.0, NEG_INF))
        m = jnp.max(s_own, axis=1, keepdims=True)
        for s in logits:
            m = jnp.maximum(m, jnp.max(s, axis=1, keepdims=True))
        p_own = jnp.exp(s_own - m)
        ps = [jnp.exp(s - m) for s in logits]
        l = functools.reduce(jnp.add, [jnp.sum(p, axis=1, keepdims=True) for p in ps + [p_own]])
        probs.append(([_bf(p) for p in ps], _bf(p_own), l))
    for j, (ps, p_own, l) in enumerate(probs):
        acc = jnp.dot(p_own, _bf(vpad[j]), preferred_element_type=F32)
        for n in range(N_PAST_BLK):
            acc = acc + _dg(ps[n], _bf(vt_buf[j, :, blocks[n]]), 1, 1)
        o = jnp.where(own_head, acc * (1.0 / l), 0.0)
        o = jnp.sum(o.reshape(t_new, H_A, W_A), axis=1)
        o_ref[j] = o * _silu(ga_ref[j])


def _moba_kernel(pt_ref, tab_ref, q_ref, k_ref, vt_ref, km_ref, bias_ref, ga_ref,
                 qrep_ref, knew_ref, vnew_ref, gas_ref, bias_s_ref, bfar_ref, ck_hbm, cv_hbm,
                 o_ref, os_ref, rbf_ref, rbs_ref, qs_ref, kt_buf, vt_buf, kpad, vpad, sems):
    step = pl.program_id(0) * pl.num_programs(2) + pl.program_id(2)
    n_steps = pl.num_programs(0) * pl.num_programs(2)
    seq0 = SEQ_PER_TILE * step
    copies = functools.partial(_page_copies, pt_ref, ck_hbm, cv_hbm, kt_buf, vt_buf, sems)

    @pl.when(step == 0)
    def _():
        kpad[...] = jnp.zeros(kpad.shape, F32)
        vpad[...] = jnp.zeros(vpad.shape, F32)
        for j in range(SEQ_PER_TILE):
            for cp in copies(j, j):
                cp.start()

    for j in range(SEQ_PER_TILE):
        for cp in copies(seq0 + j, j):
            cp.wait()
    _moba_sample_seqs(qrep_ref, knew_ref, vnew_ref, gas_ref, bias_s_ref, bfar_ref, kt_buf, vt_buf, kpad, vpad, os_ref)
    _moba_prompt_select(tab_ref, q_ref, km_ref, rbf_ref, rbs_ref, qs_ref)

    @pl.when(step + 1 < n_steps)
    def _():
        for j in range(SEQ_PER_TILE):
            for cp in copies(seq0 + SEQ_PER_TILE + j, j):
                cp.start()

    _moba_prompt_attend(k_ref, vt_ref, bias_ref, ga_ref, o_ref, rbf_ref, rbs_ref, qs_ref)


def _moba(page_table, rel_bias, qa, kbf, vt, kmean, bias_t, ga,
          qrep, knew, vnew, ga_s, bias_s, bfar, cache_kt, cache_vt):
    nb, s, _ = qa.shape
    nq = s // MOBA_BLOCK
    nseq, t_new, _ = knew.shape
    assert H_A == MOBA_HS and nseq == SEQ_PER_TILE * nb * nq
    tile = pl.BlockSpec((1, MOBA_BLOCK, W_A), lambda b, hg, i, pt: (b, i, 0))
    per_b = lambda shp: pl.BlockSpec((1,) + shp, lambda b, hg, i, pt: (b, 0, 0))
    seqs = lambda r: pl.BlockSpec((SEQ_PER_TILE, r, W_A), lambda b, hg, i, pt: (b * nq + i, 0, 0))
    const = lambda a: pl.BlockSpec(a.shape, lambda b, hg, i, pt: (0,) * a.ndim)
    grid_spec = pltpu.PrefetchScalarGridSpec(
        num_scalar_prefetch=1,
        grid=(nb, 1, nq),
        in_specs=[pl.BlockSpec(memory_space=pltpu.SMEM),
                  tile, per_b((s, W_A)), per_b((H_A * V_ROWS, s)), per_b((nq, W_A)), const(bias_t), tile,
                  seqs(N_ROWS_S), seqs(t_new), seqs(t_new), seqs(t_new), const(bias_s), const(bfar),
                  pl.BlockSpec(memory_space=pl.ANY), pl.BlockSpec(memory_space=pl.ANY)],
        out_specs=(tile, seqs(t_new)),
        scratch_shapes=[pltpu.VMEM((MOBA_HS, nq, MOBA_BLOCK), F32),
                        pltpu.VMEM((MOBA_HS, nq, MOBA_BLOCK), F32),
                        pltpu.VMEM((MOBA_HS, MOBA_BLOCK, 128), BF16),
                        pltpu.VMEM((SEQ_PER_TILE, W_A, PAST_LEN), F32),
                        pltpu.VMEM((SEQ_PER_TILE, W_A, PAST_LEN), F32),
                        pltpu.VMEM((SEQ_PER_TILE, PAGE_SIZE, W_A), F32),
                        pltpu.VMEM((SEQ_PER_TILE, PAGE_SIZE, W_A), F32),
                        pltpu.SemaphoreType.DMA((2, SEQ_PER_TILE))],
    )
    return pl.pallas_call(
        _moba_kernel,
        grid_spec=grid_spec,
        out_shape=(jax.ShapeDtypeStruct((nb, s, W_A), BF16),
                   jax.ShapeDtypeStruct((nseq, t_new, W_A), F32)),
        compiler_params=_cparams(3),
        name="moba",
    )(page_table, rel_bias, qa, kbf, vt, kmean, bias_t, ga, qrep, knew, vnew, ga_s, bias_s, bfar, cache_kt, cache_vt)


def _pad_rows(ref, scratch, s, t):
    if t == CHUNK:
        return ref[s]
    scratch[s] = jnp.zeros(scratch.shape[1:], scratch.dtype)
    scratch[s, 0:t, :] = ref[s].astype(scratch.dtype)
    return scratch[s]


def _ret_kernel(q_ref, k_ref, v_ref, g_ref, st0_ref, dmat_ref, qdec_ref, kdec_ref, gl_ref,
                o_ref, st_ref, qpad, kpad, vpad, gpad, *, t, lq, nbs):
    c = pl.program_id(1)

    @pl.when(c == 0)
    def _():
        st_ref[...] = st0_ref[...]

    lane = lax.broadcasted_iota(jnp.int32, (1, 128), 1)
    rowsel = lax.broadcasted_iota(jnp.int32, (128, 1), 0) < DK_B
    work = []
    for s in range(nbs):
        q = _pad_rows(q_ref, qpad, s, t)[0:lq]
        k = _pad_rows(k_ref, kpad, s, t)
        v = _pad_rows(v_ref, vpad, s, t)
        for hp in range(H_B // 2):
            cols = slice(hp * 128, (hp + 1) * 128)
            kp = _bf(k[:, cols])
            st = st_ref[s, cols, :]
            kd = _bf(k[:, cols] * kdec_ref[0:k.shape[0], cols])
            for hh in range(2):
                h = 2 * hp + hh
                qm = jnp.where((lane // DK_B) == hh, q[:, cols], 0.0)
                vh = _bf(v[:, h * DV_B:(h + 1) * DV_B])
                sc = _dg(_bf(qm), kp, 1, 1)
                so = _mm(qm * qdec_ref[0:lq, cols], st)
                upd = _dg(kd, vh, 0, 0)
                work.append((s, hp, hh, sc, so, upd, vh, st))
    outs = {}
    for (s, hp, hh, sc, so, upd, vh, st) in work:
        h = 2 * hp + hh
        o = jnp.dot(_bf(sc * dmat_ref[h, 0:lq, 0:sc.shape[1]]), vh, preferred_element_type=F32) + so
        outs[(s, h)] = o * lax.rsqrt(jnp.mean(o * o, axis=-1, keepdims=True) + EPS)
    for i in range(0, len(work), 2):
        s, hp, _, _, _, upd0, _, st = work[i]
        cols = slice(hp * 128, (hp + 1) * 128)
        st_ref[s, cols, :] = st * gl_ref[cols, :] + jnp.where(rowsel, upd0, work[i + 1][5])
    for s in range(nbs):
        g = _pad_rows(g_ref, gpad, s, t)[0:lq].astype(F32)
        o = jnp.concatenate([outs[(s, h)] for h in range(H_B)], axis=1)
        o_ref[s] = (o * _silu(g))[0:t].astype(o_ref.dtype)


def _retention(q, k, v, g, st0, dmat, qdec, kdec, gl, lq, nbs):
    nb, nc, t, _ = q.shape
    row = lambda c_: pl.BlockSpec((nbs, None, t, c_), lambda b, c: (b, c, 0, 0))
    const = lambda a: pl.BlockSpec(a.shape, lambda b, c: (0,) * a.ndim)
    st_spec = pl.BlockSpec((nbs, H_B * DK_B, DV_B), lambda b, c: (b, 0, 0))
    pad = lambda c_: pltpu.VMEM((nbs, t if t == CHUNK else SHORT_ROWS, c_), F32)
    return pl.pallas_call(
        functools.partial(_ret_kernel, t=t, lq=lq, nbs=nbs),
        grid=(nb // nbs, nc),
        in_specs=[row(256), row(256), row(512), row(512), st_spec,
                  const(dmat), const(qdec), const(kdec), const(gl)],
        out_specs=(row(512), st_spec),
        out_shape=(jax.ShapeDtypeStruct((nb, nc, t, 512), q.dtype),
                   jax.ShapeDtypeStruct((nb, H_B * DK_B, DV_B), F32)),
        scratch_shapes=[pad(256), pad(256), pad(512), pad(512)],
        compiler_params=_cparams(2),
        name="retention",
    )(q, k, v, g, st0, dmat, qdec, kdec, gl)


def _out_kernel(a_ref, b_ref, x_ref, g_ref, w_ref, o_ref):
    half = w_ref.shape[0] // 2
    y = (jnp.dot(_bf(a_ref[0]), w_ref[0:half, :], preferred_element_type=F32)
         + jnp.dot(_bf(b_ref[0]), w_ref[half:, :], preferred_element_type=F32))
    o_ref[0] = x_ref[0] + g_ref[0] * y


def _out_proj(a, b, x, gate, w_bf, slab):
    nb, s, d = x.shape
    tm = min(OUT_ROWS, s)
    g_spec = _mod_spec(tm, d, slab)
    row = lambda c: pl.BlockSpec((1, tm, c), lambda bb, i: (bb, i, 0))
    return pl.pallas_call(
        _out_kernel,
        grid=(nb, s // tm),
        in_specs=[row(512), row(512), row(d), g_spec, pl.BlockSpec(w_bf.shape, lambda bb, i: (0, 0))],
        out_specs=row(d),
        out_shape=jax.ShapeDtypeStruct((nb, s, d), F32),
        compiler_params=_cparams(2),
        name="out_proj",
    )(a, b, x, gate, w_bf)


def _odd_in_kernel(x_ref, sc_ref, sh_ref, nw_ref, w_ref, wdt_ref, sguw_ref, sgub_ref, dtb_ref,
                   oc_ref, zg_ref, xbc_ref, dt_ref, *maybe_v_ref):
    x = x_ref[0]
    tm = x.shape[0]
    ms = jnp.mean(x * x, axis=-1, keepdims=True)
    h = (x * lax.rsqrt(ms + EPS) * nw_ref[...]) * (1.0 + sc_ref[0]) + sh_ref[0]
    hb = _bf(h)

    def proj(lo, hi):
        return jnp.dot(hb, w_ref[:, lo:hi], preferred_element_type=F32)

    u = _gelu_tanh(proj(0, 512))
    v = _gelu_tanh(proj(512, 1024))
    mu = jnp.mean(v, axis=-1, keepdims=True)
    vc = v - mu
    v = vc * lax.rsqrt(jnp.mean(vc * vc, axis=-1, keepdims=True) + EPS)
    for v_ref in maybe_v_ref:
        v_ref[0] = v
    ii = lax.broadcasted_iota(jnp.int32, (CHUNK, CHUNK), 0)
    jj = lax.broadcasted_iota(jnp.int32, (CHUNK, CHUNK), 1)
    rows = []
    for ci in range(tm // CHUNK):
        cols = []
        for g in range(G_C):
            wg = jnp.where(ii >= jj, sguw_ref[g], 0.0)
            cols.append(_mm(wg, v[ci * CHUNK:(ci + 1) * CHUNK, g * 128:(g + 1) * 128]))
        rows.append(jnp.concatenate(cols, axis=1) + sgub_ref[...])
    sg = jnp.concatenate(rows, axis=0) if len(rows) > 1 else rows[0]
    oc_ref[0] = (u * sg * _silu(proj(1024, 1536))).astype(oc_ref.dtype)
    zg_ref[0] = proj(1536, 2048).astype(zg_ref.dtype)
    xbc_ref[0] = proj(2048, 3072)
    dt_ref[0] = _softplus(jnp.dot(hb, wdt_ref[...], preferred_element_type=F32) + dtb_ref[...])


def _odd_in(x, scale, shift, norm_w, w_bf, w_dt, sgu_w, sgu_b_tab, dt_bias, slabs, act, emit_v):
    nb, s, d = x.shape
    tm = PROJ_ROWS
    sc_spec, sh_spec = (_mod_spec(tm, d, p) for p in (slabs or (None, None)))
    row = lambda c: pl.BlockSpec((1, tm, c), lambda b, i: (b, i, 0))
    const = lambda shp: pl.BlockSpec(shp, lambda b, i: (0,) * len(shp))
    return pl.pallas_call(
        _odd_in_kernel,
        grid=(nb, s // tm),
        in_specs=[row(d), sc_spec, sh_spec, const((1, d)), const((d, 3072)), const(w_dt.shape),
                  const(sgu_w.shape), const(sgu_b_tab.shape), const((1, 512))],
        out_specs=(row(512), row(512), row(1024), row(512)) + ((row(512),) if emit_v else ()),
        out_shape=(jax.ShapeDtypeStruct((nb, s, 512), act),
                   jax.ShapeDtypeStruct((nb, s, 512), act),
                   jax.ShapeDtypeStruct((nb, s, 1024), F32),
                   jax.ShapeDtypeStruct((nb, s, 512), F32),
                   ) + ((jax.ShapeDtypeStruct((nb, s, 512), F32),) if emit_v else ()),
        compiler_params=_cparams(2),
        name="odd_in",
    )(x, scale, shift, norm_w, w_bf, w_dt, sgu_w, sgu_b_tab, dt_bias)


def _ssd_kernel(xbc_ref, dt_ref, zg_ref, tail_ref, st0_ref, cw_ref, cb_ref, alog_ref, dsk_ref, nw_ref,
                tri_ref, sel_ref, y_ref, st_ref, ext, dtpad, zpad, *, t, lq, nc, nbs):
    rows = dtpad.shape[1]
    c = pl.program_id(1)

    @pl.when(c == 0)
    def _():
        st_ref[...] = st0_ref[...]
        ext[...] = jnp.zeros(ext.shape, F32)
        ext[:, 0:8, :] = tail_ref[...]

    ii = lax.broadcasted_iota(jnp.int32, (lq, rows), 0)
    jj = lax.broadcasted_iota(jnp.int32, (lq, rows), 1)
    lane = lax.broadcasted_iota(jnp.int32, (1, 128), 1)
    hpg = H_D // G_D
    neg_a = -jnp.exp(alog_ref[...])
    seqs = []
    for s in range(nbs):
        ext[s, 8:8 + t, :] = xbc_ref[s]
        win = ext[s]
        conv = cb_ref[...] + win[8:8 + rows] * cw_ref[CONV_W - 1:CONV_W, :]
        for w in range(CONV_W - 1):
            back = CONV_W - 1 - w
            conv = conv + pltpu.roll(win, back, 0)[8:8 + rows] * cw_ref[w:w + 1, :]
        if nc > 1:
            ext[s, 0:8, :] = ext[s, rows:rows + 8, :]
        xc = _silu(conv)
        dt = _pad_rows(dt_ref, dtpad, s, t)
        cum = _mm_exact_lhs(tri_ref[0:rows, 0:rows], dt * neg_a, 1, 0)
        seqs.append((xc, dt, cum))
    st1 = []
    for s in range(nbs):
        xc, dt, cum = seqs[s]
        xh = xc[:, 0:W_D]
        last = cum[rows - 1:rows, :]
        dtx = _bf(xh * dt)
        xw = _bf(xh * (jnp.exp(last - cum) * dt))
        cum_rows = _mm_exact_lhs(sel_ref[...], cum, 1, 1)
        per_g = []
        for g in range(G_D):
            bg = _bf(xc[:, W_D + g * N_D:W_D + (g + 1) * N_D])
            cg = _bf(xc[0:lq, W_D + G_D * N_D + g * N_D:W_D + G_D * N_D + (g + 1) * N_D])
            gr = slice(g * hpg * P_D, (g + 1) * hpg * P_D)
            cb = _dg(cg, bg, 1, 1)
            yoff = _dg(cg, _bf(st_ref[s, gr, :]), 1, 1)
            upd = _dg(xw[:, gr], bg, 0, 0)
            per_g.append((cb, yoff, upd))
        st1.append((dtx, cum_rows, per_g))
    for s in range(nbs):
        xc, dt, cum = seqs[s]
        dtx, cum_rows, per_g = st1[s]
        ecum = jnp.exp(cum[0:lq])
        elast = jnp.exp(cum[rows - 1:rows, :])
        ys = []
        for g in range(G_D):
            cb, yoff, upd = per_g[g]
            for pr in range(hpg // 2):
                l0 = g * hpg * P_D + pr * 128
                yh = []
                for hh in range(2):
                    h = g * hpg + pr * 2 + hh
                    col = jnp.broadcast_to(cum[0:lq, h * P_D:h * P_D + 1], (lq, rows))
                    seg = jnp.minimum(col - cum_rows[h:h + 1, :], 0.0)
                    mh = jnp.where(ii >= jj, cb * jnp.exp(seg), 0.0)
                    yh.append(jnp.dot(_bf(mh), dtx[:, l0:l0 + 128], preferred_element_type=F32))
                ypair = jnp.where(lane < P_D, yh[0], yh[1])
                ys.append(ypair + yoff[:, pr * 128:(pr + 1) * 128] * ecum[:, l0:l0 + 128])
            for hl in range(hpg):
                h = g * hpg + hl
                r = slice(h * P_D, (h + 1) * P_D)
                dec = jnp.broadcast_to(elast[0:1, h * P_D:h * P_D + 1], (P_D, N_D))
                st_ref[s, r, :] = st_ref[s, r, :] * dec + upd[hl * P_D:(hl + 1) * P_D, :]
        y = jnp.concatenate(ys, axis=1)
        zg = _pad_rows(zg_ref, zpad, s, t)[0:lq].astype(F32)
        y = (y + xc[0:lq, 0:W_D] * dsk_ref[...]) * _silu(zg)
        gw = W_D // G_D
        outs = []
        for g in range(G_D):
            yg = y[:, g * gw:(g + 1) * gw]
            outs.append(yg * lax.rsqrt(jnp.mean(yg * yg, axis=-1, keepdims=True) + EPS))
        y_ref[s] = (jnp.concatenate(outs, axis=1) * nw_ref[...])[0:t].astype(y_ref.dtype)


def _ssd(xbc, dt, zg, tail, st0, conv_w, conv_b, a_log, d_skip, norm_w, tri, sel, lq, nbs):
    nb, nc, t, _ = xbc.shape
    rows = t if t == CHUNK else SHORT_ROWS
    row = lambda c_: pl.BlockSpec((nbs, None, t, c_), lambda b, c: (b, c, 0, 0))
    const = lambda a: pl.BlockSpec(a.shape, lambda b, c: (0,) * a.ndim)
    st_spec = pl.BlockSpec((nbs, H_D * P_D, N_D), lambda b, c: (b, 0, 0))
    return pl.pallas_call(
        functools.partial(_ssd_kernel, t=t, lq=lq, nc=nc, nbs=nbs),
        grid=(nb // nbs, nc),
        in_specs=[row(1024), row(512), row(512),
                  pl.BlockSpec((nbs, 8, CONV_DIM), lambda b, c: (b, 0, 0)), st_spec,
                  const(conv_w), const(conv_b), const(a_log), const(d_skip), const(norm_w),
                  const(tri), const(sel)],
        out_specs=(row(512), st_spec),
        out_shape=(jax.ShapeDtypeStruct((nb, nc, t, 512), zg.dtype),
                   jax.ShapeDtypeStruct((nb, H_D * P_D, N_D), F32)),
        scratch_shapes=[pltpu.VMEM((nbs, rows + 8, CONV_DIM), F32),
                        pltpu.VMEM((nbs, rows, 512), F32), pltpu.VMEM((nbs, rows, 512), F32)],
        compiler_params=_cparams(2),
        name="ssd",
    )(xbc, dt, zg, tail, st0, conv_w, conv_b, a_log, d_skip, norm_w, tri, sel)


def _rotary_tables(pos):
    half = DK_B // 2
    inv = (np.float32(1.0) / np.float32(10000.0) ** (np.arange(half, dtype=np.float32) / np.float32(half)))
    ang = (np.asarray(pos).astype(np.float32)[:, None] * inv.astype(np.float32)[None, :]).astype(np.float64)
    cos, sin = np.cos(ang), np.sin(ang)
    cos_t = np.tile(np.concatenate([cos, cos], axis=1), (1, H_B))
    sin_t = np.tile(np.concatenate([-sin, sin], axis=1), (1, H_B))
    return jnp.asarray(cos_t, F32), jnp.asarray(sin_t, F32)


def _retention_tables(chunk_len):
    log_g = np.log(1.0 - 2.0 ** (-5.0 - np.arange(H_B, dtype=np.float64)))
    idx = np.arange(CHUNK, dtype=np.float64)
    diff = idx[:, None] - idx[None, :]
    dmat = np.where(diff[None] >= 0, np.exp(np.maximum(diff, 0.0)[None] * log_g[:, None, None]), 0.0)
    qdec = np.exp((idx + 1.0)[:, None] * log_g[None, :])
    kdec = np.where(idx[:, None] < chunk_len, np.exp((chunk_len - 1.0 - idx)[:, None] * log_g[None, :]), 0.0)
    gl = np.exp(chunk_len * log_g)
    return (jnp.asarray(dmat, F32),
            jnp.asarray(np.repeat(qdec, DK_B, axis=1), F32),
            jnp.asarray(np.repeat(kdec, DK_B, axis=1), F32),
            jnp.asarray(np.repeat(np.repeat(gl, DK_B)[:, None], DV_B, axis=1), F32))


def _prompt_bias_idx():
    kk = np.arange(MOBA_BLOCK)[:, None]
    qq = np.arange(MOBA_BLOCK)[None, :]
    diag = np.where(qq >= kk, _t5_bucket_np(qq - kk), -1)
    sub = _t5_bucket_np(qq + MOBA_BLOCK - kk)
    return np.concatenate([diag, sub], axis=0).astype(np.int32)


def _sample_bias_idx(t_new):
    row_t = (np.arange(N_ROWS_S) // H_A)[:, None]
    qpos = PAST_LEN + row_t
    near = _t5_bucket_np(qpos - (PAST_LEN - MOBA_BLOCK + np.arange(MOBA_BLOCK))[None, :])
    own_k = np.arange(PAGE_SIZE)[None, :]
    own = np.where((own_k <= row_t) & (own_k < t_new), _t5_bucket_np(row_t - own_k), -1)
    return np.concatenate([near, own], axis=1).astype(np.int32)


def kernel(x_prompt, x_sample, cache_k, cache_v, state_ret, state_ssm, state_conv, page_table, c_prompt, c_sample, rel_bias, e_norm_w, e_ada_w, e_ada_b, e_in_w, e_q_norm_w, e_k_norm_w, e_out_w, o_norm_w, o_ada_w, o_ada_b, o_in_w, o_sgu_w, o_sgu_b, o_conv_w, o_conv_b, o_dt_bias, o_A_log, o_D, o_ssm_norm_w, o_out_w):
    bp, s_len, d = x_prompt.shape
    bs, t_len, _ = x_sample.shape
    n_s = bs * t_len

    c_all = jnp.concatenate([jnp.repeat(c_sample, t_len, axis=0), c_prompt,
                             jnp.zeros((8 - (bp + n_s) % 8, d), F32)], axis=0)
    mods = []
    for ada_w, ada_b in ((e_ada_w[0], e_ada_b[0]), (o_ada_w[0], o_ada_b[0])):
        mod = _ada_mod(c_all, ada_w, ada_b)
        parts_p = [mod[i, n_s:n_s + bp].reshape(bp, 1, d) for i in range(3)]
        mods.append((parts_p, mod))
    (e_mod_p, e_mod_s), (o_mod_p, o_mod_s) = mods

    seg = jnp.asarray(np.kron(np.eye(H_A), np.ones((HD_A, HD_A))), BF16)
    qnw = jnp.tile(e_q_norm_w[0], H_A).reshape(1, W_A)
    knw = jnp.tile(e_k_norm_w[0], H_A).reshape(1, W_A)
    e_in_bf = _bf(e_in_w[0])
    e_out_bf = _bf(e_out_w[0])
    o_in_bf = _bf(o_in_w[0])
    o_dt_bf = _bf(jnp.repeat(o_in_w[0][:, 3072:], P_D, axis=1))
    o_out_bf = _bf(o_out_w[0])
    x_s = x_sample.reshape(1, n_s, d)
    cos_p, sin_p = _rotary_tables(np.arange(s_len))
    cos_s, sin_s = _rotary_tables(PAST_LEN + (np.arange(n_s) % t_len))
    bias_p = _bias_tables(rel_bias, _prompt_bias_idx(), LOG2E).reshape(H_A, 2, MOBA_BLOCK, MOBA_BLOCK)
    bias_s_h = _bias_tables(rel_bias, _sample_bias_idx(t_len))
    row_h = jnp.arange(N_ROWS_S) % H_A
    bias_s = jnp.sum(jnp.where((jnp.arange(H_A)[:, None] == row_h[None, :])[:, :, None], bias_s_h, 0.0), axis=0)
    bfar = rel_bias[NUM_BUCKETS - 1, row_h].reshape(N_ROWS_S, 1)

    (qa, ka, va, kbf, vt, ga, qb, kb, vb, gb, kmean) = _even_in(
        x_prompt, e_mod_p[1], e_mod_p[0], e_norm_w[0].reshape(1, d), e_in_bf, qnw, knw, seg, cos_p, sin_p, None, BF16)
    (qa_s, ka_s, va_s, _, _, ga_s, qb_s, kb_s, vb_s, gb_s, _) = _even_in(
        x_s, e_mod_s, e_mod_s, e_norm_w[0].reshape(1, d), e_in_bf, qnw, knw, seg, cos_s, sin_s, (1, 0), F32)
    sq = lambda a: a.reshape(bs, t_len, a.shape[-1])
    qrep = jnp.repeat(sq(qa_s), H_A, axis=1)
    n_phys = cache_k.shape[1]
    page_t = lambda c: jnp.transpose(c[0], (0, 2, 3, 1)).reshape(n_phys, W_A, PAGE_SIZE)
    oa, oa_s = _moba(page_table, rel_bias, qa, kbf, vt, kmean.reshape(bp, s_len // MOBA_BLOCK, W_A), bias_p, ga,
                     qrep, sq(ka_s), sq(va_s), sq(ga_s), bias_s, bfar, page_t(cache_k), page_t(cache_v))
    nc_p = s_len // CHUNK
    ch = lambda a: a.reshape(bp, nc_p, CHUNK, a.shape[-1])
    ob, ret_p = _retention(ch(qb), ch(kb), ch(vb), ch(gb), jnp.zeros((bp, H_B * DK_B, DV_B), F32),
                           *_retention_tables(CHUNK), lq=CHUNK, nbs=bp)
    xp1 = _out_proj(oa, ob.reshape(bp, s_len, W_B), x_prompt, e_mod_p[2], e_out_bf, None)
    k_prompt = ka.reshape(1, bp, s_len, H_A, HD_A)
    v_prompt = va.reshape(1, bp, s_len, H_A, HD_A)
    ret_state_prompt = ret_p.reshape(1, bp, H_B, DK_B, DV_B)

    sc = lambda a: a.reshape(bs, 1, t_len, a.shape[-1])
    ob_s, ret_s = _retention(sc(qb_s), sc(kb_s), sc(vb_s), sc(gb_s),
                             state_ret[0].reshape(bs, H_B * DK_B, DV_B), *_retention_tables(t_len), lq=8, nbs=SEQ_PER_STEP)
    xs1 = _out_proj(oa_s.reshape(1, n_s, W_A), ob_s.reshape(1, n_s, W_B), x_s, e_mod_s, e_out_bf, 2)
    k_sample = ka_s.reshape(1, bs, t_len, H_A, HD_A)
    v_sample = va_s.reshape(1, bs, t_len, H_A, HD_A)
    ret_state_sample = ret_s.reshape(1, bs, H_B, DK_B, DV_B)

    tri = jnp.asarray(np.tril(np.ones((CHUNK, CHUNK))), BF16)
    sel = jnp.asarray(np.kron(np.eye(H_D), np.eye(1, P_D)), BF16)
    rep = lambda a: jnp.repeat(a, P_D).reshape(1, W_D)
    dt_bias, a_log, d_skip = rep(o_dt_bias[0]), rep(o_A_log[0]), rep(o_D[0])
    ssm_nw = o_ssm_norm_w[0].reshape(1, W_D)
    conv_b = o_conv_b[0].reshape(1, CONV_DIM)
    o_nw = o_norm_w[0].reshape(1, d)

    sgu_b_p = jnp.repeat(o_sgu_b[0].T, W_C // G_C, axis=1)
    oc, zg, xbc, dtp = _odd_in(xp1, o_mod_p[1], o_mod_p[0], o_nw, o_in_bf, o_dt_bf, o_sgu_w[0], sgu_b_p, dt_bias,
                               None, BF16, False)
    yn, ssm_p = _ssd(ch(xbc), ch(dtp), ch(zg), jnp.zeros((bp, 8, CONV_DIM), F32),
                     jnp.zeros((bp, H_D * P_D, N_D), F32), o_conv_w[0], conv_b, a_log, d_skip, ssm_nw,
                     tri, sel, lq=CHUNK, nbs=bp)
    y_prompt = _out_proj(oc, yn.reshape(bp, s_len, W_D), xp1, o_mod_p[2], o_out_bf, None)
    ssm_state_prompt = ssm_p.reshape(1, bp, H_D, P_D, N_D)
    conv_state_prompt = xbc[:, -(CONV_W - 1):][None]

    per_chunk = CHUNK // t_len
    w_small = o_sgu_w[0][:, :t_len, :t_len]
    same_seq = jnp.asarray(np.kron(np.eye(per_chunk), np.ones((t_len, t_len))), F32)
    tok_of_row = jnp.asarray(np.tile(np.eye(t_len), (per_chunk, 1)), F32)
    sgu_w_s = jnp.einsum('ia,gab,jb->gij', tok_of_row, w_small, tok_of_row,
                         precision=lax.Precision.HIGHEST) * same_seq
    sgu_b_s = jnp.repeat(jnp.tile(o_sgu_b[0][:, :t_len].T, (per_chunk, 1)), W_C // G_C, axis=1)
    oc_s, zg_s, xbc_s, dt_s, v_s = _odd_in(xs1, o_mod_s, o_mod_s, o_nw, o_in_bf, o_dt_bf, sgu_w_s, sgu_b_s, dt_bias,
                                           (1, 0), F32, True)
    tail_s = jnp.concatenate([jnp.zeros((bs, 8 - (CONV_W - 1), CONV_DIM), F32), state_conv[0]], axis=1)
    yn_s, ssm_s = _ssd(sc(xbc_s), sc(dt_s), sc(zg_s), tail_s, state_ssm[0].reshape(bs, H_D * P_D, N_D),
                       o_conv_w[0], conv_b, a_log, d_skip, ssm_nw, tri, sel, lq=8, nbs=SEQ_PER_STEP)
    xs2 = _out_proj(oc_s, yn_s.reshape(1, n_s, W_D), xs1, o_mod_s, o_out_bf, 2)
    y_sample = xs2.reshape(bs, t_len, d)
    sgu_v_sample = v_s.reshape(1, bs, t_len, W_C)
    ssm_state_sample = ssm_s.reshape(1, bs, H_D, P_D, N_D)
    xin = jnp.concatenate([state_conv[0], xbc_s.reshape(bs, t_len, CONV_DIM)], axis=1)
    conv_state_sample = xin[:, -(CONV_W - 1):][None]

    return (y_prompt, y_sample, k_prompt, v_prompt, k_sample, v_sample, ret_state_prompt, ret_state_sample,
            sgu_v_sample, ssm_state_prompt, ssm_state_sample, conv_state_prompt, conv_state_sample)
```
